```python
import math
import jax, jax.numpy as jnp
from jax import lax
import numpy as np

D_MODEL = 1024
BATCH = 8
SEQ = 2048
DEPTH = 2
DEC_BATCH = 128
DEC_SEQ = 8
PAST_LEN = 16384
PAGE_SIZE = 128

MIX_WIDTH = 2 * D_MODEL
SSD_WIDTH = MIX_WIDTH // 2
SSD_HEAD_DIM = 64
SSD_HEADS = SSD_WIDTH // SSD_HEAD_DIM
SSD_GROUPS = 2
SSD_HPG = SSD_HEADS // SSD_GROUPS
SSD_STATE = 128
SSD_CONV_DIM = SSD_WIDTH + 2 * SSD_GROUPS * SSD_STATE
ML_WIDTH = MIX_WIDTH // 4
ML_HEADS = 4
ML_HEAD_DIM = ML_WIDTH // ML_HEADS
LRU_WIDTH = MIX_WIDTH // 4
LRU_BLOCKS = 4
LRU_BLOCK_DIM = LRU_WIDTH // LRU_BLOCKS
LRU_C = 8.0
CONV_W = 4
CHUNK = 128
D_FF = 4 * D_MODEL
EPS = 1e-6
IN_SEGMENTS = (SSD_WIDTH, SSD_CONV_DIM, SSD_HEADS, ML_WIDTH, ML_WIDTH, ML_WIDTH, ML_WIDTH, ML_HEADS, ML_HEADS, LRU_WIDTH, LRU_WIDTH)
IN_DIM = sum(IN_SEGMENTS)

kernel_name = 'hymba_ssd_mlstm_rglru_decoder_step'


def _split_points():
    pts, acc = [], 0
    for s in IN_SEGMENTS[:-1]:
        acc += s
        pts.append(acc)
    return pts


def rmsnorm(x, w):
    xf = x.astype(jnp.float32)
    return xf * lax.rsqrt(jnp.mean(xf * xf, axis=-1, keepdims=True) + EPS) * w.astype(jnp.float32)


def causal_conv(x, buf, w, b):
    T = x.shape[1]
    xpad = jnp.concatenate([buf.astype(jnp.float32), x], axis=1)
    w = w.astype(jnp.float32)
    out = b.astype(jnp.float32) + xpad[:, 0:T] * w[0]
    for j in range(1, CONV_W):
        out = out + xpad[:, j:j + T] * w[j]
    return out, xpad[:, xpad.shape[1] - (CONV_W - 1):]


def causal_mask(L):
    i = jnp.arange(L)
    return i[:, None] >= i[None, :]


def to_chunks(a, L):
    return a.reshape((a.shape[0], a.shape[1] // L, L) + a.shape[2:]).swapaxes(0, 1)


def from_chunks(a):
    a = a.swapaxes(0, 1)
    return a.reshape((a.shape[0], a.shape[1] * a.shape[2]) + a.shape[3:])


def ssd_chunked(x, dt, a, bm, cm, s0):
    L = math.gcd(x.shape[1], CHUNK)
    mask = causal_mask(L)[None, :, :, None, None]

    def step(s, inp):
        xc, dtc, bc, cc = inp
        cum = jnp.cumsum(dtc * a, axis=1)
        decay = jnp.exp(jnp.where(mask, cum[:, :, None] - cum[:, None, :], -jnp.inf))
        att = jnp.einsum('btgn,bsgn->btsg', cc, bc)[..., None] * decay
        xdt = xc * dtc[..., None]
        y = (jnp.einsum('btsge,bsgep->btgep', att, xdt)
             + jnp.einsum('btgn,bgepn->btgep', cc, s) * jnp.exp(cum)[..., None])
        xend = xdt * jnp.exp(cum[:, -1:] - cum)[..., None]
        s = jnp.exp(cum[:, -1])[..., None, None] * s + jnp.einsum('bsgn,bsgep->bgepn', bc, xend)
        return s, y

    s, ys = lax.scan(step, s0, tuple(to_chunks(t, L) for t in (x, dt, bm, cm)))
    return from_chunks(ys), s


def mlstm_chunked(q, k, v, ig, lf, c0, n0, m0):
    L = math.gcd(q.shape[1], CHUNK)
    mask = causal_mask(L)[None, :, :, None]

    def step(carry, inp):
        c, n, m = carry
        qc, kc, vc, ic, fc = inp
        bc = jnp.cumsum(fc, axis=1)
        dmat = jnp.where(mask, bc[:, :, None] - bc[:, None, :] + ic[:, None, :], -jnp.inf)
        g = bc + m[:, None]
        mt = jnp.maximum(g, jnp.max(dmat, axis=2))
        w = jnp.exp(dmat - mt[:, :, None]) * jnp.einsum('bthd,bshd->btsh', qc, kc)
        inter = jnp.exp(g - mt)
        num = jnp.einsum('btsh,bshd->bthd', w, vc) + inter[..., None] * jnp.einsum('bhvd,bthd->bthv', c, qc)
        den = jnp.sum(w, axis=2) + inter * jnp.einsum('bhd,bthd->bth', n, qc)
        hout = num / jnp.maximum(jnp.abs(den), jnp.exp(-mt))[..., None]
        m_new = mt[:, -1]
        w_end = jnp.exp(bc[:, -1:] - bc + ic - m_new[:, None])
        dc = jnp.exp(bc[:, -1] + m - m_new)
        c = dc[..., None, None] * c + jnp.einsum('bshv,bshd->bhvd', vc * w_end[..., None], kc)
        n = dc[..., None] * n + jnp.einsum('bsh,bshd->bhd', w_end, kc)
        return (c, n, m_new), hout

    (c, n, m), hs = lax.scan(step, (c0, n0, m0), tuple(to_chunks(t, L) for t in (q, k, v, ig, lf)))
    return from_chunks(hs), c, n, m


def block_diag(x, w):
    b, T = x.shape[0], x.shape[1]
    xb = x.reshape(b, T, LRU_BLOCKS, LRU_BLOCK_DIM)
    return jnp.einsum('btkc,kcd->btkd', xb, w.astype(jnp.float32)).reshape(b, T, LRU_WIDTH)


def lru_scan(a, u, h0):
    u = u.at[:, 0].add(a[:, 0] * h0)

    def combine(left, right):
        return left[0] * right[0], right[0] * left[1] + right[1]

    _, hs = lax.associative_scan(combine, (a, u), axis=1)
    return hs, hs[:, -1]


def mixer(h, l, P, st, start_pos):
    f32 = jnp.float32
    b, T = h.shape[0], h.shape[1]
    ssm0, sconv0, mc0, mn0, mm0, lh0, lconv0 = (s.astype(f32) for s in st)
    u = jnp.matmul(h, P['w_in'][l]).astype(f32)
    z, xbc, dt_raw, q, k, v, o, ig, fg, xr, gr = jnp.split(u, _split_points(), axis=-1)

    xbc, sconv = causal_conv(xbc, sconv0, P['ssd_conv_w'][l], P['ssd_conv_b'][l])
    xbc = jax.nn.silu(xbc)
    xs, bm, cm = jnp.split(xbc, [SSD_WIDTH, SSD_WIDTH + SSD_GROUPS * SSD_STATE], axis=-1)
    dt = jax.nn.softplus(dt_raw + P['ssd_dt_bias'][l].astype(f32))
    a = -jnp.exp(P['ssd_a_log'][l].astype(f32))
    y, ssm = ssd_chunked(
        xs.reshape(b, T, SSD_GROUPS, SSD_HPG, SSD_HEAD_DIM),
        dt.reshape(b, T, SSD_GROUPS, SSD_HPG),
        a.reshape(SSD_GROUPS, SSD_HPG),
        bm.reshape(b, T, SSD_GROUPS, SSD_STATE),
        cm.reshape(b, T, SSD_GROUPS, SSD_STATE),
        ssm0.reshape(b, SSD_GROUPS, SSD_HPG, SSD_HEAD_DIM, SSD_STATE))
    y = y.reshape(b, T, SSD_WIDTH) + xs * jnp.repeat(P['ssd_d'][l].astype(f32), SSD_HEAD_DIM)
    y_ssd = rmsnorm(y * jax.nn.silu(z), P['ssd_norm_w'][l])
    ssm = ssm.reshape(b, SSD_HEADS, SSD_HEAD_DIM, SSD_STATE)

    hd = (b, T, ML_HEADS, ML_HEAD_DIM)
    hm, mc, mn, mm = mlstm_chunked(
        q.reshape(hd), k.reshape(hd) * (ML_HEAD_DIM ** -0.5), v.reshape(hd),
        ig + P['ml_i_bias'][l].astype(f32),
        jax.nn.log_sigmoid(fg + P['ml_f_bias'][l].astype(f32)),
        mc0, mn0, mm0)
    y_ml = rmsnorm(hm, P['ml_norm_w'][l].reshape(ML_HEADS, ML_HEAD_DIM)).reshape(b, T, ML_WIDTH) * jax.nn.sigmoid(o)

    xr, lconv = causal_conv(xr, lconv0, P['lru_conv_w'][l], P['lru_conv_b'][l])
    r = jax.nn.sigmoid(block_diag(xr, P['lru_wa'][l]) + P['lru_ba'][l].astype(f32))
    i = jax.nn.sigmoid(block_diag(xr, P['lru_wx'][l]) + P['lru_bx'][l].astype(f32))
    log_a = -LRU_C * r * jax.nn.softplus(-P['lru_lambda'][l].astype(f32))
    pos = start_pos + jnp.arange(T)
    mult = jnp.where((pos == 0)[None, :, None], 1.0, jnp.sqrt(-jnp.expm1(2.0 * log_a)))
    hr, lh = lru_scan(jnp.exp(log_a), mult * i * xr, lh0)
    y_lru = hr * jax.nn.gelu(gr)

    ycat = jnp.concatenate([y_ssd, y_ml, y_lru], axis=-1).astype(h.dtype)
    out = jnp.matmul(ycat, P['w_out'][l])
    return out, (ssm, sconv, mc, mn, mm, lh, lconv)


def zero_states(b):
    z = lambda s: jnp.zeros(s, jnp.float32)
    return (z((DEPTH, b, SSD_HEADS, SSD_HEAD_DIM, SSD_STATE)),
            z((DEPTH, b, CONV_W - 1, SSD_CONV_DIM)),
            z((DEPTH, b, ML_HEADS, ML_HEAD_DIM, ML_HEAD_DIM)),
            z((DEPTH, b, ML_HEADS, ML_HEAD_DIM)),
            z((DEPTH, b, ML_HEADS)),
            z((DEPTH, b, LRU_WIDTH)),
            z((DEPTH, b, CONV_W - 1, LRU_WIDTH)))


def trunk(x, c, states, P, start_pos):
    f32 = jnp.float32
    new = [[] for _ in states]
    cs = jax.nn.silu(c.astype(f32))
    for l in range(DEPTH):
        mod = jnp.matmul(cs, P['ada_w'][l].astype(f32)) + P['ada_b'][l].astype(f32)
        sh1, sc1, g1, sh2, sc2, g2 = (m[:, None, :] for m in jnp.split(mod, 6, axis=-1))
        hn = (rmsnorm(x, P['norm1_w'][l]) * (1.0 + sc1) + sh1).astype(x.dtype)
        mix, st_new = mixer(hn, l, P, tuple(s[l] for s in states), start_pos)
        x = (x.astype(f32) + g1 * mix.astype(f32)).astype(x.dtype)
        hn = (rmsnorm(x, P['norm2_w'][l]) * (1.0 + sc2) + sh2).astype(x.dtype)
        ff = jnp.matmul(jnp.square(jax.nn.relu(jnp.matmul(hn, P['mlp_up'][l]))), P['mlp_down'][l])
        x = (x.astype(f32) + g2 * ff.astype(f32)).astype(x.dtype)
        for lst, s in zip(new, st_new):
            lst.append(s)
    y = rmsnorm(x, P['final_norm_w']).astype(x.dtype)
    return y, tuple(jnp.stack(lst) for lst in new)


def setup_inputs(seed: int = 0) -> dict:
    key = jax.random.key(seed)
    ks = iter(jax.random.split(key, 48))
    f32 = jnp.float32

    def nrm(shape, scale):
        return scale * jax.random.normal(next(ks), shape, f32)

    def unif(shape, lo, hi):
        return jax.random.uniform(next(ks), shape, f32, lo, hi)

    dt0 = jnp.exp(unif((DEPTH, SSD_HEADS), math.log(1e-3), math.log(1e-1)))
    lam_a = unif((DEPTH, LRU_WIDTH), 0.9, 0.999)
    return {
        'x_prompt': nrm((BATCH, SEQ, D_MODEL), 1.0),
        'x_sample': nrm((DEC_BATCH, DEC_SEQ, D_MODEL), 1.0),
        'c_prompt': nrm((BATCH, D_MODEL), 1.0),
        'c_sample': nrm((DEC_BATCH, D_MODEL), 1.0),
        'state_ssm': nrm((DEPTH, DEC_BATCH, SSD_HEADS, SSD_HEAD_DIM, SSD_STATE), 0.1),
        'state_ssd_conv': nrm((DEPTH, DEC_BATCH, CONV_W - 1, SSD_CONV_DIM), 1.0),
        'state_mlstm_c': nrm((DEPTH, DEC_BATCH, ML_HEADS, ML_HEAD_DIM, ML_HEAD_DIM), 0.1),
        'state_mlstm_n': nrm((DEPTH, DEC_BATCH, ML_HEADS, ML_HEAD_DIM), 0.1),
        'state_mlstm_m': nrm((DEPTH, DEC_BATCH, ML_HEADS), 1.0),
        'state_lru_h': nrm((DEPTH, DEC_BATCH, LRU_WIDTH), 0.5),
        'state_lru_conv': nrm((DEPTH, DEC_BATCH, CONV_W - 1, LRU_WIDTH), 1.0),
        'ada_w': nrm((DEPTH, D_MODEL, 6 * D_MODEL), D_MODEL ** -0.5),
        'ada_b': nrm((DEPTH, 6 * D_MODEL), 0.01),
        'norm1_w': 1.0 + nrm((DEPTH, D_MODEL), 0.01),
        'norm2_w': 1.0 + nrm((DEPTH, D_MODEL), 0.01),
        'w_in': nrm((DEPTH, D_MODEL, IN_DIM), D_MODEL ** -0.5),
        'ssd_conv_w': nrm((DEPTH, CONV_W, SSD_CONV_DIM), CONV_W ** -0.5),
        'ssd_conv_b': nrm((DEPTH, SSD_CONV_DIM), 0.01),
        'ssd_dt_bias': dt0 + jnp.log(-jnp.expm1(-dt0)),
        'ssd_a_log': jnp.log(unif((DEPTH, SSD_HEADS), 1.0, 16.0)),
        'ssd_d': 1.0 + nrm((DEPTH, SSD_HEADS), 0.1),
        'ssd_norm_w': 1.0 + nrm((DEPTH, SSD_WIDTH), 0.01),
        'ml_i_bias': nrm((DEPTH, ML_HEADS), 0.1),
        'ml_f_bias': jnp.linspace(3.0, 6.0, ML_HEADS, dtype=f32)[None, :] + nrm((DEPTH, ML_HEADS), 0.1),
        'ml_norm_w': 1.0 + nrm((DEPTH, ML_WIDTH), 0.01),
        'lru_conv_w': nrm((DEPTH, CONV_W, LRU_WIDTH), CONV_W ** -0.5),
        'lru_conv_b': nrm((DEPTH, LRU_WIDTH), 0.01),
        'lru_wa': nrm((DEPTH, LRU_BLOCKS, LRU_BLOCK_DIM, LRU_BLOCK_DIM), LRU_BLOCK_DIM ** -0.5),
        'lru_ba': nrm((DEPTH, LRU_WIDTH), 0.01),
        'lru_wx': nrm((DEPTH, LRU_BLOCKS, LRU_BLOCK_DIM, LRU_BLOCK_DIM), LRU_BLOCK_DIM ** -0.5),
        'lru_bx': nrm((DEPTH, LRU_WIDTH), 0.01),
        'lru_lambda': jnp.log(lam_a) - jnp.log1p(-lam_a),
        'w_out': nrm((DEPTH, MIX_WIDTH, D_MODEL), MIX_WIDTH ** -0.5),
        'mlp_up': nrm((DEPTH, D_MODEL, D_FF), D_MODEL ** -0.5),
        'mlp_down': nrm((DEPTH, D_FF, D_MODEL), D_FF ** -0.5),
        'final_norm_w': 1.0 + nrm((D_MODEL,), 0.01),
    }


def reference(x_prompt, x_sample, c_prompt, c_sample,
              state_ssm, state_ssd_conv, state_mlstm_c, state_mlstm_n, state_mlstm_m,
              state_lru_h, state_lru_conv,
              ada_w, ada_b, norm1_w, norm2_w, w_in,
              ssd_conv_w, ssd_conv_b, ssd_dt_bias, ssd_a_log, ssd_d, ssd_norm_w,
              ml_i_bias, ml_f_bias, ml_norm_w,
              lru_conv_w, lru_conv_b, lru_wa, lru_ba, lru_wx, lru_bx, lru_lambda,
              w_out, mlp_up, mlp_down, final_norm_w):
    P = dict(ada_w=ada_w, ada_b=ada_b, norm1_w=norm1_w, norm2_w=norm2_w, w_in=w_in,
             ssd_conv_w=ssd_conv_w, ssd_conv_b=ssd_conv_b, ssd_dt_bias=ssd_dt_bias,
             ssd_a_log=ssd_a_log, ssd_d=ssd_d, ssd_norm_w=ssd_norm_w,
             ml_i_bias=ml_i_bias, ml_f_bias=ml_f_bias, ml_norm_w=ml_norm_w,
             lru_conv_w=lru_conv_w, lru_conv_b=lru_conv_b, lru_wa=lru_wa, lru_ba=lru_ba,
             lru_wx=lru_wx, lru_bx=lru_bx, lru_lambda=lru_lambda,
             w_out=w_out, mlp_up=mlp_up, mlp_down=mlp_down, final_norm_w=final_norm_w)
    y_prompt, (p_ssm, p_sconv, p_mc, p_mn, p_mm, p_lh, p_lconv) = trunk(
        x_prompt, c_prompt, zero_states(x_prompt.shape[0]), P, 0)
    y_sample, (s_ssm, s_sconv, s_mc, s_mn, s_mm, s_lh, s_lconv) = trunk(
        x_sample, c_sample,
        (state_ssm, state_ssd_conv, state_mlstm_c, state_mlstm_n, state_mlstm_m, state_lru_h, state_lru_conv),
        P, PAST_LEN)
    return (y_prompt, y_sample,
            p_ssm, p_sconv, p_mc, p_mn, p_mm, p_lh, p_lconv,
            s_ssm, s_sconv, s_mc, s_mn, s_mm, s_lh, s_lconv)
```

```python
import functools
import math

import jax
import jax.numpy as jnp
from jax import lax
from jax.experimental import pallas as pl
from jax.experimental.pallas import tpu as pltpu

f32 = jnp.float32
bf16 = jnp.bfloat16

D_MODEL = 1024
N_LAYERS = 2
MIX_WIDTH = 2 * D_MODEL
SSD_WIDTH = 1024
SSD_HEAD_DIM = 64
SSD_HEADS = 16
SSD_GROUPS = 2
SSD_STATE = 128
SSD_CONV_DIM = SSD_WIDTH + 2 * SSD_GROUPS * SSD_STATE
ML_WIDTH = 512
ML_HEADS = 4
ML_HEAD_DIM = 128
LRU_WIDTH = 512
LRU_BLOCKS = 4
LRU_BLOCK_DIM = 128
LRU_C = 8.0
CONV_W = 4
CHUNK = 128
D_FF = 4 * D_MODEL
EPS = 1e-6
PAST_LEN = 16384
DEC_SEQ = 8

_O_Z, _O_XBC, _O_DT, _O_Q, _O_O_END, _O_I, _O_F, _O_XR, _O_END = (
    0, 1024, 2560, 2576, 4624, 4624, 4628, 4632, 5656)

P_Z = 0
P_XBC = 1024
P_Q = 2560
P_K = 3072
P_V = 3584
P_O = 4096
P_XR = 4608
P_GR = 5120
P_MAIN = 5632
P_SMALL = 5632
NP_COLS = 5760
S_DT = 5632
S_I = 6656
S_F = 7168
NS_COLS = 7680

V7X_LANES = 128
V7X_SUBLANES = 8
VMEM_LIMIT = 56 * 1024 * 1024

_NT = (((1,), (1,)), ((), ()))
_TN = (((0,), (0,)), ((), ()))


def _softplus(x):
    return jnp.maximum(x, 0.0) + jnp.log1p(jnp.exp(-jnp.abs(x)))


def _log_sigmoid(x):
    return -_softplus(-x)


def _sigmoid(x):
    return 1.0 / (1.0 + jnp.exp(-x))


def _silu(x):
    return x * _sigmoid(x)


def _gelu_tanh(x):
    c = math.sqrt(2.0 / math.pi)
    return 0.5 * x * (1.0 + jnp.tanh(c * (x + 0.044715 * (x * x * x))))


def _neg_expm1(x):
    return -jnp.tanh(0.5 * x) * (jnp.exp(x) + 1.0)


def _rms(x, w):
    return x * lax.rsqrt(jnp.mean(x * x, axis=-1, keepdims=True) + EPS) * w


def _full(shape):
    n = len(shape)
    return pl.BlockSpec(shape, lambda *_: (0,) * n)


def _resident(shape):
    n = len(shape)
    return pl.BlockSpec(shape, lambda *_: (0,) * n, pipeline_mode=pl.Buffered(1))


def _ada_kernel(c_ref, w_ref, b_ref, o_ref):
    cs = _silu(c_ref[...]).astype(bf16)
    o_ref[0] = jnp.dot(cs, w_ref[0].astype(bf16), preferred_element_type=f32) + b_ref[0]


def _ada(c_all, ada_w, ada_b):
    n = c_all.shape[0]
    tn = 1024
    return pl.pallas_call(
        _ada_kernel,
        out_shape=jax.ShapeDtypeStruct((N_LAYERS, n, 6 * D_MODEL), f32),
        grid=(N_LAYERS, 6 * D_MODEL // tn),
        in_specs=[
            pl.BlockSpec((n, D_MODEL), lambda l, j: (0, 0)),
            pl.BlockSpec((1, D_MODEL, tn), lambda l, j: (l, 0, j)),
            pl.BlockSpec((1, 1, tn), lambda l, j: (l, 0, j)),
        ],
        out_specs=pl.BlockSpec((1, n, tn), lambda l, j: (l, 0, j)),
        compiler_params=pltpu.CompilerParams(
            dimension_semantics=("arbitrary", "arbitrary"), vmem_limit_bytes=VMEM_LIMIT),
        name="ada_mod",
    )(c_all, ada_w, ada_b.reshape(N_LAYERS, 1, 6 * D_MODEL))


def _inproj_kernel(x_ref, mod_ref, nw_ref, w_ref, o_ref):
    x = x_ref[...]
    bb, tt, d = x.shape
    hn = _rms(x, nw_ref[...]) * (1.0 + mod_ref[:, 1:2, :]) + mod_ref[:, 0:1, :]
    hn = hn.reshape(bb * tt, d).astype(bf16)
    u = jnp.dot(hn, w_ref[...], preferred_element_type=f32)
    o_ref[...] = u.reshape(bb, tt, u.shape[-1])


def _inproj(x, mod, nw, w, bb, tt):
    b, t, d = x.shape
    n = w.shape[1]
    return pl.pallas_call(
        _inproj_kernel,
        out_shape=jax.ShapeDtypeStruct((b, t, n), f32),
        grid=(b // bb, t // tt),
        in_specs=[
            pl.BlockSpec((bb, tt, d), lambda i, j: (i, j, 0)),
            pl.BlockSpec((bb, 6, d), lambda i, j: (i, 0, 0)),
            _full((1, d)),
            _resident((d, n)),
        ],
        out_specs=pl.BlockSpec((bb, tt, n), lambda i, j: (i, j, 0)),
        compiler_params=pltpu.CompilerParams(
            dimension_semantics=("arbitrary", "arbitrary"), vmem_limit_bytes=VMEM_LIMIT),
        name="inproj",
    )(x, mod, nw, w)


def _outmlp_kernel(x_ref, y_ref, mod_ref, wout_ref, n2w_ref, up_ref, down_ref, fnw_ref, o_ref,
                   *, final):
    x = x_ref[...]
    bb, tt, d = x.shape
    m = bb * tt
    ycat = y_ref[...].reshape(m, MIX_WIDTH).astype(bf16)
    mix =jnp.dot(ycat, wout_ref[...], preferred_element_type=f32).reshape(bb, tt, d)
    x1 = x + mod_ref[:, 2:3, :] * mix
    hn = _rms(x1, n2w_ref[...]) * (1.0 + mod_ref[:, 4:5, :]) + mod_ref[:, 3:4, :]
    hn = hn.reshape(m, d).astype(bf16)
    ff = jnp.zeros((m, d), f32)
    fc = 1024
    for c in range(D_FF // fc):
        h = jnp.dot(hn, up_ref[:, c * fc:(c + 1) * fc], preferred_element_type=f32)
        h = jnp.square(jnp.maximum(h, 0.0)).astype(bf16)
        ff = ff + jnp.dot(h, down_ref[c * fc:(c + 1) * fc, :], preferred_element_type=f32)
    x2 = x1 + mod_ref[:, 5:6, :] * ff.reshape(bb, tt, d)
    if final:
        x2 = _rms(x2, fnw_ref[...])
    o_ref[...] = x2


def _outmlp(x, ycat, mod, wout, n2w, up, down, fnw, bb, tt, final):
    b, t, d = x.shape
    return pl.pallas_call(
        functools.partial(_outmlp_kernel, final=final),
        out_shape=jax.ShapeDtypeStruct((b, t, d), f32),
        grid=(b // bb, t // tt),
        in_specs=[
            pl.BlockSpec((bb, tt, d), lambda i, j: (i, j, 0)),
            pl.BlockSpec((bb, tt, MIX_WIDTH), lambda i, j: (i, j, 0)),
            pl.BlockSpec((bb, 6, d), lambda i, j: (i, 0, 0)),
            _resident((MIX_WIDTH, d)),
            _full((1, d)),
            _resident((d, D_FF)),
            _resident((D_FF, d)),
            _full((1, d)),
        ],
        out_specs=pl.BlockSpec((bb, tt, d), lambda i, j: (i, j, 0)),
        compiler_params=pltpu.CompilerParams(
            dimension_semantics=("arbitrary", "arbitrary"), vmem_limit_bytes=VMEM_LIMIT),
        name="outproj_mlp",
    )(x, ycat, mod, wout, n2w, up, down, fnw)


def _pmix_kernel(u_ref, cw_ref, cb_ref, sbias_ref, alog_ref, dskip_ref, snw_ref, mnw_ref,
                 lcw_ref, lcb_ref, wa_ref, wx_ref, ba_ref, bx_ref, lam_ref,
                 y_ref, ssm_ref, sconv_ref, mc_ref, mn_ref, mm_ref, lh_ref, lconv_ref,
                 xbuf, lbuf, st, cst, nst, mst, hst):
    ci = pl.program_id(1)
    nc = pl.num_programs(1)
    L = CHUNK
    HEAD = V7X_SUBLANES

    @pl.when(ci == 0)
    def _():
        xbuf[0:HEAD, :] = jnp.zeros((HEAD, SSD_CONV_DIM), f32)
        lbuf[0:HEAD, :] = jnp.zeros((HEAD, LRU_WIDTH), f32)
        st[...] = jnp.zeros(st.shape, f32)
        cst[...] = jnp.zeros(cst.shape, f32)
        nst[...] = jnp.zeros(nst.shape, f32)
        mst[...] = jnp.zeros(mst.shape, f32)
        hst[...] = jnp.zeros(hst.shape, f32)

    row = lax.broadcasted_iota(jnp.int32, (L, L), 0)
    col = lax.broadcasted_iota(jnp.int32, (L, L), 1)
    tri = row >= col
    lo = col < SSD_HEAD_DIM

    def bcl(a, j):
        return jnp.broadcast_to(a[:, j:j + 1], (L, L))

    xbuf[HEAD:HEAD + L, :] = u_ref[0, :, P_XBC:P_XBC + SSD_CONV_DIM]
    xbc = cb_ref[...] + xbuf[HEAD - 3:HEAD - 3 + L, :] * cw_ref[0:1, :]
    for j in range(1, CONV_W):
        xbc = xbc + xbuf[HEAD - 3 + j:HEAD - 3 + j + L, :] * cw_ref[j:j + 1, :]
    xbc = _silu(xbc)
    xbuf[0:HEAD, :] = xbuf[L:L + HEAD, :]

    lbuf[HEAD:HEAD + L, :] = u_ref[0, :, P_XR:P_XR + LRU_WIDTH]
    xr = lcb_ref[...] + lbuf[HEAD - 3:HEAD - 3 + L, :] * lcw_ref[0:1, :]
    for j in range(1, CONV_W):
        xr = xr + lbuf[HEAD - 3 + j:HEAD - 3 + j + L, :] * lcw_ref[j:j + 1, :]
    lbuf[0:HEAD, :] = lbuf[L:L + HEAD, :]

    sm = u_ref[0, :, P_SMALL:P_SMALL + V7X_LANES] + sbias_ref[...]
    dt = _softplus(sm)
    a_row = -jnp.exp(alog_ref[...])
    gates = jnp.where(col < SSD_HEADS, dt * a_row,
                      jnp.where((col >= 20) & (col < 24), _log_sigmoid(sm), 0.0))
    cum = jnp.dot(tri.astype(f32), gates, precision=lax.Precision.HIGHEST,
                  preferred_element_type=f32)
    cum_t = cum.T
    sm_t = sm.T

    ys = []
    for g in range(SSD_GROUPS):
        bm = xbc[:, SSD_WIDTH + g * SSD_STATE:SSD_WIDTH + (g + 1) * SSD_STATE].astype(bf16)
        cm = xbc[:, SSD_WIDTH + (SSD_GROUPS + g) * SSD_STATE:
                 SSD_WIDTH + (SSD_GROUPS + g + 1) * SSD_STATE].astype(bf16)
        cb = lax.dot_general(cm, bm, _NT, preferred_element_type=f32)
        inter = jnp.dot(cm, st[g].astype(bf16), preferred_element_type=f32)
        for jj in range(4):
            jp = g * 4 + jj
            e0, e1 = 2 * jp, 2 * jp + 1
            c0 = bcl(cum, e0)
            c1 = bcl(cum, e1)
            cum_p = jnp.where(lo, c0, c1)
            dt_p = jnp.where(lo, bcl(dt, e0), bcl(dt, e1))
            xs_p = xbc[:, jp * L:(jp + 1) * L]
            xdt = xs_p * dt_p
            dec0 = jnp.exp(jnp.where(tri, c0 - cum_t[e0:e0 + 1, :], -jnp.inf))
            dec1 = jnp.exp(jnp.where(tri, c1 - cum_t[e1:e1 + 1, :], -jnp.inf))
            att0 = (cb * dec0).astype(bf16)
            att1 = (cb * dec1).astype(bf16)
            x_lo = jnp.where(lo, xdt, 0.0).astype(bf16)
            x_hi = jnp.where(lo, 0.0, xdt).astype(bf16)
            y_p = (jnp.dot(att0, x_lo, preferred_element_type=f32)
                   + jnp.dot(att1, x_hi, preferred_element_type=f32)
                   + inter[:, jj * L:(jj + 1) * L] * jnp.exp(cum_p)
                   + xs_p * dskip_ref[:, jp * L:(jp + 1) * L])
            ys.append(y_p)
            c_last = cum_p[L - 1:L, :]
            xend = (xdt * jnp.exp(c_last - cum_p)).astype(bf16)
            st[g, :, jj * L:(jj + 1) * L] = (
                jnp.exp(c_last) * st[g, :, jj * L:(jj + 1) * L]
                + lax.dot_general(bm, xend, _TN, preferred_element_type=f32))
    y = jnp.concatenate(ys, axis=1)
    z = u_ref[0, :, P_Z:P_Z + SSD_WIDTH]
    y_ref[0, :, 0:SSD_WIDTH] = _rms(y * _silu(z), snw_ref[...]).astype(bf16)

    scale = ML_HEAD_DIM ** -0.5
    for h in range(ML_HEADS):
        sl = slice(h * ML_HEAD_DIM, (h + 1) * ML_HEAD_DIM)
        q = u_ref[0, :, P_Q + h * ML_HEAD_DIM:P_Q + (h + 1) * ML_HEAD_DIM]
        k = u_ref[0, :, P_K + h * ML_HEAD_DIM:P_K + (h + 1) * ML_HEAD_DIM] * scale
        v = u_ref[0, :, P_V + h * ML_HEAD_DIM:P_V + (h + 1) * ML_HEAD_DIM]
        o = u_ref[0, :, P_O + h * ML_HEAD_DIM:P_O + (h + 1) * ML_HEAD_DIM]
        qb = q.astype(bf16)
        kb = k.astype(bf16)
        jf = 20 + h
        ji = 16 + h
        b_col = cum[:, jf:jf + 1]
        i_col = sm[:, ji:ji + 1]
        dm = jnp.where(tri, bcl(cum, jf) - cum_t[jf:jf + 1, :] + sm_t[ji:ji + 1, :], -jnp.inf)
        m_prev = mst[h, :, 0:1]
        g_col = b_col + m_prev
        mt = jnp.maximum(g_col, jnp.max(dm, axis=-1, keepdims=True))
        w = jnp.exp(dm - mt) * lax.dot_general(qb, kb, _NT, preferred_element_type=f32)
        inter = jnp.exp(g_col - mt)
        c_old = cst[h]
        n_old = nst[h]
        num = (jnp.dot(w.astype(bf16), v.astype(bf16), preferred_element_type=f32)
               + inter * lax.dot_general(qb, c_old.astype(bf16), _NT, preferred_element_type=f32))
        den = (jnp.sum(w, axis=-1, keepdims=True)
               + inter * jnp.sum(q * n_old, axis=-1, keepdims=True))
        hout = num / jnp.maximum(jnp.abs(den), jnp.exp(-mt))
        m_new = mt[L - 1:L, :]
        b_last = b_col[L - 1:L, :]
        w_end = jnp.exp(b_last - b_col + i_col - m_new)
        dc = jnp.exp(b_last + m_prev - m_new)
        cst[h] = dc * c_old + lax.dot_general((v * w_end).astype(bf16), kb, _TN,
                                              preferred_element_type=f32)
        nst[h] = dc * n_old + jnp.sum(w_end * k, axis=0, keepdims=True)
        mst[h] = jnp.broadcast_to(m_new, (1, V7X_LANES))
        yh = _rms(hout, mnw_ref[:, sl]) * _sigmoid(o)
        y_ref[0, :, SSD_WIDTH + h * ML_HEAD_DIM:SSD_WIDTH + (h + 1) * ML_HEAD_DIM] = yh.astype(bf16)

    rs, is_ = [], []
    for kb_ in range(LRU_BLOCKS):
        xk = xr[:, kb_ * LRU_BLOCK_DIM:(kb_ + 1) * LRU_BLOCK_DIM].astype(bf16)
        rs.append(jnp.dot(xk, wa_ref[kb_], preferred_element_type=f32))
        is_.append(jnp.dot(xk, wx_ref[kb_], preferred_element_type=f32))
    r = _sigmoid(jnp.concatenate(rs, axis=1) + ba_ref[...])
    i_g = _sigmoid(jnp.concatenate(is_, axis=1) + bx_ref[...])
    log_a = -LRU_C * r * _softplus(-lam_ref[...])
    mult = jnp.sqrt(_neg_expm1(2.0 * log_a))
    rowl = lax.broadcasted_iota(jnp.int32, (L, LRU_WIDTH), 0)
    first_row = jnp.where(ci == 0, 0, -1)
    mult = jnp.where(rowl == first_row, 1.0, mult)
    a_s = jnp.exp(log_a)
    h_s = mult * i_g * xr
    k_ = 1
    while k_ < L:
        keep = rowl >= k_
        a_sh = jnp.where(keep, pltpu.roll(a_s, k_, 0), 1.0)
        h_sh = jnp.where(keep, pltpu.roll(h_s, k_, 0), 0.0)
        h_s = a_s * h_sh + h_s
        a_s = a_s * a_sh
        k_ *= 2
    hr = h_s + a_s * hst[...]
    hst[...] = hr[L - 1:L, :]
    gr = u_ref[0, :, P_GR:P_GR + LRU_WIDTH]
    y_ref[0, :, SSD_WIDTH + ML_WIDTH:MIX_WIDTH] = (hr * _gelu_tanh(gr)).astype(bf16)

    @pl.when(ci == nc - 1)
    def _():
        for g in range(SSD_GROUPS):
            for jj in range(4):
                ssm_ref[0, g * 4 + jj] = st[g, :, jj * L:(jj + 1) * L].T
        sconv_ref[0] = xbuf[HEAD - 3:HEAD, :]
        lconv_ref[0] = lbuf[HEAD - 3:HEAD, :]
        for h in range(ML_HEADS):
            mc_ref[0, h] = cst[h]
            mn_ref[0, h:h + 1, :] = nst[h]
            mm_ref[0, h:h + 1, :] = mst[h]
        lh_ref[0] = hst[...]


def _pmix(u, w):
    b, t, _ = u.shape
    nc = t // CHUNK
    row = lambda n: _full((1, n))
    outs = [
        jax.ShapeDtypeStruct((b, t, MIX_WIDTH), bf16),
        jax.ShapeDtypeStruct((b, 8, 128, 128), f32),
        jax.ShapeDtypeStruct((b, CONV_W - 1, SSD_CONV_DIM), f32),
        jax.ShapeDtypeStruct((b, ML_HEADS, ML_HEAD_DIM, ML_HEAD_DIM), f32),
        jax.ShapeDtypeStruct((b, ML_HEADS, ML_HEAD_DIM), f32),
        jax.ShapeDtypeStruct((b, ML_HEADS, V7X_LANES), f32),
        jax.ShapeDtypeStruct((b, 1, LRU_WIDTH), f32),
        jax.ShapeDtypeStruct((b, CONV_W - 1, LRU_WIDTH), f32),
    ]
    st_spec = lambda s: pl.BlockSpec((1,) + s[1:], lambda i, j: (i,) + (0,) * (len(s) - 1))
    return pl.pallas_call(
        _pmix_kernel,
        out_shape=outs,
        grid=(b, nc),
        in_specs=[
            pl.BlockSpec((1, CHUNK, NP_COLS), lambda i, j: (i, j, 0)),
            _full((CONV_W, SSD_CONV_DIM)), row(SSD_CONV_DIM), row(V7X_LANES), row(V7X_LANES),
            row(SSD_WIDTH), row(SSD_WIDTH), row(ML_WIDTH),
            _full((CONV_W, LRU_WIDTH)), row(LRU_WIDTH),
            _full((LRU_BLOCKS, LRU_BLOCK_DIM, LRU_BLOCK_DIM)),
            _full((LRU_BLOCKS, LRU_BLOCK_DIM, LRU_BLOCK_DIM)),
            row(LRU_WIDTH), row(LRU_WIDTH), row(LRU_WIDTH),
        ],
        out_specs=[pl.BlockSpec((1, CHUNK, MIX_WIDTH), lambda i, j: (i, j, 0))]
        + [st_spec(o.shape) for o in outs[1:]],
        scratch_shapes=[
            pltpu.VMEM((CHUNK + V7X_SUBLANES, SSD_CONV_DIM), f32),
            pltpu.VMEM((CHUNK + V7X_SUBLANES, LRU_WIDTH), f32),
            pltpu.VMEM((SSD_GROUPS, SSD_STATE, 512), f32),
            pltpu.VMEM((ML_HEADS, ML_HEAD_DIM, ML_HEAD_DIM), f32),
            pltpu.VMEM((ML_HEADS, 1, ML_HEAD_DIM), f32),
            pltpu.VMEM((ML_HEADS, 1, V7X_LANES), f32),
            pltpu.VMEM((1, LRU_WIDTH), f32),
        ],
        compiler_params=pltpu.CompilerParams(
            dimension_semantics=("arbitrary", "arbitrary"), vmem_limit_bytes=VMEM_LIMIT),
        name="prompt_mix",
    )(u, w["ssd_conv_w"], w["ssd_conv_b"], w["p_small_bias"], w["p_alog"], w["dskip"],
      w["ssd_norm_w"], w["ml_norm_w"], w["lru_conv_w"], w["lru_conv_b"], w["lru_wa"],
      w["lru_wx"], w["lru_ba"], w["lru_bx"], w["lru_lambda"])


def _smix_kernel(u_ref, s0_ref, sconv0_ref, c0_ref, n0_ref, m0_ref, lh0_ref, lconv0_ref,
                 cw_ref, cb_ref, dtb_ref, alog_ref, dskip_ref, snw_ref, ib_ref, fb_ref, mnw_ref,
                 lcw_ref, lcb_ref, wa_ref, wx_ref, ba_ref, bx_ref, lam_ref,
                 y_ref, s_ref, sconv_ref, c_ref, n_ref, m_ref, lh_ref, lconv_ref,
                 xbuf, lbuf):
    bb = u_ref.shape[0]
    T = DEC_SEQ
    HEAD = V7X_SUBLANES

    def bs(a, s):
        return jnp.broadcast_to(a[:, s:s + 1, :], a.shape)

    def tio(n):
        return lax.broadcasted_iota(jnp.int32, (bb, T, n), 1)

    def lio(n):
        return lax.broadcasted_iota(jnp.int32, (bb, T, n), 2)

    def cumsum_t(a):
        t = tio(a.shape[-1])
        acc = jnp.zeros_like(a)
        for s in range(T):
            acc = acc + jnp.where(t >= s, bs(a, s), 0.0)
        return acc

    def conv(buf, raw, hist, w_ref, b_ref, n):
        buf[:, HEAD - 3:HEAD, :] = hist
        buf[:, HEAD:HEAD + T, :] = raw
        acc = b_ref[...] + buf[:, HEAD - 3:HEAD - 3 + T, :] * w_ref[0:1, :]
        for j in range(1, CONV_W):
            acc = acc + buf[:, HEAD - 3 + j:HEAD - 3 + j + T, :] * w_ref[j:j + 1, :]
        return acc, buf[:, HEAD + T - 3:HEAD + T, :]

    xbc, sconv_new = conv(xbuf, u_ref[:, :, P_XBC:P_XBC + SSD_CONV_DIM], sconv0_ref[...],
                          cw_ref, cb_ref, SSD_CONV_DIM)
    sconv_ref[...] = sconv_new
    xbc = _silu(xbc)
    xr, lconv_new = conv(lbuf, u_ref[:, :, P_XR:P_XR + LRU_WIDTH], lconv0_ref[...],
                         lcw_ref, lcb_ref, LRU_WIDTH)
    lconv_ref[...] = lconv_new

    xs = xbc[:, :, 0:SSD_WIDTH]
    t_w = tio(SSD_WIDTH)
    l_w = lio(SSD_WIDTH)
    dt = _softplus(u_ref[:, :, S_DT:S_DT + SSD_WIDTH] + dtb_ref[...])
    cum = cumsum_t(dt * (-jnp.exp(alog_ref[...])))
    xdt = xs * dt
    bms = [xbc[:, :, SSD_WIDTH + g * SSD_STATE:SSD_WIDTH + (g + 1) * SSD_STATE]
           for g in range(SSD_GROUPS)]
    cms = [xbc[:, :, SSD_WIDTH + (SSD_GROUPS + g) * SSD_STATE:
               SSD_WIDTH + (SSD_GROUPS + g + 1) * SSD_STATE] for g in range(SSD_GROUPS)]
    y = xs * dskip_ref[...]
    for s in range(T):
        cb0 = jnp.sum(cms[0] * bs(bms[0], s), axis=-1, keepdims=True)
        cb1 = jnp.sum(cms[1] * bs(bms[1], s), axis=-1, keepdims=True)
        cbs = jnp.where(l_w < 512, cb0, cb1)
        dec = jnp.exp(jnp.where(t_w >= s, cum - bs(cum, s), -jnp.inf))
        y = y + cbs * dec * bs(xdt, s)
    inter = jnp.concatenate(
        [jnp.einsum('btn,bpn->btp', cms[g].astype(bf16),
                    s0_ref[:, g * 512:(g + 1) * 512, :].astype(bf16),
                    preferred_element_type=f32) for g in range(SSD_GROUPS)], axis=-1)
    y = y + inter * jnp.exp(cum)
    z = u_ref[:, :, P_Z:P_Z + SSD_WIDTH]
    y_ref[:, :, 0:SSD_WIDTH] = _rms(y * _silu(z), snw_ref[...])

    def split3(d, t_idx):
        hi = d.astype(bf16).astype(f32)
        r1 = d - hi
        mid = r1.astype(bf16).astype(f32)
        lo_ = (r1 - mid).astype(bf16).astype(f32)
        return jnp.where(t_idx == 0, hi, jnp.where(t_idx == 1, mid, jnp.where(t_idx == 2, lo_, 0.0)))

    def ones_rhs(kmat):
        zeros = jnp.zeros_like(kmat)
        return jnp.concatenate(
            [jnp.concatenate([kmat, zeros], axis=2),
             jnp.concatenate([zeros, jnp.ones_like(kmat)], axis=2)], axis=1).astype(bf16)

    c_last = bs(cum, T - 1)
    xend = xdt * jnp.exp(c_last - cum)
    lhs = jnp.concatenate([xend, split3(jnp.exp(c_last), t_w)], axis=1).astype(bf16)
    for g in range(SSD_GROUPS):
        zz = jnp.einsum('bkp,bkn->bpn', lhs[:, :, g * 512:(g + 1) * 512],
                        ones_rhs(bms[g]), preferred_element_type=f32)
        s_ref[:, g * 512:(g + 1) * 512, :] = (
            zz[:, :, SSD_STATE:] * s0_ref[:, g * 512:(g + 1) * 512, :] + zz[:, :, :SSD_STATE])

    t_m = tio(ML_WIDTH)
    q = u_ref[:, :, P_Q:P_Q + ML_WIDTH]
    k = u_ref[:, :, P_K:P_K + ML_WIDTH] * (ML_HEAD_DIM ** -0.5)
    v = u_ref[:, :, P_V:P_V + ML_WIDTH]
    o = u_ref[:, :, P_O:P_O + ML_WIDTH]
    ic = u_ref[:, :, S_I:S_I + ML_WIDTH] + ib_ref[...]
    fc = _log_sigmoid(u_ref[:, :, S_F:S_F + ML_WIDTH] + fb_ref[...])
    bc = cumsum_t(fc)
    m0 = m0_ref[...]
    g_ = bc + m0

    def headsum(a):
        return jnp.concatenate(
            [jnp.broadcast_to(jnp.sum(a[:, :, h * ML_HEAD_DIM:(h + 1) * ML_HEAD_DIM], axis=-1,
                                      keepdims=True), (bb, T, ML_HEAD_DIM))
             for h in range(ML_HEADS)], axis=-1)

    dms = [jnp.where(t_m >= s, bc - bs(bc, s) + bs(ic, s), -jnp.inf) for s in range(T)]
    mt = g_
    for s in range(T):
        mt = jnp.maximum(mt, dms[s])
    num = jnp.zeros((bb, T, ML_WIDTH), f32)
    den = jnp.zeros((bb, T, ML_WIDTH), f32)
    for s in range(T):
        w = jnp.exp(dms[s] - mt) * headsum(q * bs(k, s))
        num = num + w * bs(v, s)
        den = den + w
    inter_m = jnp.exp(g_ - mt)
    qb = q.astype(bf16)
    qc = jnp.concatenate(
        [jnp.einsum('btk,bvk->btv', qb[:, :, h * ML_HEAD_DIM:(h + 1) * ML_HEAD_DIM],
                    c0_ref[:, h * ML_HEAD_DIM:(h + 1) * ML_HEAD_DIM, :].astype(bf16),
                    preferred_element_type=f32) for h in range(ML_HEADS)], axis=-1)
    n0 = n0_ref[...]
    num = num + inter_m * qc
    den = den + inter_m * headsum(q * n0)
    hout = num / jnp.maximum(jnp.abs(den), jnp.exp(-mt))
    m_new = bs(mt, T - 1)
    b_last = bs(bc, T - 1)
    w_end = jnp.exp(b_last - bc + ic - m_new)
    dc = jnp.exp(b_last + m0 - m_new)
    m_ref[...] = mt[:, T - 1:T, :]
    n_ref[...] = dc[:, 0:1, :] * n0 + jnp.sum(w_end * k, axis=1, keepdims=True)
    lhs_m = jnp.concatenate([v * w_end, split3(dc, t_m)], axis=1).astype(bf16)
    for h in range(ML_HEADS):
        sl = slice(h * ML_HEAD_DIM, (h + 1) * ML_HEAD_DIM)
        zz = jnp.einsum('bkv,bkn->bvn', lhs_m[:, :, sl], ones_rhs(k[:, :, sl]),
                        preferred_element_type=f32)
        c_ref[:, sl, :] = zz[:, :, ML_HEAD_DIM:] * c0_ref[:, sl, :] + zz[:, :, :ML_HEAD_DIM]
    hn2 = headsum(hout * hout) * (1.0 / ML_HEAD_DIM)
    y_ml = hout * lax.rsqrt(hn2 + EPS) * mnw_ref[...] * _sigmoid(o)
    y_ref[:, :, SSD_WIDTH:SSD_WIDTH + ML_WIDTH] = y_ml

    xr2 = xr.reshape(bb * T, LRU_WIDTH)
    rs, is_ = [], []
    for kk in range(LRU_BLOCKS):
        xk = xr2[:, kk * LRU_BLOCK_DIM:(kk + 1) * LRU_BLOCK_DIM].astype(bf16)
        rs.append(jnp.dot(xk, wa_ref[kk], preferred_element_type=f32))
        is_.append(jnp.dot(xk, wx_ref[kk], preferred_element_type=f32))
    r = _sigmoid(jnp.concatenate(rs, axis=1).reshape(bb, T, LRU_WIDTH) + ba_ref[...])
    i_g = _sigmoid(jnp.concatenate(is_, axis=1).reshape(bb, T, LRU_WIDTH) + bx_ref[...])
    log_a = -LRU_C * r * _softplus(-lam_ref[...])
    a_s = jnp.exp(log_a)
    u_s = jnp.sqrt(_neg_expm1(2.0 * log_a)) * i_g * xr
    t_l = tio(LRU_WIDTH)
    cur = lh0_ref[...]
    hr = jnp.zeros((bb, T, LRU_WIDTH), f32)
    for s in range(T):
        cur = a_s[:, s:s + 1, :] * cur + u_s[:, s:s + 1, :]
        hr = jnp.where(t_l == s, jnp.broadcast_to(cur, hr.shape), hr)
    lh_ref[...] = cur
    gr = u_ref[:, :, P_GR:P_GR + LRU_WIDTH]
    y_ref[:, :, SSD_WIDTH + ML_WIDTH:MIX_WIDTH] = hr * _gelu_tanh(gr)


def _smix(u, states, w, bb):
    b = u.shape[0]
    s0, sconv0, c0, n0, m0, lh0, lconv0 = states
    row = lambda n: _full((1, n))
    blk = lambda s: pl.BlockSpec((bb,) + s[1:], lambda i: (i,) + (0,) * (len(s) - 1))
    outs = [
        jax.ShapeDtypeStruct((b, DEC_SEQ, MIX_WIDTH), f32),
        jax.ShapeDtypeStruct(s0.shape, f32),
        jax.ShapeDtypeStruct(sconv0.shape, f32),
        jax.ShapeDtypeStruct(c0.shape, f32),
        jax.ShapeDtypeStruct(n0.shape, f32),
        jax.ShapeDtypeStruct(m0.shape, f32),
        jax.ShapeDtypeStruct(lh0.shape, f32),
        jax.ShapeDtypeStruct(lconv0.shape, f32),
    ]
    return pl.pallas_call(
        _smix_kernel,
        out_shape=outs,
        grid=(b // bb,),
        in_specs=[blk(u.shape)] + [blk(s.shape) for s in states] + [
            _full((CONV_W, SSD_CONV_DIM)), row(SSD_CONV_DIM), row(SSD_WIDTH), row(SSD_WIDTH),
            row(SSD_WIDTH), row(SSD_WIDTH), row(ML_WIDTH), row(ML_WIDTH), row(ML_WIDTH),
            _full((CONV_W, LRU_WIDTH)), row(LRU_WIDTH),
            _full((LRU_BLOCKS, LRU_BLOCK_DIM, LRU_BLOCK_DIM)),
            _full((LRU_BLOCKS, LRU_BLOCK_DIM, LRU_BLOCK_DIM)),
            row(LRU_WIDTH), row(LRU_WIDTH), row(LRU_WIDTH),
        ],
        out_specs=[blk(o.shape) for o in outs],
        scratch_shapes=[
            pltpu.VMEM((bb, 2 * V7X_SUBLANES, SSD_CONV_DIM), f32),
            pltpu.VMEM((bb, 2 * V7X_SUBLANES, LRU_WIDTH), f32),
        ],
        compiler_params=pltpu.CompilerParams(
            dimension_semantics=("arbitrary",), vmem_limit_bytes=VMEM_LIMIT),
        name="sample_mix",
    )(u, s0, sconv0, c0, n0, m0, lh0, lconv0,
      w["ssd_conv_w"], w["ssd_conv_b"], w["s_dt_bias"], w["s_alog"], w["dskip"],
      w["ssd_norm_w"], w["s_i_bias"], w["s_f_bias"], w["ml_norm_w"],
      w["lru_conv_w"], w["lru_conv_b"], w["lru_wa"], w["lru_wx"], w["lru_ba"], w["lru_bx"],
      w["lru_lambda"])


def _prep_layer(l, p):
    w_in = p["w_in"][l]
    main = jnp.concatenate([w_in[:, 0:_O_DT], w_in[:, _O_Q:_O_O_END], w_in[:, _O_XR:_O_END]], axis=1)
    dt_c = w_in[:, _O_DT:_O_Q]
    i_c = w_in[:, _O_I:_O_F]
    f_c = w_in[:, _O_F:_O_XR]
    pad = jnp.zeros((D_MODEL, V7X_LANES - SSD_HEADS - 2 * ML_HEADS), f32)
    w_p = jnp.concatenate([main, dt_c, i_c, f_c, pad], axis=1).astype(bf16)
    w_s = jnp.concatenate([main, jnp.repeat(dt_c, SSD_HEAD_DIM, axis=1),
                           jnp.repeat(i_c, ML_HEAD_DIM, axis=1),
                           jnp.repeat(f_c, ML_HEAD_DIM, axis=1)], axis=1).astype(bf16)
    zpad = jnp.zeros((V7X_LANES - SSD_HEADS - 2 * ML_HEADS,), f32)
    r = lambda a: a.reshape(1, -1)
    return dict(
        w_in_p=w_p, w_in_s=w_s,
        norm1_w=r(p["norm1_w"][l]), norm2_w=r(p["norm2_w"][l]),
        ssd_conv_w=p["ssd_conv_w"][l], ssd_conv_b=r(p["ssd_conv_b"][l]),
        p_small_bias=r(jnp.concatenate([p["ssd_dt_bias"][l], p["ml_i_bias"][l],
                                        p["ml_f_bias"][l], zpad])),
        p_alog=r(jnp.concatenate([p["ssd_a_log"][l], jnp.zeros((V7X_LANES - SSD_HEADS,), f32)])),
        s_dt_bias=r(jnp.repeat(p["ssd_dt_bias"][l], SSD_HEAD_DIM)),
        s_alog=r(jnp.repeat(p["ssd_a_log"][l], SSD_HEAD_DIM)),
        s_i_bias=r(jnp.repeat(p["ml_i_bias"][l], ML_HEAD_DIM)),
        s_f_bias=r(jnp.repeat(p["ml_f_bias"][l], ML_HEAD_DIM)),
        dskip=r(jnp.repeat(p["ssd_d"][l], SSD_HEAD_DIM)),
        ssd_norm_w=r(p["ssd_norm_w"][l]), ml_norm_w=r(p["ml_norm_w"][l]),
        lru_conv_w=p["lru_conv_w"][l], lru_conv_b=r(p["lru_conv_b"][l]),
        lru_wa=p["lru_wa"][l].astype(bf16), lru_wx=p["lru_wx"][l].astype(bf16),
        lru_ba=r(p["lru_ba"][l]), lru_bx=r(p["lru_bx"][l]), lru_lambda=r(p["lru_lambda"][l]),
        w_out=p["w_out"][l].astype(bf16), mlp_up=p["mlp_up"][l].astype(bf16),
        mlp_down=p["mlp_down"][l].astype(bf16),
    )


def kernel(x_prompt, x_sample, c_prompt, c_sample, state_ssm, state_ssd_conv, state_mlstm_c, state_mlstm_n, state_mlstm_m, state_lru_h, state_lru_conv, ada_w, ada_b, norm1_w, norm2_w, w_in, ssd_conv_w, ssd_conv_b, ssd_dt_bias, ssd_a_log, ssd_d, ssd_norm_w, ml_i_bias, ml_f_bias, ml_norm_w, lru_conv_w, lru_conv_b, lru_wa, lru_ba, lru_wx, lru_bx, lru_lambda, w_out, mlp_up, mlp_down, final_norm_w):
    p = dict(norm1_w=norm1_w, norm2_w=norm2_w, w_in=w_in, ssd_conv_w=ssd_conv_w,
             ssd_conv_b=ssd_conv_b, ssd_dt_bias=ssd_dt_bias, ssd_a_log=ssd_a_log, ssd_d=ssd_d,
             ssd_norm_w=ssd_norm_w, ml_i_bias=ml_i_bias, ml_f_bias=ml_f_bias,
             ml_norm_w=ml_norm_w, lru_conv_w=lru_conv_w, lru_conv_b=lru_conv_b, lru_wa=lru_wa,
             lru_ba=lru_ba, lru_wx=lru_wx, lru_bx=lru_bx, lru_lambda=lru_lambda, w_out=w_out,
             mlp_up=mlp_up, mlp_down=mlp_down)
    layers = [_prep_layer(l, p) for l in range(N_LAYERS)]
    fnw = final_norm_w.reshape(1, D_MODEL)
    bp = x_prompt.shape[0]
    bs_ = x_sample.shape[0]

    mod = _ada(jnp.concatenate([c_prompt, c_sample], axis=0), ada_w, ada_b)
    mod = mod.reshape(N_LAYERS, bp + bs_, 6, D_MODEL)

    xp = x_prompt
    p_states = []
    for l, w in enumerate(layers):
        mod_l = mod[l, :bp]
        u = _inproj(xp, mod_l, w["norm1_w"], w["w_in_p"], 1, 256)
        ycat, ssm, sconv, mc, mn, mm, lh, lconv = _pmix(u, w)
        xp = _outmlp(xp, ycat, mod_l, w["w_out"], w["norm2_w"], w["mlp_up"], w["mlp_down"], fnw,
                     1, 512, l == N_LAYERS - 1)
        p_states.append((ssm.reshape(bp, SSD_HEADS, SSD_HEAD_DIM, SSD_STATE), sconv, mc, mn,
                         mm[:, :, 0], lh.reshape(bp, LRU_WIDTH), lconv))

    xs = x_sample
    s_states = []
    for l, w in enumerate(layers):
        mod_l = mod[l, bp:]
        u = _inproj(xs, mod_l, w["norm1_w"], w["w_in_s"], 32, DEC_SEQ)
        st_in = (
            state_ssm[l].reshape(bs_, SSD_HEADS * SSD_HEAD_DIM, SSD_STATE),
            state_ssd_conv[l],
            state_mlstm_c[l].reshape(bs_, ML_HEADS * ML_HEAD_DIM, ML_HEAD_DIM),
            state_mlstm_n[l].reshape(bs_, 1, ML_WIDTH),
            jnp.repeat(state_mlstm_m[l], ML_HEAD_DIM, axis=-1).reshape(bs_, 1, ML_WIDTH),
            state_lru_h[l].reshape(bs_, 1, LRU_WIDTH),
            state_lru_conv[l],
        )
        ycat, ssm, sconv, mc, mn, mm, lh, lconv = _smix(u, st_in, w, 8)
        xs = _outmlp(xs, ycat, mod_l, w["w_out"], w["norm2_w"], w["mlp_up"], w["mlp_down"], fnw,
                     64, DEC_SEQ, l == N_LAYERS - 1)
        s_states.append((ssm.reshape(bs_, SSD_HEADS, SSD_HEAD_DIM, SSD_STATE), sconv,
                         mc.reshape(bs_, ML_HEADS, ML_HEAD_DIM, ML_HEAD_DIM),
                         mn.reshape(bs_, ML_HEADS, ML_HEAD_DIM),
                         mm.reshape(bs_, ML_HEADS, ML_HEAD_DIM)[:, :, 0],
                         lh.reshape(bs_, LRU_WIDTH), lconv))

    stack = lambda sts: tuple(jnp.stack([s[i] for s in sts]) for i in range(7))
    return (xp, xs) + stack(p_states) + stack(s_states)
```

```python
import functools
import math

import jax
import jax.numpy as jnp
from jax import lax
from jax.experimental import pallas as pl
from jax.experimental.pallas import tpu as pltpu

f32 = jnp.float32
bf16 = jnp.bfloat16

D_MODEL = 1024
N_LAYERS = 2
MIX_WIDTH = 2 * D_MODEL
SSD_WIDTH = 1024
SSD_HEAD_DIM = 64
SSD_HEADS = 16
SSD_GROUPS = 2
SSD_STATE = 128
SSD_CONV_DIM = SSD_WIDTH + 2 * SSD_GROUPS * SSD_STATE
ML_WIDTH = 512
ML_HEADS = 4
ML_HEAD_DIM = 128
LRU_WIDTH = 512
LRU_BLOCKS = 4
LRU_BLOCK_DIM = 128
LRU_C = 8.0
CONV_W = 4
CHUNK = 128
D_FF = 4 * D_MODEL
EPS = 1e-6
PAST_LEN = 16384
DEC_SEQ = 8

_O_Z, _O_XBC, _O_DT, _O_Q, _O_O_END, _O_I, _O_F, _O_XR, _O_END = (
    0, 1024, 2560, 2576, 4624, 4624, 4628, 4632, 5656)

P_Z = 0
P_XBC = 1024
P_Q = 2560
P_K = 3072
P_V = 3584
P_O = 4096
P_XR = 4608
P_GR = 5120
P_MAIN = 5632
P_SMALL = 5632
NP_COLS = 5760
S_DT = 5632
S_I = 6656
S_F = 7168
NS_COLS = 7680

V7X_LANES = 128
V7X_SUBLANES = 8
VMEM_LIMIT = 56 * 1024 * 1024
ROW_TILES = CHUNK // V7X_SUBLANES
LOG2E = 1.4426950408889634

_NT = (((1,), (1,)), ((), ()))
_TN = (((0,), (0,)), ((), ()))


def _softplus(x):
    return jnp.maximum(x, 0.0) + jnp.log1p(jnp.exp(-jnp.abs(x)))


def _log_sigmoid(x):
    return -_softplus(-x)


def _sigmoid(x):
    return 1.0 / (1.0 + jnp.exp(-x))


def _silu(x):
    return x * _sigmoid(x)


def _gelu_tanh(x):
    c = math.sqrt(2.0 / math.pi)
    return 0.5 * x * (1.0 + jnp.tanh(c * (x + 0.044715 * (x * x * x))))


def _neg_expm1(x):
    return -jnp.tanh(0.5 * x) * (jnp.exp(x) + 1.0)


def _rms(x, w):
    return x * lax.rsqrt(jnp.mean(x * x, axis=-1, keepdims=True) + EPS) * w


def _interleave_rows(x):
    b, t, d = x.shape
    return x.reshape(b, t // CHUNK, V7X_SUBLANES, ROW_TILES, d).swapaxes(2, 3).reshape(b, t, d)


def _deinterleave_rows(x):
    b, t, d = x.shape
    return x.reshape(b, t // CHUNK, ROW_TILES, V7X_SUBLANES, d).swapaxes(2, 3).reshape(b, t, d)


def _full(shape):
    n = len(shape)
    return pl.BlockSpec(shape, lambda *_: (0,) * n)


def _resident(shape):
    n = len(shape)
    return pl.BlockSpec(shape, lambda *_: (0,) * n, pipeline_mode=pl.Buffered(1))


def _ada_kernel(c_ref, w_ref, b_ref, o_ref):
    cs = _silu(c_ref[...]).astype(bf16)
    o_ref[0] = jnp.dot(cs, w_ref[0].astype(bf16), preferred_element_type=f32) + b_ref[0]


def _ada(c_all, ada_w, ada_b):
    n = c_all.shape[0]
    tn = 1024
    return pl.pallas_call(
        _ada_kernel,
        out_shape=jax.ShapeDtypeStruct((N_LAYERS, n, 6 * D_MODEL), f32),
        grid=(N_LAYERS, 6 * D_MODEL // tn),
        in_specs=[
            pl.BlockSpec((n, D_MODEL), lambda l, j: (0, 0)),
            pl.BlockSpec((1, D_MODEL, tn), lambda l, j: (l, 0, j)),
            pl.BlockSpec((1, 1, tn), lambda l, j: (l, 0, j)),
        ],
        out_specs=pl.BlockSpec((1, n, tn), lambda l, j: (l, 0, j)),
        compiler_params=pltpu.CompilerParams(
            dimension_semantics=("arbitrary", "arbitrary"), vmem_limit_bytes=VMEM_LIMIT),
        name="ada_mod",
    )(c_all, ada_w, ada_b.reshape(N_LAYERS, 1, 6 * D_MODEL))


def _inproj_kernel(x_ref, mod_ref, nw_ref, w_ref, o_ref):
    x = x_ref[...]
    bb, tt, d = x.shape
    hn = _rms(x, nw_ref[...]) * (1.0 + mod_ref[:, 1:2, :]) + mod_ref[:, 0:1, :]
    hn = hn.reshape(bb * tt, d).astype(bf16)
    u = jnp.dot(hn, w_ref[...], preferred_element_type=f32)
    o_ref[...] = u.reshape(bb, tt, u.shape[-1])


def _inproj(x, mod, nw, w, bb, tt):
    b, t, d = x.shape
    n = w.shape[1]
    return pl.pallas_call(
        _inproj_kernel,
        out_shape=jax.ShapeDtypeStruct((b, t, n), f32),
        grid=(b // bb, t // tt),
        in_specs=[
            pl.BlockSpec((bb, tt, d), lambda i, j: (i, j, 0)),
            pl.BlockSpec((bb, 6, d), lambda i, j: (i, 0, 0)),
            _full((1, d)),
            _resident((d, n)),
        ],
        out_specs=pl.BlockSpec((bb, tt, n), lambda i, j: (i, j, 0)),
        compiler_params=pltpu.CompilerParams(
            dimension_semantics=("arbitrary", "arbitrary"), vmem_limit_bytes=VMEM_LIMIT),
        name="inproj",
    )(x, mod, nw, w)


def _outmlp_kernel(x_ref, y_ref, mod_ref, wout_ref, n2w_ref, up_ref, down_ref, fnw_ref, o_ref,
                   *, final):
    x = x_ref[...]
    bb, tt, d = x.shape
    m = bb * tt
    ycat = y_ref[...].reshape(m, MIX_WIDTH).astype(bf16)
    mix =jnp.dot(ycat, wout_ref[...], preferred_element_type=f32).reshape(bb, tt, d)
    x1 = x + mod_ref[:, 2:3, :] * mix
    hn = _rms(x1, n2w_ref[...]) * (1.0 + mod_ref[:, 4:5, :]) + mod_ref[:, 3:4, :]
    hn = hn.reshape(m, d).astype(bf16)
    ff = jnp.zeros((m, d), f32)
    fc = 1024
    for c in range(D_FF // fc):
        h = jnp.dot(hn, up_ref[:, c * fc:(c + 1) * fc], preferred_element_type=f32)
        h = jnp.square(jnp.maximum(h, 0.0)).astype(bf16)
        ff = ff + jnp.dot(h, down_ref[c * fc:(c + 1) * fc, :], preferred_element_type=f32)
    x2 = x1 + mod_ref[:, 5:6, :] * ff.reshape(bb, tt, d)
    if final:
        x2 = _rms(x2, fnw_ref[...])
    o_ref[...] = x2


def _outmlp(x, ycat, mod, wout, n2w, up, down, fnw, bb, tt, final):
    b, t, d = x.shape
    return pl.pallas_call(
        functools.partial(_outmlp_kernel, final=final),
        out_shape=jax.ShapeDtypeStruct((b, t, d), f32),
        grid=(b // bb, t // tt),
        in_specs=[
            pl.BlockSpec((bb, tt, d), lambda i, j: (i, j, 0)),
            pl.BlockSpec((bb, tt, MIX_WIDTH), lambda i, j: (i, j, 0)),
            pl.BlockSpec((bb, 6, d), lambda i, j: (i, 0, 0)),
            _resident((MIX_WIDTH, d)),
            _full((1, d)),
            _resident((d, D_FF)),
            _resident((D_FF, d)),
            _full((1, d)),
        ],
        out_specs=pl.BlockSpec((bb, tt, d), lambda i, j: (i, j, 0)),
        compiler_params=pltpu.CompilerParams(
            dimension_semantics=("arbitrary", "arbitrary"), vmem_limit_bytes=VMEM_LIMIT),
        name="outproj_mlp",
    )(x, ycat, mod, wout, n2w, up, down, fnw)


def _pmix_kernel(u_ref, cw_ref, cb_ref, sbias_ref, alog_ref, dskip_ref, snw_ref, mnw_ref,
                 lcw_ref, lcb_ref, wa_ref, wx_ref, ba_ref, bx_ref, lam_ref,
                 y_ref, ssm_ref, sconv_ref, mc_ref, mn_ref, mm_ref, lh_ref, lconv_ref,
                 xbuf, lbuf, st, cst, nst, mst, hst):
    ci = pl.program_id(1)
    nc = pl.num_programs(1)
    L = CHUNK
    SUB = V7X_SUBLANES
    HIST = (CONV_W - 1) * SUB

    @pl.when(ci == 0)
    def _():
        xbuf[...] = jnp.zeros(xbuf.shape, f32)
        lbuf[...] = jnp.zeros(lbuf.shape, f32)
        st[...] = jnp.zeros(st.shape, f32)
        cst[...] = jnp.zeros(cst.shape, f32)
        nst[...] = jnp.zeros(nst.shape, f32)
        mst[...] = jnp.zeros(mst.shape, f32)
        hst[...] = jnp.zeros(hst.shape, f32)

    row = lax.broadcasted_iota(jnp.int32, (L, L), 0)
    col = lax.broadcasted_iota(jnp.int32, (L, L), 1)
    tok = lambda i: (i & (SUB - 1)) * ROW_TILES + (i >> 3)
    tri = tok(row) >= tok(col)
    lo = col < SSD_HEAD_DIM

    def bcl(a, j):
        return jnp.broadcast_to(a[:, j:j + 1], (L, L))

    def conv(raw, hist_ref, w_ref, b_ref):
        n = raw.shape[1]
        sub = lax.broadcasted_iota(jnp.int32, (SUB, n), 0)
        prev = [pltpu.roll(jnp.where(sub == SUB - 1, hist_ref[k * SUB:(k + 1) * SUB, :],
                                     raw[L - HIST + k * SUB:L - HIST + (k + 1) * SUB, :]), 1, 0)
                for k in range(CONV_W - 1)]
        acc = b_ref[...] + raw * w_ref[CONV_W - 1:CONV_W, :]
        for d in range(1, CONV_W):
            shifted = jnp.concatenate(prev[CONV_W - 1 - d:] + [raw[0:L - d * SUB, :]], axis=0)
            acc = acc + shifted * w_ref[CONV_W - 1 - d:CONV_W - d, :]
        hist_ref[...] = raw[L - HIST:L, :]
        return acc

    xbc = _silu(conv(u_ref[0, :, P_XBC:P_XBC + SSD_CONV_DIM], xbuf, cw_ref, cb_ref))
    xr = conv(u_ref[0, :, P_XR:P_XR + LRU_WIDTH], lbuf, lcw_ref, lcb_ref)

    sm = u_ref[0, :, P_SMALL:P_SMALL + V7X_LANES] + sbias_ref[...]
    dt = _softplus(sm)
    a_row = -jnp.exp(alog_ref[...])
    gates = jnp.where(col < SSD_HEADS, dt * (a_row * LOG2E),
                      jnp.where((col >= 20) & (col < 24), _log_sigmoid(sm), 0.0))
    cum = jnp.dot(tri.astype(f32), gates, precision=lax.Precision.HIGHEST,
                  preferred_element_type=f32)
    cum_t = cum.T
    sm_t = sm.T

    ys = []
    for g in range(SSD_GROUPS):
        bm_t = xbc[:, SSD_WIDTH + g * SSD_STATE:SSD_WIDTH + (g + 1) * SSD_STATE].T.astype(bf16)
        cm = xbc[:, SSD_WIDTH + (SSD_GROUPS + g) * SSD_STATE:
                 SSD_WIDTH + (SSD_GROUPS + g + 1) * SSD_STATE].astype(bf16)
        cb = jnp.dot(cm, bm_t, preferred_element_type=f32)
        inter = jnp.dot(cm, st[g].astype(bf16), preferred_element_type=f32)
        for jj in range(4):
            jp = g * 4 + jj
            e0, e1 = 2 * jp, 2 * jp + 1
            c0 = bcl(cum, e0)
            c1 = bcl(cum, e1)
            cum_p = jnp.where(lo, c0, c1)
            dt_p = jnp.where(lo, bcl(dt, e0), bcl(dt, e1))
            xs_p = xbc[:, jp * L:(jp + 1) * L]
            xdt = xs_p * dt_p
            dec0 = jnp.exp2(jnp.where(tri, c0 - cum_t[e0:e0 + 1, :], -jnp.inf))
            dec1 = jnp.exp2(jnp.where(tri, c1 - cum_t[e1:e1 + 1, :], -jnp.inf))
            att0 = (cb * dec0).astype(bf16)
            att1 = (cb * dec1).astype(bf16)
            x_lo = jnp.where(lo, xdt, 0.0).astype(bf16)
            x_hi = jnp.where(lo, 0.0, xdt).astype(bf16)
            y_p = (jnp.dot(att0, x_lo, preferred_element_type=f32)
                   + jnp.dot(att1, x_hi, preferred_element_type=f32)
                   + inter[:, jj * L:(jj + 1) * L] * jnp.exp2(cum_p)
                   + xs_p * dskip_ref[:, jp * L:(jp + 1) * L])
            ys.append(y_p)
            c_last = cum_p[L - 1:L, :]
            xend = (xdt * jnp.exp2(c_last - cum_p)).astype(bf16)
            st[g, :, jj * L:(jj + 1) * L] = (
                jnp.exp2(c_last) * st[g, :, jj * L:(jj + 1) * L]
                + jnp.dot(bm_t, xend, preferred_element_type=f32))
    y = jnp.concatenate(ys, axis=1)
    z = u_ref[0, :, P_Z:P_Z + SSD_WIDTH]
    y_ref[0, :, 0:SSD_WIDTH] = _rms(y * _silu(z), snw_ref[...]).astype(bf16)

    scale = ML_HEAD_DIM ** -0.5
    for h in range(ML_HEADS):
        sl = slice(h * ML_HEAD_DIM, (h + 1) * ML_HEAD_DIM)
        q = u_ref[0, :, P_Q + h * ML_HEAD_DIM:P_Q + (h + 1) * ML_HEAD_DIM]
        k = u_ref[0, :, P_K + h * ML_HEAD_DIM:P_K + (h + 1) * ML_HEAD_DIM] * scale
        v = u_ref[0, :, P_V + h * ML_HEAD_DIM:P_V + (h + 1) * ML_HEAD_DIM]
        o = u_ref[0, :, P_O + h * ML_HEAD_DIM:P_O + (h + 1) * ML_HEAD_DIM]
        qb = q.astype(bf16)
        kb = k.astype(bf16)
        jf = 20 + h
        ji = 16 + h
        b_col = cum[:, jf:jf + 1]
        i_col = sm[:, ji:ji + 1]
        dm = jnp.where(tri, bcl(cum, jf) - cum_t[jf:jf + 1, :] + sm_t[ji:ji + 1, :], -jnp.inf)
        m_prev = mst[h, :, 0:1]
        g_col = b_col + m_prev
        mt = jnp.maximum(g_col, jnp.max(dm, axis=-1, keepdims=True))
        w = jnp.exp(dm - mt) * lax.dot_general(qb, kb, _NT, preferred_element_type=f32)
        inter = jnp.exp(g_col - mt)
        c_old = cst[h]
        n_old = nst[h]
        num = (jnp.dot(w.astype(bf16), v.astype(bf16), preferred_element_type=f32)
               + inter * lax.dot_general(qb, c_old.astype(bf16), _NT, preferred_element_type=f32))
        den = (jnp.sum(w, axis=-1, keepdims=True)
               + inter * jnp.sum(q * n_old, axis=-1, keepdims=True))
        hout = num / jnp.maximum(jnp.abs(den), jnp.exp(-mt))
        m_new = mt[L - 1:L, :]
        b_last = b_col[L - 1:L, :]
        w_end = jnp.exp(b_last - b_col + i_col - m_new)
        dc = jnp.exp(b_last + m_prev - m_new)
        cst[h] = dc * c_old + jnp.dot((v * w_end).T.astype(bf16), kb,
                                      preferred_element_type=f32)
        nst[h] = dc * n_old + jnp.sum(w_end * k, axis=0, keepdims=True)
        mst[h] = jnp.broadcast_to(m_new, (1, V7X_LANES))
        yh = _rms(hout, mnw_ref[:, sl]) * _sigmoid(o)
        y_ref[0, :, SSD_WIDTH + h * ML_HEAD_DIM:SSD_WIDTH + (h + 1) * ML_HEAD_DIM] = yh.astype(bf16)

    rs, is_ = [], []
    for kb_ in range(LRU_BLOCKS):
        xk = xr[:, kb_ * LRU_BLOCK_DIM:(kb_ + 1) * LRU_BLOCK_DIM].astype(bf16)
        rs.append(jnp.dot(xk, wa_ref[kb_], preferred_element_type=f32))
        is_.append(jnp.dot(xk, wx_ref[kb_], preferred_element_type=f32))
    r = _sigmoid(jnp.concatenate(rs, axis=1) + ba_ref[...])
    i_g = _sigmoid(jnp.concatenate(is_, axis=1) + bx_ref[...])
    log_a = -LRU_C * r * _softplus(-lam_ref[...])
    mult = jnp.sqrt(_neg_expm1(2.0 * log_a))
    rowl = lax.broadcasted_iota(jnp.int32, (L, LRU_WIDTH), 0)
    first_row = jnp.where(ci == 0, 0, -1)
    mult = jnp.where(rowl == first_row, 1.0, mult)
    a_s = jnp.exp(log_a)
    u_s = mult * i_g * xr
    tile = lambda a, v: a[v * SUB:(v + 1) * SUB, :]
    h_loc = [tile(u_s, 0)]
    a_cum = [tile(a_s, 0)]
    for v in range(1, ROW_TILES):
        h_loc.append(tile(a_s, v) * h_loc[-1] + tile(u_s, v))
        a_cum.append(tile(a_s, v) * a_cum[-1])
    a_e, h_e = a_cum[-1], h_loc[-1]
    subl = lax.broadcasted_iota(jnp.int32, (SUB, LRU_WIDTH), 0)
    k_ = 1
    while k_ < SUB:
        keep = subl >= k_
        a_sh = jnp.where(keep, pltpu.roll(a_e, k_, 0), 1.0)
        h_sh = jnp.where(keep, pltpu.roll(h_e, k_, 0), 0.0)
        h_e = a_e * h_sh + h_e
        a_e = a_e * a_sh
        k_ *= 2
    h0 = hst[...]
    run_end = h_e + a_e * h0
    run_in = jnp.where(subl == 0, h0, pltpu.roll(run_end, 1, 0))
    hr = jnp.concatenate([h_loc[v] + a_cum[v] * run_in for v in range(ROW_TILES)], axis=0)
    hst[...] = run_end[SUB - 1:SUB, :]
    gr = u_ref[0, :, P_GR:P_GR + LRU_WIDTH]
    y_ref[0, :, SSD_WIDTH + ML_WIDTH:MIX_WIDTH] = (hr * _gelu_tanh(gr)).astype(bf16)

    @pl.when(ci == nc - 1)
    def _():
        for g in range(SSD_GROUPS):
            for jj in range(4):
                ssm_ref[0, g * 4 + jj] = st[g, :, jj * L:(jj + 1) * L].T
        for k in range(CONV_W - 1):
            sconv_ref[0, k:k + 1, :] = xbuf[k * SUB + SUB - 1:(k + 1) * SUB, :]
            lconv_ref[0, k:k + 1, :] = lbuf[k * SUB + SUB - 1:(k + 1) * SUB, :]
        for h in range(ML_HEADS):
            mc_ref[0, h] = cst[h]
            mn_ref[0, h:h + 1, :] = nst[h]
            mm_ref[0, h:h + 1, :] = mst[h]
        lh_ref[0] = hst[...]


def _pmix(u, w):
    b, t, _ = u.shape
    nc = t // CHUNK
    row = lambda n: _full((1, n))
    outs = [
        jax.ShapeDtypeStruct((b, t, MIX_WIDTH), bf16),
        jax.ShapeDtypeStruct((b, 8, 128, 128), f32),
        jax.ShapeDtypeStruct((b, CONV_W - 1, SSD_CONV_DIM), f32),
        jax.ShapeDtypeStruct((b, ML_HEADS, ML_HEAD_DIM, ML_HEAD_DIM), f32),
        jax.ShapeDtypeStruct((b, ML_HEADS, ML_HEAD_DIM), f32),
        jax.ShapeDtypeStruct((b, ML_HEADS, V7X_LANES), f32),
        jax.ShapeDtypeStruct((b, 1, LRU_WIDTH), f32),
        jax.ShapeDtypeStruct((b, CONV_W - 1, LRU_WIDTH), f32),
    ]
    st_spec = lambda s: pl.BlockSpec((1,) + s[1:], lambda i, j: (i,) + (0,) * (len(s) - 1))
    return pl.pallas_call(
        _pmix_kernel,
        out_shape=outs,
        grid=(b, nc),
        in_specs=[
            pl.BlockSpec((1, CHUNK, NP_COLS), lambda i, j: (i, j, 0)),
            _full((CONV_W, SSD_CONV_DIM)), row(SSD_CONV_DIM), row(V7X_LANES), row(V7X_LANES),
            row(SSD_WIDTH), row(SSD_WIDTH), row(ML_WIDTH),
            _full((CONV_W, LRU_WIDTH)), row(LRU_WIDTH),
            _full((LRU_BLOCKS, LRU_BLOCK_DIM, LRU_BLOCK_DIM)),
            _full((LRU_BLOCKS, LRU_BLOCK_DIM, LRU_BLOCK_DIM)),
            row(LRU_WIDTH), row(LRU_WIDTH), row(LRU_WIDTH),
        ],
        out_specs=[pl.BlockSpec((1, CHUNK, MIX_WIDTH), lambda i, j: (i, j, 0))]
        + [st_spec(o.shape) for o in outs[1:]],
        scratch_shapes=[
            pltpu.VMEM(((CONV_W - 1) * V7X_SUBLANES, SSD_CONV_DIM), f32),
            pltpu.VMEM(((CONV_W - 1) * V7X_SUBLANES, LRU_WIDTH), f32),
            pltpu.VMEM((SSD_GROUPS, SSD_STATE, 512), f32),
            pltpu.VMEM((ML_HEADS, ML_HEAD_DIM, ML_HEAD_DIM), f32),
            pltpu.VMEM((ML_HEADS, 1, ML_HEAD_DIM), f32),
            pltpu.VMEM((ML_HEADS, 1, V7X_LANES), f32),
            pltpu.VMEM((1, LRU_WIDTH), f32),
        ],
        compiler_params=pltpu.CompilerParams(
            dimension_semantics=("arbitrary", "arbitrary"), vmem_limit_bytes=VMEM_LIMIT),
        name="prompt_mix",
    )(u, w["ssd_conv_w"], w["ssd_conv_b"], w["p_small_bias"], w["p_alog"], w["dskip"],
      w["ssd_norm_w"], w["ml_norm_w"], w["lru_conv_w"], w["lru_conv_b"], w["lru_wa"],
      w["lru_wx"], w["lru_ba"], w["lru_bx"], w["lru_lambda"])


def _smix_kernel(u_ref, s0_ref, sconv0_ref, c0_ref, n0_ref, m0_ref, lh0_ref, lconv0_ref,
                 cw_ref, cb_ref, dtb_ref, alog_ref, dskip_ref, snw_ref, ib_ref, fb_ref, mnw_ref,
                 lcw_ref, lcb_ref, wa_ref, wx_ref, ba_ref, bx_ref, lam_ref,
                 y_ref, s_ref, sconv_ref, c_ref, n_ref, m_ref, lh_ref, lconv_ref,
                 xbuf, lbuf):
    bb = u_ref.shape[0]
    T = DEC_SEQ
    HEAD = V7X_SUBLANES

    def bs(a, s):
        return jnp.broadcast_to(a[:, s:s + 1, :], a.shape)

    def tio(n):
        return lax.broadcasted_iota(jnp.int32, (bb, T, n), 1)

    def lio(n):
        return lax.broadcasted_iota(jnp.int32, (bb, T, n), 2)

    def cumsum_t(a):
        t = tio(a.shape[-1])
        acc = jnp.zeros_like(a)
        for s in range(T):
            acc = acc + jnp.where(t >= s, bs(a, s), 0.0)
        return acc

    def conv(buf, raw, hist, w_ref, b_ref, n):
        buf[:, HEAD - 3:HEAD, :] = hist
        buf[:, HEAD:HEAD + T, :] = raw
        acc = b_ref[...] + buf[:, HEAD - 3:HEAD - 3 + T, :] * w_ref[0:1, :]
        for j in range(1, CONV_W):
            acc = acc + buf[:, HEAD - 3 + j:HEAD - 3 + j + T, :] * w_ref[j:j + 1, :]
        return acc, buf[:, HEAD + T - 3:HEAD + T, :]

    xbc, sconv_new = conv(xbuf, u_ref[:, :, P_XBC:P_XBC + SSD_CONV_DIM], sconv0_ref[...],
                          cw_ref, cb_ref, SSD_CONV_DIM)
    sconv_ref[...] = sconv_new
    xbc = _silu(xbc)
    xr, lconv_new = conv(lbuf, u_ref[:, :, P_XR:P_XR + LRU_WIDTH], lconv0_ref[...],
                         lcw_ref, lcb_ref, LRU_WIDTH)
    lconv_ref[...] = lconv_new

    xs = xbc[:, :, 0:SSD_WIDTH]
    t_w = tio(SSD_WIDTH)
    l_w = lio(SSD_WIDTH)
    dt = _softplus(u_ref[:, :, S_DT:S_DT + SSD_WIDTH] + dtb_ref[...])
    cum = cumsum_t(dt * (-jnp.exp(alog_ref[...])))
    xdt = xs * dt
    bms = [xbc[:, :, SSD_WIDTH + g * SSD_STATE:SSD_WIDTH + (g + 1) * SSD_STATE]
           for g in range(SSD_GROUPS)]
    cms = [xbc[:, :, SSD_WIDTH + (SSD_GROUPS + g) * SSD_STATE:
               SSD_WIDTH + (SSD_GROUPS + g + 1) * SSD_STATE] for g in range(SSD_GROUPS)]
    y = xs * dskip_ref[...]
    for s in range(T):
        cb0 = jnp.sum(cms[0] * bs(bms[0], s), axis=-1, keepdims=True)
        cb1 = jnp.sum(cms[1] * bs(bms[1], s), axis=-1, keepdims=True)
        cbs = jnp.where(l_w < 512, cb0, cb1)
        dec = jnp.exp(jnp.where(t_w >= s, cum - bs(cum, s), -jnp.inf))
        y = y + cbs * dec * bs(xdt, s)
    inter = jnp.concatenate(
        [jnp.einsum('btn,bpn->btp', cms[g].astype(bf16),
                    s0_ref[:, g * 512:(g + 1) * 512, :].astype(bf16),
                    preferred_element_type=f32) for g in range(SSD_GROUPS)], axis=-1)
    y = y + inter * jnp.exp(cum)
    z = u_ref[:, :, P_Z:P_Z + SSD_WIDTH]
    y_ref[:, :, 0:SSD_WIDTH] = _rms(y * _silu(z), snw_ref[...])

    def split3(d, t_idx):
        hi = d.astype(bf16).astype(f32)
        r1 = d - hi
        mid = r1.astype(bf16).astype(f32)
        lo_ = (r1 - mid).astype(bf16).astype(f32)
        return jnp.where(t_idx == 0, hi, jnp.where(t_idx == 1, mid, jnp.where(t_idx == 2, lo_, 0.0)))

    def ones_rhs(kmat):
        zeros = jnp.zeros_like(kmat)
        return jnp.concatenate(
            [jnp.concatenate([kmat, zeros], axis=2),
             jnp.concatenate([zeros, jnp.ones_like(kmat)], axis=2)], axis=1).astype(bf16)

    c_last = bs(cum, T - 1)
    xend = xdt * jnp.exp(c_last - cum)
    lhs = jnp.concatenate([xend, split3(jnp.exp(c_last), t_w)], axis=1).astype(bf16)
    for g in range(SSD_GROUPS):
        zz = jnp.einsum('bkp,bkn->bpn', lhs[:, :, g * 512:(g + 1) * 512],
                        ones_rhs(bms[g]), preferred_element_type=f32)
        s_ref[:, g * 512:(g + 1) * 512, :] = (
            zz[:, :, SSD_STATE:] * s0_ref[:, g * 512:(g + 1) * 512, :] + zz[:, :, :SSD_STATE])

    t_m = tio(ML_WIDTH)
    q = u_ref[:, :, P_Q:P_Q + ML_WIDTH]
    k = u_ref[:, :, P_K:P_K + ML_WIDTH] * (ML_HEAD_DIM ** -0.5)
    v = u_ref[:, :, P_V:P_V + ML_WIDTH]
    o = u_ref[:, :, P_O:P_O + ML_WIDTH]
    ic = u_ref[:, :, S_I:S_I + ML_WIDTH] + ib_ref[...]
    fc = _log_sigmoid(u_ref[:, :, S_F:S_F + ML_WIDTH] + fb_ref[...])
    bc = cumsum_t(fc)
    m0 = m0_ref[...]
    g_ = bc + m0

    def headsum(a):
        return jnp.concatenate(
            [jnp.broadcast_to(jnp.sum(a[:, :, h * ML_HEAD_DIM:(h + 1) * ML_HEAD_DIM], axis=-1,
                                      keepdims=True), (bb, T, ML_HEAD_DIM))
             for h in range(ML_HEADS)], axis=-1)

    dms = [jnp.where(t_m >= s, bc - bs(bc, s) + bs(ic, s), -jnp.inf) for s in range(T)]
    mt = g_
    for s in range(T):
        mt = jnp.maximum(mt, dms[s])
    num = jnp.zeros((bb, T, ML_WIDTH), f32)
    den = jnp.zeros((bb, T, ML_WIDTH), f32)
    for s in range(T):
        w = jnp.exp(dms[s] - mt) * headsum(q * bs(k, s))
        num = num + w * bs(v, s)
        den = den + w
    inter_m = jnp.exp(g_ - mt)
    qb = q.astype(bf16)
    qc = jnp.concatenate(
        [jnp.einsum('btk,bvk->btv', qb[:, :, h * ML_HEAD_DIM:(h + 1) * ML_HEAD_DIM],
                    c0_ref[:, h * ML_HEAD_DIM:(h + 1) * ML_HEAD_DIM, :].astype(bf16),
                    preferred_element_type=f32) for h in range(ML_HEADS)], axis=-1)
    n0 = n0_ref[...]
    num = num + inter_m * qc
    den = den + inter_m * headsum(q * n0)
    hout = num / jnp.maximum(jnp.abs(den), jnp.exp(-mt))
    m_new = bs(mt, T - 1)
    b_last = bs(bc, T - 1)
    w_end = jnp.exp(b_last - bc + ic - m_new)
    dc = jnp.exp(b_last + m0 - m_new)
    m_ref[...] = mt[:, T - 1:T, :]
    n_ref[...] = dc[:, 0:1, :] * n0 + jnp.sum(w_end * k, axis=1, keepdims=True)
    lhs_m = jnp.concatenate([v * w_end, split3(dc, t_m)], axis=1).astype(bf16)
    for h in range(ML_HEADS):
        sl = slice(h * ML_HEAD_DIM, (h + 1) * ML_HEAD_DIM)
        zz = jnp.einsum('bkv,bkn->bvn', lhs_m[:, :, sl], ones_rhs(k[:, :, sl]),
                        preferred_element_type=f32)
        c_ref[:, sl, :] = zz[:, :, ML_HEAD_DIM:] * c0_ref[:, sl, :] + zz[:, :, :ML_HEAD_DIM]
    hn2 = headsum(hout * hout) * (1.0 / ML_HEAD_DIM)
    y_ml = hout * lax.rsqrt(hn2 + EPS) * mnw_ref[...] * _sigmoid(o)
    y_ref[:, :, SSD_WIDTH:SSD_WIDTH + ML_WIDTH] = y_ml

    xr2 = xr.reshape(bb * T, LRU_WIDTH)
    rs, is_ = [], []
    for kk in range(LRU_BLOCKS):
        xk = xr2[:, kk * LRU_BLOCK_DIM:(kk + 1) * LRU_BLOCK_DIM].astype(bf16)
        rs.append(jnp.dot(xk, wa_ref[kk], preferred_element_type=f32))
        is_.append(jnp.dot(xk, wx_ref[kk], preferred_element_type=f32))
    r = _sigmoid(jnp.concatenate(rs, axis=1).reshape(bb, T, LRU_WIDTH) + ba_ref[...])
    i_g = _sigmoid(jnp.concatenate(is_, axis=1).reshape(bb, T, LRU_WIDTH) + bx_ref[...])
    log_a = -LRU_C * r * _softplus(-lam_ref[...])
    a_s = jnp.exp(log_a)
    u_s = jnp.sqrt(_neg_expm1(2.0 * log_a)) * i_g * xr
    t_l = tio(LRU_WIDTH)
    cur = lh0_ref[...]
    hr = jnp.zeros((bb, T, LRU_WIDTH), f32)
    for s in range(T):
        cur = a_s[:, s:s + 1, :] * cur + u_s[:, s:s + 1, :]
        hr = jnp.where(t_l == s, jnp.broadcast_to(cur, hr.shape), hr)
    lh_ref[...] = cur
    gr = u_ref[:, :, P_GR:P_GR + LRU_WIDTH]
    y_ref[:, :, SSD_WIDTH + ML_WIDTH:MIX_WIDTH] = hr * _gelu_tanh(gr)


def _smix(l, u, states, prev, w, bb):
    b = u.shape[0]
    row = lambda n: _full((1, n))
    lblk = lambda s: pl.BlockSpec((None, bb) + s[2:], lambda i: (l, i) + (0,) * (len(s) - 2))
    ublk = lambda s: pl.BlockSpec((bb,) + s[1:], lambda i: (i,) + (0,) * (len(s) - 1))
    y_shape = (b, DEC_SEQ, MIX_WIDTH)
    outs = [jax.ShapeDtypeStruct(y_shape, f32)]
    outs += [jax.ShapeDtypeStruct(s.shape, f32) for s in states]
    weights = [w["ssd_conv_w"], w["ssd_conv_b"], w["s_dt_bias"], w["s_alog"], w["dskip"],
               w["ssd_norm_w"], w["s_i_bias"], w["s_f_bias"], w["ml_norm_w"],
               w["lru_conv_w"], w["lru_conv_b"], w["lru_wa"], w["lru_wx"], w["lru_ba"],
               w["lru_bx"], w["lru_lambda"]]
    n_in = 1 + len(states) + len(weights)
    prev = [] if prev is None else list(prev)

    def body(*refs):
        _smix_kernel(*refs[:n_in], *refs[n_in + len(prev):])

    return pl.pallas_call(
        body,
        out_shape=outs,
        grid=(b // bb,),
        in_specs=[ublk(u.shape)] + [lblk(s.shape) for s in states]
        + [_full(a.shape) for a in weights]
        + [pl.BlockSpec(memory_space=pl.ANY)] * len(prev),
        out_specs=[ublk(y_shape)] + [lblk(s.shape) for s in states],
        input_output_aliases={n_in + k: 1 + k for k in range(len(prev))},
        scratch_shapes=[
            pltpu.VMEM((bb, 2 * V7X_SUBLANES, SSD_CONV_DIM), f32),
            pltpu.VMEM((bb, 2 * V7X_SUBLANES, LRU_WIDTH), f32),
        ],
        compiler_params=pltpu.CompilerParams(
            dimension_semantics=("arbitrary",), vmem_limit_bytes=VMEM_LIMIT),
        name="sample_mix",
    )(u, *states, *weights, *prev)


def _prep_layer(l, p):
    w_in = p["w_in"][l].astype(bf16)
    main = jnp.concatenate([w_in[:, 0:_O_DT], w_in[:, _O_Q:_O_O_END], w_in[:, _O_XR:_O_END]], axis=1)
    dt_c = w_in[:, _O_DT:_O_Q]
    i_c = w_in[:, _O_I:_O_F]
    f_c = w_in[:, _O_F:_O_XR]
    pad = jnp.zeros((D_MODEL, V7X_LANES - SSD_HEADS - 2 * ML_HEADS), bf16)
    w_p = jnp.concatenate([main, dt_c, i_c, f_c, pad], axis=1)
    w_s = jnp.concatenate([main, jnp.repeat(dt_c, SSD_HEAD_DIM, axis=1),
                           jnp.repeat(i_c, ML_HEAD_DIM, axis=1),
                           jnp.repeat(f_c, ML_HEAD_DIM, axis=1)], axis=1)
    zpad = jnp.zeros((V7X_LANES - SSD_HEADS - 2 * ML_HEADS,), f32)
    r = lambda a: a.reshape(1, -1)
    return dict(
        w_in_p=w_p, w_in_s=w_s,
        norm1_w=r(p["norm1_w"][l]), norm2_w=r(p["norm2_w"][l]),
        ssd_conv_w=p["ssd_conv_w"][l], ssd_conv_b=r(p["ssd_conv_b"][l]),
        p_small_bias=r(jnp.concatenate([p["ssd_dt_bias"][l], p["ml_i_bias"][l],
                                        p["ml_f_bias"][l], zpad])),
        p_alog=r(jnp.concatenate([p["ssd_a_log"][l], jnp.zeros((V7X_LANES - SSD_HEADS,), f32)])),
        s_dt_bias=r(jnp.repeat(p["ssd_dt_bias"][l], SSD_HEAD_DIM)),
        s_alog=r(jnp.repeat(p["ssd_a_log"][l], SSD_HEAD_DIM)),
        s_i_bias=r(jnp.repeat(p["ml_i_bias"][l], ML_HEAD_DIM)),
        s_f_bias=r(jnp.repeat(p["ml_f_bias"][l], ML_HEAD_DIM)),
        dskip=r(jnp.repeat(p["ssd_d"][l], SSD_HEAD_DIM)),
        ssd_norm_w=r(p["ssd_norm_w"][l]), ml_norm_w=r(p["ml_norm_w"][l]),
        lru_conv_w=p["lru_conv_w"][l], lru_conv_b=r(p["lru_conv_b"][l]),
        lru_wa=p["lru_wa"][l].astype(bf16), lru_wx=p["lru_wx"][l].astype(bf16),
        lru_ba=r(p["lru_ba"][l]), lru_bx=r(p["lru_bx"][l]), lru_lambda=r(p["lru_lambda"][l]),
        w_out=p["w_out"][l].astype(bf16), mlp_up=p["mlp_up"][l].astype(bf16),
        mlp_down=p["mlp_down"][l].astype(bf16),
    )


def kernel(x_prompt, x_sample, c_prompt, c_sample, state_ssm, state_ssd_conv, state_mlstm_c, state_mlstm_n, state_mlstm_m, state_lru_h, state_lru_conv, ada_w, ada_b, norm1_w, norm2_w, w_in, ssd_conv_w, ssd_conv_b, ssd_dt_bias, ssd_a_log, ssd_d, ssd_norm_w, ml_i_bias, ml_f_bias, ml_norm_w, lru_conv_w, lru_conv_b, lru_wa, lru_ba, lru_wx, lru_bx, lru_lambda, w_out, mlp_up, mlp_down, final_norm_w):
    p = dict(norm1_w=norm1_w, norm2_w=norm2_w, w_in=w_in, ssd_conv_w=ssd_conv_w,
             ssd_conv_b=ssd_conv_b, ssd_dt_bias=ssd_dt_bias, ssd_a_log=ssd_a_log, ssd_d=ssd_d,
             ssd_norm_w=ssd_norm_w, ml_i_bias=ml_i_bias, ml_f_bias=ml_f_bias,
             ml_norm_w=ml_norm_w, lru_conv_w=lru_conv_w, lru_conv_b=lru_conv_b, lru_wa=lru_wa,
             lru_ba=lru_ba, lru_wx=lru_wx, lru_bx=lru_bx, lru_lambda=lru_lambda, w_out=w_out,
             mlp_up=mlp_up, mlp_down=mlp_down)
    layers = [_prep_layer(l, p) for l in range(N_LAYERS)]
    fnw = final_norm_w.reshape(1, D_MODEL)
    bp = x_prompt.shape[0]
    bs_ = x_sample.shape[0]

    mod = _ada(jnp.concatenate([c_prompt, c_sample], axis=0), ada_w, ada_b)
    mod = mod.reshape(N_LAYERS, bp + bs_, 6, D_MODEL)

    xp = _interleave_rows(x_prompt)
    p_states = []
    for l, w in enumerate(layers):
        mod_l = mod[l, :bp]
        u = _inproj(xp, mod_l, w["norm1_w"], w["w_in_p"], 1, 256)
        ycat, ssm, sconv, mc, mn, mm, lh, lconv = _pmix(u, w)
        xp = _outmlp(xp, ycat, mod_l, w["w_out"], w["norm2_w"], w["mlp_up"], w["mlp_down"], fnw,
                     1, 512, l == N_LAYERS - 1)
        p_states.append((ssm.reshape(bp, SSD_HEADS, SSD_HEAD_DIM, SSD_STATE), sconv, mc, mn,
                         mm[:, :, 0], lh.reshape(bp, LRU_WIDTH), lconv))
    xp = _deinterleave_rows(xp)

    xs = x_sample
    st_in = (
        state_ssm.reshape(N_LAYERS, bs_, SSD_HEADS * SSD_HEAD_DIM, SSD_STATE),
        state_ssd_conv,
        state_mlstm_c.reshape(N_LAYERS, bs_, ML_HEADS * ML_HEAD_DIM, ML_HEAD_DIM),
        state_mlstm_n.reshape(N_LAYERS, bs_, 1, ML_WIDTH),
        jnp.repeat(state_mlstm_m, ML_HEAD_DIM, axis=-1).reshape(N_LAYERS, bs_, 1, ML_WIDTH),
        state_lru_h.reshape(N_LAYERS, bs_, 1, LRU_WIDTH),
        state_lru_conv,
    )
    st_out = None
    for l, w in enumerate(layers):
        mod_l = mod[l, bp:]
        u = _inproj(xs, mod_l, w["norm1_w"], w["w_in_s"], 32, DEC_SEQ)
        ycat, *st_out = _smix(l, u, st_in, st_out, w, 8)
        xs = _outmlp(xs, ycat, mod_l, w["w_out"], w["norm2_w"], w["mlp_up"], w["mlp_down"], fnw,
                     64, DEC_SEQ, l == N_LAYERS - 1)
    ssm, sconv, mc, mn, mm, lh, lconv = st_out
    s_states = (ssm.reshape(N_LAYERS, bs_, SSD_HEADS, SSD_HEAD_DIM, SSD_STATE), sconv,
                mc.reshape(N_LAYERS, bs_, ML_HEADS, ML_HEAD_DIM, ML_HEAD_DIM),
                mn.reshape(N_LAYERS, bs_, ML_HEADS, ML_HEAD_DIM),
                mm.reshape(N_LAYERS, bs_, ML_HEADS, ML_HEAD_DIM)[..., 0],
                lh.reshape(N_LAYERS, bs_, LRU_WIDTH), lconv)

    stack = lambda sts: tuple(jnp.stack([s[i] for s in sts]) for i in range(7))
    return (xp, xs) + stack(p_states) + s_states
```

```python
import functools
import math

import jax
import jax.numpy as jnp
from jax import lax
from jax.experimental import pallas as pl
from jax.experimental.pallas import tpu as pltpu

f32 = jnp.float32
bf16 = jnp.bfloat16

D_MODEL = 1024
N_LAYERS = 2
MIX_WIDTH = 2 * D_MODEL
SSD_WIDTH = 1024
SSD_HEAD_DIM = 64
SSD_HEADS = 16
SSD_GROUPS = 2
SSD_STATE = 128
SSD_CONV_DIM = SSD_WIDTH + 2 * SSD_GROUPS * SSD_STATE
ML_WIDTH = 512
ML_HEADS = 4
ML_HEAD_DIM = 128
LRU_WIDTH = 512
LRU_BLOCKS = 4
LRU_BLOCK_DIM = 128
LRU_C = 8.0
CONV_W = 4
CHUNK = 128
D_FF = 4 * D_MODEL
EPS = 1e-6
PAST_LEN = 16384
DEC_SEQ = 8

_O_Z, _O_XBC, _O_DT, _O_Q, _O_O_END, _O_I, _O_F, _O_XR, _O_END = (
    0, 1024, 2560, 2576, 4624, 4624, 4628, 4632, 5656)

P_Z = 0
P_XBC = 1024
P_Q = 2560
P_K = 3072
P_V = 3584
P_O = 4096
P_XR = 4608
P_GR = 5120
P_MAIN = 5632
P_SMALL = 5632
NP_COLS = 5760
S_DT = 5632
S_I = 6656
S_F = 7168
NS_COLS = 7680

V7X_LANES = 128
V7X_SUBLANES = 8
VMEM_LIMIT = 56 * 1024 * 1024
PLAYER_VMEM_LIMIT = 60 * 1024 * 1024
ROW_TILES = CHUNK // V7X_SUBLANES
LOG2E = 1.4426950408889634

_NT = (((1,), (1,)), ((), ()))
_TN = (((0,), (0,)), ((), ()))


def _softplus(x):
    return jnp.maximum(x, 0.0) + jnp.log1p(jnp.exp(-jnp.abs(x)))


def _log_sigmoid(x):
    return -_softplus(-x)


def _sigmoid(x):
    return 1.0 / (1.0 + jnp.exp(-x))


def _silu(x):
    return x * _sigmoid(x)


def _gelu_tanh(x):
    c = math.sqrt(2.0 / math.pi)
    return 0.5 * x * (1.0 + jnp.tanh(c * (x + 0.044715 * (x * x * x))))


def _neg_expm1(x):
    return -jnp.tanh(0.5 * x) * (jnp.exp(x) + 1.0)


def _rms(x, w):
    return x * lax.rsqrt(jnp.mean(x * x, axis=-1, keepdims=True) + EPS) * w


def _interleave_rows(x):
    b, t, d = x.shape
    return x.reshape(b, t // CHUNK, V7X_SUBLANES, ROW_TILES, d).swapaxes(2, 3).reshape(b, t, d)


def _deinterleave_rows(x):
    b, t, d = x.shape
    return x.reshape(b, t // CHUNK, ROW_TILES, V7X_SUBLANES, d).swapaxes(2, 3).reshape(b, t, d)


def _full(shape):
    n = len(shape)
    return pl.BlockSpec(shape, lambda *_: (0,) * n)


def _resident(shape):
    n = len(shape)
    return pl.BlockSpec(shape, lambda *_: (0,) * n, pipeline_mode=pl.Buffered(1))


def _ada_kernel(c_ref, w_ref, b_ref, o_ref):
    cs = _silu(c_ref[...]).astype(bf16)
    o_ref[0] = jnp.dot(cs, w_ref[0].astype(bf16), preferred_element_type=f32) + b_ref[0]


def _ada(c_all, ada_w, ada_b):
    n = c_all.shape[0]
    tn = 1024
    return pl.pallas_call(
        _ada_kernel,
        out_shape=jax.ShapeDtypeStruct((N_LAYERS, n, 6 * D_MODEL), f32),
        grid=(N_LAYERS, 6 * D_MODEL // tn),
        in_specs=[
            pl.BlockSpec((n, D_MODEL), lambda l, j: (0, 0)),
            pl.BlockSpec((1, D_MODEL, tn), lambda l, j: (l, 0, j)),
            pl.BlockSpec((1, 1, tn), lambda l, j: (l, 0, j)),
        ],
        out_specs=pl.BlockSpec((1, n, tn), lambda l, j: (l, 0, j)),
        compiler_params=pltpu.CompilerParams(
            dimension_semantics=("arbitrary", "arbitrary"), vmem_limit_bytes=VMEM_LIMIT),
        name="ada_mod",
    )(c_all, ada_w, ada_b.reshape(N_LAYERS, 1, 6 * D_MODEL))


def _inproj_kernel(x_ref, mod_ref, nw_ref, w_ref, o_ref):
    x = x_ref[...]
    bb, tt, d = x.shape
    hn = _rms(x, nw_ref[...]) * (1.0 + mod_ref[:, 1:2, :]) + mod_ref[:, 0:1, :]
    hn = hn.reshape(bb * tt, d).astype(bf16)
    u = jnp.dot(hn, w_ref[...], preferred_element_type=f32)
    o_ref[...] = u.reshape(bb, tt, u.shape[-1])


def _inproj(x, mod, nw, w, bb, tt):
    b, t, d = x.shape
    n = w.shape[1]
    return pl.pallas_call(
        _inproj_kernel,
        out_shape=jax.ShapeDtypeStruct((b, t, n), f32),
        grid=(b // bb, t // tt),
        in_specs=[
            pl.BlockSpec((bb, tt, d), lambda i, j: (i, j, 0)),
            pl.BlockSpec((bb, 6, d), lambda i, j: (i, 0, 0)),
            _full((1, d)),
            _resident((d, n)),
        ],
        out_specs=pl.BlockSpec((bb, tt, n), lambda i, j: (i, j, 0)),
        compiler_params=pltpu.CompilerParams(
            dimension_semantics=("arbitrary", "arbitrary"), vmem_limit_bytes=VMEM_LIMIT),
        name="inproj",
    )(x, mod, nw, w)


def _outmlp_kernel(x_ref, y_ref, mod_ref, wout_ref, n2w_ref, up_ref, down_ref, fnw_ref, o_ref,
                   *, final):
    x = x_ref[...]
    bb, tt, d = x.shape
    m = bb * tt
    ycat = y_ref[...].reshape(m, MIX_WIDTH).astype(bf16)
    mix = jnp.dot(ycat, wout_ref[:, 0:d], preferred_element_type=f32).reshape(bb, tt, d)
    x1 = x + mod_ref[:, 2:3, :] * mix
    hn = _rms(x1, n2w_ref[...]) * (1.0 + mod_ref[:, 4:5, :]) + mod_ref[:, 3:4, :]
    hn = hn.reshape(m, d).astype(bf16)
    ff = jnp.zeros((m, d), f32)
    fc = 1024
    for c in range(D_FF // fc):
        h = jnp.dot(hn, up_ref[:, c * fc:(c + 1) * fc], preferred_element_type=f32)
        h = jnp.square(jnp.maximum(h, 0.0)).astype(bf16)
        ff = ff + jnp.dot(h, down_ref[c * fc:(c + 1) * fc, 0:D_MODEL],
                          preferred_element_type=f32)
    x2 = x1 + mod_ref[:, 5:6, :] * ff.reshape(bb, tt, d)
    if final:
        x2 = _rms(x2, fnw_ref[...])
    o_ref[...] = x2


def _outmlp(x, ycat, mod, wout, n2w, up, down, fnw, bb, tt, final):
    b, t, d = x.shape
    return pl.pallas_call(
        functools.partial(_outmlp_kernel, final=final),
        out_shape=jax.ShapeDtypeStruct((b, t, d), f32),
        grid=(b // bb, t // tt),
        in_specs=[
            pl.BlockSpec((bb, tt, d), lambda i, j: (i, j, 0)),
            pl.BlockSpec((bb, tt, MIX_WIDTH), lambda i, j: (i, j, 0)),
            pl.BlockSpec((bb, 6, d), lambda i, j: (i, 0, 0)),
            _resident(wout.shape),
            _full((1, d)),
            _resident(up.shape),
            _resident(down.shape),
            _full((1, d)),
        ],
        out_specs=pl.BlockSpec((bb, tt, d), lambda i, j: (i, j, 0)),
        compiler_params=pltpu.CompilerParams(
            dimension_semantics=("arbitrary", "arbitrary"), vmem_limit_bytes=VMEM_LIMIT),
        name="outproj_mlp",
    )(x, ycat, mod, wout, n2w, up, down, fnw)


def _stage_in(x, mod_ref, nw_ref, w_ref, u_ref):
    hn = _rms(x, nw_ref[...]) * (1.0 + mod_ref[0, 1:2, :]) + mod_ref[0, 0:1, :]
    hn = hn.astype(bf16)
    cb = 1152
    for c in range(NP_COLS // cb):
        u_ref[:, c * cb:(c + 1) * cb] = jnp.dot(hn, w_ref[:, c * cb:(c + 1) * cb],
                                                preferred_element_type=f32)


def _stage_out(x, y_ref, mod_ref, wout_ref, n2w_ref, up_ref, down_ref, fnw_ref, final):
    mix = jnp.dot(y_ref[...], wout_ref[:, 0:D_MODEL], preferred_element_type=f32)
    x1 = x + mod_ref[0, 2:3, :] * mix
    hn = (_rms(x1, n2w_ref[...]) * (1.0 + mod_ref[0, 4:5, :]) + mod_ref[0, 3:4, :]).astype(bf16)
    ff = jnp.zeros(x.shape, f32)
    fc = 1024
    for c in range(D_FF // fc):
        h = jnp.dot(hn, up_ref[:, c * fc:(c + 1) * fc], preferred_element_type=f32)
        h = jnp.square(jnp.maximum(h, 0.0)).astype(bf16)
        ff = ff + jnp.dot(h, down_ref[c * fc:(c + 1) * fc, 0:D_MODEL],
                          preferred_element_type=f32)
    x2 = x1 + mod_ref[0, 5:6, :] * ff
    return _rms(x2, fnw_ref[...]) if final else x2


def _reset_state_if(first, state):
    for ref in state:
        ref[...] = jnp.where(first, 0.0, ref[...])


def _stage_mix(u_ref, y_ref, prm, state, ci, outs):
    (cw_ref, cb_ref, sbias_ref, alog_ref, dskip_ref, snw_ref, mnw_ref,
     lcw_ref, lcb_ref, wa_ref, wx_ref, ba_ref, bx_ref, lam_ref) = prm
    xbuf, lbuf, st, cst, nst, mst, hst = state
    L = CHUNK
    SUB = V7X_SUBLANES
    HIST = (CONV_W - 1) * SUB

    row = lax.broadcasted_iota(jnp.int32, (L, L), 0)
    col = lax.broadcasted_iota(jnp.int32, (L, L), 1)
    tok = lambda i: (i & (SUB - 1)) * ROW_TILES + (i >> 3)
    tri = tok(row) >= tok(col)
    lo = col < SSD_HEAD_DIM

    def bcl(a, j):
        return jnp.broadcast_to(a[:, j:j + 1], (L, L))

    def conv(raw, hist_ref, w_ref, b_ref):
        n = raw.shape[1]
        sub = lax.broadcasted_iota(jnp.int32, (SUB, n), 0)
        prev = [pltpu.roll(jnp.where(sub == SUB - 1, hist_ref[k * SUB:(k + 1) * SUB, :],
                                     raw[L - HIST + k * SUB:L - HIST + (k + 1) * SUB, :]), 1, 0)
                for k in range(CONV_W - 1)]
        acc = b_ref[...] + raw * w_ref[CONV_W - 1:CONV_W, :]
        for d in range(1, CONV_W):
            shifted = jnp.concatenate(prev[CONV_W - 1 - d:] + [raw[0:L - d * SUB, :]], axis=0)
            acc = acc + shifted * w_ref[CONV_W - 1 - d:CONV_W - d, :]
        hist_ref[...] = raw[L - HIST:L, :]
        return acc

    xbc = _silu(conv(u_ref[:,P_XBC:P_XBC + SSD_CONV_DIM], xbuf, cw_ref, cb_ref))
    xr = conv(u_ref[:,P_XR:P_XR + LRU_WIDTH], lbuf, lcw_ref, lcb_ref)

    sm = u_ref[:,P_SMALL:P_SMALL + V7X_LANES] + sbias_ref[...]
    dt = _softplus(sm)
    a_row = -jnp.exp(alog_ref[...])
    gates = jnp.where(col < SSD_HEADS, dt * (a_row * LOG2E),
                      jnp.where((col >= 20) & (col < 24), _log_sigmoid(sm), 0.0))
    cum = jnp.dot(tri.astype(f32), gates, precision=lax.Precision.HIGHEST,
                  preferred_element_type=f32)
    cum_t = cum.T
    sm_t = sm.T

    ys = []
    for g in range(SSD_GROUPS):
        bm_t = xbc[:, SSD_WIDTH + g * SSD_STATE:SSD_WIDTH + (g + 1) * SSD_STATE].T.astype(bf16)
        cm = xbc[:, SSD_WIDTH + (SSD_GROUPS + g) * SSD_STATE:
                 SSD_WIDTH + (SSD_GROUPS + g + 1) * SSD_STATE].astype(bf16)
        cb = jnp.dot(cm, bm_t, preferred_element_type=f32)
        inter = jnp.dot(cm, st[g].astype(bf16), preferred_element_type=f32)
        for jj in range(4):
            jp = g * 4 + jj
            e0, e1 = 2 * jp, 2 * jp + 1
            c0 = bcl(cum, e0)
            c1 = bcl(cum, e1)
            cum_p = jnp.where(lo, c0, c1)
            dt_p = jnp.where(lo, bcl(dt, e0), bcl(dt, e1))
            xs_p = xbc[:, jp * L:(jp + 1) * L]
            xdt = xs_p * dt_p
            dec0 = jnp.exp2(jnp.where(tri, c0 - cum_t[e0:e0 + 1, :], -jnp.inf))
            dec1 = jnp.exp2(jnp.where(tri, c1 - cum_t[e1:e1 + 1, :], -jnp.inf))
            att0 = (cb * dec0).astype(bf16)
            att1 = (cb * dec1).astype(bf16)
            x_lo = jnp.where(lo, xdt, 0.0).astype(bf16)
            x_hi = jnp.where(lo, 0.0, xdt).astype(bf16)
            y_p = (jnp.dot(att0, x_lo, preferred_element_type=f32)
                   + jnp.dot(att1, x_hi, preferred_element_type=f32)
                   + inter[:, jj * L:(jj + 1) * L] * jnp.exp2(cum_p)
                   + xs_p * dskip_ref[:, jp * L:(jp + 1) * L])
            ys.append(y_p)
            c_last = cum_p[L - 1:L, :]
            xend = (xdt * jnp.exp2(c_last - cum_p)).astype(bf16)
            st[g, :, jj * L:(jj + 1) * L] = (
                jnp.exp2(c_last) * st[g, :, jj * L:(jj + 1) * L]
                + jnp.dot(bm_t, xend, preferred_element_type=f32))
    y = jnp.concatenate(ys, axis=1)
    z = u_ref[:,P_Z:P_Z + SSD_WIDTH]
    y_ref[:,0:SSD_WIDTH] = _rms(y * _silu(z), snw_ref[...]).astype(bf16)

    scale = ML_HEAD_DIM ** -0.5
    for h in range(ML_HEADS):
        sl = slice(h * ML_HEAD_DIM, (h + 1) * ML_HEAD_DIM)
        q = u_ref[:,P_Q + h * ML_HEAD_DIM:P_Q + (h + 1) * ML_HEAD_DIM]
        k = u_ref[:,P_K + h * ML_HEAD_DIM:P_K + (h + 1) * ML_HEAD_DIM] * scale
        v = u_ref[:,P_V + h * ML_HEAD_DIM:P_V + (h + 1) * ML_HEAD_DIM]
        o = u_ref[:,P_O + h * ML_HEAD_DIM:P_O + (h + 1) * ML_HEAD_DIM]
        qb = q.astype(bf16)
        kb = k.astype(bf16)
        jf = 20 + h
        ji = 16 + h
        b_col = cum[:, jf:jf + 1]
        i_col = sm[:, ji:ji + 1]
        dm = jnp.where(tri, bcl(cum, jf) - cum_t[jf:jf + 1, :] + sm_t[ji:ji + 1, :], -jnp.inf)
        m_prev = mst[h, :, 0:1]
        g_col = b_col + m_prev
        mt = jnp.maximum(g_col, jnp.max(dm, axis=-1, keepdims=True))
        w = jnp.exp(dm - mt) * lax.dot_general(qb, kb, _NT, preferred_element_type=f32)
        inter = jnp.exp(g_col - mt)
        c_old = cst[h]
        n_old = nst[h]
        num = (jnp.dot(w.astype(bf16), v.astype(bf16), preferred_element_type=f32)
               + inter * lax.dot_general(qb, c_old.astype(bf16), _NT, preferred_element_type=f32))
        den = (jnp.sum(w, axis=-1, keepdims=True)
               + inter * jnp.sum(q * n_old, axis=-1, keepdims=True))
        hout = num / jnp.maximum(jnp.abs(den), jnp.exp(-mt))
        m_new = mt[L - 1:L, :]
        b_last = b_col[L - 1:L, :]
        w_end = jnp.exp(b_last - b_col + i_col - m_new)
        dc = jnp.exp(b_last + m_prev - m_new)
        cst[h] = dc * c_old + jnp.dot((v * w_end).T.astype(bf16), kb,
                                      preferred_element_type=f32)
        nst[h] = dc * n_old + jnp.sum(w_end * k, axis=0, keepdims=True)
        mst[h] = jnp.broadcast_to(m_new, (1, V7X_LANES))
        yh = _rms(hout, mnw_ref[:, sl]) * _sigmoid(o)
        y_ref[:,SSD_WIDTH + h * ML_HEAD_DIM:SSD_WIDTH + (h + 1) * ML_HEAD_DIM] = yh.astype(bf16)

    rs, is_ = [], []
    for kb_ in range(LRU_BLOCKS):
        xk = xr[:, kb_ * LRU_BLOCK_DIM:(kb_ + 1) * LRU_BLOCK_DIM].astype(bf16)
        rs.append(jnp.dot(xk, wa_ref[kb_], preferred_element_type=f32))
        is_.append(jnp.dot(xk, wx_ref[kb_], preferred_element_type=f32))
    r = _sigmoid(jnp.concatenate(rs, axis=1) + ba_ref[...])
    i_g = _sigmoid(jnp.concatenate(is_, axis=1) + bx_ref[...])
    log_a = -LRU_C * r * _softplus(-lam_ref[...])
    mult = jnp.sqrt(_neg_expm1(2.0 * log_a))
    rowl = lax.broadcasted_iota(jnp.int32, (L, LRU_WIDTH), 0)
    first_row = jnp.where(ci == 0, 0, -1)
    mult = jnp.where(rowl == first_row, 1.0, mult)
    a_s = jnp.exp(log_a)
    u_s = mult * i_g * xr
    tile = lambda a, v: a[v * SUB:(v + 1) * SUB, :]
    h_loc = [tile(u_s, 0)]
    a_cum = [tile(a_s, 0)]
    for v in range(1, ROW_TILES):
        h_loc.append(tile(a_s, v) * h_loc[-1] + tile(u_s, v))
        a_cum.append(tile(a_s, v) * a_cum[-1])
    a_e, h_e = a_cum[-1], h_loc[-1]
    subl = lax.broadcasted_iota(jnp.int32, (SUB, LRU_WIDTH), 0)
    k_ = 1
    while k_ < SUB:
        keep = subl >= k_
        a_sh = jnp.where(keep, pltpu.roll(a_e, k_, 0), 1.0)
        h_sh = jnp.where(keep, pltpu.roll(h_e, k_, 0), 0.0)
        h_e = a_e * h_sh + h_e
        a_e = a_e * a_sh
        k_ *= 2
    h0 = hst[...]
    run_end = h_e + a_e * h0
    run_in = jnp.where(subl == 0, h0, pltpu.roll(run_end, 1, 0))
    hr = jnp.concatenate([h_loc[v] + a_cum[v] * run_in for v in range(ROW_TILES)], axis=0)
    hst[...] = run_end[SUB - 1:SUB, :]
    gr = u_ref[:,P_GR:P_GR + LRU_WIDTH]
    y_ref[:,SSD_WIDTH + ML_WIDTH:MIX_WIDTH] = (hr * _gelu_tanh(gr)).astype(bf16)

    if outs is not None:
        ssm_ref, sconv_ref, mc_ref, mn_ref, mm_ref, lh_ref, lconv_ref = outs
        for g in range(SSD_GROUPS):
            for jj in range(4):
                ssm_ref[0, g * 4 + jj] = st[g, :, jj * L:(jj + 1) * L].T
        for k in range(CONV_W - 1):
            sconv_ref[0, k:k + 1, :] = xbuf[k * SUB + SUB - 1:(k + 1) * SUB, :]
            lconv_ref[0, k:k + 1, :] = lbuf[k * SUB + SUB - 1:(k + 1) * SUB, :]
        for h in range(ML_HEADS):
            mc_ref[0, h] = cst[h]
            mn_ref[0, h:h + 1, :] = nst[h]
            mm_ref[0, h:h + 1, :] = mst[h]
        lh_ref[0] = hst[...]


N_MIX_PARAMS = 14
N_STATE_OUTS = 7


def _player_kernel(*refs, final, n_chunks, nc):
    xa_ref, moda_ref, xc_ref, modc_ref, n1w_ref, win_ref = refs[:6]
    prm = refs[6:6 + N_MIX_PARAMS]
    wout_ref, n2w_ref, up_ref, down_ref, fnw_ref = refs[6 + N_MIX_PARAMS:11 + N_MIX_PARAMS]
    o_ref = refs[11 + N_MIX_PARAMS]
    outs = refs[12 + N_MIX_PARAMS:12 + N_MIX_PARAMS + N_STATE_OUTS]
    u_bufs = refs[-11:-9]
    y_bufs = refs[-9:-7]
    state = refs[-7:]
    k = pl.program_id(0)

    @pl.when(k == 0)
    def _():
        u_bufs[1][...] = jnp.zeros(u_bufs[1].shape, f32)
        y_bufs[1][...] = jnp.zeros(y_bufs[1].shape, bf16)
        for ref in state:
            ref[...] = jnp.zeros(ref.shape, f32)

    for sub in range(2):
        rows = slice(sub * CHUNK, (sub + 1) * CHUNK)
        _stage_in(xa_ref[0, rows, :], moda_ref, n1w_ref, win_ref, u_bufs[sub])
        jb = jnp.clip(2 * k + sub - 1, 0, n_chunks - 1)
        ci = jb % nc
        if sub == 1:
            _reset_state_if(ci == 0, state)
        _stage_mix(u_bufs[1 - sub], y_bufs[sub], prm, state, ci, outs if sub == 0 else None)
        o_ref[0, rows, :] = _stage_out(xc_ref[0, rows, :], y_bufs[1 - sub], modc_ref, wout_ref,
                                       n2w_ref, up_ref, down_ref, fnw_ref, final)


def _player(x, mod, w, fnw, final):
    b, t, d = x.shape
    nc = t // CHUNK
    n_chunks = b * nc
    n_pairs = n_chunks // 2
    ppr = nc // 2
    row = lambda n: _full((1, n))
    pair_a = lambda k: jnp.minimum(k, n_pairs - 1)
    pair_c = lambda k: jnp.clip(k - 1, 0, n_pairs - 1)
    seq_b = lambda k: jnp.clip(2 * k - 1, 0, n_chunks - 1) // nc
    xspec = lambda pair: pl.BlockSpec((1, 2 * CHUNK, d), lambda k: (pair(k) // ppr, pair(k) % ppr, 0))
    mspec = lambda pair: pl.BlockSpec((1, 6, d), lambda k: (pair(k) // ppr, 0, 0))
    outs = [
        jax.ShapeDtypeStruct((b, t, d), f32),
        jax.ShapeDtypeStruct((b, 8, 128, 128), f32),
        jax.ShapeDtypeStruct((b, CONV_W - 1, SSD_CONV_DIM), f32),
        jax.ShapeDtypeStruct((b, ML_HEADS, ML_HEAD_DIM, ML_HEAD_DIM), f32),
        jax.ShapeDtypeStruct((b, ML_HEADS, ML_HEAD_DIM), f32),
        jax.ShapeDtypeStruct((b, ML_HEADS, V7X_LANES), f32),
        jax.ShapeDtypeStruct((b, 1, LRU_WIDTH), f32),
        jax.ShapeDtypeStruct((b, CONV_W - 1, LRU_WIDTH), f32),
    ]
    st_spec = lambda s: pl.BlockSpec((1,) + s[1:], lambda k: (seq_b(k),) + (0,) * (len(s) - 1))
    mix_params = [w["ssd_conv_w"], w["ssd_conv_b"], w["p_small_bias"], w["p_alog"], w["dskip"],
                  w["ssd_norm_w"], w["ml_norm_w"], w["lru_conv_w"], w["lru_conv_b"], w["lru_wa"],
                  w["lru_wx"], w["lru_ba"], w["lru_bx"], w["lru_lambda"]]
    assert len(mix_params) == N_MIX_PARAMS and len(outs) == 1 + N_STATE_OUTS
    return pl.pallas_call(
        functools.partial(_player_kernel, final=final, n_chunks=n_chunks, nc=nc),
        out_shape=outs,
        grid=(n_pairs + 1,),
        in_specs=[xspec(pair_a), mspec(pair_a), xspec(pair_c), mspec(pair_c),
                  row(d), _resident((d, NP_COLS))]
        + [_full(a.shape) for a in mix_params]
        + [_resident(w["w_out"].shape), row(d), _resident(w["mlp_up"].shape),
           _resident(w["mlp_down"].shape), row(d)],
        out_specs=[xspec(pair_c)] + [st_spec(o.shape) for o in outs[1:]],
        scratch_shapes=[
            pltpu.VMEM((CHUNK, NP_COLS), f32), pltpu.VMEM((CHUNK, NP_COLS), f32),
            pltpu.VMEM((CHUNK, MIX_WIDTH), bf16), pltpu.VMEM((CHUNK, MIX_WIDTH), bf16),
            pltpu.VMEM(((CONV_W - 1) * V7X_SUBLANES, SSD_CONV_DIM), f32),
            pltpu.VMEM(((CONV_W - 1) * V7X_SUBLANES, LRU_WIDTH), f32),
            pltpu.VMEM((SSD_GROUPS, SSD_STATE, 512), f32),
            pltpu.VMEM((ML_HEADS, ML_HEAD_DIM, ML_HEAD_DIM), f32),
            pltpu.VMEM((ML_HEADS, 1, ML_HEAD_DIM), f32),
            pltpu.VMEM((ML_HEADS, 1, V7X_LANES), f32),
            pltpu.VMEM((1, LRU_WIDTH), f32),
        ],
        compiler_params=pltpu.CompilerParams(
            dimension_semantics=("arbitrary",), vmem_limit_bytes=PLAYER_VMEM_LIMIT),
        name="prompt_layer",
    )(x, mod, x, mod, w["norm1_w"], w["w_in_p"], *mix_params,
      w["w_out"], w["norm2_w"], w["mlp_up"], w["mlp_down"], fnw)


def _smix_kernel(u_ref, s0_ref, sconv0_ref, c0_ref, n0_ref, m0_ref, lh0_ref, lconv0_ref,
                 cw_ref, cb_ref, dtb_ref, alog_ref, dskip_ref, snw_ref, ib_ref, fb_ref, mnw_ref,
                 lcw_ref, lcb_ref, wa_ref, wx_ref, ba_ref, bx_ref, lam_ref,
                 y_ref, s_ref, sconv_ref, c_ref, n_ref, m_ref, lh_ref, lconv_ref,
                 xbuf, lbuf):
    bb = u_ref.shape[0]
    T = DEC_SEQ
    HEAD = V7X_SUBLANES

    def bs(a, s):
        return jnp.broadcast_to(a[:, s:s + 1, :], a.shape)

    def tio(n):
        return lax.broadcasted_iota(jnp.int32, (bb, T, n), 1)

    def lio(n):
        return lax.broadcasted_iota(jnp.int32, (bb, T, n), 2)

    def cumsum_t(a):
        t = tio(a.shape[-1])
        acc = jnp.zeros_like(a)
        for s in range(T):
            acc = acc + jnp.where(t >= s, bs(a, s), 0.0)
        return acc

    def conv(buf, raw, hist, w_ref, b_ref, n):
        buf[:, HEAD - 3:HEAD, :] = hist
        buf[:, HEAD:HEAD + T, :] = raw
        acc = b_ref[...] + buf[:, HEAD - 3:HEAD - 3 + T, :] * w_ref[0:1, :]
        for j in range(1, CONV_W):
            acc = acc + buf[:, HEAD - 3 + j:HEAD - 3 + j + T, :] * w_ref[j:j + 1, :]
        return acc, buf[:, HEAD + T - 3:HEAD + T, :]

    xbc, sconv_new = conv(xbuf, u_ref[:, :, P_XBC:P_XBC + SSD_CONV_DIM], sconv0_ref[...],
                          cw_ref, cb_ref, SSD_CONV_DIM)
    sconv_ref[...] = sconv_new
    xbc = _silu(xbc)
    xr, lconv_new = conv(lbuf, u_ref[:, :, P_XR:P_XR + LRU_WIDTH], lconv0_ref[...],
                         lcw_ref, lcb_ref, LRU_WIDTH)
    lconv_ref[...] = lconv_new

    xs = xbc[:, :, 0:SSD_WIDTH]
    t_w = tio(SSD_WIDTH)
    l_w = lio(SSD_WIDTH)
    dt = _softplus(u_ref[:, :, S_DT:S_DT + SSD_WIDTH] + dtb_ref[...])
    cum = cumsum_t(dt * (-jnp.exp(alog_ref[...])))
    xdt = xs * dt
    bms = [xbc[:, :, SSD_WIDTH + g * SSD_STATE:SSD_WIDTH + (g + 1) * SSD_STATE]
           for g in range(SSD_GROUPS)]
    cms = [xbc[:, :, SSD_WIDTH + (SSD_GROUPS + g) * SSD_STATE:
               SSD_WIDTH + (SSD_GROUPS + g + 1) * SSD_STATE] for g in range(SSD_GROUPS)]
    y = xs * dskip_ref[...]
    for s in range(T):
        cb0 = jnp.sum(cms[0] * bs(bms[0], s), axis=-1, keepdims=True)
        cb1 = jnp.sum(cms[1] * bs(bms[1], s), axis=-1, keepdims=True)
        cbs = jnp.where(l_w < 512, cb0, cb1)
        dec = jnp.exp(jnp.where(t_w >= s, cum - bs(cum, s), -jnp.inf))
        y = y + cbs * dec * bs(xdt, s)
    inter = jnp.concatenate(
        [jnp.einsum('btn,bpn->btp', cms[g].astype(bf16),
                    s0_ref[:, g * 512:(g + 1) * 512, :].astype(bf16),
                    preferred_element_type=f32) for g in range(SSD_GROUPS)], axis=-1)
    y = y + inter * jnp.exp(cum)
    z = u_ref[:, :, P_Z:P_Z + SSD_WIDTH]
    y_ref[:, :, 0:SSD_WIDTH] = _rms(y * _silu(z), snw_ref[...])

    def split3(d, t_idx):
        hi = d.astype(bf16).astype(f32)
        r1 = d - hi
        mid = r1.astype(bf16).astype(f32)
        lo_ = (r1 - mid).astype(bf16).astype(f32)
        return jnp.where(t_idx == 0, hi, jnp.where(t_idx == 1, mid, jnp.where(t_idx == 2, lo_, 0.0)))

    def ones_rhs(kmat):
        zeros = jnp.zeros_like(kmat)
        return jnp.concatenate(
            [jnp.concatenate([kmat, zeros], axis=2),
             jnp.concatenate([zeros, jnp.ones_like(kmat)], axis=2)], axis=1).astype(bf16)

    c_last = bs(cum, T - 1)
    xend = xdt * jnp.exp(c_last - cum)
    lhs = jnp.concatenate([xend, split3(jnp.exp(c_last), t_w)], axis=1).astype(bf16)
    for g in range(SSD_GROUPS):
        zz = jnp.einsum('bkp,bkn->bpn', lhs[:, :, g * 512:(g + 1) * 512],
                        ones_rhs(bms[g]), preferred_element_type=f32)
        s_ref[:, g * 512:(g + 1) * 512, :] = (
            zz[:, :, SSD_STATE:] * s0_ref[:, g * 512:(g + 1) * 512, :] + zz[:, :, :SSD_STATE])

    t_m = tio(ML_WIDTH)
    q = u_ref[:, :, P_Q:P_Q + ML_WIDTH]
    k = u_ref[:, :, P_K:P_K + ML_WIDTH] * (ML_HEAD_DIM ** -0.5)
    v = u_ref[:, :, P_V:P_V + ML_WIDTH]
    o = u_ref[:, :, P_O:P_O + ML_WIDTH]
    ic = u_ref[:, :, S_I:S_I + ML_WIDTH] + ib_ref[...]
    fc = _log_sigmoid(u_ref[:, :, S_F:S_F + ML_WIDTH] + fb_ref[...])
    bc = cumsum_t(fc)
    m0 = m0_ref[...]
    g_ = bc + m0

    def headsum(a):
        return jnp.concatenate(
            [jnp.broadcast_to(jnp.sum(a[:, :, h * ML_HEAD_DIM:(h + 1) * ML_HEAD_DIM], axis=-1,
                                      keepdims=True), (bb, T, ML_HEAD_DIM))
             for h in range(ML_HEADS)], axis=-1)

    dms = [jnp.where(t_m >= s, bc - bs(bc, s) + bs(ic, s), -jnp.inf) for s in range(T)]
    mt = g_
    for s in range(T):
        mt = jnp.maximum(mt, dms[s])
    num = jnp.zeros((bb, T, ML_WIDTH), f32)
    den = jnp.zeros((bb, T, ML_WIDTH), f32)
    for s in range(T):
        w = jnp.exp(dms[s] - mt) * headsum(q * bs(k, s))
        num = num + w * bs(v, s)
        den = den + w
    inter_m = jnp.exp(g_ - mt)
    qb = q.astype(bf16)
    qc = jnp.concatenate(
        [jnp.einsum('btk,bvk->btv', qb[:, :, h * ML_HEAD_DIM:(h + 1) * ML_HEAD_DIM],
                    c0_ref[:, h * ML_HEAD_DIM:(h + 1) * ML_HEAD_DIM, :].astype(bf16),
                    preferred_element_type=f32) for h in range(ML_HEADS)], axis=-1)
    n0 = n0_ref[...]
    num = num + inter_m * qc
    den = den + inter_m * headsum(q * n0)
    hout = num / jnp.maximum(jnp.abs(den), jnp.exp(-mt))
    m_new = bs(mt, T - 1)
    b_last = bs(bc, T - 1)
    w_end = jnp.exp(b_last - bc + ic - m_new)
    dc = jnp.exp(b_last + m0 - m_new)
    m_ref[...] = mt[:, T - 1:T, :]
    n_ref[...] = dc[:, 0:1, :] * n0 + jnp.sum(w_end * k, axis=1, keepdims=True)
    lhs_m = jnp.concatenate([v * w_end, split3(dc, t_m)], axis=1).astype(bf16)
    for h in range(ML_HEADS):
        sl = slice(h * ML_HEAD_DIM, (h + 1) * ML_HEAD_DIM)
        zz = jnp.einsum('bkv,bkn->bvn', lhs_m[:, :, sl], ones_rhs(k[:, :, sl]),
                        preferred_element_type=f32)
        c_ref[:, sl, :] = zz[:, :, ML_HEAD_DIM:] * c0_ref[:, sl, :] + zz[:, :, :ML_HEAD_DIM]
    hn2 = headsum(hout * hout) * (1.0 / ML_HEAD_DIM)
    y_ml = hout * lax.rsqrt(hn2 + EPS) * mnw_ref[...] * _sigmoid(o)
    y_ref[:, :, SSD_WIDTH:SSD_WIDTH + ML_WIDTH] = y_ml

    xr2 = xr.reshape(bb * T, LRU_WIDTH)
    rs, is_ = [], []
    for kk in range(LRU_BLOCKS):
        xk = xr2[:, kk * LRU_BLOCK_DIM:(kk + 1) * LRU_BLOCK_DIM].astype(bf16)
        rs.append(jnp.dot(xk, wa_ref[kk], preferred_element_type=f32))
        is_.append(jnp.dot(xk, wx_ref[kk], preferred_element_type=f32))
    r = _sigmoid(jnp.concatenate(rs, axis=1).reshape(bb, T, LRU_WIDTH) + ba_ref[...])
    i_g = _sigmoid(jnp.concatenate(is_, axis=1).reshape(bb, T, LRU_WIDTH) + bx_ref[...])
    log_a = -LRU_C * r * _softplus(-lam_ref[...])
    a_s = jnp.exp(log_a)
    u_s = jnp.sqrt(_neg_expm1(2.0 * log_a)) * i_g * xr
    t_l = tio(LRU_WIDTH)
    cur = lh0_ref[...]
    hr = jnp.zeros((bb, T, LRU_WIDTH), f32)
    for s in range(T):
        cur = a_s[:, s:s + 1, :] * cur + u_s[:, s:s + 1, :]
        hr = jnp.where(t_l == s, jnp.broadcast_to(cur, hr.shape), hr)
    lh_ref[...] = cur
    gr = u_ref[:, :, P_GR:P_GR + LRU_WIDTH]
    y_ref[:, :, SSD_WIDTH + ML_WIDTH:MIX_WIDTH] = hr * _gelu_tanh(gr)


def _smix(l, u, states, prev, w, bb):
    b = u.shape[0]
    row = lambda n: _full((1, n))
    lblk = lambda s: pl.BlockSpec((None, bb) + s[2:], lambda i: (l, i) + (0,) * (len(s) - 2))
    ublk = lambda s: pl.BlockSpec((bb,) + s[1:], lambda i: (i,) + (0,) * (len(s) - 1))
    y_shape = (b, DEC_SEQ, MIX_WIDTH)
    outs = [jax.ShapeDtypeStruct(y_shape, f32)]
    outs += [jax.ShapeDtypeStruct(s.shape, f32) for s in states]
    weights = [w["ssd_conv_w"], w["ssd_conv_b"], w["s_dt_bias"], w["s_alog"], w["dskip"],
               w["ssd_norm_w"], w["s_i_bias"], w["s_f_bias"], w["ml_norm_w"],
               w["lru_conv_w"], w["lru_conv_b"], w["lru_wa"], w["lru_wx"], w["lru_ba"],
               w["lru_bx"], w["lru_lambda"]]
    n_in = 1 + len(states) + len(weights)
    prev = list(prev)

    def body(*refs):
        _smix_kernel(*refs[:n_in], *refs[n_in + len(prev):])

    return pl.pallas_call(
        body,
        out_shape=outs,
        grid=(b // bb,),
        in_specs=[ublk(u.shape)] + [lblk(s.shape) for s in states]
        + [_full(a.shape) for a in weights]
        + [pl.BlockSpec(memory_space=pl.ANY)] * len(prev),
        out_specs=[ublk(y_shape)] + [lblk(s.shape) for s in states],
        input_output_aliases={n_in + k: 1 + k for k in range(len(prev))},
        scratch_shapes=[
            pltpu.VMEM((bb, 2 * V7X_SUBLANES, SSD_CONV_DIM), f32),
            pltpu.VMEM((bb, 2 * V7X_SUBLANES, LRU_WIDTH), f32),
        ],
        compiler_params=pltpu.CompilerParams(
            dimension_semantics=("arbitrary",), vmem_limit_bytes=VMEM_LIMIT),
        name="sample_mix",
    )(u, *states, *weights, *prev)


def _prep_layer(l, p):
    w_in = p["w_in"][l].astype(bf16)
    main = jnp.concatenate([w_in[:, 0:_O_DT], w_in[:, _O_Q:_O_O_END], w_in[:, _O_XR:_O_END]], axis=1)
    dt_c = w_in[:, _O_DT:_O_Q]
    i_c = w_in[:, _O_I:_O_F]
    f_c = w_in[:, _O_F:_O_XR]
    pad = jnp.zeros((D_MODEL, V7X_LANES - SSD_HEADS - 2 * ML_HEADS), bf16)
    w_p = jnp.concatenate([main, dt_c, i_c, f_c, pad], axis=1)
    w_s = jnp.concatenate([main, jnp.repeat(dt_c, SSD_HEAD_DIM, axis=1),
                           jnp.repeat(i_c, ML_HEAD_DIM, axis=1),
                           jnp.repeat(f_c, ML_HEAD_DIM, axis=1)], axis=1)
    zpad = jnp.zeros((V7X_LANES - SSD_HEADS - 2 * ML_HEADS,), f32)
    r = lambda a: a.reshape(1, -1)
    lane_pad = lambda a: jnp.pad(a, ((0, 0), (0, V7X_LANES)))
    return dict(
        w_in_p=w_p, w_in_s=w_s,
        norm1_w=r(p["norm1_w"][l]), norm2_w=r(p["norm2_w"][l]),
        ssd_conv_w=p["ssd_conv_w"][l], ssd_conv_b=r(p["ssd_conv_b"][l]),
        p_small_bias=r(jnp.concatenate([p["ssd_dt_bias"][l], p["ml_i_bias"][l],
                                        p["ml_f_bias"][l], zpad])),
        p_alog=r(jnp.concatenate([p["ssd_a_log"][l], jnp.zeros((V7X_LANES - SSD_HEADS,), f32)])),
        s_dt_bias=r(jnp.repeat(p["ssd_dt_bias"][l], SSD_HEAD_DIM)),
        s_alog=r(jnp.repeat(p["ssd_a_log"][l], SSD_HEAD_DIM)),
        s_i_bias=r(jnp.repeat(p["ml_i_bias"][l], ML_HEAD_DIM)),
        s_f_bias=r(jnp.repeat(p["ml_f_bias"][l], ML_HEAD_DIM)),
        dskip=r(jnp.repeat(p["ssd_d"][l], SSD_HEAD_DIM)),
        ssd_norm_w=r(p["ssd_norm_w"][l]), ml_norm_w=r(p["ml_norm_w"][l]),
        lru_conv_w=p["lru_conv_w"][l], lru_conv_b=r(p["lru_conv_b"][l]),
        lru_wa=p["lru_wa"][l].astype(bf16), lru_wx=p["lru_wx"][l].astype(bf16),
        lru_ba=r(p["lru_ba"][l]), lru_bx=r(p["lru_bx"][l]), lru_lambda=r(p["lru_lambda"][l]),
        w_out=lane_pad(p["w_out"][l].astype(bf16)), mlp_up=lane_pad(p["mlp_up"][l].astype(bf16)),
        mlp_down=lane_pad(p["mlp_down"][l].astype(bf16)),
    )


def kernel(x_prompt, x_sample, c_prompt, c_sample, state_ssm, state_ssd_conv, state_mlstm_c, state_mlstm_n, state_mlstm_m, state_lru_h, state_lru_conv, ada_w, ada_b, norm1_w, norm2_w, w_in, ssd_conv_w, ssd_conv_b, ssd_dt_bias, ssd_a_log, ssd_d, ssd_norm_w, ml_i_bias, ml_f_bias, ml_norm_w, lru_conv_w, lru_conv_b, lru_wa, lru_ba, lru_wx, lru_bx, lru_lambda, w_out, mlp_up, mlp_down, final_norm_w):
    p = dict(norm1_w=norm1_w, norm2_w=norm2_w, w_in=w_in, ssd_conv_w=ssd_conv_w,
             ssd_conv_b=ssd_conv_b, ssd_dt_bias=ssd_dt_bias, ssd_a_log=ssd_a_log, ssd_d=ssd_d,
             ssd_norm_w=ssd_norm_w, ml_i_bias=ml_i_bias, ml_f_bias=ml_f_bias,
             ml_norm_w=ml_norm_w, lru_conv_w=lru_conv_w, lru_conv_b=lru_conv_b, lru_wa=lru_wa,
             lru_ba=lru_ba, lru_wx=lru_wx, lru_bx=lru_bx, lru_lambda=lru_lambda, w_out=w_out,
             mlp_up=mlp_up, mlp_down=mlp_down)
    layers = [_prep_layer(l, p) for l in range(N_LAYERS)]
    fnw = final_norm_w.reshape(1, D_MODEL)
    bp = x_prompt.shape[0]
    bs_ = x_sample.shape[0]

    mod = _ada(jnp.concatenate([c_prompt, c_sample], axis=0), ada_w, ada_b)
    mod = mod.reshape(N_LAYERS, bp + bs_, 6, D_MODEL)

    xp = _interleave_rows(x_prompt)
    p_states = []
    for l, w in enumerate(layers):
        mod_l = mod[l, :bp]
        xp, ssm, sconv, mc, mn, mm, lh, lconv = _player(xp, mod_l, w, fnw, l == N_LAYERS - 1)
        p_states.append((ssm.reshape(bp, SSD_HEADS, SSD_HEAD_DIM, SSD_STATE), sconv, mc, mn,
                         mm[:, :, 0], lh.reshape(bp, LRU_WIDTH), lconv))
    xp = _deinterleave_rows(xp)

    xs = x_sample
    st_in = (
        state_ssm.reshape(N_LAYERS, bs_, SSD_HEADS * SSD_HEAD_DIM, SSD_STATE),
        state_ssd_conv,
        state_mlstm_c.reshape(N_LAYERS, bs_, ML_HEADS * ML_HEAD_DIM, ML_HEAD_DIM),
        state_mlstm_n.reshape(N_LAYERS, bs_, 1, ML_WIDTH),
        jnp.repeat(state_mlstm_m, ML_HEAD_DIM, axis=-1).reshape(N_LAYERS, bs_, 1, ML_WIDTH),
        state_lru_h.reshape(N_LAYERS, bs_, 1, LRU_WIDTH),
        state_lru_conv,
    )
    st_out = [jnp.zeros(s.shape, f32) for s in st_in]
    for l, w in enumerate(layers):
        mod_l = mod[l, bp:]
        u = _inproj(xs, mod_l, w["norm1_w"], w["w_in_s"], 32, DEC_SEQ)
        ycat, *st_out = _smix(l, u, st_in, st_out, w, 8)
        xs = _outmlp(xs, ycat, mod_l, w["w_out"], w["norm2_w"], w["mlp_up"], w["mlp_down"], fnw,
                     64, DEC_SEQ, l == N_LAYERS - 1)
    ssm, sconv, mc, mn, mm, lh, lconv = st_out
    s_states = (ssm.reshape(N_LAYERS, bs_, SSD_HEADS, SSD_HEAD_DIM, SSD_STATE), sconv,
                mc.reshape(N_LAYERS, bs_, ML_HEADS, ML_HEAD_DIM, ML_HEAD_DIM),
                mn.reshape(N_LAYERS, bs_, ML_HEADS, ML_HEAD_DIM),
                mm.reshape(N_LAYERS, bs_, ML_HEADS, ML_HEAD_DIM)[..., 0],
                lh.reshape(N_LAYERS, bs_, LRU_WIDTH), lconv)

    stack = lambda sts: tuple(jnp.stack([s[i] for s in sts]) for i in range(7))
    return (xp, xs) + stack(p_states) + s_states
```

```python
import functools
import math

import jax
import jax.numpy as jnp
from jax import lax
from jax.experimental import pallas as pl
from jax.experimental.pallas import tpu as pltpu

f32 = jnp.float32
bf16 = jnp.bfloat16

D_MODEL = 1024
N_LAYERS = 2
MIX_WIDTH = 2 * D_MODEL
SSD_WIDTH = 1024
SSD_HEAD_DIM = 64
SSD_HEADS = 16
SSD_GROUPS = 2
SSD_STATE = 128
SSD_CONV_DIM = SSD_WIDTH + 2 * SSD_GROUPS * SSD_STATE
ML_WIDTH = 512
ML_HEADS = 4
ML_HEAD_DIM = 128
LRU_WIDTH = 512
LRU_BLOCKS = 4
LRU_BLOCK_DIM = 128
LRU_C = 8.0
CONV_W = 4
CHUNK = 128
D_FF = 4 * D_MODEL
EPS = 1e-6
PAST_LEN = 16384
DEC_SEQ = 8

_O_Z, _O_XBC, _O_DT, _O_Q, _O_O_END, _O_I, _O_F, _O_XR, _O_END = (
    0, 1024, 2560, 2576, 4624, 4624, 4628, 4632, 5656)

P_Z = 0
P_XBC = 1024
P_Q = 2560
P_K = 3072
P_V = 3584
P_O = 4096
P_XR = 4608
P_GR = 5120
P_MAIN = 5632
P_SMALL = 5632
NP_COLS = 5760
S_DT = 5632
S_I = 6656
S_F = 7168
NS_COLS = 7680

V7X_LANES = 128
V7X_SUBLANES = 8
VMEM_LIMIT = 56 * 1024 * 1024
PLAYER_VMEM_LIMIT = 60 * 1024 * 1024
ROW_TILES = CHUNK // V7X_SUBLANES
LOG2E = 1.4426950408889634

_NT = (((1,), (1,)), ((), ()))
_TN = (((0,), (0,)), ((), ()))


def _softplus(x):
    return jnp.maximum(x, 0.0) + jnp.log1p(jnp.exp(-jnp.abs(x)))


def _log_sigmoid(x):
    return -_softplus(-x)


def _sigmoid(x):
    return 1.0 / (1.0 + jnp.exp(-x))


def _silu(x):
    return x * _sigmoid(x)


def _gelu_tanh(x):
    c = math.sqrt(2.0 / math.pi)
    return 0.5 * x * (1.0 + jnp.tanh(c * (x + 0.044715 * (x * x * x))))


def _neg_expm1(x):
    return -jnp.tanh(0.5 * x) * (jnp.exp(x) + 1.0)


def _rms(x, w):
    return x * lax.rsqrt(jnp.mean(x * x, axis=-1, keepdims=True) + EPS) * w


def _interleave_rows(x):
    b, t, d = x.shape
    return x.reshape(b, t // CHUNK, V7X_SUBLANES, ROW_TILES, d).swapaxes(2, 3).reshape(b, t, d)


def _deinterleave_rows(x):
    b, t, d = x.shape
    return x.reshape(b, t // CHUNK, ROW_TILES, V7X_SUBLANES, d).swapaxes(2, 3).reshape(b, t, d)


def _full(shape):
    n = len(shape)
    return pl.BlockSpec(shape, lambda *_: (0,) * n)


def _resident(shape):
    n = len(shape)
    return pl.BlockSpec(shape, lambda *_: (0,) * n, pipeline_mode=pl.Buffered(1))


def _ada_kernel(c_ref, w_ref, b_ref, o_ref):
    cs = _silu(c_ref[...]).astype(bf16)
    o_ref[0] = jnp.dot(cs, w_ref[0].astype(bf16), preferred_element_type=f32) + b_ref[0]


def _ada(c_all, ada_w, ada_b):
    n = c_all.shape[0]
    tn = 1024
    return pl.pallas_call(
        _ada_kernel,
        out_shape=jax.ShapeDtypeStruct((N_LAYERS, n, 6 * D_MODEL), f32),
        grid=(N_LAYERS, 6 * D_MODEL // tn),
        in_specs=[
            pl.BlockSpec((n, D_MODEL), lambda l, j: (0, 0)),
            pl.BlockSpec((1, D_MODEL, tn), lambda l, j: (l, 0, j)),
            pl.BlockSpec((1, 1, tn), lambda l, j: (l, 0, j)),
        ],
        out_specs=pl.BlockSpec((1, n, tn), lambda l, j: (l, 0, j)),
        compiler_params=pltpu.CompilerParams(
            dimension_semantics=("arbitrary", "arbitrary"), vmem_limit_bytes=VMEM_LIMIT),
        name="ada_mod",
    )(c_all, ada_w, ada_b.reshape(N_LAYERS, 1, 6 * D_MODEL))


def _inproj_kernel(x_ref, mod_ref, nw_ref, w_ref, o_ref):
    x = x_ref[...]
    bb, tt, d = x.shape
    hn = _rms(x, nw_ref[...]) * (1.0 + mod_ref[:, 1:2, :]) + mod_ref[:, 0:1, :]
    hn = hn.reshape(bb * tt, d).astype(bf16)
    u = jnp.dot(hn, w_ref[...], preferred_element_type=f32)
    o_ref[...] = u.reshape(bb, tt, u.shape[-1])


def _inproj(x, mod, nw, w, bb, tt):
    b, t, d = x.shape
    n = w.shape[1]
    return pl.pallas_call(
        _inproj_kernel,
        out_shape=jax.ShapeDtypeStruct((b, t, n), f32),
        grid=(b // bb, t // tt),
        in_specs=[
            pl.BlockSpec((bb, tt, d), lambda i, j: (i, j, 0)),
            pl.BlockSpec((bb, 6, d), lambda i, j: (i, 0, 0)),
            _full((1, d)),
            _resident((d, n)),
        ],
        out_specs=pl.BlockSpec((bb, tt, n), lambda i, j: (i, j, 0)),
        compiler_params=pltpu.CompilerParams(
            dimension_semantics=("arbitrary", "arbitrary"), vmem_limit_bytes=VMEM_LIMIT),
        name="inproj",
    )(x, mod, nw, w)


def _outmlp_kernel(x_ref, y_ref, mod_ref, wout_ref, n2w_ref, up_ref, down_ref, fnw_ref, o_ref,
                   *, final):
    x = x_ref[...]
    bb, tt, d = x.shape
    m = bb * tt
    ycat = y_ref[...].reshape(m, MIX_WIDTH).astype(bf16)
    mix = jnp.dot(ycat, wout_ref[:, 0:d], preferred_element_type=f32).reshape(bb, tt, d)
    x1 = x + mod_ref[:, 2:3, :] * mix
    hn = _rms(x1, n2w_ref[...]) * (1.0 + mod_ref[:, 4:5, :]) + mod_ref[:, 3:4, :]
    hn = hn.reshape(m, d).astype(bf16)
    ff = jnp.zeros((m, d), f32)
    fc = 1024
    for c in range(D_FF // fc):
        h = jnp.dot(hn, up_ref[:, c * fc:(c + 1) * fc], preferred_element_type=f32)
        h = jnp.square(jnp.maximum(h, 0.0)).astype(bf16)
        ff = ff + jnp.dot(h, down_ref[c * fc:(c + 1) * fc, 0:D_MODEL],
                          preferred_element_type=f32)
    x2 = x1 + mod_ref[:, 5:6, :] * ff.reshape(bb, tt, d)
    if final:
        x2 = _rms(x2, fnw_ref[...])
    o_ref[...] = x2


def _outmlp(x, ycat, mod, wout, n2w, up, down, fnw, bb, tt, final):
    b, t, d = x.shape
    return pl.pallas_call(
        functools.partial(_outmlp_kernel, final=final),
        out_shape=jax.ShapeDtypeStruct((b, t, d), f32),
        grid=(b // bb, t // tt),
        in_specs=[
            pl.BlockSpec((bb, tt, d), lambda i, j: (i, j, 0)),
            pl.BlockSpec((bb, tt, MIX_WIDTH), lambda i, j: (i, j, 0)),
            pl.BlockSpec((bb, 6, d), lambda i, j: (i, 0, 0)),
            _resident(wout.shape),
            _full((1, d)),
            _resident(up.shape),
            _resident(down.shape),
            _full((1, d)),
        ],
        out_specs=pl.BlockSpec((bb, tt, d), lambda i, j: (i, j, 0)),
        compiler_params=pltpu.CompilerParams(
            dimension_semantics=("arbitrary", "arbitrary"), vmem_limit_bytes=VMEM_LIMIT),
        name="outproj_mlp",
    )(x, ycat, mod, wout, n2w, up, down, fnw)


PIECE = 512


def _stage_in(x, mod_ref, nw_ref, w_ref, u_ref):
    hn = _rms(x, nw_ref[...]) * (1.0 + mod_ref[0, 1:2, :]) + mod_ref[0, 0:1, :]
    hn = hn.astype(bf16)
    yield
    for c0 in range(0, NP_COLS, PIECE):
        c1 = min(c0 + PIECE, NP_COLS)
        u_ref[:, c0:c1] = jnp.dot(hn, w_ref[:, c0:c1], preferred_element_type=f32)
        yield


def _stage_out(x, y_ref, o_ref, rows, mod_ref, wout_ref, n2w_ref, up_ref, down_ref, fnw_ref,
               final):
    halves = [slice(h * PIECE, (h + 1) * PIECE) for h in range(D_MODEL // PIECE)]
    ycat = y_ref[...]
    mix = []
    for hs in halves:
        mix.append(jnp.dot(ycat, wout_ref[:, hs], preferred_element_type=f32))
        yield
    x1 = x + mod_ref[0, 2:3, :] * jnp.concatenate(mix, axis=1)
    hn = (_rms(x1, n2w_ref[...]) * (1.0 + mod_ref[0, 4:5, :]) + mod_ref[0, 3:4, :]).astype(bf16)
    yield
    ff = [jnp.zeros((x.shape[0], PIECE), f32) for _ in halves]
    fc = 2 * PIECE
    for c in range(D_FF // fc):
        h = []
        for k in range(fc // PIECE):
            cols = slice(c * fc + k * PIECE, c * fc + (k + 1) * PIECE)
            hk = jnp.dot(hn, up_ref[:, cols], preferred_element_type=f32)
            h.append(jnp.square(jnp.maximum(hk, 0.0)).astype(bf16))
            yield
        h = jnp.concatenate(h, axis=1)
        for i, hs in enumerate(halves):
            ff[i] = ff[i] + jnp.dot(h, down_ref[c * fc:(c + 1) * fc, hs],
                                    preferred_element_type=f32)
            yield
    x2 = x1 + mod_ref[0, 5:6, :] * jnp.concatenate(ff, axis=1)
    o_ref[0, rows, :] = _rms(x2, fnw_ref[...]) if final else x2
    yield


def _trace_interleaved(main, fillers, n_main, n_fill):
    fillers = list(fillers)
    due = 0.0
    nxt = 0
    main_live = True
    while main_live or fillers:
        if main_live:
            main_live = next(main, _DONE) is not _DONE
            due += n_fill / n_main
        else:
            due = float(len(fillers))
        while fillers and due >= 1.0:
            nxt %= len(fillers)
            if next(fillers[nxt], _DONE) is _DONE:
                fillers.pop(nxt)
            else:
                nxt += 1
                due -= 1.0


_DONE = object()


def _reset_state_if(first, state):
    for ref in state:
        ref[...] = jnp.where(first, 0.0, ref[...])


def _stage_mix(u_ref, y_ref, prm, state, ci, outs):
    (cw_ref, cb_ref, sbias_ref, alog_ref, dskip_ref, snw_ref, mnw_ref,
     lcw_ref, lcb_ref, wa_ref, wx_ref, ba_ref, bx_ref, lam_ref) = prm
    xbuf, lbuf, st, cst, nst, mst, hst = state
    L = CHUNK
    SUB = V7X_SUBLANES
    HIST = (CONV_W - 1) * SUB

    row = lax.broadcasted_iota(jnp.int32, (L, L), 0)
    col = lax.broadcasted_iota(jnp.int32, (L, L), 1)
    tok = lambda i: (i & (SUB - 1)) * ROW_TILES + (i >> 3)
    tri = tok(row) >= tok(col)
    lo = col < SSD_HEAD_DIM

    def bcl(a, j):
        return jnp.broadcast_to(a[:, j:j + 1], (L, L))

    def conv(u0, hist_ref, w_ref, b_ref, c0, c1):
        raw = u_ref[:, u0 + c0:u0 + c1]
        sub = lax.broadcasted_iota(jnp.int32, (SUB, c1 - c0), 0)
        prev = [pltpu.roll(jnp.where(sub == SUB - 1, hist_ref[k * SUB:(k + 1) * SUB, c0:c1],
                                     raw[L - HIST + k * SUB:L - HIST + (k + 1) * SUB, :]), 1, 0)
                for k in range(CONV_W - 1)]
        acc = b_ref[:, c0:c1] + raw * w_ref[CONV_W - 1:CONV_W, c0:c1]
        for d in range(1, CONV_W):
            shifted = jnp.concatenate(prev[CONV_W - 1 - d:] + [raw[0:L - d * SUB, :]], axis=0)
            acc = acc + shifted * w_ref[CONV_W - 1 - d:CONV_W - d, c0:c1]
        hist_ref[:, c0:c1] = raw[L - HIST:L, :]
        return acc

    cblk = 512
    xbc = []
    for c0 in range(0, SSD_CONV_DIM, cblk):
        xbc.append(_silu(conv(P_XBC, xbuf, cw_ref, cb_ref, c0, c0 + cblk)))
        yield
    xbc = jnp.concatenate(xbc, axis=1)
    xr = conv(P_XR, lbuf, lcw_ref, lcb_ref, 0, LRU_WIDTH)
    yield

    sm = u_ref[:,P_SMALL:P_SMALL + V7X_LANES] + sbias_ref[...]
    dt = _softplus(sm)
    a_row = -jnp.exp(alog_ref[...])
    gates = jnp.where(col < SSD_HEADS, dt * (a_row * LOG2E),
                      jnp.where((col >= 20) & (col < 24), _log_sigmoid(sm), 0.0))
    cum = jnp.dot(tri.astype(f32), gates, precision=lax.Precision.HIGHEST,
                  preferred_element_type=f32)
    cum_t = cum.T
    sm_t = sm.T
    yield

    ys = []
    for g in range(SSD_GROUPS):
        bm_t = xbc[:, SSD_WIDTH + g * SSD_STATE:SSD_WIDTH + (g + 1) * SSD_STATE].T.astype(bf16)
        cm = xbc[:, SSD_WIDTH + (SSD_GROUPS + g) * SSD_STATE:
                 SSD_WIDTH + (SSD_GROUPS + g + 1) * SSD_STATE].astype(bf16)
        cb = jnp.dot(cm, bm_t, preferred_element_type=f32)
        inter = jnp.dot(cm, st[g].astype(bf16), preferred_element_type=f32)
        yield
        for jj in range(4):
            jp = g * 4 + jj
            e0, e1 = 2 * jp, 2 * jp + 1
            c0 = bcl(cum, e0)
            c1 = bcl(cum, e1)
            cum_p = jnp.where(lo, c0, c1)
            dt_p = jnp.where(lo, bcl(dt, e0), bcl(dt, e1))
            xs_p = xbc[:, jp * L:(jp + 1) * L]
            xdt = xs_p * dt_p
            dec0 = jnp.exp2(jnp.where(tri, c0 - cum_t[e0:e0 + 1, :], -jnp.inf))
            dec1 = jnp.exp2(jnp.where(tri, c1 - cum_t[e1:e1 + 1, :], -jnp.inf))
            att0 = (cb * dec0).astype(bf16)
            att1 = (cb * dec1).astype(bf16)
            x_lo = jnp.where(lo, xdt, 0.0).astype(bf16)
            x_hi = jnp.where(lo, 0.0, xdt).astype(bf16)
            y_p = (jnp.dot(att0, x_lo, preferred_element_type=f32)
                   + jnp.dot(att1, x_hi, preferred_element_type=f32)
                   + inter[:, jj * L:(jj + 1) * L] * jnp.exp2(cum_p)
                   + xs_p * dskip_ref[:, jp * L:(jp + 1) * L])
            ys.append(y_p)
            c_last = cum_p[L - 1:L, :]
            xend = (xdt * jnp.exp2(c_last - cum_p)).astype(bf16)
            st[g, :, jj * L:(jj + 1) * L] = (
                jnp.exp2(c_last) * st[g, :, jj * L:(jj + 1) * L]
                + jnp.dot(bm_t, xend, preferred_element_type=f32))
            yield
    y = jnp.concatenate(ys, axis=1)
    z = u_ref[:,P_Z:P_Z + SSD_WIDTH]
    y_ref[:,0:SSD_WIDTH] = _rms(y * _silu(z), snw_ref[...]).astype(bf16)
    yield

    scale = ML_HEAD_DIM ** -0.5
    for h in range(ML_HEADS):
        sl = slice(h * ML_HEAD_DIM, (h + 1) * ML_HEAD_DIM)
        q = u_ref[:,P_Q + h * ML_HEAD_DIM:P_Q + (h + 1) * ML_HEAD_DIM]
        k = u_ref[:,P_K + h * ML_HEAD_DIM:P_K + (h + 1) * ML_HEAD_DIM] * scale
        v = u_ref[:,P_V + h * ML_HEAD_DIM:P_V + (h + 1) * ML_HEAD_DIM]
        o = u_ref[:,P_O + h * ML_HEAD_DIM:P_O + (h + 1) * ML_HEAD_DIM]
        qb = q.astype(bf16)
        kb = k.astype(bf16)
        jf = 20 + h
        ji = 16 + h
        b_col = cum[:, jf:jf + 1]
        i_col = sm[:, ji:ji + 1]
        dm = jnp.where(tri, bcl(cum, jf) - cum_t[jf:jf + 1, :] + sm_t[ji:ji + 1, :], -jnp.inf)
        m_prev = mst[h, :, 0:1]
        g_col = b_col + m_prev
        mt = jnp.maximum(g_col, jnp.max(dm, axis=-1, keepdims=True))
        w = jnp.exp(dm - mt) * lax.dot_general(qb, kb, _NT, preferred_element_type=f32)
        inter = jnp.exp(g_col - mt)
        c_old = cst[h]
        n_old = nst[h]
        num = (jnp.dot(w.astype(bf16), v.astype(bf16), preferred_element_type=f32)
               + inter * lax.dot_general(qb, c_old.astype(bf16), _NT, preferred_element_type=f32))
        den = (jnp.sum(w, axis=-1, keepdims=True)
               + inter * jnp.sum(q * n_old, axis=-1, keepdims=True))
        hout = num / jnp.maximum(jnp.abs(den), jnp.exp(-mt))
        m_new = mt[L - 1:L, :]
        b_last = b_col[L - 1:L, :]
        w_end = jnp.exp(b_last - b_col + i_col - m_new)
        dc = jnp.exp(b_last + m_prev - m_new)
        cst[h] = dc * c_old + jnp.dot((v * w_end).T.astype(bf16), kb,
                                      preferred_element_type=f32)
        nst[h] = dc * n_old + jnp.sum(w_end * k, axis=0, keepdims=True)
        mst[h] = jnp.broadcast_to(m_new, (1, V7X_LANES))
        yh = _rms(hout, mnw_ref[:, sl]) * _sigmoid(o)
        y_ref[:,SSD_WIDTH + h * ML_HEAD_DIM:SSD_WIDTH + (h + 1) * ML_HEAD_DIM] = yh.astype(bf16)
        yield

    rs, is_ = [], []
    for kb_ in range(LRU_BLOCKS):
        xk = xr[:, kb_ * LRU_BLOCK_DIM:(kb_ + 1) * LRU_BLOCK_DIM].astype(bf16)
        rs.append(jnp.dot(xk, wa_ref[kb_], preferred_element_type=f32))
        is_.append(jnp.dot(xk, wx_ref[kb_], preferred_element_type=f32))
    r = _sigmoid(jnp.concatenate(rs, axis=1) + ba_ref[...])
    i_g = _sigmoid(jnp.concatenate(is_, axis=1) + bx_ref[...])
    yield
    log_a = -LRU_C * r * _softplus(-lam_ref[...])
    mult = jnp.sqrt(_neg_expm1(2.0 * log_a))
    rowl = lax.broadcasted_iota(jnp.int32, (L, LRU_WIDTH), 0)
    first_row = jnp.where(ci == 0, 0, -1)
    mult = jnp.where(rowl == first_row, 1.0, mult)
    a_s = jnp.exp(log_a)
    u_s = mult * i_g * xr
    yield
    tile = lambda a, v: a[v * SUB:(v + 1) * SUB, :]
    h_loc = [tile(u_s, 0)]
    a_cum = [tile(a_s, 0)]
    for v in range(1, ROW_TILES):
        h_loc.append(tile(a_s, v) * h_loc[-1] + tile(u_s, v))
        a_cum.append(tile(a_s, v) * a_cum[-1])
    a_e, h_e = a_cum[-1], h_loc[-1]
    subl = lax.broadcasted_iota(jnp.int32, (SUB, LRU_WIDTH), 0)
    k_ = 1
    while k_ < SUB:
        keep = subl >= k_
        a_sh = jnp.where(keep, pltpu.roll(a_e, k_, 0), 1.0)
        h_sh = jnp.where(keep, pltpu.roll(h_e, k_, 0), 0.0)
        h_e = a_e * h_sh + h_e
        a_e = a_e * a_sh
        k_ *= 2
    h0 = hst[...]
    run_end = h_e + a_e * h0
    run_in = jnp.where(subl == 0, h0, pltpu.roll(run_end, 1, 0))
    hr = jnp.concatenate([h_loc[v] + a_cum[v] * run_in for v in range(ROW_TILES)], axis=0)
    hst[...] = run_end[SUB - 1:SUB, :]
    gr = u_ref[:,P_GR:P_GR + LRU_WIDTH]
    y_ref[:,SSD_WIDTH + ML_WIDTH:MIX_WIDTH] = (hr * _gelu_tanh(gr)).astype(bf16)
    yield

    if outs is not None:
        ssm_ref, sconv_ref, mc_ref, mn_ref, mm_ref, lh_ref, lconv_ref = outs
        for g in range(SSD_GROUPS):
            for jj in range(4):
                ssm_ref[0, g * 4 + jj] = st[g, :, jj * L:(jj + 1) * L].T
        for k in range(CONV_W - 1):
            sconv_ref[0, k:k + 1, :] = xbuf[k * SUB + SUB - 1:(k + 1) * SUB, :]
            lconv_ref[0, k:k + 1, :] = lbuf[k * SUB + SUB - 1:(k + 1) * SUB, :]
        for h in range(ML_HEADS):
            mc_ref[0, h] = cst[h]
            mn_ref[0, h:h + 1, :] = nst[h]
            mm_ref[0, h:h + 1, :] = mst[h]
        lh_ref[0] = hst[...]


N_MIX_PARAMS = 14
N_STATE_OUTS = 7
N_MIX_PHASES = 24
N_FILL_PIECES = 33


def _player_kernel(*refs, final, n_chunks, nc):
    xa_ref, moda_ref, xc_ref, modc_ref, n1w_ref, win_ref = refs[:6]
    prm = refs[6:6 + N_MIX_PARAMS]
    wout_ref, n2w_ref, up_ref, down_ref, fnw_ref = refs[6 + N_MIX_PARAMS:11 + N_MIX_PARAMS]
    o_ref = refs[11 + N_MIX_PARAMS]
    outs = refs[12 + N_MIX_PARAMS:12 + N_MIX_PARAMS + N_STATE_OUTS]
    u_bufs = refs[-11:-9]
    y_bufs = refs[-9:-7]
    state = refs[-7:]
    k = pl.program_id(0)

    @pl.when(k == 0)
    def _():
        u_bufs[1][...] = jnp.zeros(u_bufs[1].shape, f32)
        y_bufs[1][...] = jnp.zeros(y_bufs[1].shape, bf16)
        for ref in state:
            ref[...] = jnp.zeros(ref.shape, f32)

    for sub in range(2):
        rows = slice(sub * CHUNK, (sub + 1) * CHUNK)
        stage_in = _stage_in(xa_ref[0, rows, :], moda_ref, n1w_ref, win_ref, u_bufs[sub])
        jb = jnp.clip(2 * k + sub - 1, 0, n_chunks - 1)
        ci = jb % nc
        if sub == 1:
            _reset_state_if(ci == 0, state)
        stage_mix = _stage_mix(u_bufs[1 - sub], y_bufs[sub], prm, state, ci,
                               outs if sub == 0 else None)
        stage_out = _stage_out(xc_ref[0, rows, :], y_bufs[1 - sub], o_ref, rows, modc_ref,
                               wout_ref, n2w_ref, up_ref, down_ref, fnw_ref, final)
        _trace_interleaved(stage_mix, [stage_in, stage_out], N_MIX_PHASES, N_FILL_PIECES)


def _player(x, mod, w, fnw, final):
    b, t, d = x.shape
    nc = t // CHUNK
    n_chunks = b * nc
    n_pairs = n_chunks // 2
    ppr = nc // 2
    row = lambda n: _full((1, n))
    pair_a = lambda k: jnp.minimum(k, n_pairs - 1)
    pair_c = lambda k: jnp.clip(k - 1, 0, n_pairs - 1)
    seq_b = lambda k: jnp.clip(2 * k - 1, 0, n_chunks - 1) // nc
    xspec = lambda pair: pl.BlockSpec((1, 2 * CHUNK, d), lambda k: (pair(k) // ppr, pair(k) % ppr, 0))
    mspec = lambda pair: pl.BlockSpec((1, 6, d), lambda k: (pair(k) // ppr, 0, 0))
    outs = [
        jax.ShapeDtypeStruct((b, t, d), f32),
        jax.ShapeDtypeStruct((b, 8, 128, 128), f32),
        jax.ShapeDtypeStruct((b, CONV_W - 1, SSD_CONV_DIM), f32),
        jax.ShapeDtypeStruct((b, ML_HEADS, ML_HEAD_DIM, ML_HEAD_DIM), f32),
        jax.ShapeDtypeStruct((b, ML_HEADS, ML_HEAD_DIM), f32),
        jax.ShapeDtypeStruct((b, ML_HEADS, V7X_LANES), f32),
        jax.ShapeDtypeStruct((b, 1, LRU_WIDTH), f32),
        jax.ShapeDtypeStruct((b, CONV_W - 1, LRU_WIDTH), f32),
    ]
    st_spec = lambda s: pl.BlockSpec((1,) + s[1:], lambda k: (seq_b(k),) + (0,) * (len(s) - 1))
    mix_params = [w["ssd_conv_w"], w["ssd_conv_b"], w["p_small_bias"], w["p_alog"], w["dskip"],
                  w["ssd_norm_w"], w["ml_norm_w"], w["lru_conv_w"], w["lru_conv_b"], w["lru_wa"],
                  w["lru_wx"], w["lru_ba"], w["lru_bx"], w["lru_lambda"]]
    assert len(mix_params) == N_MIX_PARAMS and len(outs) == 1 + N_STATE_OUTS
    return pl.pallas_call(
        functools.partial(_player_kernel, final=final, n_chunks=n_chunks, nc=nc),
        out_shape=outs,
        grid=(n_pairs + 1,),
        in_specs=[xspec(pair_a), mspec(pair_a), xspec(pair_c), mspec(pair_c),
                  row(d), _resident((d, NP_COLS))]
        + [_full(a.shape) for a in mix_params]
        + [_resident(w["w_out"].shape), row(d), _resident(w["mlp_up"].shape),
           _resident(w["mlp_down"].shape), row(d)],
        out_specs=[xspec(pair_c)] + [st_spec(o.shape) for o in outs[1:]],
        scratch_shapes=[
            pltpu.VMEM((CHUNK, NP_COLS), f32), pltpu.VMEM((CHUNK, NP_COLS), f32),
            pltpu.VMEM((CHUNK, MIX_WIDTH), bf16), pltpu.VMEM((CHUNK, MIX_WIDTH), bf16),
            pltpu.VMEM(((CONV_W - 1) * V7X_SUBLANES, SSD_CONV_DIM), f32),
            pltpu.VMEM(((CONV_W - 1) * V7X_SUBLANES, LRU_WIDTH), f32),
            pltpu.VMEM((SSD_GROUPS, SSD_STATE, 512), f32),
            pltpu.VMEM((ML_HEADS, ML_HEAD_DIM, ML_HEAD_DIM), f32),
            pltpu.VMEM((ML_HEADS, 1, ML_HEAD_DIM), f32),
            pltpu.VMEM((ML_HEADS, 1, V7X_LANES), f32),
            pltpu.VMEM((1, LRU_WIDTH), f32),
        ],
        compiler_params=pltpu.CompilerParams(
            dimension_semantics=("arbitrary",), vmem_limit_bytes=PLAYER_VMEM_LIMIT),
        name="prompt_layer",
    )(x, mod, x, mod, w["norm1_w"], w["w_in_p"], *mix_params,
      w["w_out"], w["norm2_w"], w["mlp_up"], w["mlp_down"], fnw)


def _smix_kernel(u_ref, s0_ref, sconv0_ref, c0_ref, n0_ref, m0_ref, lh0_ref, lconv0_ref,
                 cw_ref, cb_ref, dtb_ref, alog_ref, dskip_ref, snw_ref, ib_ref, fb_ref, mnw_ref,
                 lcw_ref, lcb_ref, wa_ref, wx_ref, ba_ref, bx_ref, lam_ref,
                 y_ref, s_ref, sconv_ref, c_ref, n_ref, m_ref, lh_ref, lconv_ref,
                 xbuf, lbuf):
    bb = u_ref.shape[0]
    T = DEC_SEQ
    HEAD = V7X_SUBLANES

    def bs(a, s):
        return jnp.broadcast_to(a[:, s:s + 1, :], a.shape)

    def tio(n):
        return lax.broadcasted_iota(jnp.int32, (bb, T, n), 1)

    def lio(n):
        return lax.broadcasted_iota(jnp.int32, (bb, T, n), 2)

    def cumsum_t(a):
        t = tio(a.shape[-1])
        acc = jnp.zeros_like(a)
        for s in range(T):
            acc = acc + jnp.where(t >= s, bs(a, s), 0.0)
        return acc

    def conv(buf, raw, hist, w_ref, b_ref, n):
        buf[:, HEAD - 3:HEAD, :] = hist
        buf[:, HEAD:HEAD + T, :] = raw
        acc = b_ref[...] + buf[:, HEAD - 3:HEAD - 3 + T, :] * w_ref[0:1, :]
        for j in range(1, CONV_W):
            acc = acc + buf[:, HEAD - 3 + j:HEAD - 3 + j + T, :] * w_ref[j:j + 1, :]
        return acc, buf[:, HEAD + T - 3:HEAD + T, :]

    xbc, sconv_new = conv(xbuf, u_ref[:, :, P_XBC:P_XBC + SSD_CONV_DIM], sconv0_ref[...],
                          cw_ref, cb_ref, SSD_CONV_DIM)
    sconv_ref[...] = sconv_new
    xbc = _silu(xbc)
    xr, lconv_new = conv(lbuf, u_ref[:, :, P_XR:P_XR + LRU_WIDTH], lconv0_ref[...],
                         lcw_ref, lcb_ref, LRU_WIDTH)
    lconv_ref[...] = lconv_new

    xs = xbc[:, :, 0:SSD_WIDTH]
    t_w = tio(SSD_WIDTH)
    l_w = lio(SSD_WIDTH)
    dt = _softplus(u_ref[:, :, S_DT:S_DT + SSD_WIDTH] + dtb_ref[...])
    cum = cumsum_t(dt * (-jnp.exp(alog_ref[...])))
    xdt = xs * dt
    bms = [xbc[:, :, SSD_WIDTH + g * SSD_STATE:SSD_WIDTH + (g + 1) * SSD_STATE]
           for g in range(SSD_GROUPS)]
    cms = [xbc[:, :, SSD_WIDTH + (SSD_GROUPS + g) * SSD_STATE:
               SSD_WIDTH + (SSD_GROUPS + g + 1) * SSD_STATE] for g in range(SSD_GROUPS)]
    y = xs * dskip_ref[...]
    for s in range(T):
        cb0 = jnp.sum(cms[0] * bs(bms[0], s), axis=-1, keepdims=True)
        cb1 = jnp.sum(cms[1] * bs(bms[1], s), axis=-1, keepdims=True)
        cbs = jnp.where(l_w < 512, cb0, cb1)
        dec = jnp.exp(jnp.where(t_w >= s, cum - bs(cum, s), -jnp.inf))
        y = y + cbs * dec * bs(xdt, s)
    inter = jnp.concatenate(
        [jnp.einsum('btn,bpn->btp', cms[g].astype(bf16),
                    s0_ref[:, g * 512:(g + 1) * 512, :].astype(bf16),
                    preferred_element_type=f32) for g in range(SSD_GROUPS)], axis=-1)
    y = y + inter * jnp.exp(cum)
    z = u_ref[:, :, P_Z:P_Z + SSD_WIDTH]
    y_ref[:, :, 0:SSD_WIDTH] = _rms(y * _silu(z), snw_ref[...])

    def split3(d, t_idx):
        hi = d.astype(bf16).astype(f32)
        r1 = d - hi
        mid = r1.astype(bf16).astype(f32)
        lo_ = (r1 - mid).astype(bf16).astype(f32)
        return jnp.where(t_idx == 0, hi, jnp.where(t_idx == 1, mid, jnp.where(t_idx == 2, lo_, 0.0)))

    def ones_rhs(kmat):
        zeros = jnp.zeros_like(kmat)
        return jnp.concatenate(
            [jnp.concatenate([kmat, zeros], axis=2),
             jnp.concatenate([zeros, jnp.ones_like(kmat)], axis=2)], axis=1).astype(bf16)

    c_last = bs(cum, T - 1)
    xend = xdt * jnp.exp(c_last - cum)
    lhs = jnp.concatenate([xend, split3(jnp.exp(c_last), t_w)], axis=1).astype(bf16)
    for g in range(SSD_GROUPS):
        zz = jnp.einsum('bkp,bkn->bpn', lhs[:, :, g * 512:(g + 1) * 512],
                        ones_rhs(bms[g]), preferred_element_type=f32)
        s_ref[:, g * 512:(g + 1) * 512, :] = (
            zz[:, :, SSD_STATE:] * s0_ref[:, g * 512:(g + 1) * 512, :] + zz[:, :, :SSD_STATE])

    t_m = tio(ML_WIDTH)
    q = u_ref[:, :, P_Q:P_Q + ML_WIDTH]
    k = u_ref[:, :, P_K:P_K + ML_WIDTH] * (ML_HEAD_DIM ** -0.5)
    v = u_ref[:, :, P_V:P_V + ML_WIDTH]
    o = u_ref[:, :, P_O:P_O + ML_WIDTH]
    ic = u_ref[:, :, S_I:S_I + ML_WIDTH] + ib_ref[...]
    fc = _log_sigmoid(u_ref[:, :, S_F:S_F + ML_WIDTH] + fb_ref[...])
    bc = cumsum_t(fc)
    m0 = m0_ref[...]
    g_ = bc + m0

    def headsum(a):
        return jnp.concatenate(
            [jnp.broadcast_to(jnp.sum(a[:, :, h * ML_HEAD_DIM:(h + 1) * ML_HEAD_DIM], axis=-1,
                                      keepdims=True), (bb, T, ML_HEAD_DIM))
             for h in range(ML_HEADS)], axis=-1)

    dms = [jnp.where(t_m >= s, bc - bs(bc, s) + bs(ic, s), -jnp.inf) for s in range(T)]
    mt = g_
    for s in range(T):
        mt = jnp.maximum(mt, dms[s])
    num = jnp.zeros((bb, T, ML_WIDTH), f32)
    den = jnp.zeros((bb, T, ML_WIDTH), f32)
    for s in range(T):
        w = jnp.exp(dms[s] - mt) * headsum(q * bs(k, s))
        num = num + w * bs(v, s)
        den = den + w
    inter_m = jnp.exp(g_ - mt)
    qb = q.astype(bf16)
    qc = jnp.concatenate(
        [jnp.einsum('btk,bvk->btv', qb[:, :, h * ML_HEAD_DIM:(h + 1) * ML_HEAD_DIM],
                    c0_ref[:, h * ML_HEAD_DIM:(h + 1) * ML_HEAD_DIM, :].astype(bf16),
                    preferred_element_type=f32) for h in range(ML_HEADS)], axis=-1)
    n0 = n0_ref[...]
    num = num + inter_m * qc
    den = den + inter_m * headsum(q * n0)
    hout = num / jnp.maximum(jnp.abs(den), jnp.exp(-mt))
    m_new = bs(mt, T - 1)
    b_last = bs(bc, T - 1)
    w_end = jnp.exp(b_last - bc + ic - m_new)
    dc = jnp.exp(b_last + m0 - m_new)
    m_ref[...] = mt[:, T - 1:T, :]
    n_ref[...] = dc[:, 0:1, :] * n0 + jnp.sum(w_end * k, axis=1, keepdims=True)
    lhs_m = jnp.concatenate([v * w_end, split3(dc, t_m)], axis=1).astype(bf16)
    for h in range(ML_HEADS):
        sl = slice(h * ML_HEAD_DIM, (h + 1) * ML_HEAD_DIM)
        zz = jnp.einsum('bkv,bkn->bvn', lhs_m[:, :, sl], ones_rhs(k[:, :, sl]),
                        preferred_element_type=f32)
        c_ref[:, sl, :] = zz[:, :, ML_HEAD_DIM:] * c0_ref[:, sl, :] + zz[:, :, :ML_HEAD_DIM]
    hn2 = headsum(hout * hout) * (1.0 / ML_HEAD_DIM)
    y_ml = hout * lax.rsqrt(hn2 + EPS) * mnw_ref[...] * _sigmoid(o)
    y_ref[:, :, SSD_WIDTH:SSD_WIDTH + ML_WIDTH] = y_ml

    xr2 = xr.reshape(bb * T, LRU_WIDTH)
    rs, is_ = [], []
    for kk in range(LRU_BLOCKS):
        xk = xr2[:, kk * LRU_BLOCK_DIM:(kk + 1) * LRU_BLOCK_DIM].astype(bf16)
        rs.append(jnp.dot(xk, wa_ref[kk], preferred_element_type=f32))
        is_.append(jnp.dot(xk, wx_ref[kk], preferred_element_type=f32))
    r = _sigmoid(jnp.concatenate(rs, axis=1).reshape(bb, T, LRU_WIDTH) + ba_ref[...])
    i_g = _sigmoid(jnp.concatenate(is_, axis=1).reshape(bb, T, LRU_WIDTH) + bx_ref[...])
    log_a = -LRU_C * r * _softplus(-lam_ref[...])
    a_s = jnp.exp(log_a)
    u_s = jnp.sqrt(_neg_expm1(2.0 * log_a)) * i_g * xr
    t_l = tio(LRU_WIDTH)
    cur = lh0_ref[...]
    hr = jnp.zeros((bb, T, LRU_WIDTH), f32)
    for s in range(T):
        cur = a_s[:, s:s + 1, :] * cur + u_s[:, s:s + 1, :]
        hr = jnp.where(t_l == s, jnp.broadcast_to(cur, hr.shape), hr)
    lh_ref[...] = cur
    gr = u_ref[:, :, P_GR:P_GR + LRU_WIDTH]
    y_ref[:, :, SSD_WIDTH + ML_WIDTH:MIX_WIDTH] = hr * _gelu_tanh(gr)


def _smix(l, u, states, prev, w, bb):
    b = u.shape[0]
    row = lambda n: _full((1, n))
    lblk = lambda s: pl.BlockSpec((None, bb) + s[2:], lambda i: (l, i) + (0,) * (len(s) - 2))
    ublk = lambda s: pl.BlockSpec((bb,) + s[1:], lambda i: (i,) + (0,) * (len(s) - 1))
    y_shape = (b, DEC_SEQ, MIX_WIDTH)
    outs = [jax.ShapeDtypeStruct(y_shape, f32)]
    outs += [jax.ShapeDtypeStruct(s.shape, f32) for s in states]
    weights = [w["ssd_conv_w"], w["ssd_conv_b"], w["s_dt_bias"], w["s_alog"], w["dskip"],
               w["ssd_norm_w"], w["s_i_bias"], w["s_f_bias"], w["ml_norm_w"],
               w["lru_conv_w"], w["lru_conv_b"], w["lru_wa"], w["lru_wx"], w["lru_ba"],
               w["lru_bx"], w["lru_lambda"]]
    n_in = 1 + len(states) + len(weights)
    prev = list(prev)

    def body(*refs):
        _smix_kernel(*refs[:n_in], *refs[n_in + len(prev):])

    return pl.pallas_call(
        body,
        out_shape=outs,
        grid=(b // bb,),
        in_specs=[ublk(u.shape)] + [lblk(s.shape) for s in states]
        + [_full(a.shape) for a in weights]
        + [pl.BlockSpec(memory_space=pl.ANY)] * len(prev),
        out_specs=[ublk(y_shape)] + [lblk(s.shape) for s in states],
        input_output_aliases={n_in + k: 1 + k for k in range(len(prev))},
        scratch_shapes=[
            pltpu.VMEM((bb, 2 * V7X_SUBLANES, SSD_CONV_DIM), f32),
            pltpu.VMEM((bb, 2 * V7X_SUBLANES, LRU_WIDTH), f32),
        ],
        compiler_params=pltpu.CompilerParams(
            dimension_semantics=("arbitrary",), vmem_limit_bytes=VMEM_LIMIT),
        name="sample_mix",
    )(u, *states, *weights, *prev)


def _prep_layer(l, p):
    w_in = p["w_in"][l].astype(bf16)
    main = jnp.concatenate([w_in[:, 0:_O_DT], w_in[:, _O_Q:_O_O_END], w_in[:, _O_XR:_O_END]], axis=1)
    dt_c = w_in[:, _O_DT:_O_Q]
    i_c = w_in[:, _O_I:_O_F]
    f_c = w_in[:, _O_F:_O_XR]
    pad = jnp.zeros((D_MODEL, V7X_LANES - SSD_HEADS - 2 * ML_HEADS), bf16)
    w_p = jnp.concatenate([main, dt_c, i_c, f_c, pad], axis=1)
    w_s = jnp.concatenate([main, jnp.repeat(dt_c, SSD_HEAD_DIM, axis=1),
                           jnp.repeat(i_c, ML_HEAD_DIM, axis=1),
                           jnp.repeat(f_c, ML_HEAD_DIM, axis=1)], axis=1)
    zpad = jnp.zeros((V7X_LANES - SSD_HEADS - 2 * ML_HEADS,), f32)
    r = lambda a: a.reshape(1, -1)
    lane_pad = lambda a: jnp.pad(a, ((0, 0), (0, V7X_LANES)))
    return dict(
        w_in_p=w_p, w_in_s=w_s,
        norm1_w=r(p["norm1_w"][l]), norm2_w=r(p["norm2_w"][l]),
        ssd_conv_w=p["ssd_conv_w"][l], ssd_conv_b=r(p["ssd_conv_b"][l]),
        p_small_bias=r(jnp.concatenate([p["ssd_dt_bias"][l], p["ml_i_bias"][l],
                                        p["ml_f_bias"][l], zpad])),
        p_alog=r(jnp.concatenate([p["ssd_a_log"][l], jnp.zeros((V7X_LANES - SSD_HEADS,), f32)])),
        s_dt_bias=r(jnp.repeat(p["ssd_dt_bias"][l], SSD_HEAD_DIM)),
        s_alog=r(jnp.repeat(p["ssd_a_log"][l], SSD_HEAD_DIM)),
        s_i_bias=r(jnp.repeat(p["ml_i_bias"][l], ML_HEAD_DIM)),
        s_f_bias=r(jnp.repeat(p["ml_f_bias"][l], ML_HEAD_DIM)),
        dskip=r(jnp.repeat(p["ssd_d"][l], SSD_HEAD_DIM)),
        ssd_norm_w=r(p["ssd_norm_w"][l]), ml_norm_w=r(p["ml_norm_w"][l]),
        lru_conv_w=p["lru_conv_w"][l], lru_conv_b=r(p["lru_conv_b"][l]),
        lru_wa=p["lru_wa"][l].astype(bf16), lru_wx=p["lru_wx"][l].astype(bf16),
        lru_ba=r(p["lru_ba"][l]), lru_bx=r(p["lru_bx"][l]), lru_lambda=r(p["lru_lambda"][l]),
        w_out=lane_pad(p["w_out"][l].astype(bf16)), mlp_up=lane_pad(p["mlp_up"][l].astype(bf16)),
        mlp_down=lane_pad(p["mlp_down"][l].astype(bf16)),
    )


def kernel(x_prompt, x_sample, c_prompt, c_sample, state_ssm, state_ssd_conv, state_mlstm_c, state_mlstm_n, state_mlstm_m, state_lru_h, state_lru_conv, ada_w, ada_b, norm1_w, norm2_w, w_in, ssd_conv_w, ssd_conv_b, ssd_dt_bias, ssd_a_log, ssd_d, ssd_norm_w, ml_i_bias, ml_f_bias, ml_norm_w, lru_conv_w, lru_conv_b, lru_wa, lru_ba, lru_wx, lru_bx, lru_lambda, w_out, mlp_up, mlp_down, final_norm_w):
    p = dict(norm1_w=norm1_w, norm2_w=norm2_w, w_in=w_in, ssd_conv_w=ssd_conv_w,
             ssd_conv_b=ssd_conv_b, ssd_dt_bias=ssd_dt_bias, ssd_a_log=ssd_a_log, ssd_d=ssd_d,
             ssd_norm_w=ssd_norm_w, ml_i_bias=ml_i_bias, ml_f_bias=ml_f_bias,
             ml_norm_w=ml_norm_w, lru_conv_w=lru_conv_w, lru_conv_b=lru_conv_b, lru_wa=lru_wa,
             lru_ba=lru_ba, lru_wx=lru_wx, lru_bx=lru_bx, lru_lambda=lru_lambda, w_out=w_out,
             mlp_up=mlp_up, mlp_down=mlp_down)
    layers = [_prep_layer(l, p) for l in range(N_LAYERS)]
    fnw = final_norm_w.reshape(1, D_MODEL)
    bp = x_prompt.shape[0]
    bs_ = x_sample.shape[0]

    mod = _ada(jnp.concatenate([c_prompt, c_sample], axis=0), ada_w, ada_b)
    mod = mod.reshape(N_LAYERS, bp + bs_, 6, D_MODEL)

    xp = _interleave_rows(x_prompt)
    p_states = []
    for l, w in enumerate(layers):
        mod_l = mod[l, :bp]
        xp, ssm, sconv, mc, mn, mm, lh, lconv = _player(xp, mod_l, w, fnw, l == N_LAYERS - 1)
        p_states.append((ssm.reshape(bp, SSD_HEADS, SSD_HEAD_DIM, SSD_STATE), sconv, mc, mn,
                         mm[:, :, 0], lh.reshape(bp, LRU_WIDTH), lconv))
    xp = _deinterleave_rows(xp)

    xs = x_sample
    st_in = (
        state_ssm.reshape(N_LAYERS, bs_, SSD_HEADS * SSD_HEAD_DIM, SSD_STATE),
        state_ssd_conv,
        state_mlstm_c.reshape(N_LAYERS, bs_, ML_HEADS * ML_HEAD_DIM, ML_HEAD_DIM),
        state_mlstm_n.reshape(N_LAYERS, bs_, 1, ML_WIDTH),
        jnp.repeat(state_mlstm_m, ML_HEAD_DIM, axis=-1).reshape(N_LAYERS, bs_, 1, ML_WIDTH),
        state_lru_h.reshape(N_LAYERS, bs_, 1, LRU_WIDTH),
        state_lru_conv,
    )
    st_out = [jnp.zeros(s.shape, f32) for s in st_in]
    for l, w in enumerate(layers):
        mod_l = mod[l, bp:]
        u = _inproj(xs, mod_l, w["norm1_w"], w["w_in_s"], 32, DEC_SEQ)
        ycat, *st_out = _smix(l, u, st_in, st_out, w, 8)
        xs = _outmlp(xs, ycat, mod_l, w["w_out"], w["norm2_w"], w["mlp_up"], w["mlp_down"], fnw,
                     64, DEC_SEQ, l == N_LAYERS - 1)
    ssm, sconv, mc, mn, mm, lh, lconv = st_out
    s_states = (ssm.reshape(N_LAYERS, bs_, SSD_HEADS, SSD_HEAD_DIM, SSD_STATE), sconv,
                mc.reshape(N_LAYERS, bs_, ML_HEADS, ML_HEAD_DIM, ML_HEAD_DIM),
                mn.reshape(N_LAYERS, bs_, ML_HEADS, ML_HEAD_DIM),
                mm.reshape(N_LAYERS, bs_, ML_HEADS, ML_HEAD_DIM)[..., 0],
                lh.reshape(N_LAYERS, bs_, LRU_WIDTH), lconv)

    stack = lambda sts: tuple(jnp.stack([s[i] for s in sts]) for i in range(7))
    return (xp, xs) + stack(p_states) + s_states
```

```python
import functools
import math

import jax
import jax.numpy as jnp
from jax import lax
from jax.experimental import pallas as pl
from jax.experimental.pallas import tpu as pltpu

f32 = jnp.float32
bf16 = jnp.bfloat16

D_MODEL = 1024
N_LAYERS = 2
MIX_WIDTH = 2 * D_MODEL
SSD_WIDTH = 1024
SSD_HEAD_DIM = 64
SSD_HEADS = 16
SSD_GROUPS = 2
SSD_STATE = 128
SSD_CONV_DIM = SSD_WIDTH + 2 * SSD_GROUPS * SSD_STATE
ML_WIDTH = 512
ML_HEADS = 4
ML_HEAD_DIM = 128
LRU_WIDTH = 512
LRU_BLOCKS = 4
LRU_BLOCK_DIM = 128
LRU_C = 8.0
CONV_W = 4
CHUNK = 128
D_FF = 4 * D_MODEL
EPS = 1e-6
PAST_LEN = 16384
DEC_SEQ = 8

_O_Z, _O_XBC, _O_DT, _O_Q, _O_O_END, _O_I, _O_F, _O_XR, _O_END = (
    0, 1024, 2560, 2576, 4624, 4624, 4628, 4632, 5656)

P_Z = 0
P_XBC = 1024
P_Q = 2560
P_K = 3072
P_V = 3584
P_O = 4096
P_XR = 4608
P_GR = 5120
P_MAIN = 5632
P_SMALL = 5632
NP_COLS = 5760
S_DT = 5632
S_I = 6656
S_F = 7168
NS_COLS = 7680

V7X_LANES = 128
V7X_SUBLANES = 8
VMEM_LIMIT = 56 * 1024 * 1024
PLAYER_VMEM_LIMIT = 60 * 1024 * 1024
ROW_TILES = CHUNK // V7X_SUBLANES
LOG2E = 1.4426950408889634

_NT = (((1,), (1,)), ((), ()))
_TN = (((0,), (0,)), ((), ()))


def _softplus(x):
    return jnp.maximum(x, 0.0) + jnp.log1p(jnp.exp(-jnp.abs(x)))


def _log_sigmoid(x):
    return -_softplus(-x)


def _sigmoid(x):
    return 1.0 / (1.0 + jnp.exp(-x))


def _silu(x):
    return x * _sigmoid(x)


def _gelu_tanh(x):
    c = math.sqrt(2.0 / math.pi)
    return 0.5 * x * (1.0 + jnp.tanh(c * (x + 0.044715 * (x * x * x))))


def _neg_expm1(x):
    return -jnp.tanh(0.5 * x) * (jnp.exp(x) + 1.0)


def _rms(x, w):
    return x * lax.rsqrt(jnp.mean(x * x, axis=-1, keepdims=True) + EPS) * w


def _interleave_rows(x):
    b, t, d = x.shape
    return x.reshape(b, t // CHUNK, V7X_SUBLANES, ROW_TILES, d).swapaxes(2, 3).reshape(b, t, d)


def _deinterleave_rows(x):
    b, t, d = x.shape
    return x.reshape(b, t // CHUNK, ROW_TILES, V7X_SUBLANES, d).swapaxes(2, 3).reshape(b, t, d)


def _full(shape):
    n = len(shape)
    return pl.BlockSpec(shape, lambda *_: (0,) * n)


def _layer(a, l, resident=False):
    nd = a.ndim - 1
    kw = dict(pipeline_mode=pl.Buffered(1)) if resident else {}
    return pl.BlockSpec((None,) + a.shape[1:], lambda *_: (l,) + (0,) * nd, **kw)


def _ada_kernel(c_ref, w_ref, b_ref, o_ref):
    cs = _silu(c_ref[...]).astype(bf16)
    o_ref[0] = jnp.dot(cs, w_ref[0].astype(bf16), preferred_element_type=f32) + b_ref[0]


def _ada(c_all, ada_w, ada_b):
    n = c_all.shape[0]
    tn = 1024
    return pl.pallas_call(
        _ada_kernel,
        out_shape=jax.ShapeDtypeStruct((N_LAYERS, n, 6 * D_MODEL), f32),
        grid=(N_LAYERS, 6 * D_MODEL // tn),
        in_specs=[
            pl.BlockSpec((n, D_MODEL), lambda l, j: (0, 0)),
            pl.BlockSpec((1, D_MODEL, tn), lambda l, j: (l, 0, j)),
            pl.BlockSpec((1, 1, tn), lambda l, j: (l, 0, j)),
        ],
        out_specs=pl.BlockSpec((1, n, tn), lambda l, j: (l, 0, j)),
        compiler_params=pltpu.CompilerParams(
            dimension_semantics=("arbitrary", "arbitrary"), vmem_limit_bytes=VMEM_LIMIT),
        name="ada_mod",
    )(c_all, ada_w, ada_b.reshape(N_LAYERS, 1, 6 * D_MODEL))


def _inproj_kernel(x_ref, mod_ref, nw_ref, w_ref, o_ref):
    x = x_ref[...]
    bb, tt, d = x.shape
    hn = _rms(x, nw_ref[...]) * (1.0 + mod_ref[:, 1:2, :]) + mod_ref[:, 0:1, :]
    hn = hn.reshape(bb * tt, d).astype(bf16)
    u = jnp.dot(hn, w_ref[...], preferred_element_type=f32)
    o_ref[...] = u.reshape(bb, tt, u.shape[-1])


def _inproj(l, x, mod, nw, w, bb, tt):
    b, t, d = x.shape
    n = w.shape[-1]
    return pl.pallas_call(
        _inproj_kernel,
        out_shape=jax.ShapeDtypeStruct((b, t, n), f32),
        grid=(b // bb, t // tt),
        in_specs=[
            pl.BlockSpec((bb, tt, d), lambda i, j: (i, j, 0)),
            pl.BlockSpec((None, bb, 6, d), lambda i, j: (l, i, 0, 0)),
            _layer(nw, l),
            _layer(w, l, resident=True),
        ],
        out_specs=pl.BlockSpec((bb, tt, n), lambda i, j: (i, j, 0)),
        compiler_params=pltpu.CompilerParams(
            dimension_semantics=("arbitrary", "arbitrary"), vmem_limit_bytes=VMEM_LIMIT),
        name="inproj",
    )(x, mod, nw, w)


def _outmlp_kernel(x_ref, y_ref, mod_ref, wout_ref, n2w_ref, up_ref, down_ref, fnw_ref, o_ref,
                   *, final):
    x = x_ref[...]
    bb, tt, d = x.shape
    m = bb * tt
    ycat = y_ref[...].reshape(m, MIX_WIDTH).astype(bf16)
    mix = jnp.dot(ycat, wout_ref[:, 0:d], preferred_element_type=f32).reshape(bb, tt, d)
    x1 = x + mod_ref[:, 2:3, :] * mix
    hn = _rms(x1, n2w_ref[...]) * (1.0 + mod_ref[:, 4:5, :]) + mod_ref[:, 3:4, :]
    hn = hn.reshape(m, d).astype(bf16)
    ff = jnp.zeros((m, d), f32)
    fc = 1024
    for c in range(D_FF // fc):
        h = jnp.dot(hn, up_ref[:, c * fc:(c + 1) * fc], preferred_element_type=f32)
        h = jnp.square(jnp.maximum(h, 0.0)).astype(bf16)
        ff = ff + jnp.dot(h, down_ref[c * fc:(c + 1) * fc, 0:D_MODEL],
                          preferred_element_type=f32)
    x2 = x1 + mod_ref[:, 5:6, :] * ff.reshape(bb, tt, d)
    if final:
        x2 = _rms(x2, fnw_ref[...])
    o_ref[...] = x2


def _outmlp(l, x, ycat, mod, wout, n2w, up, down, fnw, bb, tt, final):
    b, t, d = x.shape
    return pl.pallas_call(
        functools.partial(_outmlp_kernel, final=final),
        out_shape=jax.ShapeDtypeStruct((b, t, d), f32),
        grid=(b // bb, t // tt),
        in_specs=[
            pl.BlockSpec((bb, tt, d), lambda i, j: (i, j, 0)),
            pl.BlockSpec((bb, tt, MIX_WIDTH), lambda i, j: (i, j, 0)),
            pl.BlockSpec((None, bb, 6, d), lambda i, j: (l, i, 0, 0)),
            _layer(wout, l, resident=True),
            _layer(n2w, l),
            _layer(up, l, resident=True),
            _layer(down, l, resident=True),
            _full((1, d)),
        ],
        out_specs=pl.BlockSpec((bb, tt, d), lambda i, j: (i, j, 0)),
        compiler_params=pltpu.CompilerParams(
            dimension_semantics=("arbitrary", "arbitrary"), vmem_limit_bytes=VMEM_LIMIT),
        name="outproj_mlp",
    )(x, ycat, mod, wout, n2w, up, down, fnw)


PIECE = 512


def _stage_in(x, mod_ref, nw_ref, w_ref, u_ref):
    hn = _rms(x, nw_ref[...]) * (1.0 + mod_ref[0, 1:2, :]) + mod_ref[0, 0:1, :]
    hn = hn.astype(bf16)
    yield
    for c0 in range(0, NP_COLS, PIECE):
        c1 = min(c0 + PIECE, NP_COLS)
        u_ref[:, c0:c1] = jnp.dot(hn, w_ref[:, c0:c1], preferred_element_type=f32)
        yield


def _stage_out(x, y_ref, o_ref, rows, mod_ref, wout_ref, n2w_ref, up_ref, down_ref, fnw_ref,
               final):
    halves = [slice(h * PIECE, (h + 1) * PIECE) for h in range(D_MODEL // PIECE)]
    ycat = y_ref[...]
    mix = []
    for hs in halves:
        mix.append(jnp.dot(ycat, wout_ref[:, hs], preferred_element_type=f32))
        yield
    x1 = x + mod_ref[0, 2:3, :] * jnp.concatenate(mix, axis=1)
    hn = (_rms(x1, n2w_ref[...]) * (1.0 + mod_ref[0, 4:5, :]) + mod_ref[0, 3:4, :]).astype(bf16)
    yield
    ff = [jnp.zeros((x.shape[0], PIECE), f32) for _ in halves]
    fc = 2 * PIECE
    for c in range(D_FF // fc):
        h = []
        for k in range(fc // PIECE):
            cols = slice(c * fc + k * PIECE, c * fc + (k + 1) * PIECE)
            hk = jnp.dot(hn, up_ref[:, cols], preferred_element_type=f32)
            h.append(jnp.square(jnp.maximum(hk, 0.0)).astype(bf16))
            yield
        h = jnp.concatenate(h, axis=1)
        for i, hs in enumerate(halves):
            ff[i] = ff[i] + jnp.dot(h, down_ref[c * fc:(c + 1) * fc, hs],
                                    preferred_element_type=f32)
            yield
    x2 = x1 + mod_ref[0, 5:6, :] * jnp.concatenate(ff, axis=1)
    o_ref[0, rows, :] = _rms(x2, fnw_ref[...]) if final else x2
    yield


def _trace_interleaved(main, fillers, n_main, n_fill):
    fillers = list(fillers)
    due = 0.0
    nxt = 0
    main_live = True
    while main_live or fillers:
        if main_live:
            main_live = next(main, _DONE) is not _DONE
            due += n_fill / n_main
        else:
            due = float(len(fillers))
        while fillers and due >= 1.0:
            nxt %= len(fillers)
            if next(fillers[nxt], _DONE) is _DONE:
                fillers.pop(nxt)
            else:
                nxt += 1
                due -= 1.0


_DONE = object()


def _reset_state_if(first, state):
    for ref in state:
        ref[...] = jnp.where(first, 0.0, ref[...])


def _stage_mix(u_ref, y_ref, prm, state, ci, outs):
    (cw_ref, cb_ref, sbias_ref, alog_ref, dskip_ref, snw_ref, mnw_ref,
     lcw_ref, lcb_ref, wax_ref, ba_ref, bx_ref, lam_ref) = prm
    xbuf, lbuf, st, cst, nst, mst, hst = state
    L = CHUNK
    SUB = V7X_SUBLANES
    HIST = (CONV_W - 1) * SUB

    row = lax.broadcasted_iota(jnp.int32, (L, L), 0)
    col = lax.broadcasted_iota(jnp.int32, (L, L), 1)
    tok = lambda i: (i & (SUB - 1)) * ROW_TILES + (i >> 3)
    tri = tok(row) >= tok(col)
    lo = col < SSD_HEAD_DIM

    def bcl(a, j):
        return jnp.broadcast_to(a[:, j:j + 1], (L, L))

    def conv(u0, hist_ref, w_ref, b_ref, c0, c1):
        raw = u_ref[:, u0 + c0:u0 + c1]
        sub = lax.broadcasted_iota(jnp.int32, (SUB, c1 - c0), 0)
        prev = [pltpu.roll(jnp.where(sub == SUB - 1, hist_ref[k * SUB:(k + 1) * SUB, c0:c1],
                                     raw[L - HIST + k * SUB:L - HIST + (k + 1) * SUB, :]), 1, 0)
                for k in range(CONV_W - 1)]
        acc = b_ref[:, c0:c1] + raw * w_ref[CONV_W - 1:CONV_W, c0:c1]
        for d in range(1, CONV_W):
            shifted = jnp.concatenate(prev[CONV_W - 1 - d:] + [raw[0:L - d * SUB, :]], axis=0)
            acc = acc + shifted * w_ref[CONV_W - 1 - d:CONV_W - d, c0:c1]
        hist_ref[:, c0:c1] = raw[L - HIST:L, :]
        return acc

    cblk = 512
    xbc = []
    for c0 in range(0, SSD_CONV_DIM, cblk):
        xbc.append(_silu(conv(P_XBC, xbuf, cw_ref, cb_ref, c0, c0 + cblk)))
        yield
    xbc = jnp.concatenate(xbc, axis=1)
    xr = conv(P_XR, lbuf, lcw_ref, lcb_ref, 0, LRU_WIDTH)
    yield

    sm = u_ref[:,P_SMALL:P_SMALL + V7X_LANES] + sbias_ref[...]
    dt = _softplus(sm)
    a_row = -jnp.exp(alog_ref[...])
    gates = jnp.where(col < SSD_HEADS, dt * (a_row * LOG2E),
                      jnp.where((col >= 20) & (col < 24), _log_sigmoid(sm), 0.0))
    g_hi = gates.astype(bf16)
    g_r = gates - g_hi.astype(f32)
    g_mid = g_r.astype(bf16)
    g_lo = (g_r - g_mid.astype(f32)).astype(bf16)
    cum3 = jnp.dot(jnp.where(tri, 1.0, 0.0).astype(bf16),
                   jnp.concatenate([g_hi, g_mid, g_lo], axis=1), preferred_element_type=f32)
    cum = cum3[:, 0:L] + cum3[:, L:2 * L] + cum3[:, 2 * L:3 * L]
    cum_t = cum.T
    sm_t = sm.T
    yield

    ys = []
    for g in range(SSD_GROUPS):
        bm_t = xbc[:, SSD_WIDTH + g * SSD_STATE:SSD_WIDTH + (g + 1) * SSD_STATE].T.astype(bf16)
        cm = xbc[:, SSD_WIDTH + (SSD_GROUPS + g) * SSD_STATE:
                 SSD_WIDTH + (SSD_GROUPS + g + 1) * SSD_STATE].astype(bf16)
        cb = jnp.dot(cm, bm_t, preferred_element_type=f32)
        inter = jnp.dot(cm, st[g].astype(bf16), preferred_element_type=f32)
        yield
        xends, keeps = [], []
        for jj in range(4):
            jp = g * 4 + jj
            e0, e1 = 2 * jp, 2 * jp + 1
            c0 = bcl(cum, e0)
            c1 = bcl(cum, e1)
            cum_p = jnp.where(lo, c0, c1)
            dt_p = jnp.where(lo, bcl(dt, e0), bcl(dt, e1))
            xs_p = xbc[:, jp * L:(jp + 1) * L]
            xdt = xs_p * dt_p
            dec0 = jnp.exp2(jnp.where(tri, c0 - cum_t[e0:e0 + 1, :], -jnp.inf))
            dec1 = jnp.exp2(jnp.where(tri, c1 - cum_t[e1:e1 + 1, :], -jnp.inf))
            att = jnp.concatenate([(cb * dec0).astype(bf16), (cb * dec1).astype(bf16)], axis=1)
            x2 = jnp.concatenate([jnp.where(lo, xdt, 0.0).astype(bf16),
                                  jnp.where(lo, 0.0, xdt).astype(bf16)], axis=0)
            y_p = (jnp.dot(att, x2, preferred_element_type=f32)
                   + inter[:, jj * L:(jj + 1) * L] * jnp.exp2(cum_p)
                   + xs_p * dskip_ref[:, jp * L:(jp + 1) * L])
            ys.append(y_p)
            c_last = cum_p[L - 1:L, :]
            xends.append((xdt * jnp.exp2(c_last - cum_p)).astype(bf16))
            keeps.append(jnp.exp2(c_last))
            yield
        st[g] = (jnp.concatenate(keeps, axis=1) * st[g]
                 + jnp.dot(bm_t, jnp.concatenate(xends, axis=1), preferred_element_type=f32))
    y = jnp.concatenate(ys, axis=1)
    z = u_ref[:,P_Z:P_Z + SSD_WIDTH]
    y_ref[:,0:SSD_WIDTH] = _rms(y * _silu(z), snw_ref[...]).astype(bf16)
    yield

    scale = ML_HEAD_DIM ** -0.5
    for h in range(ML_HEADS):
        sl = slice(h * ML_HEAD_DIM, (h + 1) * ML_HEAD_DIM)
        q = u_ref[:,P_Q + h * ML_HEAD_DIM:P_Q + (h + 1) * ML_HEAD_DIM]
        k = u_ref[:,P_K + h * ML_HEAD_DIM:P_K + (h + 1) * ML_HEAD_DIM] * scale
        v = u_ref[:,P_V + h * ML_HEAD_DIM:P_V + (h + 1) * ML_HEAD_DIM]
        o = u_ref[:,P_O + h * ML_HEAD_DIM:P_O + (h + 1) * ML_HEAD_DIM]
        k_t = k.T.astype(bf16)
        jf = 20 + h
        ji = 16 + h
        b_col = cum[:, jf:jf + 1]
        i_col = sm[:, ji:ji + 1]
        dm = jnp.where(tri, bcl(cum, jf) - cum_t[jf:jf + 1, :] + sm_t[ji:ji + 1, :], -jnp.inf)
        m_prev = mst[h, :, 0:1]
        g_col = b_col + m_prev
        mt = jnp.maximum(g_col, jnp.max(dm, axis=-1, keepdims=True))
        w = jnp.exp(dm - mt) * jnp.dot(q.astype(bf16), k_t, preferred_element_type=f32)
        inter = jnp.exp(g_col - mt)
        ct_old = cst[h]
        n_old = nst[h]
        num = jnp.dot(jnp.concatenate([w.astype(bf16), (inter * q).astype(bf16)], axis=1),
                      jnp.concatenate([v.astype(bf16), ct_old.astype(bf16)], axis=0),
                      preferred_element_type=f32)
        den = (jnp.sum(w, axis=-1, keepdims=True)
               + inter * jnp.sum(q * n_old, axis=-1, keepdims=True))
        hout = num / jnp.maximum(jnp.abs(den), jnp.exp(-mt))
        m_new = mt[L - 1:L, :]
        b_last = b_col[L - 1:L, :]
        w_end = jnp.exp(b_last - b_col + i_col - m_new)
        dc = jnp.exp(b_last + m_prev - m_new)
        cst[h] = dc * ct_old + jnp.dot(k_t, (v * w_end).astype(bf16),
                                       preferred_element_type=f32)
        nst[h] = dc * n_old + jnp.sum(w_end * k, axis=0, keepdims=True)
        mst[h] = jnp.broadcast_to(m_new, (1, V7X_LANES))
        yh = _rms(hout, mnw_ref[:, sl]) * _sigmoid(o)
        y_ref[:,SSD_WIDTH + h * ML_HEAD_DIM:SSD_WIDTH + (h + 1) * ML_HEAD_DIM] = yh.astype(bf16)
        yield

    rs, is_ = [], []
    for kb_ in range(LRU_BLOCKS):
        xk = xr[:, kb_ * LRU_BLOCK_DIM:(kb_ + 1) * LRU_BLOCK_DIM].astype(bf16)
        ri = jnp.dot(xk, wax_ref[kb_], preferred_element_type=f32)
        rs.append(ri[:, 0:LRU_BLOCK_DIM])
        is_.append(ri[:, LRU_BLOCK_DIM:2 * LRU_BLOCK_DIM])
    r = _sigmoid(jnp.concatenate(rs, axis=1) + ba_ref[...])
    i_g = _sigmoid(jnp.concatenate(is_, axis=1) + bx_ref[...])
    yield
    log_a = -LRU_C * r * _softplus(-lam_ref[...])
    mult = jnp.sqrt(_neg_expm1(2.0 * log_a))
    rowl = lax.broadcasted_iota(jnp.int32, (L, LRU_WIDTH), 0)
    first_row = jnp.where(ci == 0, 0, -1)
    mult = jnp.where(rowl == first_row, 1.0, mult)
    a_s = jnp.exp(log_a)
    u_s = mult * i_g * xr
    yield
    tile = lambda a, v: a[v * SUB:(v + 1) * SUB, :]
    h_loc = [tile(u_s, 0)]
    a_cum = [tile(a_s, 0)]
    for v in range(1, ROW_TILES):
        h_loc.append(tile(a_s, v) * h_loc[-1] + tile(u_s, v))
        a_cum.append(tile(a_s, v) * a_cum[-1])
    a_e, h_e = a_cum[-1], h_loc[-1]
    subl = lax.broadcasted_iota(jnp.int32, (SUB, LRU_WIDTH), 0)
    k_ = 1
    while k_ < SUB:
        keep = subl >= k_
        a_sh = jnp.where(keep, pltpu.roll(a_e, k_, 0), 1.0)
        h_sh = jnp.where(keep, pltpu.roll(h_e, k_, 0), 0.0)
        h_e = a_e * h_sh + h_e
        a_e = a_e * a_sh
        k_ *= 2
    h0 = hst[...]
    run_end = h_e + a_e * h0
    run_in = jnp.where(subl == 0, h0, pltpu.roll(run_end, 1, 0))
    hr = jnp.concatenate([h_loc[v] + a_cum[v] * run_in for v in range(ROW_TILES)], axis=0)
    hst[...] = run_end[SUB - 1:SUB, :]
    gr = u_ref[:,P_GR:P_GR + LRU_WIDTH]
    y_ref[:,SSD_WIDTH + ML_WIDTH:MIX_WIDTH] = (hr * _gelu_tanh(gr)).astype(bf16)
    yield

    if outs is not None:
        ssm_ref, sconv_ref, mc_ref, mn_ref, mm_ref, lh_ref, lconv_ref = outs
        for g in range(SSD_GROUPS):
            for jj in range(4):
                ssm_ref[0, g * 4 + jj] = st[g, :, jj * L:(jj + 1) * L].T
        for k in range(CONV_W - 1):
            sconv_ref[0, k:k + 1, :] = xbuf[k * SUB + SUB - 1:(k + 1) * SUB, :]
            lconv_ref[0, k:k + 1, :] = lbuf[k * SUB + SUB - 1:(k + 1) * SUB, :]
        for h in range(ML_HEADS):
            mc_ref[0, h] = cst[h].T
            mn_ref[0, h:h + 1, :] = nst[h]
            mm_ref[0, h:h + 1, :] = mst[h]
        lh_ref[0] = hst[...]


N_MIX_PARAMS = 13
N_STATE_OUTS = 7
N_MIX_PHASES = 24
N_FILL_PIECES = 33


def _player_kernel(*refs, final, n_chunks, nc):
    xa_ref, moda_ref, xc_ref, modc_ref, n1w_ref, win_ref = refs[:6]
    prm = refs[6:6 + N_MIX_PARAMS]
    wout_ref, n2w_ref, up_ref, down_ref, fnw_ref = refs[6 + N_MIX_PARAMS:11 + N_MIX_PARAMS]
    n_in = 11 + N_MIX_PARAMS + N_STATE_OUTS
    o_ref = refs[n_in]
    outs = refs[n_in + 1:n_in + 1 + N_STATE_OUTS]
    u_bufs = refs[-11:-9]
    y_bufs = refs[-9:-7]
    state = refs[-7:]
    k = pl.program_id(0)

    @pl.when(k == 0)
    def _():
        u_bufs[1][...] = jnp.zeros(u_bufs[1].shape, f32)
        y_bufs[1][...] = jnp.zeros(y_bufs[1].shape, bf16)
        for ref in state:
            ref[...] = jnp.zeros(ref.shape, f32)

    for sub in range(2):
        rows = slice(sub * CHUNK, (sub + 1) * CHUNK)
        stage_in = _stage_in(xa_ref[0, rows, :], moda_ref, n1w_ref, win_ref, u_bufs[sub])
        jb = jnp.clip(2 * k + sub - 1, 0, n_chunks - 1)
        ci = jb % nc
        if sub == 1:
            _reset_state_if(ci == 0, state)
        stage_mix = _stage_mix(u_bufs[1 - sub], y_bufs[sub], prm, state, ci,
                               outs if sub == 0 else None)
        stage_out = _stage_out(xc_ref[0, rows, :], y_bufs[1 - sub], o_ref, rows, modc_ref,
                               wout_ref, n2w_ref, up_ref, down_ref, fnw_ref, final)
        _trace_interleaved(stage_mix, [stage_in, stage_out], N_MIX_PHASES, N_FILL_PIECES)


def _player(l, x, mod, mod_row0, w, fnw, final, prev):
    b, t, d = x.shape
    nc = t // CHUNK
    n_chunks = b * nc
    n_pairs = n_chunks // 2
    ppr = nc // 2
    pair_a = lambda k: jnp.minimum(k, n_pairs - 1)
    pair_c = lambda k: jnp.clip(k - 1, 0, n_pairs - 1)
    seq_b = lambda k: jnp.clip(2 * k - 1, 0, n_chunks - 1) // nc
    xspec = lambda pair: pl.BlockSpec((1, 2 * CHUNK, d), lambda k: (pair(k) // ppr, pair(k) % ppr, 0))
    mspec = lambda pair: pl.BlockSpec((None, 1, 6, d),
                                      lambda k: (l, mod_row0 + pair(k) // ppr, 0, 0))
    st_spec = lambda s: pl.BlockSpec((None, 1) + s[2:],
                                     lambda k: (l, seq_b(k)) + (0,) * (len(s) - 2))
    outs = [jax.ShapeDtypeStruct((b, t, d), f32)]
    outs += [jax.ShapeDtypeStruct(a.shape, f32) for a in prev]
    mix_params = [w["ssd_conv_w"], w["ssd_conv_b"], w["p_small_bias"], w["p_alog"], w["dskip"],
                  w["ssd_norm_w"], w["ml_norm_w"], w["lru_conv_w"], w["lru_conv_b"], w["lru_wax"],
                  w["lru_ba"], w["lru_bx"], w["lru_lambda"]]
    assert len(mix_params) == N_MIX_PARAMS and len(prev) == N_STATE_OUTS
    n_in = 11 + N_MIX_PARAMS
    return pl.pallas_call(
        functools.partial(_player_kernel, final=final, n_chunks=n_chunks, nc=nc),
        out_shape=outs,
        grid=(n_pairs + 1,),
        in_specs=[xspec(pair_a), mspec(pair_a), xspec(pair_c), mspec(pair_c),
                  _layer(w["norm1_w"], l), _layer(w["w_in_p"], l, resident=True)]
        + [_layer(a, l) for a in mix_params]
        + [_layer(w["w_out"], l, resident=True), _layer(w["norm2_w"], l),
           _layer(w["mlp_up"], l, resident=True), _layer(w["mlp_down"], l, resident=True),
           _full((1, d))]
        + [pl.BlockSpec(memory_space=pl.ANY)] * len(prev),
        out_specs=[xspec(pair_c)] + [st_spec(a.shape) for a in prev],
        input_output_aliases={n_in + i: 1 + i for i in range(len(prev))},
        scratch_shapes=[
            pltpu.VMEM((CHUNK, NP_COLS), f32), pltpu.VMEM((CHUNK, NP_COLS), f32),
            pltpu.VMEM((CHUNK, MIX_WIDTH), bf16), pltpu.VMEM((CHUNK, MIX_WIDTH), bf16),
            pltpu.VMEM(((CONV_W - 1) * V7X_SUBLANES, SSD_CONV_DIM), f32),
            pltpu.VMEM(((CONV_W - 1) * V7X_SUBLANES, LRU_WIDTH), f32),
            pltpu.VMEM((SSD_GROUPS, SSD_STATE, 512), f32),
            pltpu.VMEM((ML_HEADS, ML_HEAD_DIM, ML_HEAD_DIM), f32),
            pltpu.VMEM((ML_HEADS, 1, ML_HEAD_DIM), f32),
            pltpu.VMEM((ML_HEADS, 1, V7X_LANES), f32),
            pltpu.VMEM((1, LRU_WIDTH), f32),
        ],
        compiler_params=pltpu.CompilerParams(
            dimension_semantics=("arbitrary",), vmem_limit_bytes=PLAYER_VMEM_LIMIT),
        name="prompt_layer",
    )(x, mod, x, mod, w["norm1_w"], w["w_in_p"], *mix_params,
      w["w_out"], w["norm2_w"], w["mlp_up"], w["mlp_down"], fnw, *prev)


def _smix_kernel(u_ref, s0_ref, sconv0_ref, c0_ref, n0_ref, m0_ref, lh0_ref, lconv0_ref,
                 cw_ref, cb_ref, dtb_ref, alog_ref, dskip_ref, snw_ref, ib_ref, fb_ref, mnw_ref,
                 lcw_ref, lcb_ref, wa_ref, wx_ref, ba_ref, bx_ref, lam_ref,
                 y_ref, s_ref, sconv_ref, c_ref, n_ref, m_ref, lh_ref, lconv_ref,
                 xbuf, lbuf):
    bb = u_ref.shape[0]
    T = DEC_SEQ
    HEAD = V7X_SUBLANES

    def bs(a, s):
        return jnp.broadcast_to(a[:, s:s + 1, :], a.shape)

    def tio(n):
        return lax.broadcasted_iota(jnp.int32, (bb, T, n), 1)

    def lio(n):
        return lax.broadcasted_iota(jnp.int32, (bb, T, n), 2)

    def cumsum_t(a):
        t = tio(a.shape[-1])
        acc = jnp.zeros_like(a)
        for s in range(T):
            acc = acc + jnp.where(t >= s, bs(a, s), 0.0)
        return acc

    def conv(buf, raw, hist, w_ref, b_ref, n):
        buf[:, HEAD - 3:HEAD, :] = hist
        buf[:, HEAD:HEAD + T, :] = raw
        acc = b_ref[...] + buf[:, HEAD - 3:HEAD - 3 + T, :] * w_ref[0:1, :]
        for j in range(1, CONV_W):
            acc = acc + buf[:, HEAD - 3 + j:HEAD - 3 + j + T, :] * w_ref[j:j + 1, :]
        return acc, buf[:, HEAD + T - 3:HEAD + T, :]

    xbc, sconv_new = conv(xbuf, u_ref[:, :, P_XBC:P_XBC + SSD_CONV_DIM], sconv0_ref[...],
                          cw_ref, cb_ref, SSD_CONV_DIM)
    sconv_ref[...] = sconv_new
    xbc = _silu(xbc)
    xr, lconv_new = conv(lbuf, u_ref[:, :, P_XR:P_XR + LRU_WIDTH], lconv0_ref[...],
                         lcw_ref, lcb_ref, LRU_WIDTH)
    lconv_ref[...] = lconv_new

    xs = xbc[:, :, 0:SSD_WIDTH]
    t_w = tio(SSD_WIDTH)
    l_w = lio(SSD_WIDTH)
    dt = _softplus(u_ref[:, :, S_DT:S_DT + SSD_WIDTH] + dtb_ref[...])
    cum = cumsum_t(dt * (-jnp.exp(alog_ref[...])))
    xdt = xs * dt
    bms = [xbc[:, :, SSD_WIDTH + g * SSD_STATE:SSD_WIDTH + (g + 1) * SSD_STATE]
           for g in range(SSD_GROUPS)]
    cms = [xbc[:, :, SSD_WIDTH + (SSD_GROUPS + g) * SSD_STATE:
               SSD_WIDTH + (SSD_GROUPS + g + 1) * SSD_STATE] for g in range(SSD_GROUPS)]
    y = xs * dskip_ref[...]
    for s in range(T):
        cb0 = jnp.sum(cms[0] * bs(bms[0], s), axis=-1, keepdims=True)
        cb1 = jnp.sum(cms[1] * bs(bms[1], s), axis=-1, keepdims=True)
        cbs = jnp.where(l_w < 512, cb0, cb1)
        dec = jnp.exp(jnp.where(t_w >= s, cum - bs(cum, s), -jnp.inf))
        y = y + cbs * dec * bs(xdt, s)
    inter = jnp.concatenate(
        [jnp.einsum('btn,bpn->btp', cms[g].astype(bf16),
                    s0_ref[:, g * 512:(g + 1) * 512, :].astype(bf16),
                    preferred_element_type=f32) for g in range(SSD_GROUPS)], axis=-1)
    y = y + inter * jnp.exp(cum)
    z = u_ref[:, :, P_Z:P_Z + SSD_WIDTH]
    y_ref[:, :, 0:SSD_WIDTH] = _rms(y * _silu(z), snw_ref[...])

    def split3(d, t_idx):
        hi = d.astype(bf16).astype(f32)
        r1 = d - hi
        mid = r1.astype(bf16).astype(f32)
        lo_ = (r1 - mid).astype(bf16).astype(f32)
        return jnp.where(t_idx == 0, hi, jnp.where(t_idx == 1, mid, jnp.where(t_idx == 2, lo_, 0.0)))

    def ones_rhs(kmat):
        zeros = jnp.zeros_like(kmat)
        return jnp.concatenate(
            [jnp.concatenate([kmat, zeros], axis=2),
             jnp.concatenate([zeros, jnp.ones_like(kmat)], axis=2)], axis=1).astype(bf16)

    c_last = bs(cum, T - 1)
    xend = xdt * jnp.exp(c_last - cum)
    lhs = jnp.concatenate([xend, split3(jnp.exp(c_last), t_w)], axis=1).astype(bf16)
    for g in range(SSD_GROUPS):
        zz = jnp.einsum('bkp,bkn->bpn', lhs[:, :, g * 512:(g + 1) * 512],
                        ones_rhs(bms[g]), preferred_element_type=f32)
        s_ref[:, g * 512:(g + 1) * 512, :] = (
            zz[:, :, SSD_STATE:] * s0_ref[:, g * 512:(g + 1) * 512, :] + zz[:, :, :SSD_STATE])

    t_m = tio(ML_WIDTH)
    q = u_ref[:, :, P_Q:P_Q + ML_WIDTH]
    k = u_ref[:, :, P_K:P_K + ML_WIDTH] * (ML_HEAD_DIM ** -0.5)
    v = u_ref[:, :, P_V:P_V + ML_WIDTH]
    o = u_ref[:, :, P_O:P_O + ML_WIDTH]
    ic = u_ref[:, :, S_I:S_I + ML_WIDTH] + ib_ref[...]
    fc = _log_sigmoid(u_ref[:, :, S_F:S_F + ML_WIDTH] + fb_ref[...])
    bc = cumsum_t(fc)
    m0 = m0_ref[...]
    g_ = bc + m0

    def headsum(a):
        return jnp.concatenate(
            [jnp.broadcast_to(jnp.sum(a[:, :, h * ML_HEAD_DIM:(h + 1) * ML_HEAD_DIM], axis=-1,
                                      keepdims=True), (bb, T, ML_HEAD_DIM))
             for h in range(ML_HEADS)], axis=-1)

    dms = [jnp.where(t_m >= s, bc - bs(bc, s) + bs(ic, s), -jnp.inf) for s in range(T)]
    mt = g_
    for s in range(T):
        mt = jnp.maximum(mt, dms[s])
    num = jnp.zeros((bb, T, ML_WIDTH), f32)
    den = jnp.zeros((bb, T, ML_WIDTH), f32)
    for s in range(T):
        w = jnp.exp(dms[s] - mt) * headsum(q * bs(k, s))
        num = num + w * bs(v, s)
        den = den + w
    inter_m = jnp.exp(g_ - mt)
    qb = q.astype(bf16)
    qc = jnp.concatenate(
        [jnp.einsum('btk,bvk->btv', qb[:, :, h * ML_HEAD_DIM:(h + 1) * ML_HEAD_DIM],
                    c0_ref[:, h * ML_HEAD_DIM:(h + 1) * ML_HEAD_DIM, :].astype(bf16),
                    preferred_element_type=f32) for h in range(ML_HEADS)], axis=-1)
    n0 = n0_ref[...]
    num = num + inter_m * qc
    den = den + inter_m * headsum(q * n0)
    hout = num / jnp.maximum(jnp.abs(den), jnp.exp(-mt))
    m_new = bs(mt, T - 1)
    b_last = bs(bc, T - 1)
    w_end = jnp.exp(b_last - bc + ic - m_new)
    dc = jnp.exp(b_last + m0 - m_new)
    m_ref[...] = mt[:, T - 1:T, :]
    n_ref[...] = dc[:, 0:1, :] * n0 + jnp.sum(w_end * k, axis=1, keepdims=True)
    lhs_m = jnp.concatenate([v * w_end, split3(dc, t_m)], axis=1).astype(bf16)
    for h in range(ML_HEADS):
        sl = slice(h * ML_HEAD_DIM, (h + 1) * ML_HEAD_DIM)
        zz = jnp.einsum('bkv,bkn->bvn', lhs_m[:, :, sl], ones_rhs(k[:, :, sl]),
                        preferred_element_type=f32)
        c_ref[:, sl, :] = zz[:, :, ML_HEAD_DIM:] * c0_ref[:, sl, :] + zz[:, :, :ML_HEAD_DIM]
    hn2 = headsum(hout * hout) * (1.0 / ML_HEAD_DIM)
    y_ml = hout * lax.rsqrt(hn2 + EPS) * mnw_ref[...] * _sigmoid(o)
    y_ref[:, :, SSD_WIDTH:SSD_WIDTH + ML_WIDTH] = y_ml

    xr2 = xr.reshape(bb * T, LRU_WIDTH)
    rs, is_ = [], []
    for kk in range(LRU_BLOCKS):
        xk = xr2[:, kk * LRU_BLOCK_DIM:(kk + 1) * LRU_BLOCK_DIM].astype(bf16)
        rs.append(jnp.dot(xk, wa_ref[kk], preferred_element_type=f32))
        is_.append(jnp.dot(xk, wx_ref[kk], preferred_element_type=f32))
    r = _sigmoid(jnp.concatenate(rs, axis=1).reshape(bb, T, LRU_WIDTH) + ba_ref[...])
    i_g = _sigmoid(jnp.concatenate(is_, axis=1).reshape(bb, T, LRU_WIDTH) + bx_ref[...])
    log_a = -LRU_C * r * _softplus(-lam_ref[...])
    a_s = jnp.exp(log_a)
    u_s = jnp.sqrt(_neg_expm1(2.0 * log_a)) * i_g * xr
    t_l = tio(LRU_WIDTH)
    cur = lh0_ref[...]
    hr = jnp.zeros((bb, T, LRU_WIDTH), f32)
    for s in range(T):
        cur = a_s[:, s:s + 1, :] * cur + u_s[:, s:s + 1, :]
        hr = jnp.where(t_l == s, jnp.broadcast_to(cur, hr.shape), hr)
    lh_ref[...] = cur
    gr = u_ref[:, :, P_GR:P_GR + LRU_WIDTH]
    y_ref[:, :, SSD_WIDTH + ML_WIDTH:MIX_WIDTH] = hr * _gelu_tanh(gr)


def _smix(l, u, states, prev, w, bb):
    b = u.shape[0]
    row = lambda n: _full((1, n))
    lblk = lambda s: pl.BlockSpec((None, bb) + s[2:], lambda i: (l, i) + (0,) * (len(s) - 2))
    ublk = lambda s: pl.BlockSpec((bb,) + s[1:], lambda i: (i,) + (0,) * (len(s) - 1))
    y_shape = (b, DEC_SEQ, MIX_WIDTH)
    outs = [jax.ShapeDtypeStruct(y_shape, f32)]
    outs += [jax.ShapeDtypeStruct(s.shape, f32) for s in states]
    weights = [w["ssd_conv_w"], w["ssd_conv_b"], w["s_dt_bias"], w["s_alog"], w["dskip"],
               w["ssd_norm_w"], w["s_i_bias"], w["s_f_bias"], w["ml_norm_w"],
               w["lru_conv_w"], w["lru_conv_b"], w["lru_wa"], w["lru_wx"], w["lru_ba"],
               w["lru_bx"], w["lru_lambda"]]
    n_in = 1 + len(states) + len(weights)
    prev = list(prev)

    def body(*refs):
        _smix_kernel(*refs[:n_in], *refs[n_in + len(prev):])

    return pl.pallas_call(
        body,
        out_shape=outs,
        grid=(b // bb,),
        in_specs=[ublk(u.shape)] + [lblk(s.shape) for s in states]
        + [_layer(a, l) for a in weights]
        + [pl.BlockSpec(memory_space=pl.ANY)] * len(prev),
        out_specs=[ublk(y_shape)] + [lblk(s.shape) for s in states],
        input_output_aliases={n_in + k: 1 + k for k in range(len(prev))},
        scratch_shapes=[
            pltpu.VMEM((bb, 2 * V7X_SUBLANES, SSD_CONV_DIM), f32),
            pltpu.VMEM((bb, 2 * V7X_SUBLANES, LRU_WIDTH), f32),
        ],
        compiler_params=pltpu.CompilerParams(
            dimension_semantics=("arbitrary",), vmem_limit_bytes=VMEM_LIMIT),
        name="sample_mix",
    )(u, *states, *weights, *prev)


def _prep_params(p):
    w_in = p["w_in"].astype(bf16)
    main = jnp.concatenate([w_in[..., 0:_O_DT], w_in[..., _O_Q:_O_O_END], w_in[..., _O_XR:_O_END]],
                           axis=-1)
    dt_c = w_in[..., _O_DT:_O_Q]
    if_c = w_in[..., _O_I:_O_XR]
    pad = jnp.zeros((N_LAYERS, D_MODEL, V7X_LANES - SSD_HEADS - 2 * ML_HEADS), bf16)
    w_p = jnp.concatenate([main, dt_c, if_c, pad], axis=-1)
    w_s = jnp.concatenate([main, jnp.repeat(dt_c, SSD_HEAD_DIM, axis=-1),
                           jnp.repeat(if_c, ML_HEAD_DIM, axis=-1)], axis=-1)
    r = lambda a: a.reshape(N_LAYERS, 1, -1)
    rep = lambda a, n: r(jnp.repeat(a, n, axis=-1))
    lane_pad = lambda a: jnp.pad(a.astype(bf16), ((0, 0), (0, 0), (0, V7X_LANES)))
    small = jnp.concatenate([p["ssd_dt_bias"], p["ml_i_bias"], p["ml_f_bias"]], axis=-1)
    return dict(
        w_in_p=w_p, w_in_s=w_s,
        norm1_w=r(p["norm1_w"]), norm2_w=r(p["norm2_w"]),
        ssd_conv_w=p["ssd_conv_w"], ssd_conv_b=r(p["ssd_conv_b"]),
        p_small_bias=r(jnp.pad(small, ((0, 0), (0, V7X_LANES - small.shape[-1])))),
        p_alog=r(jnp.pad(p["ssd_a_log"], ((0, 0), (0, V7X_LANES - SSD_HEADS)))),
        s_dt_bias=rep(p["ssd_dt_bias"], SSD_HEAD_DIM), s_alog=rep(p["ssd_a_log"], SSD_HEAD_DIM),
        s_i_bias=rep(p["ml_i_bias"], ML_HEAD_DIM), s_f_bias=rep(p["ml_f_bias"], ML_HEAD_DIM),
        dskip=rep(p["ssd_d"], SSD_HEAD_DIM),
        ssd_norm_w=r(p["ssd_norm_w"]), ml_norm_w=r(p["ml_norm_w"]),
        lru_conv_w=p["lru_conv_w"], lru_conv_b=r(p["lru_conv_b"]),
        lru_wa=p["lru_wa"].astype(bf16), lru_wx=p["lru_wx"].astype(bf16),
        lru_wax=jnp.concatenate([p["lru_wa"], p["lru_wx"]], axis=-1).astype(bf16),
        lru_ba=r(p["lru_ba"]), lru_bx=r(p["lru_bx"]), lru_lambda=r(p["lru_lambda"]),
        w_out=lane_pad(p["w_out"]), mlp_up=lane_pad(p["mlp_up"]), mlp_down=lane_pad(p["mlp_down"]),
    )


def kernel(x_prompt, x_sample, c_prompt, c_sample, state_ssm, state_ssd_conv, state_mlstm_c, state_mlstm_n, state_mlstm_m, state_lru_h, state_lru_conv, ada_w, ada_b, norm1_w, norm2_w, w_in, ssd_conv_w, ssd_conv_b, ssd_dt_bias, ssd_a_log, ssd_d, ssd_norm_w, ml_i_bias, ml_f_bias, ml_norm_w, lru_conv_w, lru_conv_b, lru_wa, lru_ba, lru_wx, lru_bx, lru_lambda, w_out, mlp_up, mlp_down, final_norm_w):
    p = dict(norm1_w=norm1_w, norm2_w=norm2_w, w_in=w_in, ssd_conv_w=ssd_conv_w,
             ssd_conv_b=ssd_conv_b, ssd_dt_bias=ssd_dt_bias, ssd_a_log=ssd_a_log, ssd_d=ssd_d,
             ssd_norm_w=ssd_norm_w, ml_i_bias=ml_i_bias, ml_f_bias=ml_f_bias,
             ml_norm_w=ml_norm_w, lru_conv_w=lru_conv_w, lru_conv_b=lru_conv_b, lru_wa=lru_wa,
             lru_ba=lru_ba, lru_wx=lru_wx, lru_bx=lru_bx, lru_lambda=lru_lambda, w_out=w_out,
             mlp_up=mlp_up, mlp_down=mlp_down)
    w = _prep_params(p)
    fnw = final_norm_w.reshape(1, D_MODEL)
    bp = x_prompt.shape[0]
    bs_ = x_sample.shape[0]

    mod = _ada(jnp.concatenate([c_sample, c_prompt], axis=0), ada_w, ada_b)
    mod = mod.reshape(N_LAYERS, bs_ + bp, 6, D_MODEL)

    xp = _interleave_rows(x_prompt)
    p_out = [jnp.zeros((N_LAYERS, bp) + s, f32) for s in (
        (SSD_HEADS // 2, 2 * SSD_HEAD_DIM, SSD_STATE),
        (CONV_W - 1, SSD_CONV_DIM), (ML_HEADS, ML_HEAD_DIM, ML_HEAD_DIM),
        (ML_HEADS, ML_HEAD_DIM), (ML_HEADS, V7X_LANES),
        (1, LRU_WIDTH), (CONV_W - 1, LRU_WIDTH))]
    for l in range(N_LAYERS):
        xp, *p_out = _player(l, xp, mod, bs_, w, fnw, l == N_LAYERS - 1, p_out)
    xp = _deinterleave_rows(xp)
    ssm, sconv, mc, mn, mm, lh, lconv = p_out
    p_states = (ssm.reshape(N_LAYERS, bp, SSD_HEADS, SSD_HEAD_DIM, SSD_STATE), sconv, mc, mn,
                mm[..., 0], lh.reshape(N_LAYERS, bp, LRU_WIDTH), lconv)

    xs = x_sample
    st_in = (
        state_ssm.reshape(N_LAYERS, bs_, SSD_HEADS * SSD_HEAD_DIM, SSD_STATE),
        state_ssd_conv,
        state_mlstm_c.reshape(N_LAYERS, bs_, ML_HEADS * ML_HEAD_DIM, ML_HEAD_DIM),
        state_mlstm_n.reshape(N_LAYERS, bs_, 1, ML_WIDTH),
        jnp.repeat(state_mlstm_m, ML_HEAD_DIM, axis=-1).reshape(N_LAYERS, bs_, 1, ML_WIDTH),
        state_lru_h.reshape(N_LAYERS, bs_, 1, LRU_WIDTH),
        state_lru_conv,
    )
    st_out = [jnp.zeros(s.shape, f32) for s in st_in]
    for l in range(N_LAYERS):
        u = _inproj(l, xs, mod, w["norm1_w"], w["w_in_s"], 32, DEC_SEQ)
        ycat, *st_out = _smix(l, u, st_in, st_out, w, 8)
        xs = _outmlp(l, xs, ycat, mod, w["w_out"], w["norm2_w"], w["mlp_up"], w["mlp_down"], fnw,
                     64, DEC_SEQ, l == N_LAYERS - 1)
    ssm, sconv, mc, mn, mm, lh, lconv = st_out
    s_states = (ssm.reshape(N_LAYERS, bs_, SSD_HEADS, SSD_HEAD_DIM, SSD_STATE), sconv,
                mc.reshape(N_LAYERS, bs_, ML_HEADS, ML_HEAD_DIM, ML_HEAD_DIM),
                mn.reshape(N_LAYERS, bs_, ML_HEADS, ML_HEAD_DIM),
                mm.reshape(N_LAYERS, bs_, ML_HEADS, ML_HEAD_DIM)[..., 0],
                lh.reshape(N_LAYERS, bs_, LRU_WIDTH), lconv)

    return (xp, xs) + p_states + s_states
```

```python
import functools
import math

import jax
import jax.numpy as jnp
from jax import lax
from jax.experimental import pallas as pl
from jax.experimental.pallas import tpu as pltpu

f32 = jnp.float32
bf16 = jnp.bfloat16

D_MODEL = 1024
N_LAYERS = 2
MIX_WIDTH = 2 * D_MODEL
SSD_WIDTH = 1024
SSD_HEAD_DIM = 64
SSD_HEADS = 16
SSD_GROUPS = 2
SSD_STATE = 128
SSD_CONV_DIM = SSD_WIDTH + 2 * SSD_GROUPS * SSD_STATE
ML_WIDTH = 512
ML_HEADS = 4
ML_HEAD_DIM = 128
LRU_WIDTH = 512
LRU_BLOCKS = 4
LRU_BLOCK_DIM = 128
LRU_C = 8.0
CONV_W = 4
CHUNK = 128
D_FF = 4 * D_MODEL
EPS = 1e-6
PAST_LEN = 16384
DEC_SEQ = 8

_O_Z, _O_XBC, _O_DT, _O_Q, _O_O_END, _O_I, _O_F, _O_XR, _O_END = (
    0, 1024, 2560, 2576, 4624, 4624, 4628, 4632, 5656)

P_Z = 0
P_XBC = 1024
P_Q = 2560
P_K = 3072
P_V = 3584
P_O = 4096
P_XR = 4608
P_GR = 5120
P_MAIN = 5632
P_SMALL = 5632
NP_COLS = 5760
S_DT = 5632
S_I = 6656
S_F = 7168
NS_COLS = 7680

V7X_LANES = 128
V7X_SUBLANES = 8
VMEM_LIMIT = 56 * 1024 * 1024
PLAYER_VMEM_LIMIT = 60 * 1024 * 1024
ROW_TILES = CHUNK // V7X_SUBLANES
LOG2E = 1.4426950408889634

_NT = (((1,), (1,)), ((), ()))
_TN = (((0,), (0,)), ((), ()))


def _softplus(x):
    return jnp.maximum(x, 0.0) + jnp.log1p(jnp.exp(-jnp.abs(x)))


def _log_sigmoid(x):
    return -_softplus(-x)


def _sigmoid(x):
    return 1.0 / (1.0 + jnp.exp(-x))


def _silu(x):
    return x * _sigmoid(x)


def _gelu_tanh(x):
    c = math.sqrt(2.0 / math.pi)
    return 0.5 * x * (1.0 + jnp.tanh(c * (x + 0.044715 * (x * x * x))))


def _neg_expm1(x):
    return -jnp.tanh(0.5 * x) * (jnp.exp(x) + 1.0)


def _rms(x, w):
    return x * lax.rsqrt(jnp.mean(x * x, axis=-1, keepdims=True) + EPS) * w


def _interleave_rows(x):
    b, t, d = x.shape
    return x.reshape(b, t // CHUNK, V7X_SUBLANES, ROW_TILES, d).swapaxes(2, 3).reshape(b, t, d)


def _deinterleave_rows(x):
    b, t, d = x.shape
    return x.reshape(b, t // CHUNK, ROW_TILES, V7X_SUBLANES, d).swapaxes(2, 3).reshape(b, t, d)


def _full(shape):
    n = len(shape)
    return pl.BlockSpec(shape, lambda *_: (0,) * n)


def _layer(a, l, resident=False):
    nd = a.ndim - 1
    kw = dict(pipeline_mode=pl.Buffered(1)) if resident else {}
    return pl.BlockSpec((None,) + a.shape[1:], lambda *_: (l,) + (0,) * nd, **kw)


def _ada_kernel(c_ref, w_ref, b_ref, o_ref):
    cs = _silu(c_ref[...]).astype(bf16)
    o_ref[0] = jnp.dot(cs, w_ref[0].astype(bf16), preferred_element_type=f32) + b_ref[0]


def _ada(c_all, ada_w, ada_b):
    n = c_all.shape[0]
    tn = 1024
    return pl.pallas_call(
        _ada_kernel,
        out_shape=jax.ShapeDtypeStruct((N_LAYERS, n, 6 * D_MODEL), f32),
        grid=(N_LAYERS, 6 * D_MODEL // tn),
        in_specs=[
            pl.BlockSpec((n, D_MODEL), lambda l, j: (0, 0)),
            pl.BlockSpec((1, D_MODEL, tn), lambda l, j: (l, 0, j)),
            pl.BlockSpec((1, 1, tn), lambda l, j: (l, 0, j)),
        ],
        out_specs=pl.BlockSpec((1, n, tn), lambda l, j: (l, 0, j)),
        compiler_params=pltpu.CompilerParams(
            dimension_semantics=("arbitrary", "arbitrary"), vmem_limit_bytes=VMEM_LIMIT),
        name="ada_mod",
    )(c_all, ada_w, ada_b.reshape(N_LAYERS, 1, 6 * D_MODEL))


def _inproj_kernel(x_ref, mod_ref, nw_ref, w_ref, o_ref):
    x = x_ref[...]
    bb, tt, d = x.shape
    hn = _rms(x, nw_ref[...]) * (1.0 + mod_ref[:, 1:2, :]) + mod_ref[:, 0:1, :]
    hn = hn.reshape(bb * tt, d).astype(bf16)
    u = jnp.dot(hn, w_ref[...], preferred_element_type=f32)
    o_ref[...] = u.reshape(bb, tt, u.shape[-1])


def _inproj(l, x, mod, nw, w, bb, tt):
    b, t, d = x.shape
    n = w.shape[-1]
    return pl.pallas_call(
        _inproj_kernel,
        out_shape=jax.ShapeDtypeStruct((b, t, n), f32),
        grid=(b // bb, t // tt),
        in_specs=[
            pl.BlockSpec((bb, tt, d), lambda i, j: (i, j, 0)),
            pl.BlockSpec((None, bb, 6, d), lambda i, j: (l, i, 0, 0)),
            _layer(nw, l),
            _layer(w, l, resident=True),
        ],
        out_specs=pl.BlockSpec((bb, tt, n), lambda i, j: (i, j, 0)),
        compiler_params=pltpu.CompilerParams(
            dimension_semantics=("arbitrary", "arbitrary"), vmem_limit_bytes=VMEM_LIMIT),
        name="inproj",
    )(x, mod, nw, w)


def _outmlp_kernel(x_ref, y_ref, mod_ref, wout_ref, n2w_ref, up_ref, down_ref, fnw_ref, o_ref,
                   *, final):
    x = x_ref[...]
    bb, tt, d = x.shape
    m = bb * tt
    ycat = y_ref[...].reshape(m, MIX_WIDTH).astype(bf16)
    mix = jnp.dot(ycat, wout_ref[:, 0:d], preferred_element_type=f32).reshape(bb, tt, d)
    x1 = x + mod_ref[:, 2:3, :] * mix
    hn = _rms(x1, n2w_ref[...]) * (1.0 + mod_ref[:, 4:5, :]) + mod_ref[:, 3:4, :]
    hn = hn.reshape(m, d).astype(bf16)
    ff = jnp.zeros((m, d), f32)
    fc = 1024
    for c in range(D_FF // fc):
        h = jnp.dot(hn, up_ref[:, c * fc:(c + 1) * fc], preferred_element_type=f32)
        h = jnp.square(jnp.maximum(h, 0.0)).astype(bf16)
        ff = ff + jnp.dot(h, down_ref[c * fc:(c + 1) * fc, 0:D_MODEL],
                          preferred_element_type=f32)
    x2 = x1 + mod_ref[:, 5:6, :] * ff.reshape(bb, tt, d)
    if final:
        x2 = _rms(x2, fnw_ref[...])
    o_ref[...] = x2


def _outmlp(l, x, ycat, mod, wout, n2w, up, down, fnw, bb, tt, final):
    b, t, d = x.shape
    return pl.pallas_call(
        functools.partial(_outmlp_kernel, final=final),
        out_shape=jax.ShapeDtypeStruct((b, t, d), f32),
        grid=(b // bb, t // tt),
        in_specs=[
            pl.BlockSpec((bb, tt, d), lambda i, j: (i, j, 0)),
            pl.BlockSpec((bb, tt, MIX_WIDTH), lambda i, j: (i, j, 0)),
            pl.BlockSpec((None, bb, 6, d), lambda i, j: (l, i, 0, 0)),
            _layer(wout, l, resident=True),
            _layer(n2w, l),
            _layer(up, l, resident=True),
            _layer(down, l, resident=True),
            _full((1, d)),
        ],
        out_specs=pl.BlockSpec((bb, tt, d), lambda i, j: (i, j, 0)),
        compiler_params=pltpu.CompilerParams(
            dimension_semantics=("arbitrary", "arbitrary"), vmem_limit_bytes=VMEM_LIMIT),
        name="outproj_mlp",
    )(x, ycat, mod, wout, n2w, up, down, fnw)


PIECE = 512


def _stage_in(x, mod_ref, nw_ref, w_ref, u_ref):
    hn = _rms(x, nw_ref[...]) * (1.0 + mod_ref[0, 1:2, :]) + mod_ref[0, 0:1, :]
    hn = hn.astype(bf16)
    yield
    for c0 in range(0, NP_COLS, PIECE):
        c1 = min(c0 + PIECE, NP_COLS)
        u_ref[:, c0:c1] = jnp.dot(hn, w_ref[:, c0:c1], preferred_element_type=f32)
        yield


OUT_PIECES_FIRST = 9


def _stage_out(x, ycat, o_ref, mod_ref, wout_ref, n2w_ref, up_ref, down_ref, fnw_ref, final):
    mix = []
    for c0 in range(0, D_MODEL, PIECE):
        mix.append(jnp.dot(ycat, wout_ref[:, c0:c0 + PIECE], preferred_element_type=f32))
        yield
    x1 = x + mod_ref[0, 2:3, :] * jnp.concatenate(mix, axis=1)
    hn = (_rms(x1, n2w_ref[...]) * (1.0 + mod_ref[0, 4:5, :]) + mod_ref[0, 3:4, :]).astype(bf16)
    yield
    h = []
    for c0 in range(0, D_FF, PIECE):
        hk = jnp.dot(hn, up_ref[:, c0:c0 + PIECE], preferred_element_type=f32)
        h.append(jnp.square(jnp.maximum(hk, 0.0)).astype(bf16))
        yield
    h = jnp.concatenate(h, axis=1)
    ff = []
    for c0 in range(0, D_MODEL, PIECE // 2):
        ff.append(jnp.dot(h, down_ref[:, c0:c0 + PIECE // 2], preferred_element_type=f32))
        yield
    x2 = x1 + mod_ref[0, 5:6, :] * jnp.concatenate(ff, axis=1)
    o_ref[0] = _rms(x2, fnw_ref[...]) if final else x2
    yield


def _trace_interleaved(main, fillers, weights):
    fillers = [[g, n] for g, n in fillers if n is None or n > 0]
    n_fill = sum(N_IN_PIECES if n is None else n for _, n in fillers)
    due = 0.0
    nxt = 0
    phase = 0
    main_live = True
    while main_live or fillers:
        if main_live:
            main_live = next(main, _DONE) is not _DONE
            due += n_fill * weights[min(phase, len(weights) - 1)] / sum(weights)
            phase += 1
        else:
            due = float(len(fillers))
        while fillers and due >= 1.0:
            nxt %= len(fillers)
            gen, left = fillers[nxt]
            done = next(gen, _DONE) is _DONE
            if not done and left is not None:
                fillers[nxt][1] = left = left - 1
                done = left == 0
            if done:
                fillers.pop(nxt)
            else:
                nxt += 1
            due -= 1.0


_DONE = object()


def _reset_state_if(first, state):
    for ref in state:
        ref[...] = jnp.where(first, 0.0, ref[...])


def _stage_mix(u_ref, y_ref, prm, state, ci, outs):
    (cw_ref, cb_ref, sbias_ref, alog_ref, dskip_ref, snw_ref, mnw_ref,
     lcw_ref, lcb_ref, wax_ref, ba_ref, bx_ref, lam_ref) = prm
    xbuf, lbuf, st, cst, nst, mst, hst = state
    L = CHUNK
    SUB = V7X_SUBLANES
    HIST = (CONV_W - 1) * SUB

    row = lax.broadcasted_iota(jnp.int32, (L, L), 0)
    col = lax.broadcasted_iota(jnp.int32, (L, L), 1)
    tok = lambda i: (i & (SUB - 1)) * ROW_TILES + (i >> 3)
    tri = tok(row) >= tok(col)
    lo = col < SSD_HEAD_DIM

    def bcl(a, j):
        return jnp.broadcast_to(a[:, j:j + 1], (L, L))

    def conv(u0, hist_ref, w_ref, b_ref, c0, c1):
        raw = u_ref[:, u0 + c0:u0 + c1]
        sub = lax.broadcasted_iota(jnp.int32, (SUB, c1 - c0), 0)
        prev = [pltpu.roll(jnp.where(sub == SUB - 1, hist_ref[k * SUB:(k + 1) * SUB, c0:c1],
                                     raw[L - HIST + k * SUB:L - HIST + (k + 1) * SUB, :]), 1, 0)
                for k in range(CONV_W - 1)]
        acc = b_ref[:, c0:c1] + raw * w_ref[CONV_W - 1:CONV_W, c0:c1]
        for d in range(1, CONV_W):
            shifted = jnp.concatenate(prev[CONV_W - 1 - d:] + [raw[0:L - d * SUB, :]], axis=0)
            acc = acc + shifted * w_ref[CONV_W - 1 - d:CONV_W - d, c0:c1]
        hist_ref[:, c0:c1] = raw[L - HIST:L, :]
        return acc

    cblk = 512
    xbc = []
    for c0 in range(0, SSD_CONV_DIM, cblk):
        xbc.append(_silu(conv(P_XBC, xbuf, cw_ref, cb_ref, c0, c0 + cblk)))
        yield
    xbc = jnp.concatenate(xbc, axis=1)
    xr = conv(P_XR, lbuf, lcw_ref, lcb_ref, 0, LRU_WIDTH)
    yield

    sm = u_ref[:,P_SMALL:P_SMALL + V7X_LANES] + sbias_ref[...]
    dt = _softplus(sm)
    a_row = -jnp.exp(alog_ref[...])
    gates = jnp.where(col < SSD_HEADS, dt * (a_row * LOG2E),
                      jnp.where((col >= 20) & (col < 24), _log_sigmoid(sm), 0.0))
    g_hi = gates.astype(bf16)
    g_r = gates - g_hi.astype(f32)
    g_mid = g_r.astype(bf16)
    g_lo = (g_r - g_mid.astype(f32)).astype(bf16)
    cum3 = jnp.dot(jnp.where(tri, 1.0, 0.0).astype(bf16),
                   jnp.concatenate([g_hi, g_mid, g_lo], axis=1), preferred_element_type=f32)
    cum = cum3[:, 0:L] + cum3[:, L:2 * L] + cum3[:, 2 * L:3 * L]
    cum_t = cum.T
    sm_t = sm.T
    yield

    ys = []
    for g in range(SSD_GROUPS):
        bm_t = xbc[:, SSD_WIDTH + g * SSD_STATE:SSD_WIDTH + (g + 1) * SSD_STATE].T.astype(bf16)
        cm = xbc[:, SSD_WIDTH + (SSD_GROUPS + g) * SSD_STATE:
                 SSD_WIDTH + (SSD_GROUPS + g + 1) * SSD_STATE].astype(bf16)
        cb = jnp.dot(cm, bm_t, preferred_element_type=f32)
        inter = jnp.dot(cm, st[g].astype(bf16), preferred_element_type=f32)
        yield
        xends, keeps = [], []
        for jj in range(4):
            jp = g * 4 + jj
            e0, e1 = 2 * jp, 2 * jp + 1
            c0 = bcl(cum, e0)
            c1 = bcl(cum, e1)
            cum_p = jnp.where(lo, c0, c1)
            dt_p = jnp.where(lo, bcl(dt, e0), bcl(dt, e1))
            xs_p = xbc[:, jp * L:(jp + 1) * L]
            xdt = xs_p * dt_p
            dec0 = jnp.exp2(jnp.where(tri, c0 - cum_t[e0:e0 + 1, :], -jnp.inf))
            dec1 = jnp.exp2(jnp.where(tri, c1 - cum_t[e1:e1 + 1, :], -jnp.inf))
            att = jnp.concatenate([(cb * dec0).astype(bf16), (cb * dec1).astype(bf16)], axis=1)
            x2 = jnp.concatenate([jnp.where(lo, xdt, 0.0).astype(bf16),
                                  jnp.where(lo, 0.0, xdt).astype(bf16)], axis=0)
            y_p = (jnp.dot(att, x2, preferred_element_type=f32)
                   + inter[:, jj * L:(jj + 1) * L] * jnp.exp2(cum_p)
                   + xs_p * dskip_ref[:, jp * L:(jp + 1) * L])
            ys.append(y_p)
            c_last = cum_p[L - 1:L, :]
            xends.append((xdt * jnp.exp2(c_last - cum_p)).astype(bf16))
            keeps.append(jnp.exp2(c_last))
            yield
        st[g] = (jnp.concatenate(keeps, axis=1) * st[g]
                 + jnp.dot(bm_t, jnp.concatenate(xends, axis=1), preferred_element_type=f32))
    y = jnp.concatenate(ys, axis=1)
    z = u_ref[:,P_Z:P_Z + SSD_WIDTH]
    y_ref[:,0:SSD_WIDTH] = _rms(y * _silu(z), snw_ref[...]).astype(bf16)
    yield

    scale = ML_HEAD_DIM ** -0.5
    for h in range(ML_HEADS):
        sl = slice(h * ML_HEAD_DIM, (h + 1) * ML_HEAD_DIM)
        q = u_ref[:,P_Q + h * ML_HEAD_DIM:P_Q + (h + 1) * ML_HEAD_DIM]
        k = u_ref[:,P_K + h * ML_HEAD_DIM:P_K + (h + 1) * ML_HEAD_DIM] * scale
        v = u_ref[:,P_V + h * ML_HEAD_DIM:P_V + (h + 1) * ML_HEAD_DIM]
        o = u_ref[:,P_O + h * ML_HEAD_DIM:P_O + (h + 1) * ML_HEAD_DIM]
        k_t = k.T.astype(bf16)
        jf = 20 + h
        ji = 16 + h
        b_col = cum[:, jf:jf + 1]
        i_col = sm[:, ji:ji + 1]
        dm = jnp.where(tri, bcl(cum, jf) - cum_t[jf:jf + 1, :] + sm_t[ji:ji + 1, :], -jnp.inf)
        m_prev = mst[h, :, 0:1]
        g_col = b_col + m_prev
        mt = jnp.maximum(g_col, jnp.max(dm, axis=-1, keepdims=True))
        w = jnp.exp(dm - mt) * jnp.dot(q.astype(bf16), k_t, preferred_element_type=f32)
        inter = jnp.exp(g_col - mt)
        ct_old = cst[h]
        n_old = nst[h]
        num = jnp.dot(jnp.concatenate([w.astype(bf16), (inter * q).astype(bf16)], axis=1),
                      jnp.concatenate([v.astype(bf16), ct_old.astype(bf16)], axis=0),
                      preferred_element_type=f32)
        den = (jnp.sum(w, axis=-1, keepdims=True)
               + inter * jnp.sum(q * n_old, axis=-1, keepdims=True))
        hout = num / jnp.maximum(jnp.abs(den), jnp.exp(-mt))
        m_new = mt[L - 1:L, :]
        b_last = b_col[L - 1:L, :]
        w_end = jnp.exp(b_last - b_col + i_col - m_new)
        dc = jnp.exp(b_last + m_prev - m_new)
        cst[h] = dc * ct_old + jnp.dot(k_t, (v * w_end).astype(bf16),
                                       preferred_element_type=f32)
        nst[h] = dc * n_old + jnp.sum(w_end * k, axis=0, keepdims=True)
        mst[h] = jnp.broadcast_to(m_new, (1, V7X_LANES))
        yh = _rms(hout, mnw_ref[:, sl]) * _sigmoid(o)
        y_ref[:,SSD_WIDTH + h * ML_HEAD_DIM:SSD_WIDTH + (h + 1) * ML_HEAD_DIM] = yh.astype(bf16)
        yield

    rs, is_ = [], []
    for kb_ in range(LRU_BLOCKS):
        xk = xr[:, kb_ * LRU_BLOCK_DIM:(kb_ + 1) * LRU_BLOCK_DIM].astype(bf16)
        ri = jnp.dot(xk, wax_ref[kb_], preferred_element_type=f32)
        rs.append(ri[:, 0:LRU_BLOCK_DIM])
        is_.append(ri[:, LRU_BLOCK_DIM:2 * LRU_BLOCK_DIM])
    r = _sigmoid(jnp.concatenate(rs, axis=1) + ba_ref[...])
    i_g = _sigmoid(jnp.concatenate(is_, axis=1) + bx_ref[...])
    yield
    log_a = -LRU_C * r * _softplus(-lam_ref[...])
    mult = jnp.sqrt(_neg_expm1(2.0 * log_a))
    rowl = lax.broadcasted_iota(jnp.int32, (L, LRU_WIDTH), 0)
    first_row = jnp.where(ci == 0, 0, -1)
    mult = jnp.where(rowl == first_row, 1.0, mult)
    a_s = jnp.exp(log_a)
    u_s = mult * i_g * xr
    yield
    tile = lambda a, v: a[v * SUB:(v + 1) * SUB, :]
    h_loc = [tile(u_s, 0)]
    a_cum = [tile(a_s, 0)]
    for v in range(1, ROW_TILES):
        h_loc.append(tile(a_s, v) * h_loc[-1] + tile(u_s, v))
        a_cum.append(tile(a_s, v) * a_cum[-1])
    a_e, h_e = a_cum[-1], h_loc[-1]
    subl = lax.broadcasted_iota(jnp.int32, (SUB, LRU_WIDTH), 0)
    k_ = 1
    while k_ < SUB:
        keep = subl >= k_
        a_sh = jnp.where(keep, pltpu.roll(a_e, k_, 0), 1.0)
        h_sh = jnp.where(keep, pltpu.roll(h_e, k_, 0), 0.0)
        h_e = a_e * h_sh + h_e
        a_e = a_e * a_sh
        k_ *= 2
    h0 = hst[...]
    run_end = h_e + a_e * h0
    run_in = jnp.where(subl == 0, h0, pltpu.roll(run_end, 1, 0))
    hr = jnp.concatenate([h_loc[v] + a_cum[v] * run_in for v in range(ROW_TILES)], axis=0)
    hst[...] = run_end[SUB - 1:SUB, :]
    gr = u_ref[:,P_GR:P_GR + LRU_WIDTH]
    y_ref[:,SSD_WIDTH + ML_WIDTH:MIX_WIDTH] = (hr * _gelu_tanh(gr)).astype(bf16)
    yield

    if outs is not None:
        ssm_ref, sconv_ref, mc_ref, mn_ref, mm_ref, lh_ref, lconv_ref = outs
        for g in range(SSD_GROUPS):
            for jj in range(4):
                ssm_ref[0, g * 4 + jj] = st[g, :, jj * L:(jj + 1) * L].T
        for k in range(CONV_W - 1):
            sconv_ref[0, k:k + 1, :] = xbuf[k * SUB + SUB - 1:(k + 1) * SUB, :]
            lconv_ref[0, k:k + 1, :] = lbuf[k * SUB + SUB - 1:(k + 1) * SUB, :]
        for h in range(ML_HEADS):
            mc_ref[0, h] = cst[h].T
            mn_ref[0, h:h + 1, :] = nst[h]
            mm_ref[0, h:h + 1, :] = mst[h]
        lh_ref[0] = hst[...]


N_MIX_PARAMS = 13
N_STATE_OUTS = 7
STEPS = 2
Y_SLOTS = 3
N_IN_PIECES = 13
MIX_PHASE_WEIGHTS = (2, 2, 2, 1, 2) + ((1,) + (3,) * 4) * 2 + (3,) + (4,) * 4 + (2, 2, 3) + (1,)


def _player_kernel(*refs, final, n_chunks, nc):
    xa_ref, moda_ref, xc_ref, modc_ref, n1w_ref, win_ref = refs[:6]
    prm = refs[6:6 + N_MIX_PARAMS]
    wout_ref, n2w_ref, up_ref, down_ref, fnw_ref = refs[6 + N_MIX_PARAMS:11 + N_MIX_PARAMS]
    n_in = 11 + N_MIX_PARAMS + N_STATE_OUTS
    o_ref = refs[n_in]
    outs = refs[n_in + 1:n_in + 1 + N_STATE_OUTS]
    u_bufs = refs[-10:-8]
    y_ring = refs[-8]
    state = refs[-7:]
    k = pl.program_id(0)

    @pl.when(k == 0)
    def _():
        u_bufs[1][...] = jnp.zeros(u_bufs[1].shape, f32)
        y_ring[...] = jnp.zeros(y_ring.shape, bf16)
        for ref in state:
            ref[...] = jnp.zeros(ref.shape, f32)

    stage_out = _stage_out(xc_ref[0], y_ring[(k + Y_SLOTS - 2) % Y_SLOTS], o_ref, modc_ref,
                           wout_ref, n2w_ref, up_ref, down_ref, fnw_ref, final)
    for sub in range(STEPS):
        rows = slice(sub * CHUNK, (sub + 1) * CHUNK)
        stage_in = _stage_in(xa_ref[0, rows, :], moda_ref, n1w_ref, win_ref, u_bufs[sub])
        jb = jnp.clip(STEPS * k + sub - 1, 0, n_chunks - 1)
        ci = jb % nc
        if sub == 1:
            _reset_state_if(ci == 0, state)
        y_ref = y_ring.at[(k + Y_SLOTS - 1 + sub) % Y_SLOTS, pl.ds((1 - sub) * CHUNK, CHUNK)]
        stage_mix = _stage_mix(u_bufs[1 - sub], y_ref, prm, state, ci, outs if sub == 0 else None)
        _trace_interleaved(stage_mix, [(stage_in, None),
                                       (stage_out, OUT_PIECES_FIRST if sub == 0 else None)],
                           MIX_PHASE_WEIGHTS)


def _player(l, x, mod, mod_row0, w, fnw, final, prev):
    b, t, d = x.shape
    nc = t // CHUNK
    n_chunks = b * nc
    assert nc % STEPS == 0
    n_pairs = n_chunks // STEPS
    ppr = nc // STEPS
    pair_a = lambda k: jnp.minimum(k, n_pairs - 1)
    pair_c = lambda k: jnp.clip(k - 2, 0, n_pairs - 1)
    seq_b = lambda k: jnp.where(STEPS * k - 1 < n_chunks, jnp.maximum(STEPS * k - 1, 0) // nc, b)
    xspec = lambda pair: pl.BlockSpec((1, STEPS * CHUNK, d),
                                      lambda k: (pair(k) // ppr, pair(k) % ppr, 0))
    mspec = lambda pair: pl.BlockSpec((None, 1, 6, d),
                                      lambda k: (l, mod_row0 + pair(k) // ppr, 0, 0))
    st_spec = lambda s: pl.BlockSpec((None, 1) + s[2:],
                                     lambda k: (l, seq_b(k)) + (0,) * (len(s) - 2))
    outs = [jax.ShapeDtypeStruct((b, t, d), f32)]
    outs += [jax.ShapeDtypeStruct(a.shape, f32) for a in prev]
    mix_params = [w["ssd_conv_w"], w["ssd_conv_b"], w["p_small_bias"], w["p_alog"], w["dskip"],
                  w["ssd_norm_w"], w["ml_norm_w"], w["lru_conv_w"], w["lru_conv_b"], w["lru_wax"],
                  w["lru_ba"], w["lru_bx"], w["lru_lambda"]]
    assert len(mix_params) == N_MIX_PARAMS and len(prev) == N_STATE_OUTS
    n_in = 11 + N_MIX_PARAMS
    return pl.pallas_call(
        functools.partial(_player_kernel, final=final, n_chunks=n_chunks, nc=nc),
        out_shape=outs,
        grid=(n_pairs + 2,),
        in_specs=[xspec(pair_a), mspec(pair_a), xspec(pair_c), mspec(pair_c),
                  _layer(w["norm1_w"], l), _layer(w["w_in_p"], l, resident=True)]
        + [_layer(a, l) for a in mix_params]
        + [_layer(w["w_out"], l, resident=True), _layer(w["norm2_w"], l),
           _layer(w["mlp_up"], l, resident=True), _layer(w["mlp_down"], l, resident=True),
           _full((1, d))]
        + [pl.BlockSpec(memory_space=pl.ANY)] * len(prev),
        out_specs=[xspec(pair_c)] + [st_spec(a.shape) for a in prev],
        input_output_aliases={n_in + i: 1 + i for i in range(len(prev))},
        scratch_shapes=[
            pltpu.VMEM((CHUNK, NP_COLS), f32), pltpu.VMEM((CHUNK, NP_COLS), f32)]
        + [pltpu.VMEM((Y_SLOTS, STEPS * CHUNK, MIX_WIDTH), bf16)] + [
            pltpu.VMEM(((CONV_W - 1) * V7X_SUBLANES, SSD_CONV_DIM), f32),
            pltpu.VMEM(((CONV_W - 1) * V7X_SUBLANES, LRU_WIDTH), f32),
            pltpu.VMEM((SSD_GROUPS, SSD_STATE, 512), f32),
            pltpu.VMEM((ML_HEADS, ML_HEAD_DIM, ML_HEAD_DIM), f32),
            pltpu.VMEM((ML_HEADS, 1, ML_HEAD_DIM), f32),
            pltpu.VMEM((ML_HEADS, 1, V7X_LANES), f32),
            pltpu.VMEM((1, LRU_WIDTH), f32),
        ],
        compiler_params=pltpu.CompilerParams(
            dimension_semantics=("arbitrary",), vmem_limit_bytes=PLAYER_VMEM_LIMIT),
        name="prompt_layer",
    )(x, mod, x, mod, w["norm1_w"], w["w_in_p"], *mix_params,
      w["w_out"], w["norm2_w"], w["mlp_up"], w["mlp_down"], fnw, *prev)


def _smix_kernel(u_ref, s0_ref, sconv0_ref, c0_ref, n0_ref, m0_ref, lh0_ref, lconv0_ref,
                 cw_ref, cb_ref, dtb_ref, alog_ref, dskip_ref, snw_ref, ib_ref, fb_ref, mnw_ref,
                 lcw_ref, lcb_ref, wa_ref, wx_ref, ba_ref, bx_ref, lam_ref,
                 y_ref, s_ref, sconv_ref, c_ref, n_ref, m_ref, lh_ref, lconv_ref,
                 xbuf, lbuf):
    bb = u_ref.shape[0]
    T = DEC_SEQ
    HEAD = V7X_SUBLANES

    def bs(a, s):
        return jnp.broadcast_to(a[:, s:s + 1, :], a.shape)

    def tio(n):
        return lax.broadcasted_iota(jnp.int32, (bb, T, n), 1)

    def lio(n):
        return lax.broadcasted_iota(jnp.int32, (bb, T, n), 2)

    def cumsum_t(a):
        t = tio(a.shape[-1])
        acc = jnp.zeros_like(a)
        for s in range(T):
            acc = acc + jnp.where(t >= s, bs(a, s), 0.0)
        return acc

    def conv(buf, raw, hist, w_ref, b_ref, n):
        buf[:, HEAD - 3:HEAD, :] = hist
        buf[:, HEAD:HEAD + T, :] = raw
        acc = b_ref[...] + buf[:, HEAD - 3:HEAD - 3 + T, :] * w_ref[0:1, :]
        for j in range(1, CONV_W):
            acc = acc + buf[:, HEAD - 3 + j:HEAD - 3 + j + T, :] * w_ref[j:j + 1, :]
        return acc, buf[:, HEAD + T - 3:HEAD + T, :]

    xbc, sconv_new = conv(xbuf, u_ref[:, :, P_XBC:P_XBC + SSD_CONV_DIM], sconv0_ref[...],
                          cw_ref, cb_ref, SSD_CONV_DIM)
    sconv_ref[...] = sconv_new
    xbc = _silu(xbc)
    xr, lconv_new = conv(lbuf, u_ref[:, :, P_XR:P_XR + LRU_WIDTH], lconv0_ref[...],
                         lcw_ref, lcb_ref, LRU_WIDTH)
    lconv_ref[...] = lconv_new

    xs = xbc[:, :, 0:SSD_WIDTH]
    t_w = tio(SSD_WIDTH)
    l_w = lio(SSD_WIDTH)
    dt = _softplus(u_ref[:, :, S_DT:S_DT + SSD_WIDTH] + dtb_ref[...])
    cum = cumsum_t(dt * (-jnp.exp(alog_ref[...])))
    xdt = xs * dt
    bms = [xbc[:, :, SSD_WIDTH + g * SSD_STATE:SSD_WIDTH + (g + 1) * SSD_STATE]
           for g in range(SSD_GROUPS)]
    cms = [xbc[:, :, SSD_WIDTH + (SSD_GROUPS + g) * SSD_STATE:
               SSD_WIDTH + (SSD_GROUPS + g + 1) * SSD_STATE] for g in range(SSD_GROUPS)]
    y = xs * dskip_ref[...]
    for s in range(T):
        cb0 = jnp.sum(cms[0] * bs(bms[0], s), axis=-1, keepdims=True)
        cb1 = jnp.sum(cms[1] * bs(bms[1], s), axis=-1, keepdims=True)
        cbs = jnp.where(l_w < 512, cb0, cb1)
        dec = jnp.exp(jnp.where(t_w >= s, cum - bs(cum, s), -jnp.inf))
        y = y + cbs * dec * bs(xdt, s)
    inter = jnp.concatenate(
        [jnp.einsum('btn,bpn->btp', cms[g].astype(bf16),
                    s0_ref[:, g * 512:(g + 1) * 512, :].astype(bf16),
                    preferred_element_type=f32) for g in range(SSD_GROUPS)], axis=-1)
    y = y + inter * jnp.exp(cum)
    z = u_ref[:, :, P_Z:P_Z + SSD_WIDTH]
    y_ref[:, :, 0:SSD_WIDTH] = _rms(y * _silu(z), snw_ref[...])

    def split3(d, t_idx):
        hi = d.astype(bf16).astype(f32)
        r1 = d - hi
        mid = r1.astype(bf16).astype(f32)
        lo_ = (r1 - mid).astype(bf16).astype(f32)
        return jnp.where(t_idx == 0, hi, jnp.where(t_idx == 1, mid, jnp.where(t_idx == 2, lo_, 0.0)))

    def ones_rhs(kmat):
        zeros = jnp.zeros_like(kmat)
        return jnp.concatenate(
            [jnp.concatenate([kmat, zeros], axis=2),
             jnp.concatenate([zeros, jnp.ones_like(kmat)], axis=2)], axis=1).astype(bf16)

    c_last = bs(cum, T - 1)
    xend = xdt * jnp.exp(c_last - cum)
    lhs = jnp.concatenate([xend, split3(jnp.exp(c_last), t_w)], axis=1).astype(bf16)
    for g in range(SSD_GROUPS):
        zz = jnp.einsum('bkp,bkn->bpn', lhs[:, :, g * 512:(g + 1) * 512],
                        ones_rhs(bms[g]), preferred_element_type=f32)
        s_ref[:, g * 512:(g + 1) * 512, :] = (
            zz[:, :, SSD_STATE:] * s0_ref[:, g * 512:(g + 1) * 512, :] + zz[:, :, :SSD_STATE])

    t_m = tio(ML_WIDTH)
    q = u_ref[:, :, P_Q:P_Q + ML_WIDTH]
    k = u_ref[:, :, P_K:P_K + ML_WIDTH] * (ML_HEAD_DIM ** -0.5)
    v = u_ref[:, :, P_V:P_V + ML_WIDTH]
    o = u_ref[:, :, P_O:P_O + ML_WIDTH]
    ic = u_ref[:, :, S_I:S_I + ML_WIDTH] + ib_ref[...]
    fc = _log_sigmoid(u_ref[:, :, S_F:S_F + ML_WIDTH] + fb_ref[...])
    bc = cumsum_t(fc)
    m0 = m0_ref[...]
    g_ = bc + m0

    def headsum(a):
        return jnp.concatenate(
            [jnp.broadcast_to(jnp.sum(a[:, :, h * ML_HEAD_DIM:(h + 1) * ML_HEAD_DIM], axis=-1,
                                      keepdims=True), (bb, T, ML_HEAD_DIM))
             for h in range(ML_HEADS)], axis=-1)

    dms = [jnp.where(t_m >= s, bc - bs(bc, s) + bs(ic, s), -jnp.inf) for s in range(T)]
    mt = g_
    for s in range(T):
        mt = jnp.maximum(mt, dms[s])
    num = jnp.zeros((bb, T, ML_WIDTH), f32)
    den = jnp.zeros((bb, T, ML_WIDTH), f32)
    for s in range(T):
        w = jnp.exp(dms[s] - mt) * headsum(q * bs(k, s))
        num = num + w * bs(v, s)
        den = den + w
    inter_m = jnp.exp(g_ - mt)
    qb = q.astype(bf16)
    qc = jnp.concatenate(
        [jnp.einsum('btk,bvk->btv', qb[:, :, h * ML_HEAD_DIM:(h + 1) * ML_HEAD_DIM],
                    c0_ref[:, h * ML_HEAD_DIM:(h + 1) * ML_HEAD_DIM, :].astype(bf16),
                    preferred_element_type=f32) for h in range(ML_HEADS)], axis=-1)
    n0 = n0_ref[...]
    num = num + inter_m * qc
    den = den + inter_m * headsum(q * n0)
    hout = num / jnp.maximum(jnp.abs(den), jnp.exp(-mt))
    m_new = bs(mt, T - 1)
    b_last = bs(bc, T - 1)
    w_end = jnp.exp(b_last - bc + ic - m_new)
    dc = jnp.exp(b_last + m0 - m_new)
    m_ref[...] = mt[:, T - 1:T, :]
    n_ref[...] = dc[:, 0:1, :] * n0 + jnp.sum(w_end * k, axis=1, keepdims=True)
    lhs_m = jnp.concatenate([v * w_end, split3(dc, t_m)], axis=1).astype(bf16)
    for h in range(ML_HEADS):
        sl = slice(h * ML_HEAD_DIM, (h + 1) * ML_HEAD_DIM)
        zz = jnp.einsum('bkv,bkn->bvn', lhs_m[:, :, sl], ones_rhs(k[:, :, sl]),
                        preferred_element_type=f32)
        c_ref[:, sl, :] = zz[:, :, ML_HEAD_DIM:] * c0_ref[:, sl, :] + zz[:, :, :ML_HEAD_DIM]
    hn2 = headsum(hout * hout) * (1.0 / ML_HEAD_DIM)
    y_ml = hout * lax.rsqrt(hn2 + EPS) * mnw_ref[...] * _sigmoid(o)
    y_ref[:, :, SSD_WIDTH:SSD_WIDTH + ML_WIDTH] = y_ml

    xr2 = xr.reshape(bb * T, LRU_WIDTH)
    rs, is_ = [], []
    for kk in range(LRU_BLOCKS):
        xk = xr2[:, kk * LRU_BLOCK_DIM:(kk + 1) * LRU_BLOCK_DIM].astype(bf16)
        rs.append(jnp.dot(xk, wa_ref[kk], preferred_element_type=f32))
        is_.append(jnp.dot(xk, wx_ref[kk], preferred_element_type=f32))
    r = _sigmoid(jnp.concatenate(rs, axis=1).reshape(bb, T, LRU_WIDTH) + ba_ref[...])
    i_g = _sigmoid(jnp.concatenate(is_, axis=1).reshape(bb, T, LRU_WIDTH) + bx_ref[...])
    log_a = -LRU_C * r * _softplus(-lam_ref[...])
    a_s = jnp.exp(log_a)
    u_s = jnp.sqrt(_neg_expm1(2.0 * log_a)) * i_g * xr
    t_l = tio(LRU_WIDTH)
    cur = lh0_ref[...]
    hr = jnp.zeros((bb, T, LRU_WIDTH), f32)
    for s in range(T):
        cur = a_s[:, s:s + 1, :] * cur + u_s[:, s:s + 1, :]
        hr = jnp.where(t_l == s, jnp.broadcast_to(cur, hr.shape), hr)
    lh_ref[...] = cur
    gr = u_ref[:, :, P_GR:P_GR + LRU_WIDTH]
    y_ref[:, :, SSD_WIDTH + ML_WIDTH:MIX_WIDTH] = hr * _gelu_tanh(gr)


def _smix(l, u, states, prev, w, bb):
    b = u.shape[0]
    row = lambda n: _full((1, n))
    lblk = lambda s: pl.BlockSpec((None, bb) + s[2:], lambda i: (l, i) + (0,) * (len(s) - 2))
    ublk = lambda s: pl.BlockSpec((bb,) + s[1:], lambda i: (i,) + (0,) * (len(s) - 1))
    y_shape = (b, DEC_SEQ, MIX_WIDTH)
    outs = [jax.ShapeDtypeStruct(y_shape, f32)]
    outs += [jax.ShapeDtypeStruct(s.shape, f32) for s in states]
    weights = [w["ssd_conv_w"], w["ssd_conv_b"], w["s_dt_bias"], w["s_alog"], w["dskip"],
               w["ssd_norm_w"], w["s_i_bias"], w["s_f_bias"], w["ml_norm_w"],
               w["lru_conv_w"], w["lru_conv_b"], w["lru_wa"], w["lru_wx"], w["lru_ba"],
               w["lru_bx"], w["lru_lambda"]]
    n_in = 1 + len(states) + len(weights)
    prev = list(prev)

    def body(*refs):
        _smix_kernel(*refs[:n_in], *refs[n_in + len(prev):])

    return pl.pallas_call(
        body,
        out_shape=outs,
        grid=(b // bb,),
        in_specs=[ublk(u.shape)] + [lblk(s.shape) for s in states]
        + [_layer(a, l) for a in weights]
        + [pl.BlockSpec(memory_space=pl.ANY)] * len(prev),
        out_specs=[ublk(y_shape)] + [lblk(s.shape) for s in states],
        input_output_aliases={n_in + k: 1 + k for k in range(len(prev))},
        scratch_shapes=[
            pltpu.VMEM((bb, 2 * V7X_SUBLANES, SSD_CONV_DIM), f32),
            pltpu.VMEM((bb, 2 * V7X_SUBLANES, LRU_WIDTH), f32),
        ],
        compiler_params=pltpu.CompilerParams(
            dimension_semantics=("arbitrary",), vmem_limit_bytes=VMEM_LIMIT),
        name="sample_mix",
    )(u, *states, *weights, *prev)


def _prep_params(p):
    w_in = p["w_in"].astype(bf16)
    main = jnp.concatenate([w_in[..., 0:_O_DT], w_in[..., _O_Q:_O_O_END], w_in[..., _O_XR:_O_END]],
                           axis=-1)
    dt_c = w_in[..., _O_DT:_O_Q]
    if_c = w_in[..., _O_I:_O_XR]
    pad = jnp.zeros((N_LAYERS, D_MODEL, V7X_LANES - SSD_HEADS - 2 * ML_HEADS), bf16)
    w_p = jnp.concatenate([main, dt_c, if_c, pad], axis=-1)
    w_s = jnp.concatenate([main, jnp.repeat(dt_c, SSD_HEAD_DIM, axis=-1),
                           jnp.repeat(if_c, ML_HEAD_DIM, axis=-1)], axis=-1)
    r = lambda a: a.reshape(N_LAYERS, 1, -1)
    rep = lambda a, n: r(jnp.repeat(a, n, axis=-1))
    lane_pad = lambda a: jnp.pad(a.astype(bf16), ((0, 0), (0, 0), (0, V7X_LANES)))
    small = jnp.concatenate([p["ssd_dt_bias"], p["ml_i_bias"], p["ml_f_bias"]], axis=-1)
    return dict(
        w_in_p=w_p, w_in_s=w_s,
        norm1_w=r(p["norm1_w"]), norm2_w=r(p["norm2_w"]),
        ssd_conv_w=p["ssd_conv_w"], ssd_conv_b=r(p["ssd_conv_b"]),
        p_small_bias=r(jnp.pad(small, ((0, 0), (0, V7X_LANES - small.shape[-1])))),
        p_alog=r(jnp.pad(p["ssd_a_log"], ((0, 0), (0, V7X_LANES - SSD_HEADS)))),
        s_dt_bias=rep(p["ssd_dt_bias"], SSD_HEAD_DIM), s_alog=rep(p["ssd_a_log"], SSD_HEAD_DIM),
        s_i_bias=rep(p["ml_i_bias"], ML_HEAD_DIM), s_f_bias=rep(p["ml_f_bias"], ML_HEAD_DIM),
        dskip=rep(p["ssd_d"], SSD_HEAD_DIM),
        ssd_norm_w=r(p["ssd_norm_w"]), ml_norm_w=r(p["ml_norm_w"]),
        lru_conv_w=p["lru_conv_w"], lru_conv_b=r(p["lru_conv_b"]),
        lru_wa=p["lru_wa"].astype(bf16), lru_wx=p["lru_wx"].astype(bf16),
        lru_wax=jnp.concatenate([p["lru_wa"], p["lru_wx"]], axis=-1).astype(bf16),
        lru_ba=r(p["lru_ba"]), lru_bx=r(p["lru_bx"]), lru_lambda=r(p["lru_lambda"]),
        w_out=lane_pad(p["w_out"]), mlp_up=lane_pad(p["mlp_up"]), mlp_down=lane_pad(p["mlp_down"]),
    )


def kernel(x_prompt, x_sample, c_prompt, c_sample, state_ssm, state_ssd_conv, state_mlstm_c, state_mlstm_n, state_mlstm_m, state_lru_h, state_lru_conv, ada_w, ada_b, norm1_w, norm2_w, w_in, ssd_conv_w, ssd_conv_b, ssd_dt_bias, ssd_a_log, ssd_d, ssd_norm_w, ml_i_bias, ml_f_bias, ml_norm_w, lru_conv_w, lru_conv_b, lru_wa, lru_ba, lru_wx, lru_bx, lru_lambda, w_out, mlp_up, mlp_down, final_norm_w):
    p = dict(norm1_w=norm1_w, norm2_w=norm2_w, w_in=w_in, ssd_conv_w=ssd_conv_w,
             ssd_conv_b=ssd_conv_b, ssd_dt_bias=ssd_dt_bias, ssd_a_log=ssd_a_log, ssd_d=ssd_d,
             ssd_norm_w=ssd_norm_w, ml_i_bias=ml_i_bias, ml_f_bias=ml_f_bias,
             ml_norm_w=ml_norm_w, lru_conv_w=lru_conv_w, lru_conv_b=lru_conv_b, lru_wa=lru_wa,
             lru_ba=lru_ba, lru_wx=lru_wx, lru_bx=lru_bx, lru_lambda=lru_lambda, w_out=w_out,
             mlp_up=mlp_up, mlp_down=mlp_down)
    w = _prep_params(p)
    fnw = final_norm_w.reshape(1, D_MODEL)
    bp = x_prompt.shape[0]
    bs_ = x_sample.shape[0]

    mod = _ada(jnp.concatenate([c_sample, c_prompt], axis=0), ada_w, ada_b)
    mod = mod.reshape(N_LAYERS, bs_ + bp, 6, D_MODEL)

    xp = _interleave_rows(x_prompt)
    p_out = [jnp.zeros((N_LAYERS, bp + 1) + s, f32) for s in (
        (SSD_HEADS // 2, 2 * SSD_HEAD_DIM, SSD_STATE),
        (CONV_W - 1, SSD_CONV_DIM), (ML_HEADS, ML_HEAD_DIM, ML_HEAD_DIM),
        (ML_HEADS, ML_HEAD_DIM), (ML_HEADS, V7X_LANES),
        (1, LRU_WIDTH), (CONV_W - 1, LRU_WIDTH))]
    for l in range(N_LAYERS):
        xp, *p_out = _player(l, xp, mod, bs_, w, fnw, l == N_LAYERS - 1, p_out)
    xp = _deinterleave_rows(xp)
    ssm, sconv, mc, mn, mm, lh, lconv = (a[:, :bp] for a in p_out)
    p_states = (ssm.reshape(N_LAYERS, bp, SSD_HEADS, SSD_HEAD_DIM, SSD_STATE), sconv, mc, mn,
                mm[..., 0], lh.reshape(N_LAYERS, bp, LRU_WIDTH), lconv)

    xs = x_sample
    st_in = (
        state_ssm.reshape(N_LAYERS, bs_, SSD_HEADS * SSD_HEAD_DIM, SSD_STATE),
        state_ssd_conv,
        state_mlstm_c.reshape(N_LAYERS, bs_, ML_HEADS * ML_HEAD_DIM, ML_HEAD_DIM),
        state_mlstm_n.reshape(N_LAYERS, bs_, 1, ML_WIDTH),
        jnp.repeat(state_mlstm_m, ML_HEAD_DIM, axis=-1).reshape(N_LAYERS, bs_, 1, ML_WIDTH),
        state_lru_h.reshape(N_LAYERS, bs_, 1, LRU_WIDTH),
        state_lru_conv,
    )
    st_out = [jnp.zeros(s.shape, f32) for s in st_in]
    for l in range(N_LAYERS):
        u = _inproj(l, xs, mod, w["norm1_w"], w["w_in_s"], 32, DEC_SEQ)
        ycat, *st_out = _smix(l, u, st_in, st_out, w, 8)
        xs = _outmlp(l, xs, ycat, mod, w["w_out"], w["norm2_w"], w["mlp_up"], w["mlp_down"], fnw,
                     64, DEC_SEQ, l == N_LAYERS - 1)
    ssm, sconv, mc, mn, mm, lh, lconv = st_out
    s_states = (ssm.reshape(N_LAYERS, bs_, SSD_HEADS, SSD_HEAD_DIM, SSD_STATE), sconv,
                mc.reshape(N_LAYERS, bs_, ML_HEADS, ML_HEAD_DIM, ML_HEAD_DIM),
                mn.reshape(N_LAYERS, bs_, ML_HEADS, ML_HEAD_DIM),
                mm.reshape(N_LAYERS, bs_, ML_HEADS, ML_HEAD_DIM)[..., 0],
                lh.reshape(N_LAYERS, bs_, LRU_WIDTH), lconv)

    return (xp, xs) + p_states + s_states
```

```python
import functools
import math

import jax
import jax.numpy as jnp
from jax import lax
from jax.experimental import pallas as pl
from jax.experimental.pallas import tpu as pltpu

f32 = jnp.float32
bf16 = jnp.bfloat16

D_MODEL = 1024
N_LAYERS = 2
MIX_WIDTH = 2 * D_MODEL
SSD_WIDTH = 1024
SSD_HEAD_DIM = 64
SSD_HEADS = 16
SSD_GROUPS = 2
SSD_STATE = 128
SSD_CONV_DIM = SSD_WIDTH + 2 * SSD_GROUPS * SSD_STATE
ML_WIDTH = 512
ML_HEADS = 4
ML_HEAD_DIM = 128
LRU_WIDTH = 512
LRU_BLOCKS = 4
LRU_BLOCK_DIM = 128
LRU_C = 8.0
CONV_W = 4
CHUNK = 128
D_FF = 4 * D_MODEL
EPS = 1e-6
PAST_LEN = 16384
DEC_SEQ = 8

_O_Z, _O_XBC, _O_DT, _O_Q, _O_O_END, _O_I, _O_F, _O_XR, _O_END = (
    0, 1024, 2560, 2576, 4624, 4624, 4628, 4632, 5656)

P_Z = 0
P_XBC = 1024
P_Q = 2560
P_K = 3072
P_V = 3584
P_O = 4096
P_XR = 4608
P_GR = 5120
P_MAIN = 5632
P_SMALL = 5632
NP_COLS = 5760
S_DT = 5632
S_I = 6656
S_F = 7168
NS_COLS = 7680

V7X_LANES = 128
V7X_SUBLANES = 8
VMEM_LIMIT = 56 * 1024 * 1024
PLAYER_VMEM_LIMIT = 60 * 1024 * 1024
ROW_TILES = CHUNK // V7X_SUBLANES
LOG2E = 1.4426950408889634

_NT = (((1,), (1,)), ((), ()))
_TN = (((0,), (0,)), ((), ()))


def _softplus(x):
    return jnp.maximum(x, 0.0) + jnp.log1p(jnp.exp(-jnp.abs(x)))


def _log_sigmoid(x):
    return -_softplus(-x)


def _sigmoid(x):
    return 1.0 / (1.0 + jnp.exp(-x))


def _silu(x):
    return x * _sigmoid(x)


def _gelu_tanh(x):
    c = math.sqrt(2.0 / math.pi)
    return 0.5 * x * (1.0 + jnp.tanh(c * (x + 0.044715 * (x * x * x))))


def _neg_expm1(x):
    return -jnp.tanh(0.5 * x) * (jnp.exp(x) + 1.0)


def _rms(x, w):
    return x * lax.rsqrt(jnp.mean(x * x, axis=-1, keepdims=True) + EPS) * w


def _interleave_rows(x):
    b, t, d = x.shape
    return x.reshape(b, t // CHUNK, V7X_SUBLANES, ROW_TILES, d).swapaxes(2, 3).reshape(b, t, d)


def _deinterleave_rows(x):
    b, t, d = x.shape
    return x.reshape(b, t // CHUNK, ROW_TILES, V7X_SUBLANES, d).swapaxes(2, 3).reshape(b, t, d)


def _full(shape):
    n = len(shape)
    return pl.BlockSpec(shape, lambda *_: (0,) * n)


def _layer(a, l, resident=False):
    nd = a.ndim - 1
    kw = dict(pipeline_mode=pl.Buffered(1)) if resident else {}
    return pl.BlockSpec((None,) + a.shape[1:], lambda *_: (l,) + (0,) * nd, **kw)


def _ada_kernel(c_ref, w_ref, b_ref, o_ref):
    cs = _silu(c_ref[...]).astype(bf16)
    o_ref[0] = jnp.dot(cs, w_ref[0].astype(bf16), preferred_element_type=f32) + b_ref[0]


def _ada(c_all, ada_w, ada_b):
    n = c_all.shape[0]
    tn = 1024
    return pl.pallas_call(
        _ada_kernel,
        out_shape=jax.ShapeDtypeStruct((N_LAYERS, n, 6 * D_MODEL), f32),
        grid=(N_LAYERS, 6 * D_MODEL // tn),
        in_specs=[
            pl.BlockSpec((n, D_MODEL), lambda l, j: (0, 0)),
            pl.BlockSpec((1, D_MODEL, tn), lambda l, j: (l, 0, j)),
            pl.BlockSpec((1, 1, tn), lambda l, j: (l, 0, j)),
        ],
        out_specs=pl.BlockSpec((1, n, tn), lambda l, j: (l, 0, j)),
        compiler_params=pltpu.CompilerParams(
            dimension_semantics=("arbitrary", "arbitrary"), vmem_limit_bytes=VMEM_LIMIT),
        name="ada_mod",
    )(c_all, ada_w, ada_b.reshape(N_LAYERS, 1, 6 * D_MODEL))


def _inproj_kernel(x_ref, mod_ref, nw_ref, w_ref, o_ref):
    x = x_ref[...]
    t, bb, d = x.shape
    hn = _rms(x, nw_ref[...]) * (1.0 + mod_ref[1]) + mod_ref[0]
    hn = hn.reshape(t * bb, d).astype(bf16)
    u = jnp.dot(hn, w_ref[...], preferred_element_type=f32)
    o_ref[...] = u.reshape(t, bb, u.shape[-1])


def _inproj(l, x, mod, nw, w, bb):
    t, b, d = x.shape
    n = w.shape[-1]
    return pl.pallas_call(
        _inproj_kernel,
        out_shape=jax.ShapeDtypeStruct((t, b, n), f32),
        grid=(b // bb,),
        in_specs=[
            pl.BlockSpec((t, bb, d), lambda i: (0, i, 0)),
            pl.BlockSpec((None, 6, bb, d), lambda i: (l, 0, i, 0)),
            _layer(nw, l),
            _layer(w, l, resident=True),
        ],
        out_specs=pl.BlockSpec((t, bb, n), lambda i: (0, i, 0)),
        compiler_params=pltpu.CompilerParams(
            dimension_semantics=("arbitrary",), vmem_limit_bytes=VMEM_LIMIT),
        name="inproj",
    )(x, mod, nw, w)


def _outmlp_kernel(x_ref, y_ref, mod_ref, wout_ref, n2w_ref, up_ref, down_ref, fnw_ref, o_ref,
                   *, final):
    x = x_ref[...]
    t, bb, d = x.shape
    m = t * bb
    ycat = y_ref[...].reshape(m, MIX_WIDTH).astype(bf16)
    mix = jnp.dot(ycat, wout_ref[:, 0:d], preferred_element_type=f32).reshape(t, bb, d)
    x1 = x + mod_ref[2] * mix
    hn = _rms(x1, n2w_ref[...]) * (1.0 + mod_ref[4]) + mod_ref[3]
    hn = hn.reshape(m, d).astype(bf16)
    ff = jnp.zeros((m, d), f32)
    fc = 1024
    for c in range(D_FF // fc):
        h = jnp.dot(hn, up_ref[:, c * fc:(c + 1) * fc], preferred_element_type=f32)
        h = jnp.square(jnp.maximum(h, 0.0)).astype(bf16)
        ff = ff + jnp.dot(h, down_ref[c * fc:(c + 1) * fc, 0:D_MODEL],
                          preferred_element_type=f32)
    x2 = x1 + mod_ref[5] * ff.reshape(t, bb, d)
    if final:
        x2 = _rms(x2, fnw_ref[...])
    o_ref[...] = x2


def _outmlp(l, x, ycat, mod, wout, n2w, up, down, fnw, bb, final):
    t, b, d = x.shape
    return pl.pallas_call(
        functools.partial(_outmlp_kernel, final=final),
        out_shape=jax.ShapeDtypeStruct((t, b, d), f32),
        grid=(b // bb,),
        in_specs=[
            pl.BlockSpec((t, bb, d), lambda i: (0, i, 0)),
            pl.BlockSpec((t, bb, MIX_WIDTH), lambda i: (0, i, 0)),
            pl.BlockSpec((None, 6, bb, d), lambda i: (l, 0, i, 0)),
            _layer(wout, l, resident=True),
            _layer(n2w, l),
            _layer(up, l, resident=True),
            _layer(down, l, resident=True),
            _full((1, d)),
        ],
        out_specs=pl.BlockSpec((t, bb, d), lambda i: (0, i, 0)),
        compiler_params=pltpu.CompilerParams(
            dimension_semantics=("arbitrary",), vmem_limit_bytes=VMEM_LIMIT),
        name="outproj_mlp",
    )(x, ycat, mod, wout, n2w, up, down, fnw)


PIECE = 512


def _stage_in(x, mod_ref, nw_ref, w_ref, u_ref):
    hn = _rms(x, nw_ref[...]) * (1.0 + mod_ref[0, 1:2, :]) + mod_ref[0, 0:1, :]
    hn = hn.astype(bf16)
    yield
    for c0 in range(0, NP_COLS, PIECE):
        c1 = min(c0 + PIECE, NP_COLS)
        u_ref[:, c0:c1] = jnp.dot(hn, w_ref[:, c0:c1], preferred_element_type=f32)
        yield


def _stage_out(x, y_ref, o_ref, rows, mod_ref, wout_ref, n2w_ref, up_ref, down_ref, fnw_ref,
               final):
    halves = [slice(h * PIECE, (h + 1) * PIECE) for h in range(D_MODEL // PIECE)]
    ycat = y_ref[...]
    mix = []
    for hs in halves:
        mix.append(jnp.dot(ycat, wout_ref[:, hs], preferred_element_type=f32))
        yield
    x1 = x + mod_ref[0, 2:3, :] * jnp.concatenate(mix, axis=1)
    hn = (_rms(x1, n2w_ref[...]) * (1.0 + mod_ref[0, 4:5, :]) + mod_ref[0, 3:4, :]).astype(bf16)
    yield
    ff = [jnp.zeros((x.shape[0], PIECE), f32) for _ in halves]
    fc = 2 * PIECE
    for c in range(D_FF // fc):
        h = []
        for k in range(fc // PIECE):
            cols = slice(c * fc + k * PIECE, c * fc + (k + 1) * PIECE)
            hk = jnp.dot(hn, up_ref[:, cols], preferred_element_type=f32)
            h.append(jnp.square(jnp.maximum(hk, 0.0)).astype(bf16))
            yield
        h = jnp.concatenate(h, axis=1)
        for i, hs in enumerate(halves):
            ff[i] = ff[i] + jnp.dot(h, down_ref[c * fc:(c + 1) * fc, hs],
                                    preferred_element_type=f32)
            yield
    x2 = x1 + mod_ref[0, 5:6, :] * jnp.concatenate(ff, axis=1)
    o_ref[0, rows, :] = _rms(x2, fnw_ref[...]) if final else x2
    yield


def _trace_interleaved(main, fillers, n_main, n_fill):
    fillers = list(fillers)
    due = 0.0
    nxt = 0
    main_live = True
    while main_live or fillers:
        if main_live:
            main_live = next(main, _DONE) is not _DONE
            due += n_fill / n_main
        else:
            due = float(len(fillers))
        while fillers and due >= 1.0:
            nxt %= len(fillers)
            if next(fillers[nxt], _DONE) is _DONE:
                fillers.pop(nxt)
            else:
                nxt += 1
                due -= 1.0


_DONE = object()


def _reset_state_if(first, state):
    for ref in state:
        ref[...] = jnp.where(first, 0.0, ref[...])


def _stage_mix(u_ref, y_ref, prm, state, ci, outs):
    (cw_ref, cb_ref, sbias_ref, alog_ref, dskip_ref, snw_ref, mnw_ref,
     lcw_ref, lcb_ref, wax_ref, ba_ref, bx_ref, lam_ref) = prm
    xbuf, lbuf, st, cst, nst, mst, hst = state
    L = CHUNK
    SUB = V7X_SUBLANES
    HIST = (CONV_W - 1) * SUB

    row = lax.broadcasted_iota(jnp.int32, (L, L), 0)
    col = lax.broadcasted_iota(jnp.int32, (L, L), 1)
    tok = lambda i: (i & (SUB - 1)) * ROW_TILES + (i >> 3)
    tri = tok(row) >= tok(col)
    lo = col < SSD_HEAD_DIM

    def bcl(a, j):
        return jnp.broadcast_to(a[:, j:j + 1], (L, L))

    def conv(u0, hist_ref, w_ref, b_ref, c0, c1):
        raw = u_ref[:, u0 + c0:u0 + c1]
        sub = lax.broadcasted_iota(jnp.int32, (SUB, c1 - c0), 0)
        prev = [pltpu.roll(jnp.where(sub == SUB - 1, hist_ref[k * SUB:(k + 1) * SUB, c0:c1],
                                     raw[L - HIST + k * SUB:L - HIST + (k + 1) * SUB, :]), 1, 0)
                for k in range(CONV_W - 1)]
        acc = b_ref[:, c0:c1] + raw * w_ref[CONV_W - 1:CONV_W, c0:c1]
        for d in range(1, CONV_W):
            shifted = jnp.concatenate(prev[CONV_W - 1 - d:] + [raw[0:L - d * SUB, :]], axis=0)
            acc = acc + shifted * w_ref[CONV_W - 1 - d:CONV_W - d, c0:c1]
        hist_ref[:, c0:c1] = raw[L - HIST:L, :]
        return acc

    cblk = 512
    xbc = []
    for c0 in range(0, SSD_CONV_DIM, cblk):
        xbc.append(_silu(conv(P_XBC, xbuf, cw_ref, cb_ref, c0, c0 + cblk)))
        yield
    xbc = jnp.concatenate(xbc, axis=1)
    xr = conv(P_XR, lbuf, lcw_ref, lcb_ref, 0, LRU_WIDTH)
    yield

    sm = u_ref[:,P_SMALL:P_SMALL + V7X_LANES] + sbias_ref[...]
    dt = _softplus(sm)
    a_row = -jnp.exp(alog_ref[...])
    gates = jnp.where(col < SSD_HEADS, dt * (a_row * LOG2E),
                      jnp.where((col >= 20) & (col < 24), _log_sigmoid(sm), 0.0))
    g_hi = gates.astype(bf16)
    g_r = gates - g_hi.astype(f32)
    g_mid = g_r.astype(bf16)
    g_lo = (g_r - g_mid.astype(f32)).astype(bf16)
    cum3 = jnp.dot(jnp.where(tri, 1.0, 0.0).astype(bf16),
                   jnp.concatenate([g_hi, g_mid, g_lo], axis=1), preferred_element_type=f32)
    cum = cum3[:, 0:L] + cum3[:, L:2 * L] + cum3[:, 2 * L:3 * L]
    cum_t = cum.T
    sm_t = sm.T
    yield

    ys = []
    for g in range(SSD_GROUPS):
        bm_t = xbc[:, SSD_WIDTH + g * SSD_STATE:SSD_WIDTH + (g + 1) * SSD_STATE].T.astype(bf16)
        cm = xbc[:, SSD_WIDTH + (SSD_GROUPS + g) * SSD_STATE:
                 SSD_WIDTH + (SSD_GROUPS + g + 1) * SSD_STATE].astype(bf16)
        cb = jnp.dot(cm, bm_t, preferred_element_type=f32)
        inter = jnp.dot(cm, st[g].astype(bf16), preferred_element_type=f32)
        yield
        xends, keeps = [], []
        for jj in range(4):
            jp = g * 4 + jj
            e0, e1 = 2 * jp, 2 * jp + 1
            c0 = bcl(cum, e0)
            c1 = bcl(cum, e1)
            cum_p = jnp.where(lo, c0, c1)
            dt_p = jnp.where(lo, bcl(dt, e0), bcl(dt, e1))
            xs_p = xbc[:, jp * L:(jp + 1) * L]
            xdt = xs_p * dt_p
            dec0 = jnp.exp2(jnp.where(tri, c0 - cum_t[e0:e0 + 1, :], -jnp.inf))
            dec1 = jnp.exp2(jnp.where(tri, c1 - cum_t[e1:e1 + 1, :], -jnp.inf))
            att = jnp.concatenate([(cb * dec0).astype(bf16), (cb * dec1).astype(bf16)], axis=1)
            x2 = jnp.concatenate([jnp.where(lo, xdt, 0.0).astype(bf16),
                                  jnp.where(lo, 0.0, xdt).astype(bf16)], axis=0)
            y_p = (jnp.dot(att, x2, preferred_element_type=f32)
                   + inter[:, jj * L:(jj + 1) * L] * jnp.exp2(cum_p)
                   + xs_p * dskip_ref[:, jp * L:(jp + 1) * L])
            ys.append(y_p)
            c_last = cum_p[L - 1:L, :]
            xends.append((xdt * jnp.exp2(c_last - cum_p)).astype(bf16))
            keeps.append(jnp.exp2(c_last))
            yield
        st[g] = (jnp.concatenate(keeps, axis=1) * st[g]
                 + jnp.dot(bm_t, jnp.concatenate(xends, axis=1), preferred_element_type=f32))
    y = jnp.concatenate(ys, axis=1)
    z = u_ref[:,P_Z:P_Z + SSD_WIDTH]
    y_ref[:,0:SSD_WIDTH] = _rms(y * _silu(z), snw_ref[...]).astype(bf16)
    yield

    scale = ML_HEAD_DIM ** -0.5
    for h in range(ML_HEADS):
        sl = slice(h * ML_HEAD_DIM, (h + 1) * ML_HEAD_DIM)
        q = u_ref[:,P_Q + h * ML_HEAD_DIM:P_Q + (h + 1) * ML_HEAD_DIM]
        k = u_ref[:,P_K + h * ML_HEAD_DIM:P_K + (h + 1) * ML_HEAD_DIM] * scale
        v = u_ref[:,P_V + h * ML_HEAD_DIM:P_V + (h + 1) * ML_HEAD_DIM]
        o = u_ref[:,P_O + h * ML_HEAD_DIM:P_O + (h + 1) * ML_HEAD_DIM]
        k_t = k.T.astype(bf16)
        jf = 20 + h
        ji = 16 + h
        b_col = cum[:, jf:jf + 1]
        i_col = sm[:, ji:ji + 1]
        dm = jnp.where(tri, bcl(cum, jf) - cum_t[jf:jf + 1, :] + sm_t[ji:ji + 1, :], -jnp.inf)
        m_prev = mst[h, :, 0:1]
        g_col = b_col + m_prev
        mt = jnp.maximum(g_col, jnp.max(dm, axis=-1, keepdims=True))
        w = jnp.exp(dm - mt) * jnp.dot(q.astype(bf16), k_t, preferred_element_type=f32)
        inter = jnp.exp(g_col - mt)
        ct_old = cst[h]
        n_old = nst[h]
        num = jnp.dot(jnp.concatenate([w.astype(bf16), (inter * q).astype(bf16)], axis=1),
                      jnp.concatenate([v.astype(bf16), ct_old.astype(bf16)], axis=0),
                      preferred_element_type=f32)
        den = (jnp.sum(w, axis=-1, keepdims=True)
               + inter * jnp.sum(q * n_old, axis=-1, keepdims=True))
        hout = num / jnp.maximum(jnp.abs(den), jnp.exp(-mt))
        m_new = mt[L - 1:L, :]
        b_last = b_col[L - 1:L, :]
        w_end = jnp.exp(b_last - b_col + i_col - m_new)
        dc = jnp.exp(b_last + m_prev - m_new)
        cst[h] = dc * ct_old + jnp.dot(k_t, (v * w_end).astype(bf16),
                                       preferred_element_type=f32)
        nst[h] = dc * n_old + jnp.sum(w_end * k, axis=0, keepdims=True)
        mst[h] = jnp.broadcast_to(m_new, (1, V7X_LANES))
        yh = _rms(hout, mnw_ref[:, sl]) * _sigmoid(o)
        y_ref[:,SSD_WIDTH + h * ML_HEAD_DIM:SSD_WIDTH + (h + 1) * ML_HEAD_DIM] = yh.astype(bf16)
        yield

    rs, is_ = [], []
    for kb_ in range(LRU_BLOCKS):
        xk = xr[:, kb_ * LRU_BLOCK_DIM:(kb_ + 1) * LRU_BLOCK_DIM].astype(bf16)
        ri = jnp.dot(xk, wax_ref[kb_], preferred_element_type=f32)
        rs.append(ri[:, 0:LRU_BLOCK_DIM])
        is_.append(ri[:, LRU_BLOCK_DIM:2 * LRU_BLOCK_DIM])
    r = _sigmoid(jnp.concatenate(rs, axis=1) + ba_ref[...])
    i_g = _sigmoid(jnp.concatenate(is_, axis=1) + bx_ref[...])
    yield
    log_a = -LRU_C * r * _softplus(-lam_ref[...])
    mult = jnp.sqrt(_neg_expm1(2.0 * log_a))
    rowl = lax.broadcasted_iota(jnp.int32, (L, LRU_WIDTH), 0)
    first_row = jnp.where(ci == 0, 0, -1)
    mult = jnp.where(rowl == first_row, 1.0, mult)
    a_s = jnp.exp(log_a)
    u_s = mult * i_g * xr
    yield
    tile = lambda a, v: a[v * SUB:(v + 1) * SUB, :]
    h_loc = [tile(u_s, 0)]
    a_cum = [tile(a_s, 0)]
    for v in range(1, ROW_TILES):
        h_loc.append(tile(a_s, v) * h_loc[-1] + tile(u_s, v))
        a_cum.append(tile(a_s, v) * a_cum[-1])
    a_e, h_e = a_cum[-1], h_loc[-1]
    subl = lax.broadcasted_iota(jnp.int32, (SUB, LRU_WIDTH), 0)
    k_ = 1
    while k_ < SUB:
        keep = subl >= k_
        a_sh = jnp.where(keep, pltpu.roll(a_e, k_, 0), 1.0)
        h_sh = jnp.where(keep, pltpu.roll(h_e, k_, 0), 0.0)
        h_e = a_e * h_sh + h_e
        a_e = a_e * a_sh
        k_ *= 2
    h0 = hst[...]
    run_end = h_e + a_e * h0
    run_in = jnp.where(subl == 0, h0, pltpu.roll(run_end, 1, 0))
    hr = jnp.concatenate([h_loc[v] + a_cum[v] * run_in for v in range(ROW_TILES)], axis=0)
    hst[...] = run_end[SUB - 1:SUB, :]
    gr = u_ref[:,P_GR:P_GR + LRU_WIDTH]
    y_ref[:,SSD_WIDTH + ML_WIDTH:MIX_WIDTH] = (hr * _gelu_tanh(gr)).astype(bf16)
    yield

    if outs is not None:
        ssm_ref, sconv_ref, mc_ref, mn_ref, mm_ref, lh_ref, lconv_ref = outs
        for g in range(SSD_GROUPS):
            for jj in range(4):
                ssm_ref[0, g * 4 + jj] = st[g, :, jj * L:(jj + 1) * L].T
        for k in range(CONV_W - 1):
            sconv_ref[0, k:k + 1, :] = xbuf[k * SUB + SUB - 1:(k + 1) * SUB, :]
            lconv_ref[0, k:k + 1, :] = lbuf[k * SUB + SUB - 1:(k + 1) * SUB, :]
        for h in range(ML_HEADS):
            mc_ref[0, h] = cst[h].T
            mn_ref[0, h:h + 1, :] = nst[h]
            mm_ref[0, h:h + 1, :] = mst[h]
        lh_ref[0] = hst[...]


N_MIX_PARAMS = 13
N_STATE_OUTS = 7
N_MIX_PHASES = 24
N_FILL_PIECES = 33


def _player_kernel(*refs, final, n_chunks, nc):
    xa_ref, moda_ref, xc_ref, modc_ref, n1w_ref, win_ref = refs[:6]
    prm = refs[6:6 + N_MIX_PARAMS]
    wout_ref, n2w_ref, up_ref, down_ref, fnw_ref = refs[6 + N_MIX_PARAMS:11 + N_MIX_PARAMS]
    n_in = 11 + N_MIX_PARAMS + N_STATE_OUTS
    o_ref = refs[n_in]
    outs = refs[n_in + 1:n_in + 1 + N_STATE_OUTS]
    u_bufs = refs[-11:-9]
    y_bufs = refs[-9:-7]
    state = refs[-7:]
    k = pl.program_id(0)

    @pl.when(k == 0)
    def _():
        u_bufs[1][...] = jnp.zeros(u_bufs[1].shape, f32)
        y_bufs[1][...] = jnp.zeros(y_bufs[1].shape, bf16)
        for ref in state:
            ref[...] = jnp.zeros(ref.shape, f32)

    for sub in range(2):
        rows = slice(sub * CHUNK, (sub + 1) * CHUNK)
        stage_in = _stage_in(xa_ref[0, rows, :], moda_ref, n1w_ref, win_ref, u_bufs[sub])
        jb = jnp.clip(2 * k + sub - 1, 0, n_chunks - 1)
        ci = jb % nc
        if sub == 1:
            _reset_state_if(ci == 0, state)
        stage_mix = _stage_mix(u_bufs[1 - sub], y_bufs[sub], prm, state, ci,
                               outs if sub == 0 else None)
        stage_out = _stage_out(xc_ref[0, rows, :], y_bufs[1 - sub], o_ref, rows, modc_ref,
                               wout_ref, n2w_ref, up_ref, down_ref, fnw_ref, final)
        _trace_interleaved(stage_mix, [stage_in, stage_out], N_MIX_PHASES, N_FILL_PIECES)


def _player(l, x, mod, mod_row0, w, fnw, final, prev):
    b, t, d = x.shape
    nc = t // CHUNK
    n_chunks = b * nc
    n_pairs = n_chunks // 2
    ppr = nc // 2
    pair_a = lambda k: jnp.minimum(k, n_pairs - 1)
    pair_c = lambda k: jnp.clip(k - 1, 0, n_pairs - 1)
    seq_b = lambda k: jnp.clip(2 * k - 1, 0, n_chunks - 1) // nc
    xspec = lambda pair: pl.BlockSpec((1, 2 * CHUNK, d), lambda k: (pair(k) // ppr, pair(k) % ppr, 0))
    mspec = lambda pair: pl.BlockSpec((None, 1, 6, d),
                                      lambda k: (l, mod_row0 + pair(k) // ppr, 0, 0))
    st_spec = lambda s: pl.BlockSpec((None, 1) + s[2:],
                                     lambda k: (l, seq_b(k)) + (0,) * (len(s) - 2))
    outs = [jax.ShapeDtypeStruct((b, t, d), f32)]
    outs += [jax.ShapeDtypeStruct(a.shape, f32) for a in prev]
    mix_params = [w["ssd_conv_w"], w["ssd_conv_b"], w["p_small_bias"], w["p_alog"], w["dskip"],
                  w["ssd_norm_w"], w["ml_norm_w"], w["lru_conv_w"], w["lru_conv_b"], w["lru_wax"],
                  w["lru_ba"], w["lru_bx"], w["lru_lambda"]]
    assert len(mix_params) == N_MIX_PARAMS and len(prev) == N_STATE_OUTS
    n_in = 11 + N_MIX_PARAMS
    return pl.pallas_call(
        functools.partial(_player_kernel, final=final, n_chunks=n_chunks, nc=nc),
        out_shape=outs,
        grid=(n_pairs + 1,),
        in_specs=[xspec(pair_a), mspec(pair_a), xspec(pair_c), mspec(pair_c),
                  _layer(w["norm1_w"], l), _layer(w["w_in_p"], l, resident=True)]
        + [_layer(a, l) for a in mix_params]
        + [_layer(w["w_out"], l, resident=True), _layer(w["norm2_w"], l),
           _layer(w["mlp_up"], l, resident=True), _layer(w["mlp_down"], l, resident=True),
           _full((1, d))]
        + [pl.BlockSpec(memory_space=pl.ANY)] * len(prev),
        out_specs=[xspec(pair_c)] + [st_spec(a.shape) for a in prev],
        input_output_aliases={n_in + i: 1 + i for i in range(len(prev))},
        scratch_shapes=[
            pltpu.VMEM((CHUNK, NP_COLS), f32), pltpu.VMEM((CHUNK, NP_COLS), f32),
            pltpu.VMEM((CHUNK, MIX_WIDTH), bf16), pltpu.VMEM((CHUNK, MIX_WIDTH), bf16),
            pltpu.VMEM(((CONV_W - 1) * V7X_SUBLANES, SSD_CONV_DIM), f32),
            pltpu.VMEM(((CONV_W - 1) * V7X_SUBLANES, LRU_WIDTH), f32),
            pltpu.VMEM((SSD_GROUPS, SSD_STATE, 512), f32),
            pltpu.VMEM((ML_HEADS, ML_HEAD_DIM, ML_HEAD_DIM), f32),
            pltpu.VMEM((ML_HEADS, 1, ML_HEAD_DIM), f32),
            pltpu.VMEM((ML_HEADS, 1, V7X_LANES), f32),
            pltpu.VMEM((1, LRU_WIDTH), f32),
        ],
        compiler_params=pltpu.CompilerParams(
            dimension_semantics=("arbitrary",), vmem_limit_bytes=PLAYER_VMEM_LIMIT),
        name="prompt_layer",
    )(x, mod, x, mod, w["norm1_w"], w["w_in_p"], *mix_params,
      w["w_out"], w["norm2_w"], w["mlp_up"], w["mlp_down"], fnw, *prev)


def _smix_kernel(u_ref, s0_ref, sconv0_ref, c0_ref, n0_ref, m0_ref, lh0_ref, lconv0_ref,
                 cw_ref, cb_ref, dtb_ref, alog_ref, dskip_ref, snw_ref, ib_ref, fb_ref, mnw_ref,
                 lcw_ref, lcb_ref, wax_ref, ba_ref, bx_ref, lam_ref,
                 y_ref, s_ref, sconv_ref, c_ref, n_ref, m_ref, lh_ref, lconv_ref):
    T = DEC_SEQ
    bb = u_ref.shape[1]
    H = ML_HEAD_DIM
    toks = range(T)

    def useg(t, c0, n):
        return u_ref[t, :, c0:c0 + n]

    def to_seq(rows):
        return jnp.swapaxes(jnp.stack(rows, axis=0), 0, 1)

    def to_tok(a):
        a = jnp.swapaxes(a, 0, 1)
        return [a[t] for t in toks]

    def conv(c0, n, hist_ref, w_ref, b_ref, new_ref):
        xpad = [hist_ref[k] for k in range(CONV_W - 1)] + [useg(t, c0, n) for t in toks]
        for k in range(CONV_W - 1):
            new_ref[k] = xpad[T + k]
        out = []
        for t in toks:
            acc = b_ref[...] + xpad[t] * w_ref[0:1, :]
            for j in range(1, CONV_W):
                acc = acc + xpad[t + j] * w_ref[j:j + 1, :]
            out.append(acc)
        return out

    def split3(d):
        hi = d.astype(bf16).astype(f32)
        r1 = d - hi
        mid = r1.astype(bf16).astype(f32)
        return [hi, mid, (r1 - mid).astype(bf16).astype(f32)]

    def update_lhs(rows, decay):
        pad = [jnp.zeros_like(decay)] * (T - 3)
        return to_seq(rows + split3(decay) + pad).astype(bf16)

    def ones_rhs(kmat):
        zeros = jnp.zeros_like(kmat)
        return jnp.concatenate(
            [jnp.concatenate([kmat, zeros], axis=2),
             jnp.concatenate([zeros, jnp.ones_like(kmat)], axis=2)], axis=1).astype(bf16)

    xbc = [_silu(a) for a in conv(P_XBC, SSD_CONV_DIM, sconv0_ref, cw_ref, cb_ref, sconv_ref)]
    xs = [a[:, 0:SSD_WIDTH] for a in xbc]
    seg = lambda a, j: a[:, SSD_WIDTH + j * SSD_STATE:SSD_WIDTH + (j + 1) * SSD_STATE]
    bm = [[seg(a, g) for a in xbc] for g in range(SSD_GROUPS)]
    cm = [[seg(a, SSD_GROUPS + g) for a in xbc] for g in range(SSD_GROUPS)]
    a_row = -jnp.exp(alog_ref[...]) * LOG2E
    lane = lax.broadcasted_iota(jnp.int32, (bb, SSD_WIDTH), 1)
    cum, xdt = [], []
    for t in toks:
        dt = _softplus(useg(t, S_DT, SSD_WIDTH) + dtb_ref[...])
        cum.append(dt * a_row + (cum[-1] if cum else 0.0))
        xdt.append(xs[t] * dt)
    y = [xs[t] * dskip_ref[...] for t in toks]
    for t in toks:
        for s in range(t + 1):
            cb0 = jnp.sum(cm[0][t] * bm[0][s], axis=-1, keepdims=True)
            cb1 = jnp.sum(cm[1][t] * bm[1][s], axis=-1, keepdims=True)
            term = jnp.where(lane < 512, cb0, cb1) * xdt[s]
            y[t] = y[t] + (term if s == t else term * jnp.exp2(cum[t] - cum[s]))
    inter = []
    for g in range(SSD_GROUPS):
        inter.append(to_tok(jnp.einsum(
            'btn,bpn->btp', to_seq(cm[g]).astype(bf16),
            s0_ref[:, g * 512:(g + 1) * 512, :].astype(bf16), preferred_element_type=f32)))
    for t in toks:
        yt = y[t] + jnp.concatenate([inter[0][t], inter[1][t]], axis=-1) * jnp.exp2(cum[t])
        y_ref[t, :, 0:SSD_WIDTH] = _rms(yt * _silu(useg(t, P_Z, SSD_WIDTH)), snw_ref[...])
    xend = [xdt[s] if s == T - 1 else xdt[s] * jnp.exp2(cum[T - 1] - cum[s]) for s in toks]
    lhs = update_lhs(xend, jnp.exp2(cum[T - 1]))
    for g in range(SSD_GROUPS):
        zz = jnp.einsum('bkp,bkn->bpn', lhs[:, :, g * 512:(g + 1) * 512],
                        ones_rhs(to_seq(bm[g])), preferred_element_type=f32)
        s_ref[:, g * 512:(g + 1) * 512, :] = (
            zz[:, :, SSD_STATE:] * s0_ref[:, g * 512:(g + 1) * 512, :] + zz[:, :, :SSD_STATE])

    def headsum(a):
        return jnp.concatenate(
            [jnp.broadcast_to(jnp.sum(a[:, h * H:(h + 1) * H], axis=-1, keepdims=True), (bb, H))
             for h in range(ML_HEADS)], axis=-1)

    q = [useg(t, P_Q, ML_WIDTH) for t in toks]
    k = [useg(t, P_K, ML_WIDTH) * (H ** -0.5) for t in toks]
    v = [useg(t, P_V, ML_WIDTH) for t in toks]
    ic = [useg(t, S_I, ML_WIDTH) + ib_ref[...] for t in toks]
    bc = []
    for t in toks:
        fc = _log_sigmoid(useg(t, S_F, ML_WIDTH) + fb_ref[...])
        bc.append(fc + (bc[-1] if bc else 0.0))
    m0 = m0_ref[...]
    n0 = n0_ref[...]
    q_seq = to_seq(q).astype(bf16)
    qc = to_tok(jnp.concatenate(
        [jnp.einsum('btk,bvk->btv', q_seq[:, :, h * H:(h + 1) * H],
                    c0_ref[:, h * H:(h + 1) * H, :].astype(bf16), preferred_element_type=f32)
         for h in range(ML_HEADS)], axis=-1))
    mts = []
    for t in toks:
        dm = [bc[t] - bc[s] + ic[s] for s in range(t + 1)]
        g_t = bc[t] + m0
        mt = g_t
        for d in dm:
            mt = jnp.maximum(mt, d)
        mts.append(mt)
        inter_m = jnp.exp(g_t - mt)
        num = inter_m * qc[t]
        den = inter_m * headsum(q[t] * n0)
        for s in range(t + 1):
            w = jnp.exp(dm[s] - mt) * headsum(q[t] * k[s])
            num = num + w * v[s]
            den = den + w
        hout = num / jnp.maximum(jnp.abs(den), jnp.exp(-mt))
        y_ml = (hout * lax.rsqrt(headsum(hout * hout) * (1.0 / H) + EPS) * mnw_ref[...]
                * _sigmoid(useg(t, P_O, ML_WIDTH)))
        y_ref[t, :, SSD_WIDTH:SSD_WIDTH + ML_WIDTH] = y_ml
    m_new = mts[T - 1]
    m_ref[...] = m_new
    w_end = [jnp.exp(bc[T - 1] - bc[s] + ic[s] - m_new) for s in toks]
    dc = jnp.exp(bc[T - 1] + m0 - m_new)
    n_new = dc * n0
    for s in toks:
        n_new = n_new + w_end[s] * k[s]
    n_ref[...] = n_new
    lhs_m = update_lhs([v[s] * w_end[s] for s in toks], dc)
    k_seq = to_seq(k)
    for h in range(ML_HEADS):
        sl = slice(h * H, (h + 1) * H)
        zz = jnp.einsum('bkv,bkn->bvn', lhs_m[:, :, sl], ones_rhs(k_seq[:, :, sl]),
                        preferred_element_type=f32)
        c_ref[:, sl, :] = zz[:, :, H:] * c0_ref[:, sl, :] + zz[:, :, :H]

    xr = jnp.concatenate(conv(P_XR, LRU_WIDTH, lconv0_ref, lcw_ref, lcb_ref, lconv_ref), axis=0)
    rs, is_ = [], []
    for kk in range(LRU_BLOCKS):
        xk = xr[:, kk * LRU_BLOCK_DIM:(kk + 1) * LRU_BLOCK_DIM].astype(bf16)
        ri = jnp.dot(xk, wax_ref[kk], preferred_element_type=f32)
        rs.append(ri[:, 0:LRU_BLOCK_DIM])
        is_.append(ri[:, LRU_BLOCK_DIM:2 * LRU_BLOCK_DIM])
    r = _sigmoid(jnp.concatenate(rs, axis=1) + ba_ref[...])
    i_g = _sigmoid(jnp.concatenate(is_, axis=1) + bx_ref[...])
    log_a = -LRU_C * r * _softplus(-lam_ref[...])
    a_s = jnp.exp(log_a)
    u_s = jnp.sqrt(_neg_expm1(2.0 * log_a)) * i_g * xr
    cur = lh0_ref[...]
    for t in toks:
        rows = slice(t * bb, (t + 1) * bb)
        cur = a_s[rows] * cur + u_s[rows]
        y_ref[t, :, SSD_WIDTH + ML_WIDTH:MIX_WIDTH] = cur * _gelu_tanh(useg(t, P_GR, LRU_WIDTH))
    lh_ref[...] = cur


def _smix(l, u, states, prev, w, bb):
    t, b, _ = u.shape

    def sblk(a, ax):
        shape = tuple(None if d == 0 else bb if d == ax else n for d, n in enumerate(a.shape))
        return pl.BlockSpec(shape, lambda i: tuple(l if d == 0 else i if d == ax else 0
                                                   for d in range(a.ndim)))

    ublk = lambda n: pl.BlockSpec((t, bb, n), lambda i: (0, i, 0))
    outs = [jax.ShapeDtypeStruct((t, b, MIX_WIDTH), f32)]
    outs += [jax.ShapeDtypeStruct(a.shape, f32) for a, _ in states]
    weights = [w["ssd_conv_w"], w["ssd_conv_b"], w["s_dt_bias"], w["s_alog"], w["dskip"],
               w["ssd_norm_w"], w["s_i_bias"], w["s_f_bias"], w["ml_norm_w"],
               w["lru_conv_w"], w["lru_conv_b"], w["lru_wax"], w["lru_ba"], w["lru_bx"],
               w["lru_lambda"]]
    n_in = 1 + len(states) + len(weights)
    prev = list(prev)

    def body(*refs):
        _smix_kernel(*refs[:n_in], *refs[n_in + len(prev):])

    return pl.pallas_call(
        body,
        out_shape=outs,
        grid=(b // bb,),
        in_specs=[ublk(u.shape[-1])] + [sblk(a, ax) for a, ax in states]
        + [_layer(a, l) for a in weights]
        + [pl.BlockSpec(memory_space=pl.ANY)] * len(prev),
        out_specs=[ublk(MIX_WIDTH)] + [sblk(a, ax) for a, ax in states],
        input_output_aliases={n_in + k: 1 + k for k in range(len(prev))},
        compiler_params=pltpu.CompilerParams(
            dimension_semantics=("arbitrary",), vmem_limit_bytes=VMEM_LIMIT),
        name="sample_mix",
    )(u, *[a for a, _ in states], *weights, *prev)


def _prep_params(p):
    w_in = p["w_in"].astype(bf16)
    main = jnp.concatenate([w_in[..., 0:_O_DT], w_in[..., _O_Q:_O_O_END], w_in[..., _O_XR:_O_END]],
                           axis=-1)
    dt_c = w_in[..., _O_DT:_O_Q]
    if_c = w_in[..., _O_I:_O_XR]
    pad = jnp.zeros((N_LAYERS, D_MODEL, V7X_LANES - SSD_HEADS - 2 * ML_HEADS), bf16)
    w_p = jnp.concatenate([main, dt_c, if_c, pad], axis=-1)
    w_s = jnp.concatenate([main, jnp.repeat(dt_c, SSD_HEAD_DIM, axis=-1),
                           jnp.repeat(if_c, ML_HEAD_DIM, axis=-1)], axis=-1)
    r = lambda a: a.reshape(N_LAYERS, 1, -1)
    rep = lambda a, n: r(jnp.repeat(a, n, axis=-1))
    lane_pad = lambda a: jnp.pad(a.astype(bf16), ((0, 0), (0, 0), (0, V7X_LANES)))
    small = jnp.concatenate([p["ssd_dt_bias"], p["ml_i_bias"], p["ml_f_bias"]], axis=-1)
    return dict(
        w_in_p=w_p, w_in_s=w_s,
        norm1_w=r(p["norm1_w"]), norm2_w=r(p["norm2_w"]),
        ssd_conv_w=p["ssd_conv_w"], ssd_conv_b=r(p["ssd_conv_b"]),
        p_small_bias=r(jnp.pad(small, ((0, 0), (0, V7X_LANES - small.shape[-1])))),
        p_alog=r(jnp.pad(p["ssd_a_log"], ((0, 0), (0, V7X_LANES - SSD_HEADS)))),
        s_dt_bias=rep(p["ssd_dt_bias"], SSD_HEAD_DIM), s_alog=rep(p["ssd_a_log"], SSD_HEAD_DIM),
        s_i_bias=rep(p["ml_i_bias"], ML_HEAD_DIM), s_f_bias=rep(p["ml_f_bias"], ML_HEAD_DIM),
        dskip=rep(p["ssd_d"], SSD_HEAD_DIM),
        ssd_norm_w=r(p["ssd_norm_w"]), ml_norm_w=r(p["ml_norm_w"]),
        lru_conv_w=p["lru_conv_w"], lru_conv_b=r(p["lru_conv_b"]),
        lru_wax=jnp.concatenate([p["lru_wa"], p["lru_wx"]], axis=-1).astype(bf16),
        lru_ba=r(p["lru_ba"]), lru_bx=r(p["lru_bx"]), lru_lambda=r(p["lru_lambda"]),
        w_out=lane_pad(p["w_out"]), mlp_up=lane_pad(p["mlp_up"]), mlp_down=lane_pad(p["mlp_down"]),
    )


def kernel(x_prompt, x_sample, c_prompt, c_sample, state_ssm, state_ssd_conv, state_mlstm_c, state_mlstm_n, state_mlstm_m, state_lru_h, state_lru_conv, ada_w, ada_b, norm1_w, norm2_w, w_in, ssd_conv_w, ssd_conv_b, ssd_dt_bias, ssd_a_log, ssd_d, ssd_norm_w, ml_i_bias, ml_f_bias, ml_norm_w, lru_conv_w, lru_conv_b, lru_wa, lru_ba, lru_wx, lru_bx, lru_lambda, w_out, mlp_up, mlp_down, final_norm_w):
    p = dict(norm1_w=norm1_w, norm2_w=norm2_w, w_in=w_in, ssd_conv_w=ssd_conv_w,
             ssd_conv_b=ssd_conv_b, ssd_dt_bias=ssd_dt_bias, ssd_a_log=ssd_a_log, ssd_d=ssd_d,
             ssd_norm_w=ssd_norm_w, ml_i_bias=ml_i_bias, ml_f_bias=ml_f_bias,
             ml_norm_w=ml_norm_w, lru_conv_w=lru_conv_w, lru_conv_b=lru_conv_b, lru_wa=lru_wa,
             lru_ba=lru_ba, lru_wx=lru_wx, lru_bx=lru_bx, lru_lambda=lru_lambda, w_out=w_out,
             mlp_up=mlp_up, mlp_down=mlp_down)
    w = _prep_params(p)
    fnw = final_norm_w.reshape(1, D_MODEL)
    bp = x_prompt.shape[0]
    bs_ = x_sample.shape[0]

    mod = _ada(jnp.concatenate([c_sample, c_prompt], axis=0), ada_w, ada_b)
    mod = mod.reshape(N_LAYERS, bs_ + bp, 6, D_MODEL)

    xp = _interleave_rows(x_prompt)
    p_out = [jnp.zeros((N_LAYERS, bp) + s, f32) for s in (
        (SSD_HEADS // 2, 2 * SSD_HEAD_DIM, SSD_STATE),
        (CONV_W - 1, SSD_CONV_DIM), (ML_HEADS, ML_HEAD_DIM, ML_HEAD_DIM),
        (ML_HEADS, ML_HEAD_DIM), (ML_HEADS, V7X_LANES),
        (1, LRU_WIDTH), (CONV_W - 1, LRU_WIDTH))]
    for l in range(N_LAYERS):
        xp, *p_out = _player(l, xp, mod, bs_, w, fnw, l == N_LAYERS - 1, p_out)
    xp = _deinterleave_rows(xp)
    ssm, sconv, mc, mn, mm, lh, lconv = p_out
    p_states = (ssm.reshape(N_LAYERS, bp, SSD_HEADS, SSD_HEAD_DIM, SSD_STATE), sconv, mc, mn,
                mm[..., 0], lh.reshape(N_LAYERS, bp, LRU_WIDTH), lconv)

    tok_major = lambda a: jnp.swapaxes(a, -3, -2)
    xs = tok_major(x_sample)
    mod_s = tok_major(mod[:, :bs_])
    st_in = (
        (state_ssm.reshape(N_LAYERS, bs_, SSD_HEADS * SSD_HEAD_DIM, SSD_STATE), 1),
        (tok_major(state_ssd_conv), 2),
        (state_mlstm_c.reshape(N_LAYERS, bs_, ML_HEADS * ML_HEAD_DIM, ML_HEAD_DIM), 1),
        (state_mlstm_n.reshape(N_LAYERS, bs_, ML_WIDTH), 1),
        (jnp.repeat(state_mlstm_m, ML_HEAD_DIM, axis=-1), 1),
        (state_lru_h, 1),
        (tok_major(state_lru_conv), 2),
    )
    st_out = [jnp.zeros(a.shape, f32) for a, _ in st_in]
    for l in range(N_LAYERS):
        u = _inproj(l, xs, mod_s, w["norm1_w"], w["w_in_s"], 32)
        ycat, *st_out = _smix(l, u, st_in, st_out, w, 8)
        xs = _outmlp(l, xs, ycat, mod_s, w["w_out"], w["norm2_w"], w["mlp_up"], w["mlp_down"],
                     fnw, 64, l == N_LAYERS - 1)
    xs = tok_major(xs)
    ssm, sconv, mc, mn, mm, lh, lconv = st_out
    s_states = (ssm.reshape(N_LAYERS, bs_, SSD_HEADS, SSD_HEAD_DIM, SSD_STATE), tok_major(sconv),
                mc.reshape(N_LAYERS, bs_, ML_HEADS, ML_HEAD_DIM, ML_HEAD_DIM),
                mn.reshape(N_LAYERS, bs_, ML_HEADS, ML_HEAD_DIM),
                mm.reshape(N_LAYERS, bs_, ML_HEADS, ML_HEAD_DIM)[..., 0],
                lh, tok_major(lconv))

    return (xp, xs) + p_states + s_states
```

```python
import functools
import math

import jax
import jax.numpy as jnp
from jax import lax
from jax.experimental import pallas as pl
from jax.experimental.pallas import tpu as pltpu

f32 = jnp.float32
bf16 = jnp.bfloat16

D_MODEL = 1024
N_LAYERS = 2
MIX_WIDTH = 2 * D_MODEL
SSD_WIDTH = 1024
SSD_HEAD_DIM = 64
SSD_HEADS = 16
SSD_GROUPS = 2
SSD_STATE = 128
SSD_CONV_DIM = SSD_WIDTH + 2 * SSD_GROUPS * SSD_STATE
ML_WIDTH = 512
ML_HEADS = 4
ML_HEAD_DIM = 128
LRU_WIDTH = 512
LRU_BLOCKS = 4
LRU_BLOCK_DIM = 128
LRU_C = 8.0
CONV_W = 4
CHUNK = 128
D_FF = 4 * D_MODEL
EPS = 1e-6
PAST_LEN = 16384
DEC_SEQ = 8

_O_Z, _O_XBC, _O_DT, _O_Q, _O_O_END, _O_I, _O_F, _O_XR, _O_END = (
    0, 1024, 2560, 2576, 4624, 4624, 4628, 4632, 5656)

P_Z = 0
P_XBC = 1024
P_Q = 2560
P_K = 3072
P_V = 3584
P_O = 4096
P_XR = 4608
P_GR = 5120
P_MAIN = 5632
P_SMALL = 5632
NP_COLS = 5760
S_DT = 5632
S_I = 6656
S_F = 7168
NS_COLS = 7680

V7X_LANES = 128
V7X_SUBLANES = 8
VMEM_LIMIT = 56 * 1024 * 1024
PLAYER_VMEM_LIMIT = 60 * 1024 * 1024
ROW_TILES = CHUNK // V7X_SUBLANES
LOG2E = 1.4426950408889634

_NT = (((1,), (1,)), ((), ()))
_TN = (((0,), (0,)), ((), ()))


def _softplus(x):
    return jnp.maximum(x, 0.0) + jnp.log1p(jnp.exp(-jnp.abs(x)))


def _log_sigmoid(x):
    return -_softplus(-x)


def _sigmoid(x):
    return 1.0 / (1.0 + jnp.exp(-x))


def _silu(x):
    return x * _sigmoid(x)


def _gelu_tanh(x):
    c = math.sqrt(2.0 / math.pi)
    return 0.5 * x * (1.0 + jnp.tanh(c * (x + 0.044715 * (x * x * x))))


def _neg_expm1(x):
    return -jnp.tanh(0.5 * x) * (jnp.exp(x) + 1.0)


def _rms(x, w):
    return x * lax.rsqrt(jnp.mean(x * x, axis=-1, keepdims=True) + EPS) * w


def _interleave_rows(x):
    b, t, d = x.shape
    return x.reshape(b, t // CHUNK, V7X_SUBLANES, ROW_TILES, d).swapaxes(2, 3).reshape(b, t, d)


def _deinterleave_rows(x):
    b, t, d = x.shape
    return x.reshape(b, t // CHUNK, ROW_TILES, V7X_SUBLANES, d).swapaxes(2, 3).reshape(b, t, d)


def _full(shape):
    n = len(shape)
    return pl.BlockSpec(shape, lambda *_: (0,) * n)


def _layer(a, l, resident=False):
    nd = a.ndim - 1
    kw = dict(pipeline_mode=pl.Buffered(1)) if resident else {}
    return pl.BlockSpec((None,) + a.shape[1:], lambda *_: (l,) + (0,) * nd, **kw)


def _ada_kernel(c_ref, w_ref, b_ref, o_ref):
    cs = _silu(c_ref[...]).astype(bf16)
    o_ref[0] = jnp.dot(cs, w_ref[0].astype(bf16), preferred_element_type=f32) + b_ref[0]


def _ada(c_all, ada_w, ada_b):
    n = c_all.shape[0]
    tn = 1024
    return pl.pallas_call(
        _ada_kernel,
        out_shape=jax.ShapeDtypeStruct((N_LAYERS, n, 6 * D_MODEL), f32),
        grid=(N_LAYERS, 6 * D_MODEL // tn),
        in_specs=[
            pl.BlockSpec((n, D_MODEL), lambda l, j: (0, 0)),
            pl.BlockSpec((1, D_MODEL, tn), lambda l, j: (l, 0, j)),
            pl.BlockSpec((1, 1, tn), lambda l, j: (l, 0, j)),
        ],
        out_specs=pl.BlockSpec((1, n, tn), lambda l, j: (l, 0, j)),
        compiler_params=pltpu.CompilerParams(
            dimension_semantics=("arbitrary", "arbitrary"), vmem_limit_bytes=VMEM_LIMIT),
        name="ada_mod",
    )(c_all, ada_w, ada_b.reshape(N_LAYERS, 1, 6 * D_MODEL))


def _inproj_kernel(x_ref, mod_ref, nw_ref, wm_ref, we_ref, o_ref):
    x = x_ref[...]
    t, bb, d = x.shape
    hn = _rms(x, nw_ref[...]) * (1.0 + mod_ref[1]) + mod_ref[0]
    hn = hn.reshape(t * bb, d).astype(bf16)
    o_ref[:, :, 0:P_MAIN] = jnp.dot(hn, wm_ref[...], preferred_element_type=f32).reshape(
        t, bb, P_MAIN)
    o_ref[:, :, P_MAIN:NS_COLS] = jnp.dot(hn, we_ref[...], preferred_element_type=f32).reshape(
        t, bb, NS_COLS - P_MAIN)


def _inproj(l, x, mod, nw, w_packed, w_exp, bb):
    t, b, d = x.shape
    return pl.pallas_call(
        _inproj_kernel,
        out_shape=jax.ShapeDtypeStruct((t, b, NS_COLS), f32),
        grid=(b // bb,),
        in_specs=[
            pl.BlockSpec((t, bb, d), lambda i: (0, i, 0)),
            pl.BlockSpec((None, 6, bb, d), lambda i: (l, 0, i, 0)),
            _layer(nw, l),
            pl.BlockSpec((None, d, P_MAIN), lambda i: (l, 0, 0), pipeline_mode=pl.Buffered(1)),
            _layer(w_exp, l, resident=True),
        ],
        out_specs=pl.BlockSpec((t, bb, NS_COLS), lambda i: (0, i, 0)),
        compiler_params=pltpu.CompilerParams(
            dimension_semantics=("arbitrary",), vmem_limit_bytes=VMEM_LIMIT),
        name="inproj",
    )(x, mod, nw, w_packed, w_exp)


PACK_ROWS = 128


def _pack_w_in_kernel(w_ref, wp_ref, we_ref):
    c = lambda a: a.astype(bf16)
    wp_ref[:, 0:P_Q] = c(w_ref[:, 0:_O_DT])
    wp_ref[:, P_Q:P_XR] = c(w_ref[:, _O_Q:_O_O_END])
    wp_ref[:, P_XR:P_MAIN] = c(w_ref[:, _O_XR:_O_END])
    dt = w_ref[:, _O_DT:_O_Q]
    gates = w_ref[:, _O_I:_O_XR]
    n_pad = V7X_LANES - SSD_HEADS - 2 * ML_HEADS
    wp_ref[:, P_SMALL:NP_COLS] = c(jnp.concatenate(
        [dt, gates, jnp.zeros((PACK_ROWS, n_pad), f32)], axis=1))
    rep = lambda a, j, n: jnp.broadcast_to(a[:, j:j + 1], (PACK_ROWS, n))
    we_ref[...] = c(jnp.concatenate(
        [rep(dt, h, SSD_HEAD_DIM) for h in range(SSD_HEADS)]
        + [rep(gates, h, ML_HEAD_DIM) for h in range(2 * ML_HEADS)], axis=1))


def _pack_w_in(w_in):
    n_l, d, n = w_in.shape
    blk = lambda m: pl.BlockSpec((None, PACK_ROWS, m), lambda l, i: (l, i, 0))
    return pl.pallas_call(
        _pack_w_in_kernel,
        out_shape=[jax.ShapeDtypeStruct((n_l, d, NP_COLS), bf16),
                   jax.ShapeDtypeStruct((n_l, d, NS_COLS - P_MAIN), bf16)],
        grid=(n_l, d // PACK_ROWS),
        in_specs=[blk(n)],
        out_specs=[blk(NP_COLS), blk(NS_COLS - P_MAIN)],
        compiler_params=pltpu.CompilerParams(
            dimension_semantics=("arbitrary", "arbitrary"), vmem_limit_bytes=VMEM_LIMIT),
        name="pack_w_in",
    )(w_in)


def _cast_pad_kernel(w_ref, o_ref):
    n = w_ref.shape[-1]
    o_ref[:, 0:n] = w_ref[...].astype(bf16)
    o_ref[:, n:] = jnp.zeros((w_ref.shape[0], o_ref.shape[-1] - n), bf16)


def _cast_pad(a):
    n_l, k, n = a.shape
    rows = 256
    return pl.pallas_call(
        _cast_pad_kernel,
        out_shape=jax.ShapeDtypeStruct((n_l, k, n + V7X_LANES), bf16),
        grid=(n_l, k // rows),
        in_specs=[pl.BlockSpec((None, rows, n), lambda l, i: (l, i, 0))],
        out_specs=pl.BlockSpec((None, rows, n + V7X_LANES), lambda l, i: (l, i, 0)),
        compiler_params=pltpu.CompilerParams(
            dimension_semantics=("arbitrary", "arbitrary"), vmem_limit_bytes=VMEM_LIMIT),
        name="cast_pad",
    )(a)


def _outmlp_kernel(x_ref, y_ref, mod_ref, wout_ref, n2w_ref, up_ref, down_ref, fnw_ref, o_ref,
                   *, final):
    x = x_ref[...]
    t, bb, d = x.shape
    m = t * bb
    ycat = y_ref[...].reshape(m, MIX_WIDTH).astype(bf16)
    mix = jnp.dot(ycat, wout_ref[:, 0:d], preferred_element_type=f32).reshape(t, bb, d)
    x1 = x + mod_ref[2] * mix
    hn = _rms(x1, n2w_ref[...]) * (1.0 + mod_ref[4]) + mod_ref[3]
    hn = hn.reshape(m, d).astype(bf16)
    ff = jnp.zeros((m, d), f32)
    fc = 1024
    for c in range(D_FF // fc):
        h = jnp.dot(hn, up_ref[:, c * fc:(c + 1) * fc], preferred_element_type=f32)
        h = jnp.square(jnp.maximum(h, 0.0)).astype(bf16)
        ff = ff + jnp.dot(h, down_ref[c * fc:(c + 1) * fc, 0:D_MODEL],
                          preferred_element_type=f32)
    x2 = x1 + mod_ref[5] * ff.reshape(t, bb, d)
    if final:
        x2 = _rms(x2, fnw_ref[...])
    o_ref[...] = x2


def _outmlp(l, x, ycat, mod, wout, n2w, up, down, fnw, bb, final):
    t, b, d = x.shape
    return pl.pallas_call(
        functools.partial(_outmlp_kernel, final=final),
        out_shape=jax.ShapeDtypeStruct((t, b, d), f32),
        grid=(b // bb,),
        in_specs=[
            pl.BlockSpec((t, bb, d), lambda i: (0, i, 0)),
            pl.BlockSpec((t, bb, MIX_WIDTH), lambda i: (0, i, 0)),
            pl.BlockSpec((None, 6, bb, d), lambda i: (l, 0, i, 0)),
            _layer(wout, l, resident=True),
            _layer(n2w, l),
            _layer(up, l, resident=True),
            _layer(down, l, resident=True),
            _full((1, d)),
        ],
        out_specs=pl.BlockSpec((t, bb, d), lambda i: (0, i, 0)),
        compiler_params=pltpu.CompilerParams(
            dimension_semantics=("arbitrary",), vmem_limit_bytes=VMEM_LIMIT),
        name="outproj_mlp",
    )(x, ycat, mod, wout, n2w, up, down, fnw)


PIECE = 512


def _stage_in(x, mod_ref, nw_ref, w_ref, u_ref):
    hn = _rms(x, nw_ref[...]) * (1.0 + mod_ref[0, 1:2, :]) + mod_ref[0, 0:1, :]
    hn = hn.astype(bf16)
    yield
    for c0 in range(0, NP_COLS, PIECE):
        c1 = min(c0 + PIECE, NP_COLS)
        u_ref[:, c0:c1] = jnp.dot(hn, w_ref[:, c0:c1], preferred_element_type=f32)
        yield


def _stage_out(x, y_ref, o_ref, rows, mod_ref, wout_ref, n2w_ref, up_ref, down_ref, fnw_ref,
               final):
    halves = [slice(h * PIECE, (h + 1) * PIECE) for h in range(D_MODEL // PIECE)]
    ycat = y_ref[...]
    mix = []
    for hs in halves:
        mix.append(jnp.dot(ycat, wout_ref[:, hs], preferred_element_type=f32))
        yield
    x1 = x + mod_ref[0, 2:3, :] * jnp.concatenate(mix, axis=1)
    hn = (_rms(x1, n2w_ref[...]) * (1.0 + mod_ref[0, 4:5, :]) + mod_ref[0, 3:4, :]).astype(bf16)
    yield
    ff = [jnp.zeros((x.shape[0], PIECE), f32) for _ in halves]
    fc = 2 * PIECE
    for c in range(D_FF // fc):
        h = []
        for k in range(fc // PIECE):
            cols = slice(c * fc + k * PIECE, c * fc + (k + 1) * PIECE)
            hk = jnp.dot(hn, up_ref[:, cols], preferred_element_type=f32)
            h.append(jnp.square(jnp.maximum(hk, 0.0)).astype(bf16))
            yield
        h = jnp.concatenate(h, axis=1)
        for i, hs in enumerate(halves):
            ff[i] = ff[i] + jnp.dot(h, down_ref[c * fc:(c + 1) * fc, hs],
                                    preferred_element_type=f32)
            yield
    x2 = x1 + mod_ref[0, 5:6, :] * jnp.concatenate(ff, axis=1)
    o_ref[0, rows, :] = _rms(x2, fnw_ref[...]) if final else x2
    yield


def _trace_interleaved(main, fillers, n_main, n_fill):
    fillers = list(fillers)
    due = 0.0
    nxt = 0
    main_live = True
    while main_live or fillers:
        if main_live:
            main_live = next(main, _DONE) is not _DONE
            due += n_fill / n_main
        else:
            due = float(len(fillers))
        while fillers and due >= 1.0:
            nxt %= len(fillers)
            if next(fillers[nxt], _DONE) is _DONE:
                fillers.pop(nxt)
            else:
                nxt += 1
                due -= 1.0


_DONE = object()


def _reset_state_if(first, state):
    for ref in state:
        ref[...] = jnp.where(first, 0.0, ref[...])


def _stage_mix(u_ref, y_ref, prm, state, ci, outs):
    (cw_ref, cb_ref, sbias_ref, alog_ref, dskip_ref, snw_ref, mnw_ref,
     lcw_ref, lcb_ref, wax_ref, ba_ref, bx_ref, lam_ref) = prm
    xbuf, lbuf, st, cst, nst, mst, hst = state
    L = CHUNK
    SUB = V7X_SUBLANES
    HIST = (CONV_W - 1) * SUB

    row = lax.broadcasted_iota(jnp.int32, (L, L), 0)
    col = lax.broadcasted_iota(jnp.int32, (L, L), 1)
    tok = lambda i: (i & (SUB - 1)) * ROW_TILES + (i >> 3)
    tri = tok(row) >= tok(col)
    lo = col < SSD_HEAD_DIM

    def bcl(a, j):
        return jnp.broadcast_to(a[:, j:j + 1], (L, L))

    def conv(u0, hist_ref, w_ref, b_ref, c0, c1):
        raw = u_ref[:, u0 + c0:u0 + c1]
        sub = lax.broadcasted_iota(jnp.int32, (SUB, c1 - c0), 0)
        prev = [pltpu.roll(jnp.where(sub == SUB - 1, hist_ref[k * SUB:(k + 1) * SUB, c0:c1],
                                     raw[L - HIST + k * SUB:L - HIST + (k + 1) * SUB, :]), 1, 0)
                for k in range(CONV_W - 1)]
        acc = b_ref[:, c0:c1] + raw * w_ref[CONV_W - 1:CONV_W, c0:c1]
        for d in range(1, CONV_W):
            shifted = jnp.concatenate(prev[CONV_W - 1 - d:] + [raw[0:L - d * SUB, :]], axis=0)
            acc = acc + shifted * w_ref[CONV_W - 1 - d:CONV_W - d, c0:c1]
        hist_ref[:, c0:c1] = raw[L - HIST:L, :]
        return acc

    cblk = 512
    xbc = []
    for c0 in range(0, SSD_CONV_DIM, cblk):
        xbc.append(_silu(conv(P_XBC, xbuf, cw_ref, cb_ref, c0, c0 + cblk)))
        yield
    xbc = jnp.concatenate(xbc, axis=1)
    xr = conv(P_XR, lbuf, lcw_ref, lcb_ref, 0, LRU_WIDTH)
    yield

    sm = u_ref[:,P_SMALL:P_SMALL + V7X_LANES] + sbias_ref[...]
    dt = _softplus(sm)
    a_row = -jnp.exp(alog_ref[...])
    gates = jnp.where(col < SSD_HEADS, dt * (a_row * LOG2E),
                      jnp.where((col >= 20) & (col < 24), _log_sigmoid(sm), 0.0))
    g_hi = gates.astype(bf16)
    g_r = gates - g_hi.astype(f32)
    g_mid = g_r.astype(bf16)
    g_lo = (g_r - g_mid.astype(f32)).astype(bf16)
    cum3 = jnp.dot(jnp.where(tri, 1.0, 0.0).astype(bf16),
                   jnp.concatenate([g_hi, g_mid, g_lo], axis=1), preferred_element_type=f32)
    cum = cum3[:, 0:L] + cum3[:, L:2 * L] + cum3[:, 2 * L:3 * L]
    cum_t = cum.T
    sm_t = sm.T
    yield

    ys = []
    for g in range(SSD_GROUPS):
        bm_t = xbc[:, SSD_WIDTH + g * SSD_STATE:SSD_WIDTH + (g + 1) * SSD_STATE].T.astype(bf16)
        cm = xbc[:, SSD_WIDTH + (SSD_GROUPS + g) * SSD_STATE:
                 SSD_WIDTH + (SSD_GROUPS + g + 1) * SSD_STATE].astype(bf16)
        cb = jnp.dot(cm, bm_t, preferred_element_type=f32)
        inter = jnp.dot(cm, st[g].astype(bf16), preferred_element_type=f32)
        yield
        xends, keeps = [], []
        for jj in range(4):
            jp = g * 4 + jj
            e0, e1 = 2 * jp, 2 * jp + 1
            c0 = bcl(cum, e0)
            c1 = bcl(cum, e1)
            cum_p = jnp.where(lo, c0, c1)
            dt_p = jnp.where(lo, bcl(dt, e0), bcl(dt, e1))
            xs_p = xbc[:, jp * L:(jp + 1) * L]
            xdt = xs_p * dt_p
            dec0 = jnp.exp2(jnp.where(tri, c0 - cum_t[e0:e0 + 1, :], -jnp.inf))
            dec1 = jnp.exp2(jnp.where(tri, c1 - cum_t[e1:e1 + 1, :], -jnp.inf))
            att = jnp.concatenate([(cb * dec0).astype(bf16), (cb * dec1).astype(bf16)], axis=1)
            x2 = jnp.concatenate([jnp.where(lo, xdt, 0.0).astype(bf16),
                                  jnp.where(lo, 0.0, xdt).astype(bf16)], axis=0)
            y_p = (jnp.dot(att, x2, preferred_element_type=f32)
                   + inter[:, jj * L:(jj + 1) * L] * jnp.exp2(cum_p)
                   + xs_p * dskip_ref[:, jp * L:(jp + 1) * L])
            ys.append(y_p)
            c_last = cum_p[L - 1:L, :]
            xends.append((xdt * jnp.exp2(c_last - cum_p)).astype(bf16))
            keeps.append(jnp.exp2(c_last))
            yield
        st[g] = (jnp.concatenate(keeps, axis=1) * st[g]
                 + jnp.dot(bm_t, jnp.concatenate(xends, axis=1), preferred_element_type=f32))
    y = jnp.concatenate(ys, axis=1)
    z = u_ref[:,P_Z:P_Z + SSD_WIDTH]
    y_ref[:,0:SSD_WIDTH] = _rms(y * _silu(z), snw_ref[...]).astype(bf16)
    yield

    scale = ML_HEAD_DIM ** -0.5
    for h in range(ML_HEADS):
        sl = slice(h * ML_HEAD_DIM, (h + 1) * ML_HEAD_DIM)
        q = u_ref[:,P_Q + h * ML_HEAD_DIM:P_Q + (h + 1) * ML_HEAD_DIM]
        k = u_ref[:,P_K + h * ML_HEAD_DIM:P_K + (h + 1) * ML_HEAD_DIM] * scale
        v = u_ref[:,P_V + h * ML_HEAD_DIM:P_V + (h + 1) * ML_HEAD_DIM]
        o = u_ref[:,P_O + h * ML_HEAD_DIM:P_O + (h + 1) * ML_HEAD_DIM]
        k_t = k.T.astype(bf16)
        jf = 20 + h
        ji = 16 + h
        b_col = cum[:, jf:jf + 1]
        i_col = sm[:, ji:ji + 1]
        dm = jnp.where(tri, bcl(cum, jf) - cum_t[jf:jf + 1, :] + sm_t[ji:ji + 1, :], -jnp.inf)
        m_prev = mst[h, :, 0:1]
        g_col = b_col + m_prev
        mt = jnp.maximum(g_col, jnp.max(dm, axis=-1, keepdims=True))
        w = jnp.exp(dm - mt) * jnp.dot(q.astype(bf16), k_t, preferred_element_type=f32)
        inter = jnp.exp(g_col - mt)
        ct_old = cst[h]
        n_old = nst[h]
        num = jnp.dot(jnp.concatenate([w.astype(bf16), (inter * q).astype(bf16)], axis=1),
                      jnp.concatenate([v.astype(bf16), ct_old.astype(bf16)], axis=0),
                      preferred_element_type=f32)
        den = (jnp.sum(w, axis=-1, keepdims=True)
               + inter * jnp.sum(q * n_old, axis=-1, keepdims=True))
        hout = num / jnp.maximum(jnp.abs(den), jnp.exp(-mt))
        m_new = mt[L - 1:L, :]
        b_last = b_col[L - 1:L, :]
        w_end = jnp.exp(b_last - b_col + i_col - m_new)
        dc = jnp.exp(b_last + m_prev - m_new)
        cst[h] = dc * ct_old + jnp.dot(k_t, (v * w_end).astype(bf16),
                                       preferred_element_type=f32)
        nst[h] = dc * n_old + jnp.sum(w_end * k, axis=0, keepdims=True)
        mst[h] = jnp.broadcast_to(m_new, (1, V7X_LANES))
        yh = _rms(hout, mnw_ref[:, sl]) * _sigmoid(o)
        y_ref[:,SSD_WIDTH + h * ML_HEAD_DIM:SSD_WIDTH + (h + 1) * ML_HEAD_DIM] = yh.astype(bf16)
        yield

    rs, is_ = [], []
    for kb_ in range(LRU_BLOCKS):
        xk = xr[:, kb_ * LRU_BLOCK_DIM:(kb_ + 1) * LRU_BLOCK_DIM].astype(bf16)
        ri = jnp.dot(xk, wax_ref[kb_], preferred_element_type=f32)
        rs.append(ri[:, 0:LRU_BLOCK_DIM])
        is_.append(ri[:, LRU_BLOCK_DIM:2 * LRU_BLOCK_DIM])
    r = _sigmoid(jnp.concatenate(rs, axis=1) + ba_ref[...])
    i_g = _sigmoid(jnp.concatenate(is_, axis=1) + bx_ref[...])
    yield
    log_a = -LRU_C * r * _softplus(-lam_ref[...])
    mult = jnp.sqrt(_neg_expm1(2.0 * log_a))
    rowl = lax.broadcasted_iota(jnp.int32, (L, LRU_WIDTH), 0)
    first_row = jnp.where(ci == 0, 0, -1)
    mult = jnp.where(rowl == first_row, 1.0, mult)
    a_s = jnp.exp(log_a)
    u_s = mult * i_g * xr
    yield
    tile = lambda a, v: a[v * SUB:(v + 1) * SUB, :]
    h_loc = [tile(u_s, 0)]
    a_cum = [tile(a_s, 0)]
    for v in range(1, ROW_TILES):
        h_loc.append(tile(a_s, v) * h_loc[-1] + tile(u_s, v))
        a_cum.append(tile(a_s, v) * a_cum[-1])
    a_e, h_e = a_cum[-1], h_loc[-1]
    subl = lax.broadcasted_iota(jnp.int32, (SUB, LRU_WIDTH), 0)
    k_ = 1
    while k_ < SUB:
        keep = subl >= k_
        a_sh = jnp.where(keep, pltpu.roll(a_e, k_, 0), 1.0)
        h_sh = jnp.where(keep, pltpu.roll(h_e, k_, 0), 0.0)
        h_e = a_e * h_sh + h_e
        a_e = a_e * a_sh
        k_ *= 2
    h0 = hst[...]
    run_end = h_e + a_e * h0
    run_in = jnp.where(subl == 0, h0, pltpu.roll(run_end, 1, 0))
    hr = jnp.concatenate([h_loc[v] + a_cum[v] * run_in for v in range(ROW_TILES)], axis=0)
    hst[...] = run_end[SUB - 1:SUB, :]
    gr = u_ref[:,P_GR:P_GR + LRU_WIDTH]
    y_ref[:,SSD_WIDTH + ML_WIDTH:MIX_WIDTH] = (hr * _gelu_tanh(gr)).astype(bf16)
    yield

    if outs is not None:
        ssm_ref, sconv_ref, mc_ref, mn_ref, mm_ref, lh_ref, lconv_ref = outs
        for g in range(SSD_GROUPS):
            for jj in range(4):
                ssm_ref[0, g * 4 + jj] = st[g, :, jj * L:(jj + 1) * L].T
        for k in range(CONV_W - 1):
            sconv_ref[0, k:k + 1, :] = xbuf[k * SUB + SUB - 1:(k + 1) * SUB, :]
            lconv_ref[0, k:k + 1, :] = lbuf[k * SUB + SUB - 1:(k + 1) * SUB, :]
        for h in range(ML_HEADS):
            mc_ref[0, h] = cst[h].T
            mn_ref[0, h:h + 1, :] = nst[h]
            mm_ref[0, h:h + 1, :] = mst[h]
        lh_ref[0] = hst[...]


N_MIX_PARAMS = 13
N_STATE_OUTS = 7
N_MIX_PHASES = 24
N_FILL_PIECES = 33


def _player_kernel(*refs, final, n_chunks, nc):
    xa_ref, moda_ref, xc_ref, modc_ref, n1w_ref, win_ref = refs[:6]
    prm = refs[6:6 + N_MIX_PARAMS]
    wout_ref, n2w_ref, up_ref, down_ref, fnw_ref = refs[6 + N_MIX_PARAMS:11 + N_MIX_PARAMS]
    n_in = 11 + N_MIX_PARAMS + N_STATE_OUTS
    o_ref = refs[n_in]
    outs = refs[n_in + 1:n_in + 1 + N_STATE_OUTS]
    u_bufs = refs[-11:-9]
    y_bufs = refs[-9:-7]
    state = refs[-7:]
    k = pl.program_id(0)

    @pl.when(k == 0)
    def _():
        u_bufs[1][...] = jnp.zeros(u_bufs[1].shape, f32)
        y_bufs[1][...] = jnp.zeros(y_bufs[1].shape, bf16)
        for ref in state:
            ref[...] = jnp.zeros(ref.shape, f32)

    for sub in range(2):
        rows = slice(sub * CHUNK, (sub + 1) * CHUNK)
        stage_in = _stage_in(xa_ref[0, rows, :], moda_ref, n1w_ref, win_ref, u_bufs[sub])
        jb = jnp.clip(2 * k + sub - 1, 0, n_chunks - 1)
        ci = jb % nc
        if sub == 1:
            _reset_state_if(ci == 0, state)
        stage_mix = _stage_mix(u_bufs[1 - sub], y_bufs[sub], prm, state, ci,
                               outs if sub == 0 else None)
        stage_out = _stage_out(xc_ref[0, rows, :], y_bufs[1 - sub], o_ref, rows, modc_ref,
                               wout_ref, n2w_ref, up_ref, down_ref, fnw_ref, final)
        _trace_interleaved(stage_mix, [stage_in, stage_out], N_MIX_PHASES, N_FILL_PIECES)


def _player(l, x, mod, mod_row0, w, fnw, final, prev):
    b, t, d = x.shape
    nc = t // CHUNK
    n_chunks = b * nc
    n_pairs = n_chunks // 2
    ppr = nc // 2
    pair_a = lambda k: jnp.minimum(k, n_pairs - 1)
    pair_c = lambda k: jnp.clip(k - 1, 0, n_pairs - 1)
    seq_b = lambda k: jnp.clip(2 * k - 1, 0, n_chunks - 1) // nc
    xspec = lambda pair: pl.BlockSpec((1, 2 * CHUNK, d), lambda k: (pair(k) // ppr, pair(k) % ppr, 0))
    mspec = lambda pair: pl.BlockSpec((None, 1, 6, d),
                                      lambda k: (l, mod_row0 + pair(k) // ppr, 0, 0))
    st_spec = lambda s: pl.BlockSpec((None, 1) + s[2:],
                                     lambda k: (l, seq_b(k)) + (0,) * (len(s) - 2))
    outs = [jax.ShapeDtypeStruct((b, t, d), f32)]
    outs += [jax.ShapeDtypeStruct(a.shape, f32) for a in prev]
    mix_params = [w["ssd_conv_w"], w["ssd_conv_b"], w["p_small_bias"], w["p_alog"], w["dskip"],
                  w["ssd_norm_w"], w["ml_norm_w"], w["lru_conv_w"], w["lru_conv_b"], w["lru_wax"],
                  w["lru_ba"], w["lru_bx"], w["lru_lambda"]]
    assert len(mix_params) == N_MIX_PARAMS and len(prev) == N_STATE_OUTS
    n_in = 11 + N_MIX_PARAMS
    return pl.pallas_call(
        functools.partial(_player_kernel, final=final, n_chunks=n_chunks, nc=nc),
        out_shape=outs,
        grid=(n_pairs + 1,),
        in_specs=[xspec(pair_a), mspec(pair_a), xspec(pair_c), mspec(pair_c),
                  _layer(w["norm1_w"], l), _layer(w["w_in_p"], l, resident=True)]
        + [_layer(a, l) for a in mix_params]
        + [_layer(w["w_out"], l, resident=True), _layer(w["norm2_w"], l),
           _layer(w["mlp_up"], l, resident=True), _layer(w["mlp_down"], l, resident=True),
           _full((1, d))]
        + [pl.BlockSpec(memory_space=pl.ANY)] * len(prev),
        out_specs=[xspec(pair_c)] + [st_spec(a.shape) for a in prev],
        input_output_aliases={n_in + i: 1 + i for i in range(len(prev))},
        scratch_shapes=[
            pltpu.VMEM((CHUNK, NP_COLS), f32), pltpu.VMEM((CHUNK, NP_COLS), f32),
            pltpu.VMEM((CHUNK, MIX_WIDTH), bf16), pltpu.VMEM((CHUNK, MIX_WIDTH), bf16),
            pltpu.VMEM(((CONV_W - 1) * V7X_SUBLANES, SSD_CONV_DIM), f32),
            pltpu.VMEM(((CONV_W - 1) * V7X_SUBLANES, LRU_WIDTH), f32),
            pltpu.VMEM((SSD_GROUPS, SSD_STATE, 512), f32),
            pltpu.VMEM((ML_HEADS, ML_HEAD_DIM, ML_HEAD_DIM), f32),
            pltpu.VMEM((ML_HEADS, 1, ML_HEAD_DIM), f32),
            pltpu.VMEM((ML_HEADS, 1, V7X_LANES), f32),
            pltpu.VMEM((1, LRU_WIDTH), f32),
        ],
        compiler_params=pltpu.CompilerParams(
            dimension_semantics=("arbitrary",), vmem_limit_bytes=PLAYER_VMEM_LIMIT),
        name="prompt_layer",
    )(x, mod, x, mod, w["norm1_w"], w["w_in_p"], *mix_params,
      w["w_out"], w["norm2_w"], w["mlp_up"], w["mlp_down"], fnw, *prev)


def _smix_kernel(u_ref, s0_ref, sconv0_ref, c0_ref, n0_ref, m0_ref, lh0_ref, lconv0_ref,
                 cw_ref, cb_ref, dtb_ref, alog_ref, dskip_ref, snw_ref, ib_ref, fb_ref, mnw_ref,
                 lcw_ref, lcb_ref, wax_ref, ba_ref, bx_ref, lam_ref,
                 y_ref, s_ref, sconv_ref, c_ref, n_ref, m_ref, lh_ref, lconv_ref):
    T = DEC_SEQ
    bb = u_ref.shape[1]
    H = ML_HEAD_DIM
    toks = range(T)

    def useg(t, c0, n):
        return u_ref[t, :, c0:c0 + n]

    def to_seq(rows):
        return jnp.swapaxes(jnp.stack(rows, axis=0), 0, 1)

    def to_tok(a):
        a = jnp.swapaxes(a, 0, 1)
        return [a[t] for t in toks]

    def conv(c0, n, hist_ref, w_ref, b_ref, new_ref):
        xpad = [hist_ref[k] for k in range(CONV_W - 1)] + [useg(t, c0, n) for t in toks]
        for k in range(CONV_W - 1):
            new_ref[k] = xpad[T + k]
        out = []
        for t in toks:
            acc = b_ref[...] + xpad[t] * w_ref[0:1, :]
            for j in range(1, CONV_W):
                acc = acc + xpad[t + j] * w_ref[j:j + 1, :]
            out.append(acc)
        return out

    def split3(d):
        hi = d.astype(bf16).astype(f32)
        r1 = d - hi
        mid = r1.astype(bf16).astype(f32)
        return [hi, mid, (r1 - mid).astype(bf16).astype(f32)]

    def update_lhs(rows, decay):
        pad = [jnp.zeros_like(decay)] * (T - 3)
        return to_seq(rows + split3(decay) + pad).astype(bf16)

    def ones_rhs(kmat):
        zeros = jnp.zeros_like(kmat)
        return jnp.concatenate(
            [jnp.concatenate([kmat, zeros], axis=2),
             jnp.concatenate([zeros, jnp.ones_like(kmat)], axis=2)], axis=1).astype(bf16)

    xbc = [_silu(a) for a in conv(P_XBC, SSD_CONV_DIM, sconv0_ref, cw_ref, cb_ref, sconv_ref)]
    xs = [a[:, 0:SSD_WIDTH] for a in xbc]
    seg = lambda a, j: a[:, SSD_WIDTH + j * SSD_STATE:SSD_WIDTH + (j + 1) * SSD_STATE]
    bm = [[seg(a, g) for a in xbc] for g in range(SSD_GROUPS)]
    cm = [[seg(a, SSD_GROUPS + g) for a in xbc] for g in range(SSD_GROUPS)]
    a_row = -jnp.exp(alog_ref[...]) * LOG2E
    lane = lax.broadcasted_iota(jnp.int32, (bb, SSD_WIDTH), 1)
    cum, xdt = [], []
    for t in toks:
        dt = _softplus(useg(t, S_DT, SSD_WIDTH) + dtb_ref[...])
        cum.append(dt * a_row + (cum[-1] if cum else 0.0))
        xdt.append(xs[t] * dt)
    y = [xs[t] * dskip_ref[...] for t in toks]
    for t in toks:
        for s in range(t + 1):
            cb0 = jnp.sum(cm[0][t] * bm[0][s], axis=-1, keepdims=True)
            cb1 = jnp.sum(cm[1][t] * bm[1][s], axis=-1, keepdims=True)
            term = jnp.where(lane < 512, cb0, cb1) * xdt[s]
            y[t] = y[t] + (term if s == t else term * jnp.exp2(cum[t] - cum[s]))
    inter = []
    for g in range(SSD_GROUPS):
        inter.append(to_tok(jnp.einsum(
            'btn,bpn->btp', to_seq(cm[g]).astype(bf16),
            s0_ref[:, g * 512:(g + 1) * 512, :].astype(bf16), preferred_element_type=f32)))
    for t in toks:
        yt = y[t] + jnp.concatenate([inter[0][t], inter[1][t]], axis=-1) * jnp.exp2(cum[t])
        y_ref[t, :, 0:SSD_WIDTH] = _rms(yt * _silu(useg(t, P_Z, SSD_WIDTH)), snw_ref[...])
    xend = [xdt[s] if s == T - 1 else xdt[s] * jnp.exp2(cum[T - 1] - cum[s]) for s in toks]
    lhs = update_lhs(xend, jnp.exp2(cum[T - 1]))
    for g in range(SSD_GROUPS):
        zz = jnp.einsum('bkp,bkn->bpn', lhs[:, :, g * 512:(g + 1) * 512],
                        ones_rhs(to_seq(bm[g])), preferred_element_type=f32)
        s_ref[:, g * 512:(g + 1) * 512, :] = (
            zz[:, :, SSD_STATE:] * s0_ref[:, g * 512:(g + 1) * 512, :] + zz[:, :, :SSD_STATE])

    def headsum(a):
        return jnp.concatenate(
            [jnp.broadcast_to(jnp.sum(a[:, h * H:(h + 1) * H], axis=-1, keepdims=True), (bb, H))
             for h in range(ML_HEADS)], axis=-1)

    q = [useg(t, P_Q, ML_WIDTH) for t in toks]
    k = [useg(t, P_K, ML_WIDTH) * (H ** -0.5) for t in toks]
    v = [useg(t, P_V, ML_WIDTH) for t in toks]
    ic = [useg(t, S_I, ML_WIDTH) + ib_ref[...] for t in toks]
    bc = []
    for t in toks:
        fc = _log_sigmoid(useg(t, S_F, ML_WIDTH) + fb_ref[...])
        bc.append(fc + (bc[-1] if bc else 0.0))
    m0 = m0_ref[...]
    n0 = n0_ref[...]
    q_seq = to_seq(q).astype(bf16)
    qc = to_tok(jnp.concatenate(
        [jnp.einsum('btk,bvk->btv', q_seq[:, :, h * H:(h + 1) * H],
                    c0_ref[:, h * H:(h + 1) * H, :].astype(bf16), preferred_element_type=f32)
         for h in range(ML_HEADS)], axis=-1))
    mts = []
    for t in toks:
        dm = [bc[t] - bc[s] + ic[s] for s in range(t + 1)]
        g_t = bc[t] + m0
        mt = g_t
        for d in dm:
            mt = jnp.maximum(mt, d)
        mts.append(mt)
        inter_m = jnp.exp(g_t - mt)
        num = inter_m * qc[t]
        den = inter_m * headsum(q[t] * n0)
        for s in range(t + 1):
            w = jnp.exp(dm[s] - mt) * headsum(q[t] * k[s])
            num = num + w * v[s]
            den = den + w
        hout = num / jnp.maximum(jnp.abs(den), jnp.exp(-mt))
        y_ml = (hout * lax.rsqrt(headsum(hout * hout) * (1.0 / H) + EPS) * mnw_ref[...]
                * _sigmoid(useg(t, P_O, ML_WIDTH)))
        y_ref[t, :, SSD_WIDTH:SSD_WIDTH + ML_WIDTH] = y_ml
    m_new = mts[T - 1]
    m_ref[...] = m_new
    w_end = [jnp.exp(bc[T - 1] - bc[s] + ic[s] - m_new) for s in toks]
    dc = jnp.exp(bc[T - 1] + m0 - m_new)
    n_new = dc * n0
    for s in toks:
        n_new = n_new + w_end[s] * k[s]
    n_ref[...] = n_new
    lhs_m = update_lhs([v[s] * w_end[s] for s in toks], dc)
    k_seq = to_seq(k)
    for h in range(ML_HEADS):
        sl = slice(h * H, (h + 1) * H)
        zz = jnp.einsum('bkv,bkn->bvn', lhs_m[:, :, sl], ones_rhs(k_seq[:, :, sl]),
                        preferred_element_type=f32)
        c_ref[:, sl, :] = zz[:, :, H:] * c0_ref[:, sl, :] + zz[:, :, :H]

    xr = jnp.concatenate(conv(P_XR, LRU_WIDTH, lconv0_ref, lcw_ref, lcb_ref, lconv_ref), axis=0)
    rs, is_ = [], []
    for kk in range(LRU_BLOCKS):
        xk = xr[:, kk * LRU_BLOCK_DIM:(kk + 1) * LRU_BLOCK_DIM].astype(bf16)
        ri = jnp.dot(xk, wax_ref[kk], preferred_element_type=f32)
        rs.append(ri[:, 0:LRU_BLOCK_DIM])
        is_.append(ri[:, LRU_BLOCK_DIM:2 * LRU_BLOCK_DIM])
    r = _sigmoid(jnp.concatenate(rs, axis=1) + ba_ref[...])
    i_g = _sigmoid(jnp.concatenate(is_, axis=1) + bx_ref[...])
    log_a = -LRU_C * r * _softplus(-lam_ref[...])
    a_s = jnp.exp(log_a)
    u_s = jnp.sqrt(_neg_expm1(2.0 * log_a)) * i_g * xr
    cur = lh0_ref[...]
    for t in toks:
        rows = slice(t * bb, (t + 1) * bb)
        cur = a_s[rows] * cur + u_s[rows]
        y_ref[t, :, SSD_WIDTH + ML_WIDTH:MIX_WIDTH] = cur * _gelu_tanh(useg(t, P_GR, LRU_WIDTH))
    lh_ref[...] = cur


def _smix(l, u, states, prev, w, bb):
    t, b, _ = u.shape

    def sblk(a, ax):
        shape = tuple(None if d == 0 else bb if d == ax else n for d, n in enumerate(a.shape))
        return pl.BlockSpec(shape, lambda i: tuple(l if d == 0 else i if d == ax else 0
                                                   for d in range(a.ndim)))

    ublk = lambda n: pl.BlockSpec((t, bb, n), lambda i: (0, i, 0))
    outs = [jax.ShapeDtypeStruct((t, b, MIX_WIDTH), f32)]
    outs += [jax.ShapeDtypeStruct(a.shape, f32) for a, _ in states]
    weights = [w["ssd_conv_w"], w["ssd_conv_b"], w["s_dt_bias"], w["s_alog"], w["dskip"],
               w["ssd_norm_w"], w["s_i_bias"], w["s_f_bias"], w["ml_norm_w"],
               w["lru_conv_w"], w["lru_conv_b"], w["lru_wax"], w["lru_ba"], w["lru_bx"],
               w["lru_lambda"]]
    n_in = 1 + len(states) + len(weights)
    prev = list(prev)

    def body(*refs):
        _smix_kernel(*refs[:n_in], *refs[n_in + len(prev):])

    return pl.pallas_call(
        body,
        out_shape=outs,
        grid=(b // bb,),
        in_specs=[ublk(u.shape[-1])] + [sblk(a, ax) for a, ax in states]
        + [_layer(a, l) for a in weights]
        + [pl.BlockSpec(memory_space=pl.ANY)] * len(prev),
        out_specs=[ublk(MIX_WIDTH)] + [sblk(a, ax) for a, ax in states],
        input_output_aliases={n_in + k: 1 + k for k in range(len(prev))},
        compiler_params=pltpu.CompilerParams(
            dimension_semantics=("arbitrary",), vmem_limit_bytes=VMEM_LIMIT),
        name="sample_mix",
    )(u, *[a for a, _ in states], *weights, *prev)


def _prep_params(p):
    w_p, w_exp = _pack_w_in(p["w_in"])
    r = lambda a: a.reshape(N_LAYERS, 1, -1)
    rep = lambda a, n: r(jnp.repeat(a, n, axis=-1))
    small = jnp.concatenate([p["ssd_dt_bias"], p["ml_i_bias"], p["ml_f_bias"]], axis=-1)
    return dict(
        w_in_p=w_p, w_exp=w_exp,
        norm1_w=r(p["norm1_w"]), norm2_w=r(p["norm2_w"]),
        ssd_conv_w=p["ssd_conv_w"], ssd_conv_b=r(p["ssd_conv_b"]),
        p_small_bias=r(jnp.pad(small, ((0, 0), (0, V7X_LANES - small.shape[-1])))),
        p_alog=r(jnp.pad(p["ssd_a_log"], ((0, 0), (0, V7X_LANES - SSD_HEADS)))),
        s_dt_bias=rep(p["ssd_dt_bias"], SSD_HEAD_DIM), s_alog=rep(p["ssd_a_log"], SSD_HEAD_DIM),
        s_i_bias=rep(p["ml_i_bias"], ML_HEAD_DIM), s_f_bias=rep(p["ml_f_bias"], ML_HEAD_DIM),
        dskip=rep(p["ssd_d"], SSD_HEAD_DIM),
        ssd_norm_w=r(p["ssd_norm_w"]), ml_norm_w=r(p["ml_norm_w"]),
        lru_conv_w=p["lru_conv_w"], lru_conv_b=r(p["lru_conv_b"]),
        lru_wax=jnp.concatenate([p["lru_wa"], p["lru_wx"]], axis=-1).astype(bf16),
        lru_ba=r(p["lru_ba"]), lru_bx=r(p["lru_bx"]), lru_lambda=r(p["lru_lambda"]),
        w_out=_cast_pad(p["w_out"]), mlp_up=_cast_pad(p["mlp_up"]),
        mlp_down=_cast_pad(p["mlp_down"]),
    )


def kernel(x_prompt, x_sample, c_prompt, c_sample, state_ssm, state_ssd_conv, state_mlstm_c, state_mlstm_n, state_mlstm_m, state_lru_h, state_lru_conv, ada_w, ada_b, norm1_w, norm2_w, w_in, ssd_conv_w, ssd_conv_b, ssd_dt_bias, ssd_a_log, ssd_d, ssd_norm_w, ml_i_bias, ml_f_bias, ml_norm_w, lru_conv_w, lru_conv_b, lru_wa, lru_ba, lru_wx, lru_bx, lru_lambda, w_out, mlp_up, mlp_down, final_norm_w):
    p = dict(norm1_w=norm1_w, norm2_w=norm2_w, w_in=w_in, ssd_conv_w=ssd_conv_w,
             ssd_conv_b=ssd_conv_b, ssd_dt_bias=ssd_dt_bias, ssd_a_log=ssd_a_log, ssd_d=ssd_d,
             ssd_norm_w=ssd_norm_w, ml_i_bias=ml_i_bias, ml_f_bias=ml_f_bias,
             ml_norm_w=ml_norm_w, lru_conv_w=lru_conv_w, lru_conv_b=lru_conv_b, lru_wa=lru_wa,
             lru_ba=lru_ba, lru_wx=lru_wx, lru_bx=lru_bx, lru_lambda=lru_lambda, w_out=w_out,
             mlp_up=mlp_up, mlp_down=mlp_down)
    w = _prep_params(p)
    fnw = final_norm_w.reshape(1, D_MODEL)
    bp = x_prompt.shape[0]
    bs_ = x_sample.shape[0]

    mod = _ada(jnp.concatenate([c_sample, c_prompt], axis=0), ada_w, ada_b)
    mod = mod.reshape(N_LAYERS, bs_ + bp, 6, D_MODEL)

    xp = _interleave_rows(x_prompt)
    p_out = [jnp.zeros((N_LAYERS, bp) + s, f32) for s in (
        (SSD_HEADS // 2, 2 * SSD_HEAD_DIM, SSD_STATE),
        (CONV_W - 1, SSD_CONV_DIM), (ML_HEADS, ML_HEAD_DIM, ML_HEAD_DIM),
        (ML_HEADS, ML_HEAD_DIM), (ML_HEADS, V7X_LANES),
        (1, LRU_WIDTH), (CONV_W - 1, LRU_WIDTH))]
    for l in range(N_LAYERS):
        xp, *p_out = _player(l, xp, mod, bs_, w, fnw, l == N_LAYERS - 1, p_out)
    xp = _deinterleave_rows(xp)
    ssm, sconv, mc, mn, mm, lh, lconv = p_out
    p_states = (ssm.reshape(N_LAYERS, bp, SSD_HEADS, SSD_HEAD_DIM, SSD_STATE), sconv, mc, mn,
                mm[..., 0], lh.reshape(N_LAYERS, bp, LRU_WIDTH), lconv)

    tok_major = lambda a: jnp.swapaxes(a, -3, -2)
    xs = tok_major(x_sample)
    mod_s = tok_major(mod[:, :bs_])
    st_in = (
        (state_ssm.reshape(N_LAYERS, bs_, SSD_HEADS * SSD_HEAD_DIM, SSD_STATE), 1),
        (tok_major(state_ssd_conv), 2),
        (state_mlstm_c.reshape(N_LAYERS, bs_, ML_HEADS * ML_HEAD_DIM, ML_HEAD_DIM), 1),
        (state_mlstm_n.reshape(N_LAYERS, bs_, ML_WIDTH), 1),
        (jnp.repeat(state_mlstm_m, ML_HEAD_DIM, axis=-1), 1),
        (state_lru_h, 1),
        (tok_major(state_lru_conv), 2),
    )
    st_out = [jnp.zeros(a.shape, f32) for a, _ in st_in]
    for l in range(N_LAYERS):
        u = _inproj(l, xs, mod_s, w["norm1_w"], w["w_in_p"], w["w_exp"], 32)
        ycat, *st_out = _smix(l, u, st_in, st_out, w, 8)
        xs = _outmlp(l, xs, ycat, mod_s, w["w_out"], w["norm2_w"], w["mlp_up"], w["mlp_down"],
                     fnw, 64, l == N_LAYERS - 1)
    xs = tok_major(xs)
    ssm, sconv, mc, mn, mm, lh, lconv = st_out
    s_states = (ssm.reshape(N_LAYERS, bs_, SSD_HEADS, SSD_HEAD_DIM, SSD_STATE), tok_major(sconv),
                mc.reshape(N_LAYERS, bs_, ML_HEADS, ML_HEAD_DIM, ML_HEAD_DIM),
                mn.reshape(N_LAYERS, bs_, ML_HEADS, ML_HEAD_DIM),
                mm.reshape(N_LAYERS, bs_, ML_HEADS, ML_HEAD_DIM)[..., 0],
                lh, tok_major(lconv))

    return (xp, xs) + p_states + s_states
```

```python
import functools
import math

import jax
import jax.numpy as jnp
from jax import lax
from jax.experimental import pallas as pl
from jax.experimental.pallas import tpu as pltpu

f32 = jnp.float32
bf16 = jnp.bfloat16

D_MODEL = 1024
N_LAYERS = 2
MIX_WIDTH = 2 * D_MODEL
SSD_WIDTH = 1024
SSD_HEAD_DIM = 64
SSD_HEADS = 16
SSD_GROUPS = 2
SSD_STATE = 128
SSD_CONV_DIM = SSD_WIDTH + 2 * SSD_GROUPS * SSD_STATE
ML_WIDTH = 512
ML_HEADS = 4
ML_HEAD_DIM = 128
LRU_WIDTH = 512
LRU_BLOCKS = 4
LRU_BLOCK_DIM = 128
LRU_C = 8.0
CONV_W = 4
CHUNK = 128
D_FF = 4 * D_MODEL
EPS = 1e-6
PAST_LEN = 16384
DEC_SEQ = 8

_O_Z, _O_XBC, _O_DT, _O_Q, _O_O_END, _O_I, _O_F, _O_XR, _O_END = (
    0, 1024, 2560, 2576, 4624, 4624, 4628, 4632, 5656)

P_Z = 0
P_XBC = 1024
P_Q = 2560
P_K = 3072
P_V = 3584
P_O = 4096
P_XR = 4608
P_GR = 5120
P_MAIN = 5632
P_SMALL = 5632
NP_COLS = 5760
S_DT = 5632
S_I = 6656
S_F = 7168
NS_COLS = 7680

V7X_LANES = 128
V7X_SUBLANES = 8
VMEM_LIMIT = 56 * 1024 * 1024
PLAYER_VMEM_LIMIT = 60 * 1024 * 1024
ROW_TILES = CHUNK // V7X_SUBLANES
LOG2E = 1.4426950408889634

_NT = (((1,), (1,)), ((), ()))
_TN = (((0,), (0,)), ((), ()))


def _softplus(x):
    return jnp.maximum(x, 0.0) + jnp.log1p(jnp.exp(-jnp.abs(x)))


def _log_sigmoid(x):
    return -_softplus(-x)


def _sigmoid(x):
    return 1.0 / (1.0 + jnp.exp(-x))


def _silu(x):
    return x * _sigmoid(x)


def _gelu_tanh(x):
    c = math.sqrt(2.0 / math.pi)
    return 0.5 * x * (1.0 + jnp.tanh(c * (x + 0.044715 * (x * x * x))))


def _neg_expm1(x):
    return -jnp.tanh(0.5 * x) * (jnp.exp(x) + 1.0)


def _rms(x, w):
    return x * lax.rsqrt(jnp.mean(x * x, axis=-1, keepdims=True) + EPS) * w


def _interleave_rows(x):
    b, t, d = x.shape
    return x.reshape(b, t // CHUNK, V7X_SUBLANES, ROW_TILES, d).swapaxes(2, 3).reshape(b, t, d)


def _deinterleave_rows(x):
    b, t, d = x.shape
    return x.reshape(b, t // CHUNK, ROW_TILES, V7X_SUBLANES, d).swapaxes(2, 3).reshape(b, t, d)


def _full(shape):
    n = len(shape)
    return pl.BlockSpec(shape, lambda *_: (0,) * n)


def _layer(a, l, resident=False):
    nd = a.ndim - 1
    kw = dict(pipeline_mode=pl.Buffered(1)) if resident else {}
    return pl.BlockSpec((None,) + a.shape[1:], lambda *_: (l,) + (0,) * nd, **kw)


def _ada_kernel(c_ref, w_ref, b_ref, o_ref):
    cs = _silu(c_ref[...]).astype(bf16)
    o_ref[0] = jnp.dot(cs, w_ref[0].astype(bf16), preferred_element_type=f32) + b_ref[0]


def _ada(c_all, ada_w, ada_b):
    n = c_all.shape[0]
    tn = 1024
    return pl.pallas_call(
        _ada_kernel,
        out_shape=jax.ShapeDtypeStruct((N_LAYERS, n, 6 * D_MODEL), f32),
        grid=(N_LAYERS, 6 * D_MODEL // tn),
        in_specs=[
            pl.BlockSpec((n, D_MODEL), lambda l, j: (0, 0)),
            pl.BlockSpec((1, D_MODEL, tn), lambda l, j: (l, 0, j)),
            pl.BlockSpec((1, 1, tn), lambda l, j: (l, 0, j)),
        ],
        out_specs=pl.BlockSpec((1, n, tn), lambda l, j: (l, 0, j)),
        compiler_params=pltpu.CompilerParams(
            dimension_semantics=("arbitrary", "arbitrary"), vmem_limit_bytes=VMEM_LIMIT),
        name="ada_mod",
    )(c_all, ada_w, ada_b.reshape(N_LAYERS, 1, 6 * D_MODEL))


def _inproj_kernel(x_ref, mod_ref, nw_ref, wm_ref, we_ref, o_ref):
    x = x_ref[...]
    t, bb, d = x.shape
    hn = _rms(x, nw_ref[...]) * (1.0 + mod_ref[1]) + mod_ref[0]
    hn = hn.reshape(t * bb, d).astype(bf16)
    o_ref[:, :, 0:P_MAIN] = jnp.dot(hn, wm_ref[...], preferred_element_type=f32).reshape(
        t, bb, P_MAIN)
    o_ref[:, :, P_MAIN:NS_COLS] = jnp.dot(hn, we_ref[...], preferred_element_type=f32).reshape(
        t, bb, NS_COLS - P_MAIN)


def _inproj(l, x, mod, nw, w_packed, w_exp, bb):
    t, b, d = x.shape
    return pl.pallas_call(
        _inproj_kernel,
        out_shape=jax.ShapeDtypeStruct((t, b, NS_COLS), f32),
        grid=(b // bb,),
        in_specs=[
            pl.BlockSpec((t, bb, d), lambda i: (0, i, 0)),
            pl.BlockSpec((None, 6, bb, d), lambda i: (l, 0, i, 0)),
            _layer(nw, l),
            pl.BlockSpec((None, d, P_MAIN), lambda i: (l, 0, 0), pipeline_mode=pl.Buffered(1)),
            _layer(w_exp, l, resident=True),
        ],
        out_specs=pl.BlockSpec((t, bb, NS_COLS), lambda i: (0, i, 0)),
        compiler_params=pltpu.CompilerParams(
            dimension_semantics=("arbitrary",), vmem_limit_bytes=VMEM_LIMIT),
        name="inproj",
    )(x, mod, nw, w_packed, w_exp)


PACK_ROWS = 128


def _pack_w_in_kernel(w_ref, wp_ref, we_ref):
    c = lambda a: a.astype(bf16)
    wp_ref[:, 0:P_Q] = c(w_ref[:, 0:_O_DT])
    wp_ref[:, P_Q:P_XR] = c(w_ref[:, _O_Q:_O_O_END])
    wp_ref[:, P_XR:P_MAIN] = c(w_ref[:, _O_XR:_O_END])
    dt = w_ref[:, _O_DT:_O_Q]
    gates = w_ref[:, _O_I:_O_XR]
    n_pad = V7X_LANES - SSD_HEADS - 2 * ML_HEADS
    wp_ref[:, P_SMALL:NP_COLS] = c(jnp.concatenate(
        [dt, gates, jnp.zeros((PACK_ROWS, n_pad), f32)], axis=1))
    rep = lambda a, j, n: jnp.broadcast_to(a[:, j:j + 1], (PACK_ROWS, n))
    we_ref[...] = c(jnp.concatenate(
        [rep(dt, h, SSD_HEAD_DIM) for h in range(SSD_HEADS)]
        + [rep(gates, h, ML_HEAD_DIM) for h in range(2 * ML_HEADS)], axis=1))


def _pack_w_in(w_in):
    n_l, d, n = w_in.shape
    blk = lambda m: pl.BlockSpec((None, PACK_ROWS, m), lambda l, i: (l, i, 0))
    return pl.pallas_call(
        _pack_w_in_kernel,
        out_shape=[jax.ShapeDtypeStruct((n_l, d, NP_COLS), bf16),
                   jax.ShapeDtypeStruct((n_l, d, NS_COLS - P_MAIN), bf16)],
        grid=(n_l, d // PACK_ROWS),
        in_specs=[blk(n)],
        out_specs=[blk(NP_COLS), blk(NS_COLS - P_MAIN)],
        compiler_params=pltpu.CompilerParams(
            dimension_semantics=("arbitrary", "arbitrary"), vmem_limit_bytes=VMEM_LIMIT),
        name="pack_w_in",
    )(w_in)


def _cast_pad_kernel(w_ref, o_ref):
    n = w_ref.shape[-1]
    o_ref[:, 0:n] = w_ref[...].astype(bf16)
    o_ref[:, n:] = jnp.zeros((w_ref.shape[0], o_ref.shape[-1] - n), bf16)


def _cast_pad(a):
    n_l, k, n = a.shape
    rows = 256
    return pl.pallas_call(
        _cast_pad_kernel,
        out_shape=jax.ShapeDtypeStruct((n_l, k, n + V7X_LANES), bf16),
        grid=(n_l, k // rows),
        in_specs=[pl.BlockSpec((None, rows, n), lambda l, i: (l, i, 0))],
        out_specs=pl.BlockSpec((None, rows, n + V7X_LANES), lambda l, i: (l, i, 0)),
        compiler_params=pltpu.CompilerParams(
            dimension_semantics=("arbitrary", "arbitrary"), vmem_limit_bytes=VMEM_LIMIT),
        name="cast_pad",
    )(a)


def _outmlp_kernel(x_ref, y_ref, mod_ref, wout_ref, n2w_ref, up_ref, down_ref, fnw_ref, o_ref,
                   *, final):
    x = x_ref[...]
    t, bb, d = x.shape
    m = t * bb
    ycat = y_ref[...].reshape(m, MIX_WIDTH).astype(bf16)
    mix = jnp.dot(ycat, wout_ref[:, 0:d], preferred_element_type=f32).reshape(t, bb, d)
    x1 = x + mod_ref[2] * mix
    hn = _rms(x1, n2w_ref[...]) * (1.0 + mod_ref[4]) + mod_ref[3]
    hn = hn.reshape(m, d).astype(bf16)
    ff = jnp.zeros((m, d), f32)
    fc = 1024
    for c in range(D_FF // fc):
        h = jnp.dot(hn, up_ref[:, c * fc:(c + 1) * fc], preferred_element_type=f32)
        h = jnp.square(jnp.maximum(h, 0.0)).astype(bf16)
        ff = ff + jnp.dot(h, down_ref[c * fc:(c + 1) * fc, 0:D_MODEL],
                          preferred_element_type=f32)
    x2 = x1 + mod_ref[5] * ff.reshape(t, bb, d)
    if final:
        x2 = _rms(x2, fnw_ref[...])
    o_ref[...] = x2


def _outmlp(l, x, ycat, mod, wout, n2w, up, down, fnw, bb, final):
    t, b, d = x.shape
    return pl.pallas_call(
        functools.partial(_outmlp_kernel, final=final),
        out_shape=jax.ShapeDtypeStruct((t, b, d), f32),
        grid=(b // bb,),
        in_specs=[
            pl.BlockSpec((t, bb, d), lambda i: (0, i, 0)),
            pl.BlockSpec((t, bb, MIX_WIDTH), lambda i: (0, i, 0)),
            pl.BlockSpec((None, 6, bb, d), lambda i: (l, 0, i, 0)),
            _layer(wout, l, resident=True),
            _layer(n2w, l),
            _layer(up, l, resident=True),
            _layer(down, l, resident=True),
            _full((1, d)),
        ],
        out_specs=pl.BlockSpec((t, bb, d), lambda i: (0, i, 0)),
        compiler_params=pltpu.CompilerParams(
            dimension_semantics=("arbitrary",), vmem_limit_bytes=VMEM_LIMIT),
        name="outproj_mlp",
    )(x, ycat, mod, wout, n2w, up, down, fnw)


PIECE = 512


def _stage_in(x, mod_ref, nw_ref, w_ref, u_ref):
    hn = _rms(x, nw_ref[...]) * (1.0 + mod_ref[0, 1:2, :]) + mod_ref[0, 0:1, :]
    hn = hn.astype(bf16)
    yield
    for c0 in range(0, NP_COLS, PIECE):
        c1 = min(c0 + PIECE, NP_COLS)
        u_ref[:, c0:c1] = jnp.dot(hn, w_ref[:, c0:c1], preferred_element_type=f32)
        yield


def _stage_out(x, y_ref, o_ref, rows, mod_ref, wout_ref, n2w_ref, up_ref, down_ref, fnw_ref,
               final):
    halves = [slice(h * PIECE, (h + 1) * PIECE) for h in range(D_MODEL // PIECE)]
    ycat = y_ref[...]
    mix = []
    for hs in halves:
        mix.append(jnp.dot(ycat, wout_ref[:, hs], preferred_element_type=f32))
        yield
    x1 = x + mod_ref[0, 2:3, :] * jnp.concatenate(mix, axis=1)
    hn = (_rms(x1, n2w_ref[...]) * (1.0 + mod_ref[0, 4:5, :]) + mod_ref[0, 3:4, :]).astype(bf16)
    yield
    ff = [jnp.zeros((x.shape[0], PIECE), f32) for _ in halves]
    fc = 2 * PIECE
    for c in range(D_FF // fc):
        h = []
        for k in range(fc // PIECE):
            cols = slice(c * fc + k * PIECE, c * fc + (k + 1) * PIECE)
            hk = jnp.dot(hn, up_ref[:, cols], preferred_element_type=f32)
            h.append(jnp.square(jnp.maximum(hk, 0.0)).astype(bf16))
            yield
        h = jnp.concatenate(h, axis=1)
        for i, hs in enumerate(halves):
            ff[i] = ff[i] + jnp.dot(h, down_ref[c * fc:(c + 1) * fc, hs],
                                    preferred_element_type=f32)
            yield
    x2 = x1 + mod_ref[0, 5:6, :] * jnp.concatenate(ff, axis=1)
    o_ref[0, rows, :] = _rms(x2, fnw_ref[...]) if final else x2
    yield


def _trace_interleaved(main, fillers, n_main, n_fill):
    fillers = list(fillers)
    due = 0.0
    nxt = 0
    main_live = True
    while main_live or fillers:
        if main_live:
            main_live = next(main, _DONE) is not _DONE
            due += n_fill / n_main
        else:
            due = float(len(fillers))
        while fillers and due >= 1.0:
            nxt %= len(fillers)
            if next(fillers[nxt], _DONE) is _DONE:
                fillers.pop(nxt)
            else:
                nxt += 1
                due -= 1.0


_DONE = object()


def _reset_state_if(first, state):
    for ref in state:
        ref[...] = jnp.where(first, 0.0, ref[...])


def _stage_mix(u_ref, y_ref, prm, state, ci, outs):
    (cw_ref, cb_ref, sbias_ref, alog_ref, dskip_ref, snw_ref, mnw_ref,
     lcw_ref, lcb_ref, wax_ref, ba_ref, bx_ref, lam_ref) = prm
    xbuf, lbuf, st, cst, nst, mst, hst = state
    L = CHUNK
    SUB = V7X_SUBLANES
    HIST = (CONV_W - 1) * SUB

    row = lax.broadcasted_iota(jnp.int32, (L, L), 0)
    col = lax.broadcasted_iota(jnp.int32, (L, L), 1)
    tok = lambda i: (i & (SUB - 1)) * ROW_TILES + (i >> 3)
    tri = tok(row) >= tok(col)
    lo = col < SSD_HEAD_DIM

    def bcl(a, j):
        return jnp.broadcast_to(a[:, j:j + 1], (L, L))

    def conv(u0, hist_ref, w_ref, b_ref, c0, c1):
        raw = u_ref[:, u0 + c0:u0 + c1]
        sub = lax.broadcasted_iota(jnp.int32, (SUB, c1 - c0), 0)
        prev = [pltpu.roll(jnp.where(sub == SUB - 1, hist_ref[k * SUB:(k + 1) * SUB, c0:c1],
                                     raw[L - HIST + k * SUB:L - HIST + (k + 1) * SUB, :]), 1, 0)
                for k in range(CONV_W - 1)]
        acc = b_ref[:, c0:c1] + raw * w_ref[CONV_W - 1:CONV_W, c0:c1]
        for d in range(1, CONV_W):
            shifted = jnp.concatenate(prev[CONV_W - 1 - d:] + [raw[0:L - d * SUB, :]], axis=0)
            acc = acc + shifted * w_ref[CONV_W - 1 - d:CONV_W - d, c0:c1]
        hist_ref[:, c0:c1] = raw[L - HIST:L, :]
        return acc

    cblk = 512
    xbc = []
    for c0 in range(0, SSD_CONV_DIM, cblk):
        xbc.append(_silu(conv(P_XBC, xbuf, cw_ref, cb_ref, c0, c0 + cblk)))
        yield
    xbc = jnp.concatenate(xbc, axis=1)
    xr = conv(P_XR, lbuf, lcw_ref, lcb_ref, 0, LRU_WIDTH)
    yield

    sm = u_ref[:,P_SMALL:P_SMALL + V7X_LANES] + sbias_ref[...]
    dt = _softplus(sm)
    a_row = -jnp.exp(alog_ref[...])
    gates = jnp.where(col < SSD_HEADS, dt * (a_row * LOG2E),
                      jnp.where((col >= 20) & (col < 24), _log_sigmoid(sm), 0.0))
    g_hi = gates.astype(bf16)
    g_r = gates - g_hi.astype(f32)
    g_mid = g_r.astype(bf16)
    g_lo = (g_r - g_mid.astype(f32)).astype(bf16)
    cum3 = jnp.dot(jnp.where(tri, 1.0, 0.0).astype(bf16),
                   jnp.concatenate([g_hi, g_mid, g_lo], axis=1), preferred_element_type=f32)
    cum = cum3[:, 0:L] + cum3[:, L:2 * L] + cum3[:, 2 * L:3 * L]
    cum_t = cum.T
    sm_t = sm.T
    yield

    ys = []
    for g in range(SSD_GROUPS):
        bm_t = xbc[:, SSD_WIDTH + g * SSD_STATE:SSD_WIDTH + (g + 1) * SSD_STATE].T.astype(bf16)
        cm = xbc[:, SSD_WIDTH + (SSD_GROUPS + g) * SSD_STATE:
                 SSD_WIDTH + (SSD_GROUPS + g + 1) * SSD_STATE].astype(bf16)
        cb = jnp.dot(cm, bm_t, preferred_element_type=f32)
        inter = jnp.dot(cm, st[g].astype(bf16), preferred_element_type=f32)
        yield
        xends, keeps = [], []
        for jj in range(4):
            jp = g * 4 + jj
            e0, e1 = 2 * jp, 2 * jp + 1
            c0 = bcl(cum, e0)
            c1 = bcl(cum, e1)
            cum_p = jnp.where(lo, c0, c1)
            dt_p = jnp.where(lo, bcl(dt, e0), bcl(dt, e1))
            xs_p = xbc[:, jp * L:(jp + 1) * L]
            xdt = xs_p * dt_p
            dec0 = jnp.exp2(jnp.where(tri, c0 - cum_t[e0:e0 + 1, :], -jnp.inf))
            dec1 = jnp.exp2(jnp.where(tri, c1 - cum_t[e1:e1 + 1, :], -jnp.inf))
            yield
            att =jnp.concatenate([(cb * dec0).astype(bf16), (cb * dec1).astype(bf16)], axis=1)
            x2 = jnp.concatenate([jnp.where(lo, xdt, 0.0).astype(bf16),
                                  jnp.where(lo, 0.0, xdt).astype(bf16)], axis=0)
            y_p = (jnp.dot(att, x2, preferred_element_type=f32)
                   + inter[:, jj * L:(jj + 1) * L] * jnp.exp2(cum_p)
                   + xs_p * dskip_ref[:, jp * L:(jp + 1) * L])
            ys.append(y_p)
            c_last = cum_p[L - 1:L, :]
            xends.append((xdt * jnp.exp2(c_last - cum_p)).astype(bf16))
            keeps.append(jnp.exp2(c_last))
            yield
        st[g] = (jnp.concatenate(keeps, axis=1) * st[g]
                 + jnp.dot(bm_t, jnp.concatenate(xends, axis=1), preferred_element_type=f32))
    y = jnp.concatenate(ys, axis=1)
    z = u_ref[:,P_Z:P_Z + SSD_WIDTH]
    y_ref[:,0:SSD_WIDTH] = _rms(y * _silu(z), snw_ref[...]).astype(bf16)
    yield

    scale = ML_HEAD_DIM ** -0.5
    for h in range(ML_HEADS):
        sl = slice(h * ML_HEAD_DIM, (h + 1) * ML_HEAD_DIM)
        q = u_ref[:,P_Q + h * ML_HEAD_DIM:P_Q + (h + 1) * ML_HEAD_DIM]
        k = u_ref[:,P_K + h * ML_HEAD_DIM:P_K + (h + 1) * ML_HEAD_DIM] * scale
        v = u_ref[:,P_V + h * ML_HEAD_DIM:P_V + (h + 1) * ML_HEAD_DIM]
        o = u_ref[:,P_O + h * ML_HEAD_DIM:P_O + (h + 1) * ML_HEAD_DIM]
        k_t = k.T.astype(bf16)
        jf = 20 + h
        ji = 16 + h
        b_col = cum[:, jf:jf + 1]
        i_col = sm[:, ji:ji + 1]
        dm = jnp.where(tri, bcl(cum, jf) - cum_t[jf:jf + 1, :] + sm_t[ji:ji + 1, :], -jnp.inf)
        m_prev = mst[h, :, 0:1]
        g_col = b_col + m_prev
        mt = jnp.maximum(g_col, jnp.max(dm, axis=-1, keepdims=True))
        yield
        w = jnp.exp(dm - mt) * jnp.dot(q.astype(bf16), k_t, preferred_element_type=f32)
        inter = jnp.exp(g_col - mt)
        yield
        ct_old = cst[h]
        n_old = nst[h]
        num = jnp.dot(jnp.concatenate([w.astype(bf16), (inter * q).astype(bf16)], axis=1),
                      jnp.concatenate([v.astype(bf16), ct_old.astype(bf16)], axis=0),
                      preferred_element_type=f32)
        den = (jnp.sum(w, axis=-1, keepdims=True)
               + inter * jnp.sum(q * n_old, axis=-1, keepdims=True))
        hout = num / jnp.maximum(jnp.abs(den), jnp.exp(-mt))
        yield
        m_new = mt[L - 1:L, :]
        b_last = b_col[L - 1:L, :]
        w_end = jnp.exp(b_last - b_col + i_col - m_new)
        dc = jnp.exp(b_last + m_prev - m_new)
        cst[h] = dc * ct_old + jnp.dot(k_t, (v * w_end).astype(bf16),
                                       preferred_element_type=f32)
        nst[h] = dc * n_old + jnp.sum(w_end * k, axis=0, keepdims=True)
        mst[h] = jnp.broadcast_to(m_new, (1, V7X_LANES))
        yh = _rms(hout, mnw_ref[:, sl]) * _sigmoid(o)
        y_ref[:,SSD_WIDTH + h * ML_HEAD_DIM:SSD_WIDTH + (h + 1) * ML_HEAD_DIM] = yh.astype(bf16)
        yield

    rs, is_ = [], []
    for kb_ in range(LRU_BLOCKS):
        xk = xr[:, kb_ * LRU_BLOCK_DIM:(kb_ + 1) * LRU_BLOCK_DIM].astype(bf16)
        ri = jnp.dot(xk, wax_ref[kb_], preferred_element_type=f32)
        rs.append(ri[:, 0:LRU_BLOCK_DIM])
        is_.append(ri[:, LRU_BLOCK_DIM:2 * LRU_BLOCK_DIM])
    r = _sigmoid(jnp.concatenate(rs, axis=1) + ba_ref[...])
    i_g = _sigmoid(jnp.concatenate(is_, axis=1) + bx_ref[...])
    yield
    log_a = -LRU_C * r * _softplus(-lam_ref[...])
    mult = jnp.sqrt(_neg_expm1(2.0 * log_a))
    rowl = lax.broadcasted_iota(jnp.int32, (L, LRU_WIDTH), 0)
    first_row = jnp.where(ci == 0, 0, -1)
    mult = jnp.where(rowl == first_row, 1.0, mult)
    a_s = jnp.exp(log_a)
    u_s = mult * i_g * xr
    yield
    tile = lambda a, v: a[v * SUB:(v + 1) * SUB, :]
    h_loc = [tile(u_s, 0)]
    a_cum = [tile(a_s, 0)]
    for v in range(1, ROW_TILES):
        h_loc.append(tile(a_s, v) * h_loc[-1] + tile(u_s, v))
        a_cum.append(tile(a_s, v) * a_cum[-1])
    a_e, h_e = a_cum[-1], h_loc[-1]
    subl = lax.broadcasted_iota(jnp.int32, (SUB, LRU_WIDTH), 0)
    k_ = 1
    while k_ < SUB:
        keep = subl >= k_
        a_sh = jnp.where(keep, pltpu.roll(a_e, k_, 0), 1.0)
        h_sh = jnp.where(keep, pltpu.roll(h_e, k_, 0), 0.0)
        h_e = a_e * h_sh + h_e
        a_e = a_e * a_sh
        k_ *= 2
    h0 = hst[...]
    run_end = h_e + a_e * h0
    run_in = jnp.where(subl == 0, h0, pltpu.roll(run_end, 1, 0))
    hr = jnp.concatenate([h_loc[v] + a_cum[v] * run_in for v in range(ROW_TILES)], axis=0)
    hst[...] = run_end[SUB - 1:SUB, :]
    gr = u_ref[:,P_GR:P_GR + LRU_WIDTH]
    y_ref[:,SSD_WIDTH + ML_WIDTH:MIX_WIDTH] = (hr * _gelu_tanh(gr)).astype(bf16)
    yield

    if outs is not None:
        ssm_ref, sconv_ref, mc_ref, mn_ref, mm_ref, lh_ref, lconv_ref = outs
        for g in range(SSD_GROUPS):
            for jj in range(4):
                ssm_ref[0, g * 4 + jj] = st[g, :, jj * L:(jj + 1) * L].T
        for k in range(CONV_W - 1):
            sconv_ref[0, k:k + 1, :] = xbuf[k * SUB + SUB - 1:(k + 1) * SUB, :]
            lconv_ref[0, k:k + 1, :] = lbuf[k * SUB + SUB - 1:(k + 1) * SUB, :]
        for h in range(ML_HEADS):
            mc_ref[0, h] = cst[h].T
            mn_ref[0, h:h + 1, :] = nst[h]
            mm_ref[0, h:h + 1, :] = mst[h]
        lh_ref[0] = hst[...]


N_MIX_PARAMS = 13
N_STATE_OUTS = 7
N_MIX_PHASES = 44
N_FILL_PIECES = (1 + -(-NP_COLS // PIECE)) + (D_MODEL // PIECE + 1
                                              + (D_FF // (2 * PIECE)) * (2 + D_MODEL // PIECE) + 1)


def _player_kernel(*refs, final, n_chunks, nc):
    xa_ref, moda_ref, xc_ref, modc_ref, n1w_ref, win_ref = refs[:6]
    prm = refs[6:6 + N_MIX_PARAMS]
    wout_ref, n2w_ref, up_ref, down_ref, fnw_ref = refs[6 + N_MIX_PARAMS:11 + N_MIX_PARAMS]
    n_in = 11 + N_MIX_PARAMS + N_STATE_OUTS
    o_ref = refs[n_in]
    outs = refs[n_in + 1:n_in + 1 + N_STATE_OUTS]
    u_bufs = refs[-11:-9]
    y_bufs = refs[-9:-7]
    state = refs[-7:]
    k = pl.program_id(0)

    @pl.when(k == 0)
    def _():
        u_bufs[1][...] = jnp.zeros(u_bufs[1].shape, f32)
        y_bufs[1][...] = jnp.zeros(y_bufs[1].shape, bf16)
        for ref in state:
            ref[...] = jnp.zeros(ref.shape, f32)

    for sub in range(2):
        rows = slice(sub * CHUNK, (sub + 1) * CHUNK)
        stage_in = _stage_in(xa_ref[0, rows, :], moda_ref, n1w_ref, win_ref, u_bufs[sub])
        jb = jnp.clip(2 * k + sub - 1, 0, n_chunks - 1)
        ci = jb % nc
        if sub == 1:
            _reset_state_if(ci == 0, state)
        stage_mix = _stage_mix(u_bufs[1 - sub], y_bufs[sub], prm, state, ci,
                               outs if sub == 0 else None)
        stage_out = _stage_out(xc_ref[0, rows, :], y_bufs[1 - sub], o_ref, rows, modc_ref,
                               wout_ref, n2w_ref, up_ref, down_ref, fnw_ref, final)
        _trace_interleaved(stage_mix, [stage_in, stage_out], N_MIX_PHASES, N_FILL_PIECES)


def _player(l, x, mod, mod_row0, w, fnw, final, prev):
    b, t, d = x.shape
    nc = t // CHUNK
    n_chunks = b * nc
    n_pairs = n_chunks // 2
    ppr = nc // 2
    pair_a = lambda k: jnp.minimum(k, n_pairs - 1)
    pair_c = lambda k: jnp.clip(k - 1, 0, n_pairs - 1)
    seq_b = lambda k: jnp.clip(2 * k - 1, 0, n_chunks - 1) // nc
    xspec = lambda pair: pl.BlockSpec((1, 2 * CHUNK, d), lambda k: (pair(k) // ppr, pair(k) % ppr, 0))
    mspec = lambda pair: pl.BlockSpec((None, 1, 6, d),
                                      lambda k: (l, mod_row0 + pair(k) // ppr, 0, 0))
    st_spec = lambda s: pl.BlockSpec((None, 1) + s[2:],
                                     lambda k: (l, seq_b(k)) + (0,) * (len(s) - 2))
    outs = [jax.ShapeDtypeStruct((b, t, d), f32)]
    outs += [jax.ShapeDtypeStruct(a.shape, f32) for a in prev]
    mix_params = [w["ssd_conv_w"], w["ssd_conv_b"], w["p_small_bias"], w["p_alog"], w["dskip"],
                  w["ssd_norm_w"], w["ml_norm_w"], w["lru_conv_w"], w["lru_conv_b"], w["lru_wax"],
                  w["lru_ba"], w["lru_bx"], w["lru_lambda"]]
    assert len(mix_params) == N_MIX_PARAMS and len(prev) == N_STATE_OUTS
    n_in = 11 + N_MIX_PARAMS
    return pl.pallas_call(
        functools.partial(_player_kernel, final=final, n_chunks=n_chunks, nc=nc),
        out_shape=outs,
        grid=(n_pairs + 1,),
        in_specs=[xspec(pair_a), mspec(pair_a), xspec(pair_c), mspec(pair_c),
                  _layer(w["norm1_w"], l), _layer(w["w_in_p"], l, resident=True)]
        + [_layer(a, l) for a in mix_params]
        + [_layer(w["w_out"], l, resident=True), _layer(w["norm2_w"], l),
           _layer(w["mlp_up"], l, resident=True), _layer(w["mlp_down"], l, resident=True),
           _full((1, d))]
        + [pl.BlockSpec(memory_space=pl.ANY)] * len(prev),
        out_specs=[xspec(pair_c)] + [st_spec(a.shape) for a in prev],
        input_output_aliases={n_in + i: 1 + i for i in range(len(prev))},
        scratch_shapes=[
            pltpu.VMEM((CHUNK, NP_COLS), f32), pltpu.VMEM((CHUNK, NP_COLS), f32),
            pltpu.VMEM((CHUNK, MIX_WIDTH), bf16), pltpu.VMEM((CHUNK, MIX_WIDTH), bf16),
            pltpu.VMEM(((CONV_W - 1) * V7X_SUBLANES, SSD_CONV_DIM), f32),
            pltpu.VMEM(((CONV_W - 1) * V7X_SUBLANES, LRU_WIDTH), f32),
            pltpu.VMEM((SSD_GROUPS, SSD_STATE, 512), f32),
            pltpu.VMEM((ML_HEADS, ML_HEAD_DIM, ML_HEAD_DIM), f32),
            pltpu.VMEM((ML_HEADS, 1, ML_HEAD_DIM), f32),
            pltpu.VMEM((ML_HEADS, 1, V7X_LANES), f32),
            pltpu.VMEM((1, LRU_WIDTH), f32),
        ],
        compiler_params=pltpu.CompilerParams(
            dimension_semantics=("arbitrary",), vmem_limit_bytes=PLAYER_VMEM_LIMIT),
        name="prompt_layer",
    )(x, mod, x, mod, w["norm1_w"], w["w_in_p"], *mix_params,
      w["w_out"], w["norm2_w"], w["mlp_up"], w["mlp_down"], fnw, *prev)


def _smix_kernel(u_ref, s0_ref, sconv0_ref, c0_ref, n0_ref, m0_ref, lh0_ref, lconv0_ref,
                 cw_ref, cb_ref, dtb_ref, alog_ref, dskip_ref, snw_ref, ib_ref, fb_ref, mnw_ref,
                 lcw_ref, lcb_ref, wax_ref, ba_ref, bx_ref, lam_ref,
                 y_ref, s_ref, sconv_ref, c_ref, n_ref, m_ref, lh_ref, lconv_ref):
    T = DEC_SEQ
    bb = u_ref.shape[1]
    H = ML_HEAD_DIM
    toks = range(T)

    def useg(t, c0, n):
        return u_ref[t, :, c0:c0 + n]

    def to_seq(rows):
        return jnp.swapaxes(jnp.stack(rows, axis=0), 0, 1)

    def to_tok(a):
        a = jnp.swapaxes(a, 0, 1)
        return [a[t] for t in toks]

    def conv(c0, n, hist_ref, w_ref, b_ref, new_ref):
        xpad = [hist_ref[k] for k in range(CONV_W - 1)] + [useg(t, c0, n) for t in toks]
        for k in range(CONV_W - 1):
            new_ref[k] = xpad[T + k]
        out = []
        for t in toks:
            acc = b_ref[...] + xpad[t] * w_ref[0:1, :]
            for j in range(1, CONV_W):
                acc = acc + xpad[t + j] * w_ref[j:j + 1, :]
            out.append(acc)
        return out

    def split3(d):
        hi = d.astype(bf16).astype(f32)
        r1 = d - hi
        mid = r1.astype(bf16).astype(f32)
        return [hi, mid, (r1 - mid).astype(bf16).astype(f32)]

    def update_lhs(rows, decay):
        pad = [jnp.zeros_like(decay)] * (T - 3)
        return to_seq(rows + split3(decay) + pad).astype(bf16)

    def ones_rhs(kmat):
        zeros = jnp.zeros_like(kmat)
        return jnp.concatenate(
            [jnp.concatenate([kmat, zeros], axis=2),
             jnp.concatenate([zeros, jnp.ones_like(kmat)], axis=2)], axis=1).astype(bf16)

    xbc = [_silu(a) for a in conv(P_XBC, SSD_CONV_DIM, sconv0_ref, cw_ref, cb_ref, sconv_ref)]
    xs = [a[:, 0:SSD_WIDTH] for a in xbc]
    seg = lambda a, j: a[:, SSD_WIDTH + j * SSD_STATE:SSD_WIDTH + (j + 1) * SSD_STATE]
    bm = [[seg(a, g) for a in xbc] for g in range(SSD_GROUPS)]
    cm = [[seg(a, SSD_GROUPS + g) for a in xbc] for g in range(SSD_GROUPS)]
    a_row = -jnp.exp(alog_ref[...]) * LOG2E
    lane = lax.broadcasted_iota(jnp.int32, (bb, SSD_WIDTH), 1)
    cum, xdt = [], []
    for t in toks:
        dt = _softplus(useg(t, S_DT, SSD_WIDTH) + dtb_ref[...])
        cum.append(dt * a_row + (cum[-1] if cum else 0.0))
        xdt.append(xs[t] * dt)
    y = [xs[t] * dskip_ref[...] for t in toks]
    for t in toks:
        for s in range(t + 1):
            cb0 = jnp.sum(cm[0][t] * bm[0][s], axis=-1, keepdims=True)
            cb1 = jnp.sum(cm[1][t] * bm[1][s], axis=-1, keepdims=True)
            term = jnp.where(lane < 512, cb0, cb1) * xdt[s]
            y[t] = y[t] + (term if s == t else term * jnp.exp2(cum[t] - cum[s]))
    inter = []
    for g in range(SSD_GROUPS):
        inter.append(to_tok(jnp.einsum(
            'btn,bpn->btp', to_seq(cm[g]).astype(bf16),
            s0_ref[:, g * 512:(g + 1) * 512, :].astype(bf16), preferred_element_type=f32)))
    for t in toks:
        yt = y[t] + jnp.concatenate([inter[0][t], inter[1][t]], axis=-1) * jnp.exp2(cum[t])
        y_ref[t, :, 0:SSD_WIDTH] = _rms(yt * _silu(useg(t, P_Z, SSD_WIDTH)), snw_ref[...])
    xend = [xdt[s] if s == T - 1 else xdt[s] * jnp.exp2(cum[T - 1] - cum[s]) for s in toks]
    lhs = update_lhs(xend, jnp.exp2(cum[T - 1]))
    for g in range(SSD_GROUPS):
        zz = jnp.einsum('bkp,bkn->bpn', lhs[:, :, g * 512:(g + 1) * 512],
                        ones_rhs(to_seq(bm[g])), preferred_element_type=f32)
        s_ref[:, g * 512:(g + 1) * 512, :] = (
            zz[:, :, SSD_STATE:] * s0_ref[:, g * 512:(g + 1) * 512, :] + zz[:, :, :SSD_STATE])

    def headsum(a):
        return jnp.concatenate(
            [jnp.broadcast_to(jnp.sum(a[:, h * H:(h + 1) * H], axis=-1, keepdims=True), (bb, H))
             for h in range(ML_HEADS)], axis=-1)

    q = [useg(t, P_Q, ML_WIDTH) for t in toks]
    k = [useg(t, P_K, ML_WIDTH) * (H ** -0.5) for t in toks]
    v = [useg(t, P_V, ML_WIDTH) for t in toks]
    ic = [useg(t, S_I, ML_WIDTH) + ib_ref[...] for t in toks]
    bc = []
    for t in toks:
        fc = _log_sigmoid(useg(t, S_F, ML_WIDTH) + fb_ref[...])
        bc.append(fc + (bc[-1] if bc else 0.0))
    m0 = m0_ref[...]
    n0 = n0_ref[...]
    q_seq = to_seq(q).astype(bf16)
    qc = to_tok(jnp.concatenate(
        [jnp.einsum('btk,bvk->btv', q_seq[:, :, h * H:(h + 1) * H],
                    c0_ref[:, h * H:(h + 1) * H, :].astype(bf16), preferred_element_type=f32)
         for h in range(ML_HEADS)], axis=-1))
    mts = []
    for t in toks:
        dm = [bc[t] - bc[s] + ic[s] for s in range(t + 1)]
        g_t = bc[t] + m0
        mt = g_t
        for d in dm:
            mt = jnp.maximum(mt, d)
        mts.append(mt)
        inter_m = jnp.exp(g_t - mt)
        num = inter_m * qc[t]
        den = inter_m * headsum(q[t] * n0)
        for s in range(t + 1):
            w = jnp.exp(dm[s] - mt) * headsum(q[t] * k[s])
            num = num + w * v[s]
            den = den + w
        hout = num / jnp.maximum(jnp.abs(den), jnp.exp(-mt))
        y_ml = (hout * lax.rsqrt(headsum(hout * hout) * (1.0 / H) + EPS) * mnw_ref[...]
                * _sigmoid(useg(t, P_O, ML_WIDTH)))
        y_ref[t, :, SSD_WIDTH:SSD_WIDTH + ML_WIDTH] = y_ml
    m_new = mts[T - 1]
    m_ref[...] = m_new
    w_end = [jnp.exp(bc[T - 1] - bc[s] + ic[s] - m_new) for s in toks]
    dc = jnp.exp(bc[T - 1] + m0 - m_new)
    n_new = dc * n0
    for s in toks:
        n_new = n_new + w_end[s] * k[s]
    n_ref[...] = n_new
    lhs_m = update_lhs([v[s] * w_end[s] for s in toks], dc)
    k_seq = to_seq(k)
    for h in range(ML_HEADS):
        sl = slice(h * H, (h + 1) * H)
        zz = jnp.einsum('bkv,bkn->bvn', lhs_m[:, :, sl], ones_rhs(k_seq[:, :, sl]),
                        preferred_element_type=f32)
        c_ref[:, sl, :] = zz[:, :, H:] * c0_ref[:, sl, :] + zz[:, :, :H]

    xr = jnp.concatenate(conv(P_XR, LRU_WIDTH, lconv0_ref, lcw_ref, lcb_ref, lconv_ref), axis=0)
    rs, is_ = [], []
    for kk in range(LRU_BLOCKS):
        xk = xr[:, kk * LRU_BLOCK_DIM:(kk + 1) * LRU_BLOCK_DIM].astype(bf16)
        ri = jnp.dot(xk, wax_ref[kk], preferred_element_type=f32)
        rs.append(ri[:, 0:LRU_BLOCK_DIM])
        is_.append(ri[:, LRU_BLOCK_DIM:2 * LRU_BLOCK_DIM])
    r = _sigmoid(jnp.concatenate(rs, axis=1) + ba_ref[...])
    i_g = _sigmoid(jnp.concatenate(is_, axis=1) + bx_ref[...])
    log_a = -LRU_C * r * _softplus(-lam_ref[...])
    a_s = jnp.exp(log_a)
    u_s = jnp.sqrt(_neg_expm1(2.0 * log_a)) * i_g * xr
    cur = lh0_ref[...]
    for t in toks:
        rows = slice(t * bb, (t + 1) * bb)
        cur = a_s[rows] * cur + u_s[rows]
        y_ref[t, :, SSD_WIDTH + ML_WIDTH:MIX_WIDTH] = cur * _gelu_tanh(useg(t, P_GR, LRU_WIDTH))
    lh_ref[...] = cur


def _smix(l, u, states, prev, w, bb):
    t, b, _ = u.shape

    def sblk(a, ax):
        shape = tuple(None if d == 0 else bb if d == ax else n for d, n in enumerate(a.shape))
        return pl.BlockSpec(shape, lambda i: tuple(l if d == 0 else i if d == ax else 0
                                                   for d in range(a.ndim)))

    ublk = lambda n: pl.BlockSpec((t, bb, n), lambda i: (0, i, 0))
    outs = [jax.ShapeDtypeStruct((t, b, MIX_WIDTH), f32)]
    outs += [jax.ShapeDtypeStruct(a.shape, f32) for a, _ in states]
    weights = [w["ssd_conv_w"], w["ssd_conv_b"], w["s_dt_bias"], w["s_alog"], w["dskip"],
               w["ssd_norm_w"], w["s_i_bias"], w["s_f_bias"], w["ml_norm_w"],
               w["lru_conv_w"], w["lru_conv_b"], w["lru_wax"], w["lru_ba"], w["lru_bx"],
               w["lru_lambda"]]
    n_in = 1 + len(states) + len(weights)
    prev = list(prev)

    def body(*refs):
        _smix_kernel(*refs[:n_in], *refs[n_in + len(prev):])

    return pl.pallas_call(
        body,
        out_shape=outs,
        grid=(b // bb,),
        in_specs=[ublk(u.shape[-1])] + [sblk(a, ax) for a, ax in states]
        + [_layer(a, l) for a in weights]
        + [pl.BlockSpec(memory_space=pl.ANY)] * len(prev),
        out_specs=[ublk(MIX_WIDTH)] + [sblk(a, ax) for a, ax in states],
        input_output_aliases={n_in + k: 1 + k for k in range(len(prev))},
        compiler_params=pltpu.CompilerParams(
            dimension_semantics=("arbitrary",), vmem_limit_bytes=VMEM_LIMIT),
        name="sample_mix",
    )(u, *[a for a, _ in states], *weights, *prev)


def _prep_params(p):
    w_p, w_exp = _pack_w_in(p["w_in"])
    r = lambda a: a.reshape(N_LAYERS, 1, -1)
    rep = lambda a, n: r(jnp.repeat(a, n, axis=-1))
    small = jnp.concatenate([p["ssd_dt_bias"], p["ml_i_bias"], p["ml_f_bias"]], axis=-1)
    return dict(
        w_in_p=w_p, w_exp=w_exp,
        norm1_w=r(p["norm1_w"]), norm2_w=r(p["norm2_w"]),
        ssd_conv_w=p["ssd_conv_w"], ssd_conv_b=r(p["ssd_conv_b"]),
        p_small_bias=r(jnp.pad(small, ((0, 0), (0, V7X_LANES - small.shape[-1])))),
        p_alog=r(jnp.pad(p["ssd_a_log"], ((0, 0), (0, V7X_LANES - SSD_HEADS)))),
        s_dt_bias=rep(p["ssd_dt_bias"], SSD_HEAD_DIM), s_alog=rep(p["ssd_a_log"], SSD_HEAD_DIM),
        s_i_bias=rep(p["ml_i_bias"], ML_HEAD_DIM), s_f_bias=rep(p["ml_f_bias"], ML_HEAD_DIM),
        dskip=rep(p["ssd_d"], SSD_HEAD_DIM),
        ssd_norm_w=r(p["ssd_norm_w"]), ml_norm_w=r(p["ml_norm_w"]),
        lru_conv_w=p["lru_conv_w"], lru_conv_b=r(p["lru_conv_b"]),
        lru_wax=jnp.concatenate([p["lru_wa"], p["lru_wx"]], axis=-1).astype(bf16),
        lru_ba=r(p["lru_ba"]), lru_bx=r(p["lru_bx"]), lru_lambda=r(p["lru_lambda"]),
        w_out=_cast_pad(p["w_out"]), mlp_up=_cast_pad(p["mlp_up"]),
        mlp_down=_cast_pad(p["mlp_down"]),
    )


def kernel(x_prompt, x_sample, c_prompt, c_sample, state_ssm, state_ssd_conv, state_mlstm_c, state_mlstm_n, state_mlstm_m, state_lru_h, state_lru_conv, ada_w, ada_b, norm1_w, norm2_w, w_in, ssd_conv_w, ssd_conv_b, ssd_dt_bias, ssd_a_log, ssd_d, ssd_norm_w, ml_i_bias, ml_f_bias, ml_norm_w, lru_conv_w, lru_conv_b, lru_wa, lru_ba, lru_wx, lru_bx, lru_lambda, w_out, mlp_up, mlp_down, final_norm_w):
    p = dict(norm1_w=norm1_w, norm2_w=norm2_w, w_in=w_in, ssd_conv_w=ssd_conv_w,
             ssd_conv_b=ssd_conv_b, ssd_dt_bias=ssd_dt_bias, ssd_a_log=ssd_a_log, ssd_d=ssd_d,
             ssd_norm_w=ssd_norm_w, ml_i_bias=ml_i_bias, ml_f_bias=ml_f_bias,
             ml_norm_w=ml_norm_w, lru_conv_w=lru_conv_w, lru_conv_b=lru_conv_b, lru_wa=lru_wa,
             lru_ba=lru_ba, lru_wx=lru_wx, lru_bx=lru_bx, lru_lambda=lru_lambda, w_out=w_out,
             mlp_up=mlp_up, mlp_down=mlp_down)
    w = _prep_params(p)
    fnw = final_norm_w.reshape(1, D_MODEL)
    bp = x_prompt.shape[0]
    bs_ = x_sample.shape[0]

    mod = _ada(jnp.concatenate([c_sample, c_prompt], axis=0), ada_w, ada_b)
    mod = mod.reshape(N_LAYERS, bs_ + bp, 6, D_MODEL)

    xp = _interleave_rows(x_prompt)
    p_out = [jnp.zeros((N_LAYERS, bp) + s, f32) for s in (
        (SSD_HEADS // 2, 2 * SSD_HEAD_DIM, SSD_STATE),
        (CONV_W - 1, SSD_CONV_DIM), (ML_HEADS, ML_HEAD_DIM, ML_HEAD_DIM),
        (ML_HEADS, ML_HEAD_DIM), (ML_HEADS, V7X_LANES),
        (1, LRU_WIDTH), (CONV_W - 1, LRU_WIDTH))]
    for l in range(N_LAYERS):
        xp, *p_out = _player(l, xp, mod, bs_, w, fnw, l == N_LAYERS - 1, p_out)
    xp = _deinterleave_rows(xp)
    ssm, sconv, mc, mn, mm, lh, lconv = p_out
    p_states = (ssm.reshape(N_LAYERS, bp, SSD_HEADS, SSD_HEAD_DIM, SSD_STATE), sconv, mc, mn,
                mm[..., 0], lh.reshape(N_LAYERS, bp, LRU_WIDTH), lconv)

    tok_major = lambda a: jnp.swapaxes(a, -3, -2)
    xs = tok_major(x_sample)
    mod_s = tok_major(mod[:, :bs_])
    st_in = (
        (state_ssm.reshape(N_LAYERS, bs_, SSD_HEADS * SSD_HEAD_DIM, SSD_STATE), 1),
        (tok_major(state_ssd_conv), 2),
        (state_mlstm_c.reshape(N_LAYERS, bs_, ML_HEADS * ML_HEAD_DIM, ML_HEAD_DIM), 1),
        (state_mlstm_n.reshape(N_LAYERS, bs_, ML_WIDTH), 1),
        (jnp.repeat(state_mlstm_m, ML_HEAD_DIM, axis=-1), 1),
        (state_lru_h, 1),
        (tok_major(state_lru_conv), 2),
    )
    st_out = [jnp.zeros(a.shape, f32) for a, _ in st_in]
    for l in range(N_LAYERS):
        u = _inproj(l, xs, mod_s, w["norm1_w"], w["w_in_p"], w["w_exp"], 32)
        ycat, *st_out = _smix(l, u, st_in, st_out, w, 8)
        xs = _outmlp(l, xs, ycat, mod_s, w["w_out"], w["norm2_w"], w["mlp_up"], w["mlp_down"],
                     fnw, 64, l == N_LAYERS - 1)
    xs = tok_major(xs)
    ssm, sconv, mc, mn, mm, lh, lconv = st_out
    s_states = (ssm.reshape(N_LAYERS, bs_, SSD_HEADS, SSD_HEAD_DIM, SSD_STATE), tok_major(sconv),
                mc.reshape(N_LAYERS, bs_, ML_HEADS, ML_HEAD_DIM, ML_HEAD_DIM),
                mn.reshape(N_LAYERS, bs_, ML_HEADS, ML_HEAD_DIM),
                mm.reshape(N_LAYERS, bs_, ML_HEADS, ML_HEAD_DIM)[..., 0],
                lh, tok_major(lconv))

    return (xp, xs) + p_states + s_states
```

```python
import functools
import math

import jax
import jax.numpy as jnp
from jax import lax
from jax.experimental import pallas as pl
from jax.experimental.pallas import tpu as pltpu

f32 = jnp.float32
bf16 = jnp.bfloat16

D_MODEL = 1024
N_LAYERS = 2
MIX_WIDTH = 2 * D_MODEL
SSD_WIDTH = 1024
SSD_HEAD_DIM = 64
SSD_HEADS = 16
SSD_GROUPS = 2
SSD_STATE = 128
SSD_CONV_DIM = SSD_WIDTH + 2 * SSD_GROUPS * SSD_STATE
ML_WIDTH = 512
ML_HEADS = 4
ML_HEAD_DIM = 128
LRU_WIDTH = 512
LRU_BLOCKS = 4
LRU_BLOCK_DIM = 128
LRU_C = 8.0
CONV_W = 4
CHUNK = 128
D_FF = 4 * D_MODEL
EPS = 1e-6
PAST_LEN = 16384
DEC_SEQ = 8

_O_Z, _O_XBC, _O_DT, _O_Q, _O_O_END, _O_I, _O_F, _O_XR, _O_END = (
    0, 1024, 2560, 2576, 4624, 4624, 4628, 4632, 5656)

P_Z = 0
P_XBC = 1024
P_Q = 2560
P_K = 3072
P_V = 3584
P_O = 4096
P_XR = 4608
P_GR = 5120
P_MAIN = 5632
P_SMALL = 5632
NP_COLS = 5760
S_DT = 5632
S_I = 6656
S_F = 7168
NS_COLS = 7680

V7X_LANES = 128
V7X_SUBLANES = 8
VMEM_LIMIT = 56 * 1024 * 1024
PLAYER_VMEM_LIMIT = 60 * 1024 * 1024
ROW_TILES = CHUNK // V7X_SUBLANES
LOG2E = 1.4426950408889634

_NT = (((1,), (1,)), ((), ()))
_TN = (((0,), (0,)), ((), ()))


def _softplus(x):
    return jnp.maximum(x, 0.0) + jnp.log1p(jnp.exp(-jnp.abs(x)))


def _log_sigmoid(x):
    return -_softplus(-x)


def _sigmoid(x):
    return 1.0 / (1.0 + jnp.exp(-x))


def _silu(x):
    return x * _sigmoid(x)


def _gelu_tanh(x):
    c = math.sqrt(2.0 / math.pi)
    return 0.5 * x * (1.0 + jnp.tanh(c * (x + 0.044715 * (x * x * x))))


def _neg_expm1(x):
    return -jnp.tanh(0.5 * x) * (jnp.exp(x) + 1.0)


def _rms(x, w):
    return x * lax.rsqrt(jnp.mean(x * x, axis=-1, keepdims=True) + EPS) * w


def _interleave_rows(x):
    b, t, d = x.shape
    return x.reshape(b, t // CHUNK, V7X_SUBLANES, ROW_TILES, d).swapaxes(2, 3).reshape(b, t, d)


def _deinterleave_rows(x):
    b, t, d = x.shape
    return x.reshape(b, t // CHUNK, ROW_TILES, V7X_SUBLANES, d).swapaxes(2, 3).reshape(b, t, d)


def _full(shape):
    n = len(shape)
    return pl.BlockSpec(shape, lambda *_: (0,) * n)


def _layer(a, l, resident=False):
    nd = a.ndim - 1
    kw = dict(pipeline_mode=pl.Buffered(1)) if resident else {}
    return pl.BlockSpec((None,) + a.shape[1:], lambda *_: (l,) + (0,) * nd, **kw)


def _ada_kernel(c_ref, w_ref, b_ref, o_ref):
    cs = _silu(c_ref[...]).astype(bf16)
    o_ref[0] = jnp.dot(cs, w_ref[0].astype(bf16), preferred_element_type=f32) + b_ref[0]


def _ada(c_all, ada_w, ada_b):
    n = c_all.shape[0]
    tn = 1024
    return pl.pallas_call(
        _ada_kernel,
        out_shape=jax.ShapeDtypeStruct((N_LAYERS, n, 6 * D_MODEL), f32),
        grid=(N_LAYERS, 6 * D_MODEL // tn),
        in_specs=[
            pl.BlockSpec((n, D_MODEL), lambda l, j: (0, 0)),
            pl.BlockSpec((1, D_MODEL, tn), lambda l, j: (l, 0, j)),
            pl.BlockSpec((1, 1, tn), lambda l, j: (l, 0, j)),
        ],
        out_specs=pl.BlockSpec((1, n, tn), lambda l, j: (l, 0, j)),
        compiler_params=pltpu.CompilerParams(
            dimension_semantics=("arbitrary", "arbitrary"), vmem_limit_bytes=VMEM_LIMIT),
        name="ada_mod",
    )(c_all, ada_w, ada_b.reshape(N_LAYERS, 1, 6 * D_MODEL))


def _inproj_kernel(x_ref, mod_ref, nw_ref, wm_ref, we_ref, o_ref):
    x = x_ref[...]
    t, bb, d = x.shape
    hn = _rms(x, nw_ref[...]) * (1.0 + mod_ref[1]) + mod_ref[0]
    hn = hn.reshape(t * bb, d).astype(bf16)
    o_ref[:, :, 0:P_MAIN] = jnp.dot(hn, wm_ref[...], preferred_element_type=f32).reshape(
        t, bb, P_MAIN)
    o_ref[:, :, P_MAIN:NS_COLS] = jnp.dot(hn, we_ref[...], preferred_element_type=f32).reshape(
        t, bb, NS_COLS - P_MAIN)


def _inproj(l, x, mod, nw, w_packed, w_exp, bb):
    t, b, d = x.shape
    return pl.pallas_call(
        _inproj_kernel,
        out_shape=jax.ShapeDtypeStruct((t, b, NS_COLS), f32),
        grid=(b // bb,),
        in_specs=[
            pl.BlockSpec((t, bb, d), lambda i: (0, i, 0)),
            pl.BlockSpec((None, 6, bb, d), lambda i: (l, 0, i, 0)),
            _layer(nw, l),
            pl.BlockSpec((None, d, P_MAIN), lambda i: (l, 0, 0), pipeline_mode=pl.Buffered(1)),
            _layer(w_exp, l, resident=True),
        ],
        out_specs=pl.BlockSpec((t, bb, NS_COLS), lambda i: (0, i, 0)),
        compiler_params=pltpu.CompilerParams(
            dimension_semantics=("arbitrary",), vmem_limit_bytes=VMEM_LIMIT),
        name="inproj",
    )(x, mod, nw, w_packed, w_exp)


PACK_ROWS = 128


def _pack_w_in_kernel(w_ref, wp_ref, we_ref):
    c = lambda a: a.astype(bf16)
    wp_ref[:, 0:P_Q] = c(w_ref[:, 0:_O_DT])
    wp_ref[:, P_Q:P_XR] = c(w_ref[:, _O_Q:_O_O_END])
    wp_ref[:, P_XR:P_MAIN] = c(w_ref[:, _O_XR:_O_END])
    dt = w_ref[:, _O_DT:_O_Q]
    gates = w_ref[:, _O_I:_O_XR]
    n_pad = V7X_LANES - SSD_HEADS - 2 * ML_HEADS
    wp_ref[:, P_SMALL:NP_COLS] = c(jnp.concatenate(
        [dt, gates, jnp.zeros((PACK_ROWS, n_pad), f32)], axis=1))
    rep = lambda a, j, n: jnp.broadcast_to(a[:, j:j + 1], (PACK_ROWS, n))
    we_ref[...] = c(jnp.concatenate(
        [rep(dt, h, SSD_HEAD_DIM) for h in range(SSD_HEADS)]
        + [rep(gates, h, ML_HEAD_DIM) for h in range(2 * ML_HEADS)], axis=1))


def _pack_w_in(w_in):
    n_l, d, n = w_in.shape
    blk = lambda m: pl.BlockSpec((None, PACK_ROWS, m), lambda l, i: (l, i, 0))
    return pl.pallas_call(
        _pack_w_in_kernel,
        out_shape=[jax.ShapeDtypeStruct((n_l, d, NP_COLS), bf16),
                   jax.ShapeDtypeStruct((n_l, d, NS_COLS - P_MAIN), bf16)],
        grid=(n_l, d // PACK_ROWS),
        in_specs=[blk(n)],
        out_specs=[blk(NP_COLS), blk(NS_COLS - P_MAIN)],
        compiler_params=pltpu.CompilerParams(
            dimension_semantics=("arbitrary", "arbitrary"), vmem_limit_bytes=VMEM_LIMIT),
        name="pack_w_in",
    )(w_in)


def _cast_pad_kernel(w_ref, o_ref):
    n = w_ref.shape[-1]
    o_ref[:, 0:n] = w_ref[...].astype(bf16)
    o_ref[:, n:] = jnp.zeros((w_ref.shape[0], o_ref.shape[-1] - n), bf16)


def _cast_pad(a):
    n_l, k, n = a.shape
    rows = 512
    return pl.pallas_call(
        _cast_pad_kernel,
        out_shape=jax.ShapeDtypeStruct((n_l, k, n + V7X_LANES), bf16),
        grid=(n_l, k // rows),
        in_specs=[pl.BlockSpec((None, rows, n), lambda l, i: (l, i, 0))],
        out_specs=pl.BlockSpec((None, rows, n + V7X_LANES), lambda l, i: (l, i, 0)),
        compiler_params=pltpu.CompilerParams(
            dimension_semantics=("arbitrary", "arbitrary"), vmem_limit_bytes=VMEM_LIMIT),
        name="cast_pad",
    )(a)


def _outmlp_kernel(x_ref, y_ref, mod_ref, wout_ref, n2w_ref, up_ref, down_ref, fnw_ref, o_ref,
                   *, final):
    x = x_ref[...]
    t, bb, d = x.shape
    m = t * bb
    ycat = y_ref[...].reshape(m, MIX_WIDTH).astype(bf16)
    mix = jnp.dot(ycat, wout_ref[:, 0:d], preferred_element_type=f32).reshape(t, bb, d)
    x1 = x + mod_ref[2] * mix
    hn = _rms(x1, n2w_ref[...]) * (1.0 + mod_ref[4]) + mod_ref[3]
    hn = hn.reshape(m, d).astype(bf16)
    ff = jnp.zeros((m, d), f32)
    fc = 1024
    for c in range(D_FF // fc):
        h = jnp.dot(hn, up_ref[:, c * fc:(c + 1) * fc], preferred_element_type=f32)
        h = jnp.square(jnp.maximum(h, 0.0)).astype(bf16)
        ff = ff + jnp.dot(h, down_ref[c * fc:(c + 1) * fc, 0:D_MODEL],
                          preferred_element_type=f32)
    x2 = x1 + mod_ref[5] * ff.reshape(t, bb, d)
    if final:
        x2 = _rms(x2, fnw_ref[...])
    o_ref[...] = x2


def _outmlp(l, x, ycat, mod, wout, n2w, up, down, fnw, bb, final):
    t, b, d = x.shape
    return pl.pallas_call(
        functools.partial(_outmlp_kernel, final=final),
        out_shape=jax.ShapeDtypeStruct((t, b, d), f32),
        grid=(b // bb,),
        in_specs=[
            pl.BlockSpec((t, bb, d), lambda i: (0, i, 0)),
            pl.BlockSpec((t, bb, MIX_WIDTH), lambda i: (0, i, 0)),
            pl.BlockSpec((None, 6, bb, d), lambda i: (l, 0, i, 0)),
            _layer(wout, l, resident=True),
            _layer(n2w, l),
            _layer(up, l, resident=True),
            _layer(down, l, resident=True),
            _full((1, d)),
        ],
        out_specs=pl.BlockSpec((t, bb, d), lambda i: (0, i, 0)),
        compiler_params=pltpu.CompilerParams(
            dimension_semantics=("arbitrary",), vmem_limit_bytes=VMEM_LIMIT),
        name="outproj_mlp",
    )(x, ycat, mod, wout, n2w, up, down, fnw)


PIECE = 512


def _stage_in(x, mod_ref, nw_ref, w_ref, u_ref):
    hn = _rms(x, nw_ref[...]) * (1.0 + mod_ref[0, 1:2, :]) + mod_ref[0, 0:1, :]
    hn = hn.astype(bf16)
    yield
    for c0 in range(0, NP_COLS, PIECE):
        c1 = min(c0 + PIECE, NP_COLS)
        u_ref[:, c0:c1] = jnp.dot(hn, w_ref[:, c0:c1], preferred_element_type=f32)
        yield


def _stage_out(x, y_ref, o_ref, rows, mod_ref, wout_ref, n2w_ref, up_ref, down_ref, fnw_ref,
               final):
    halves = [slice(h * PIECE, (h + 1) * PIECE) for h in range(D_MODEL // PIECE)]
    ycat = y_ref[...]
    mix = []
    for hs in halves:
        mix.append(jnp.dot(ycat, wout_ref[:, hs], preferred_element_type=f32))
        yield
    x1 = x + mod_ref[0, 2:3, :] * jnp.concatenate(mix, axis=1)
    hn = (_rms(x1, n2w_ref[...]) * (1.0 + mod_ref[0, 4:5, :]) + mod_ref[0, 3:4, :]).astype(bf16)
    yield
    ff = [jnp.zeros((x.shape[0], PIECE), f32) for _ in halves]
    fc = 2 * PIECE
    for c in range(D_FF // fc):
        h = []
        for k in range(fc // PIECE):
            cols = slice(c * fc + k * PIECE, c * fc + (k + 1) * PIECE)
            hk = jnp.dot(hn, up_ref[:, cols], preferred_element_type=f32)
            h.append(jnp.square(jnp.maximum(hk, 0.0)).astype(bf16))
            yield
        h = jnp.concatenate(h, axis=1)
        for i, hs in enumerate(halves):
            ff[i] = ff[i] + jnp.dot(h, down_ref[c * fc:(c + 1) * fc, hs],
                                    preferred_element_type=f32)
            yield
    x2 = x1 + mod_ref[0, 5:6, :] * jnp.concatenate(ff, axis=1)
    o_ref[0, rows, :] = _rms(x2, fnw_ref[...]) if final else x2
    yield


def _trace_interleaved(main, fillers, n_main, n_fill):
    fillers = list(fillers)
    due = 0.0
    nxt = 0
    main_live = True
    while main_live or fillers:
        if main_live:
            main_live = next(main, _DONE) is not _DONE
            due += n_fill / n_main
        else:
            due = float(len(fillers))
        while fillers and due >= 1.0:
            nxt %= len(fillers)
            if next(fillers[nxt], _DONE) is _DONE:
                fillers.pop(nxt)
            else:
                nxt += 1
                due -= 1.0


_DONE = object()


def _reset_state_if(first, state):
    for ref in state:
        ref[...] = jnp.where(first, 0.0, ref[...])


def _stage_mix(u_ref, y_ref, prm, state, ci, outs):
    (cw_ref, cb_ref, sbias_ref, alog_ref, dskip_ref, snw_ref, mnw_ref,
     lcw_ref, lcb_ref, wax_ref, ba_ref, bx_ref, lam_ref) = prm
    xbuf, lbuf, st, cst, nst, mst, hst = state
    L = CHUNK
    SUB = V7X_SUBLANES
    HIST = (CONV_W - 1) * SUB

    row = lax.broadcasted_iota(jnp.int32, (L, L), 0)
    col = lax.broadcasted_iota(jnp.int32, (L, L), 1)
    tok = lambda i: (i & (SUB - 1)) * ROW_TILES + (i >> 3)
    tri = tok(row) >= tok(col)
    lo = col < SSD_HEAD_DIM

    def bcl(a, j):
        return jnp.broadcast_to(a[:, j:j + 1], (L, L))

    def conv(u0, hist_ref, w_ref, b_ref, c0, c1):
        raw = u_ref[:, u0 + c0:u0 + c1]
        sub = lax.broadcasted_iota(jnp.int32, (SUB, c1 - c0), 0)
        prev = [pltpu.roll(jnp.where(sub == SUB - 1, hist_ref[k * SUB:(k + 1) * SUB, c0:c1],
                                     raw[L - HIST + k * SUB:L - HIST + (k + 1) * SUB, :]), 1, 0)
                for k in range(CONV_W - 1)]
        acc = b_ref[:, c0:c1] + raw * w_ref[CONV_W - 1:CONV_W, c0:c1]
        for d in range(1, CONV_W):
            shifted = jnp.concatenate(prev[CONV_W - 1 - d:] + [raw[0:L - d * SUB, :]], axis=0)
            acc = acc + shifted * w_ref[CONV_W - 1 - d:CONV_W - d, c0:c1]
        hist_ref[:, c0:c1] = raw[L - HIST:L, :]
        return acc

    cblk = 512
    xbc = []
    for c0 in range(0, SSD_CONV_DIM, cblk):
        xbc.append(_silu(conv(P_XBC, xbuf, cw_ref, cb_ref, c0, c0 + cblk)))
        yield
    xbc = jnp.concatenate(xbc, axis=1)
    xr = conv(P_XR, lbuf, lcw_ref, lcb_ref, 0, LRU_WIDTH)
    yield

    sm = u_ref[:,P_SMALL:P_SMALL + V7X_LANES] + sbias_ref[...]
    dt = _softplus(sm)
    a_row = -jnp.exp(alog_ref[...])
    gates = jnp.where(col < SSD_HEADS, dt * (a_row * LOG2E),
                      jnp.where((col >= 20) & (col < 24), _log_sigmoid(sm), 0.0))
    g_hi = gates.astype(bf16)
    g_r = gates - g_hi.astype(f32)
    g_mid = g_r.astype(bf16)
    g_lo = (g_r - g_mid.astype(f32)).astype(bf16)
    cum3 = jnp.dot(jnp.where(tri, 1.0, 0.0).astype(bf16),
                   jnp.concatenate([g_hi, g_mid, g_lo], axis=1), preferred_element_type=f32)
    cum = cum3[:, 0:L] + cum3[:, L:2 * L] + cum3[:, 2 * L:3 * L]
    cum_t = cum.T
    sm_t = sm.T
    yield

    ys = []
    for g in range(SSD_GROUPS):
        bm_t = xbc[:, SSD_WIDTH + g * SSD_STATE:SSD_WIDTH + (g + 1) * SSD_STATE].T.astype(bf16)
        cm = xbc[:, SSD_WIDTH + (SSD_GROUPS + g) * SSD_STATE:
                 SSD_WIDTH + (SSD_GROUPS + g + 1) * SSD_STATE].astype(bf16)
        cb = jnp.dot(cm, bm_t, preferred_element_type=f32)
        inter = jnp.dot(cm, st[g].astype(bf16), preferred_element_type=f32)
        yield
        xends, keeps = [], []
        for jj in range(4):
            jp = g * 4 + jj
            e0, e1 = 2 * jp, 2 * jp + 1
            c0 = bcl(cum, e0)
            c1 = bcl(cum, e1)
            cum_p = jnp.where(lo, c0, c1)
            dt_p = jnp.where(lo, bcl(dt, e0), bcl(dt, e1))
            xs_p = xbc[:, jp * L:(jp + 1) * L]
            xdt = xs_p * dt_p
            dec0 = jnp.exp2(jnp.where(tri, c0 - cum_t[e0:e0 + 1, :], -jnp.inf))
            dec1 = jnp.exp2(jnp.where(tri, c1 - cum_t[e1:e1 + 1, :], -jnp.inf))
            yield
            att =jnp.concatenate([(cb * dec0).astype(bf16), (cb * dec1).astype(bf16)], axis=1)
            x2 = jnp.concatenate([jnp.where(lo, xdt, 0.0).astype(bf16),
                                  jnp.where(lo, 0.0, xdt).astype(bf16)], axis=0)
            y_p = (jnp.dot(att, x2, preferred_element_type=f32)
                   + inter[:, jj * L:(jj + 1) * L] * jnp.exp2(cum_p)
                   + xs_p * dskip_ref[:, jp * L:(jp + 1) * L])
            ys.append(y_p)
            c_last = cum_p[L - 1:L, :]
            xends.append((xdt * jnp.exp2(c_last - cum_p)).astype(bf16))
            keeps.append(jnp.exp2(c_last))
            yield
        st[g] = (jnp.concatenate(keeps, axis=1) * st[g]
                 + jnp.dot(bm_t, jnp.concatenate(xends, axis=1), preferred_element_type=f32))
    y = jnp.concatenate(ys, axis=1)
    z = u_ref[:,P_Z:P_Z + SSD_WIDTH]
    y_ref[:,0:SSD_WIDTH] = _rms(y * _silu(z), snw_ref[...]).astype(bf16)
    yield

    scale = ML_HEAD_DIM ** -0.5
    for h in range(ML_HEADS):
        sl = slice(h * ML_HEAD_DIM, (h + 1) * ML_HEAD_DIM)
        q = u_ref[:,P_Q + h * ML_HEAD_DIM:P_Q + (h + 1) * ML_HEAD_DIM]
        k = u_ref[:,P_K + h * ML_HEAD_DIM:P_K + (h + 1) * ML_HEAD_DIM] * scale
        v = u_ref[:,P_V + h * ML_HEAD_DIM:P_V + (h + 1) * ML_HEAD_DIM]
        o = u_ref[:,P_O + h * ML_HEAD_DIM:P_O + (h + 1) * ML_HEAD_DIM]
        k_t = k.T.astype(bf16)
        jf = 20 + h
        ji = 16 + h
        b_col = cum[:, jf:jf + 1]
        i_col = sm[:, ji:ji + 1]
        dm = jnp.where(tri, bcl(cum, jf) - cum_t[jf:jf + 1, :] + sm_t[ji:ji + 1, :], -jnp.inf)
        m_prev = mst[h, :, 0:1]
        g_col = b_col + m_prev
        mt = jnp.maximum(g_col, jnp.max(dm, axis=-1, keepdims=True))
        yield
        w = jnp.exp(dm - mt) * jnp.dot(q.astype(bf16), k_t, preferred_element_type=f32)
        inter = jnp.exp(g_col - mt)
        yield
        ct_old = cst[h]
        n_old = nst[h]
        num = jnp.dot(jnp.concatenate([w.astype(bf16), (inter * q).astype(bf16)], axis=1),
                      jnp.concatenate([v.astype(bf16), ct_old.astype(bf16)], axis=0),
                      preferred_element_type=f32)
        den = (jnp.sum(w, axis=-1, keepdims=True)
               + inter * jnp.sum(q * n_old, axis=-1, keepdims=True))
        hout = num / jnp.maximum(jnp.abs(den), jnp.exp(-mt))
        yield
        m_new = mt[L - 1:L, :]
        b_last = b_col[L - 1:L, :]
        w_end = jnp.exp(b_last - b_col + i_col - m_new)
        dc = jnp.exp(b_last + m_prev - m_new)
        cst[h] = dc * ct_old + jnp.dot(k_t, (v * w_end).astype(bf16),
                                       preferred_element_type=f32)
        nst[h] = dc * n_old + jnp.sum(w_end * k, axis=0, keepdims=True)
        mst[h] = jnp.broadcast_to(m_new, (1, V7X_LANES))
        yh = _rms(hout, mnw_ref[:, sl]) * _sigmoid(o)
        y_ref[:,SSD_WIDTH + h * ML_HEAD_DIM:SSD_WIDTH + (h + 1) * ML_HEAD_DIM] = yh.astype(bf16)
        yield

    rs, is_ = [], []
    for kb_ in range(LRU_BLOCKS):
        xk = xr[:, kb_ * LRU_BLOCK_DIM:(kb_ + 1) * LRU_BLOCK_DIM].astype(bf16)
        ri = jnp.dot(xk, wax_ref[kb_], preferred_element_type=f32)
        rs.append(ri[:, 0:LRU_BLOCK_DIM])
        is_.append(ri[:, LRU_BLOCK_DIM:2 * LRU_BLOCK_DIM])
    r = _sigmoid(jnp.concatenate(rs, axis=1) + ba_ref[...])
    i_g = _sigmoid(jnp.concatenate(is_, axis=1) + bx_ref[...])
    yield
    log_a = -LRU_C * r * _softplus(-lam_ref[...])
    mult = jnp.sqrt(_neg_expm1(2.0 * log_a))
    rowl = lax.broadcasted_iota(jnp.int32, (L, LRU_WIDTH), 0)
    first_row = jnp.where(ci == 0, 0, -1)
    mult = jnp.where(rowl == first_row, 1.0, mult)
    a_s = jnp.exp(log_a)
    u_s = mult * i_g * xr
    yield
    tile = lambda a, v: a[v * SUB:(v + 1) * SUB, :]
    h_loc = [tile(u_s, 0)]
    a_cum = [tile(a_s, 0)]
    for v in range(1, ROW_TILES):
        h_loc.append(tile(a_s, v) * h_loc[-1] + tile(u_s, v))
        a_cum.append(tile(a_s, v) * a_cum[-1])
    a_e, h_e = a_cum[-1], h_loc[-1]
    subl = lax.broadcasted_iota(jnp.int32, (SUB, LRU_WIDTH), 0)
    k_ = 1
    while k_ < SUB:
        keep = subl >= k_
        a_sh = jnp.where(keep, pltpu.roll(a_e, k_, 0), 1.0)
        h_sh = jnp.where(keep, pltpu.roll(h_e, k_, 0), 0.0)
        h_e = a_e * h_sh + h_e
        a_e = a_e * a_sh
        k_ *= 2
    h0 = hst[...]
    run_end = h_e + a_e * h0
    run_in = jnp.where(subl == 0, h0, pltpu.roll(run_end, 1, 0))
    hr = jnp.concatenate([h_loc[v] + a_cum[v] * run_in for v in range(ROW_TILES)], axis=0)
    hst[...] = run_end[SUB - 1:SUB, :]
    gr = u_ref[:,P_GR:P_GR + LRU_WIDTH]
    y_ref[:,SSD_WIDTH + ML_WIDTH:MIX_WIDTH] = (hr * _gelu_tanh(gr)).astype(bf16)
    yield

    if outs is not None:
        ssm_ref, sconv_ref, mc_ref, mn_ref, mm_ref, lh_ref, lconv_ref = outs
        for g in range(SSD_GROUPS):
            for jj in range(4):
                ssm_ref[0, g * 4 + jj] = st[g, :, jj * L:(jj + 1) * L].T
        for k in range(CONV_W - 1):
            sconv_ref[0, k:k + 1, :] = xbuf[k * SUB + SUB - 1:(k + 1) * SUB, :]
            lconv_ref[0, k:k + 1, :] = lbuf[k * SUB + SUB - 1:(k + 1) * SUB, :]
        for h in range(ML_HEADS):
            mc_ref[0, h] = cst[h].T
            mn_ref[0, h:h + 1, :] = nst[h]
            mm_ref[0, h:h + 1, :] = mst[h]
        lh_ref[0] = hst[...]


N_MIX_PARAMS = 13
N_STATE_OUTS = 7
N_MIX_PHASES = 44
N_FILL_PIECES = (1 + -(-NP_COLS // PIECE)) + (D_MODEL // PIECE + 1
                                              + (D_FF // (2 * PIECE)) * (2 + D_MODEL // PIECE) + 1)


def _player_kernel(*refs, final, n_chunks, nc, n_clear):
    xa_ref, moda_ref, xc_ref, modc_ref, n1w_ref, win_ref = refs[:6]
    prm = refs[6:6 + N_MIX_PARAMS]
    wout_ref, n2w_ref, up_ref, down_ref, fnw_ref = refs[6 + N_MIX_PARAMS:11 + N_MIX_PARAMS]
    n_in = 11 + N_MIX_PARAMS + N_STATE_OUTS
    o_ref = refs[n_in]
    outs = refs[n_in + 1:n_in + 1 + N_STATE_OUTS]
    clear = refs[n_in + 1 + N_STATE_OUTS:n_in + 1 + N_STATE_OUTS + n_clear]
    u_bufs = refs[-11:-9]
    y_bufs = refs[-9:-7]
    state = refs[-7:]
    k = pl.program_id(0)
    n_steps = n_chunks // 2

    @pl.when(k == 0)
    def _():
        u_bufs[1][...] = jnp.zeros(u_bufs[1].shape, f32)
        y_bufs[1][...] = jnp.zeros(y_bufs[1].shape, bf16)
        for ref in state:
            ref[...] = jnp.zeros(ref.shape, f32)
        if n_clear:
            refs[-13][...] = jnp.zeros(refs[-13].shape, f32)

    def clear_copies():
        zeros, sems = refs[-13], refs[-12]
        copies = []
        for ci_, buf in enumerate(clear):
            n_b, rows = buf.shape[1], buf.shape[2]
            per_step = buf.shape[0] * n_b // n_steps
            for i in range(per_step):
                j = k * per_step + i
                copies.append(pltpu.make_async_copy(
                    zeros.at[0:rows], buf.at[j // n_b, j % n_b], sems.at[ci_, i]))
        return copies

    if n_clear:
        @pl.when(k < n_steps)
        def _():
            for cp in clear_copies():
                cp.start()

    for sub in range(2):
        rows = slice(sub * CHUNK, (sub + 1) * CHUNK)
        stage_in = _stage_in(xa_ref[0, rows, :], moda_ref, n1w_ref, win_ref, u_bufs[sub])
        jb = jnp.clip(2 * k + sub - 1, 0, n_chunks - 1)
        ci = jb % nc
        if sub == 1:
            _reset_state_if(ci == 0, state)
        stage_mix = _stage_mix(u_bufs[1 - sub], y_bufs[sub], prm, state, ci,
                               outs if sub == 0 else None)
        stage_out = _stage_out(xc_ref[0, rows, :], y_bufs[1 - sub], o_ref, rows, modc_ref,
                               wout_ref, n2w_ref, up_ref, down_ref, fnw_ref, final)
        _trace_interleaved(stage_mix, [stage_in, stage_out], N_MIX_PHASES, N_FILL_PIECES)

    if n_clear:
        @pl.when(k < n_steps)
        def _():
            for cp in clear_copies():
                cp.wait()


CLEAR_ROWS = 1024


def _player(l, x, mod, mod_row0, w, fnw, final, prev, clear_shapes=()):
    b, t, d = x.shape
    nc = t // CHUNK
    n_chunks = b * nc
    n_pairs = n_chunks // 2
    ppr = nc // 2
    pair_a = lambda k: jnp.minimum(k, n_pairs - 1)
    pair_c = lambda k: jnp.clip(k - 1, 0, n_pairs - 1)
    seq_b = lambda k: jnp.clip(2 * k - 1, 0, n_chunks - 1) // nc
    xspec = lambda pair: pl.BlockSpec((1, 2 * CHUNK, d), lambda k: (pair(k) // ppr, pair(k) % ppr, 0))
    mspec = lambda pair: pl.BlockSpec((None, 1, 6, d),
                                      lambda k: (l, mod_row0 + pair(k) // ppr, 0, 0))
    st_spec = lambda s: pl.BlockSpec((None, 1) + s[2:],
                                     lambda k: (l, seq_b(k)) + (0,) * (len(s) - 2))
    outs = [jax.ShapeDtypeStruct((b, t, d), f32)]
    outs += [jax.ShapeDtypeStruct(a.shape, f32) for a in prev]
    mix_params = [w["ssd_conv_w"], w["ssd_conv_b"], w["p_small_bias"], w["p_alog"], w["dskip"],
                  w["ssd_norm_w"], w["ml_norm_w"], w["lru_conv_w"], w["lru_conv_b"], w["lru_wax"],
                  w["lru_ba"], w["lru_bx"], w["lru_lambda"]]
    assert len(mix_params) == N_MIX_PARAMS and len(prev) == N_STATE_OUTS
    n_in = 11 + N_MIX_PARAMS
    clear_scratch = []
    if clear_shapes:
        for s in clear_shapes:
            assert s[3] == V7X_LANES and s[2] <= CLEAR_ROWS and (s[0] * s[1]) % n_pairs == 0
        outs += [jax.ShapeDtypeStruct(s, f32) for s in clear_shapes]
        per_step = max(s[0] * s[1] // n_pairs for s in clear_shapes)
        clear_scratch = [pltpu.VMEM((CLEAR_ROWS, V7X_LANES), f32),
                         pltpu.SemaphoreType.DMA((len(clear_shapes), per_step))]
    return pl.pallas_call(
        functools.partial(_player_kernel, final=final, n_chunks=n_chunks, nc=nc,
                          n_clear=len(clear_shapes)),
        out_shape=outs,
        grid=(n_pairs + 1,),
        in_specs=[xspec(pair_a), mspec(pair_a), xspec(pair_c), mspec(pair_c),
                  _layer(w["norm1_w"], l), _layer(w["w_in_p"], l, resident=True)]
        + [_layer(a, l) for a in mix_params]
        + [_layer(w["w_out"], l, resident=True), _layer(w["norm2_w"], l),
           _layer(w["mlp_up"], l, resident=True), _layer(w["mlp_down"], l, resident=True),
           _full((1, d))]
        + [pl.BlockSpec(memory_space=pl.ANY)] * len(prev),
        out_specs=[xspec(pair_c)] + [st_spec(a.shape) for a in prev]
        + [pl.BlockSpec(memory_space=pl.ANY)] * len(clear_shapes),
        input_output_aliases={n_in + i: 1 + i for i in range(len(prev))},
        scratch_shapes=clear_scratch + [
            pltpu.VMEM((CHUNK, NP_COLS), f32), pltpu.VMEM((CHUNK, NP_COLS), f32),
            pltpu.VMEM((CHUNK, MIX_WIDTH), bf16), pltpu.VMEM((CHUNK, MIX_WIDTH), bf16),
            pltpu.VMEM(((CONV_W - 1) * V7X_SUBLANES, SSD_CONV_DIM), f32),
            pltpu.VMEM(((CONV_W - 1) * V7X_SUBLANES, LRU_WIDTH), f32),
            pltpu.VMEM((SSD_GROUPS, SSD_STATE, 512), f32),
            pltpu.VMEM((ML_HEADS, ML_HEAD_DIM, ML_HEAD_DIM), f32),
            pltpu.VMEM((ML_HEADS, 1, ML_HEAD_DIM), f32),
            pltpu.VMEM((ML_HEADS, 1, V7X_LANES), f32),
            pltpu.VMEM((1, LRU_WIDTH), f32),
        ],
        compiler_params=pltpu.CompilerParams(
            dimension_semantics=("arbitrary",), vmem_limit_bytes=PLAYER_VMEM_LIMIT),
        name="prompt_layer",
    )(x, mod, x, mod, w["norm1_w"], w["w_in_p"], *mix_params,
      w["w_out"], w["norm2_w"], w["mlp_up"], w["mlp_down"], fnw, *prev)


def _smix_kernel(u_ref, s0_ref, sconv0_ref, c0_ref, n0_ref, m0_ref, lh0_ref, lconv0_ref,
                 cw_ref, cb_ref, dtb_ref, alog_ref, dskip_ref, snw_ref, ib_ref, fb_ref, mnw_ref,
                 lcw_ref, lcb_ref, wax_ref, ba_ref, bx_ref, lam_ref,
                 y_ref, s_ref, sconv_ref, c_ref, n_ref, m_ref, lh_ref, lconv_ref):
    T = DEC_SEQ
    bb = u_ref.shape[1]
    H = ML_HEAD_DIM
    toks = range(T)

    def useg(t, c0, n):
        return u_ref[t, :, c0:c0 + n]

    def to_seq(rows):
        return jnp.swapaxes(jnp.stack(rows, axis=0), 0, 1)

    def to_tok(a):
        a = jnp.swapaxes(a, 0, 1)
        return [a[t] for t in toks]

    def conv(c0, n, hist_ref, w_ref, b_ref, new_ref):
        xpad = [hist_ref[k] for k in range(CONV_W - 1)] + [useg(t, c0, n) for t in toks]
        for k in range(CONV_W - 1):
            new_ref[k] = xpad[T + k]
        out = []
        for t in toks:
            acc = b_ref[...] + xpad[t] * w_ref[0:1, :]
            for j in range(1, CONV_W):
                acc = acc + xpad[t + j] * w_ref[j:j + 1, :]
            out.append(acc)
        return out

    def split3(d):
        hi = d.astype(bf16).astype(f32)
        r1 = d - hi
        mid = r1.astype(bf16).astype(f32)
        return [hi, mid, (r1 - mid).astype(bf16).astype(f32)]

    def update_lhs(rows, decay):
        pad = [jnp.zeros_like(decay)] * (T - 3)
        return to_seq(rows + split3(decay) + pad).astype(bf16)

    def ones_rhs(kmat):
        zeros = jnp.zeros_like(kmat)
        return jnp.concatenate(
            [jnp.concatenate([kmat, zeros], axis=2),
             jnp.concatenate([zeros, jnp.ones_like(kmat)], axis=2)], axis=1).astype(bf16)

    xbc = [_silu(a) for a in conv(P_XBC, SSD_CONV_DIM, sconv0_ref, cw_ref, cb_ref, sconv_ref)]
    xs = [a[:, 0:SSD_WIDTH] for a in xbc]
    seg = lambda a, j: a[:, SSD_WIDTH + j * SSD_STATE:SSD_WIDTH + (j + 1) * SSD_STATE]
    bm = [[seg(a, g) for a in xbc] for g in range(SSD_GROUPS)]
    cm = [[seg(a, SSD_GROUPS + g) for a in xbc] for g in range(SSD_GROUPS)]
    a_row = -jnp.exp(alog_ref[...]) * LOG2E
    lane = lax.broadcasted_iota(jnp.int32, (bb, SSD_WIDTH), 1)
    cum, xdt = [], []
    for t in toks:
        dt = _softplus(useg(t, S_DT, SSD_WIDTH) + dtb_ref[...])
        cum.append(dt * a_row + (cum[-1] if cum else 0.0))
        xdt.append(xs[t] * dt)
    y = [xs[t] * dskip_ref[...] for t in toks]
    for t in toks:
        for s in range(t + 1):
            cb0 = jnp.sum(cm[0][t] * bm[0][s], axis=-1, keepdims=True)
            cb1 = jnp.sum(cm[1][t] * bm[1][s], axis=-1, keepdims=True)
            term = jnp.where(lane < 512, cb0, cb1) * xdt[s]
            y[t] = y[t] + (term if s == t else term * jnp.exp2(cum[t] - cum[s]))
    inter = []
    for g in range(SSD_GROUPS):
        inter.append(to_tok(jnp.einsum(
            'btn,bpn->btp', to_seq(cm[g]).astype(bf16),
            s0_ref[:, g * 512:(g + 1) * 512, :].astype(bf16), preferred_element_type=f32)))
    for t in toks:
        yt = y[t] + jnp.concatenate([inter[0][t], inter[1][t]], axis=-1) * jnp.exp2(cum[t])
        y_ref[t, :, 0:SSD_WIDTH] = _rms(yt * _silu(useg(t, P_Z, SSD_WIDTH)), snw_ref[...])
    xend = [xdt[s] if s == T - 1 else xdt[s] * jnp.exp2(cum[T - 1] - cum[s]) for s in toks]
    lhs = update_lhs(xend, jnp.exp2(cum[T - 1]))
    for g in range(SSD_GROUPS):
        zz = jnp.einsum('bkp,bkn->bpn', lhs[:, :, g * 512:(g + 1) * 512],
                        ones_rhs(to_seq(bm[g])), preferred_element_type=f32)
        s_ref[:, g * 512:(g + 1) * 512, :] = (
            zz[:, :, SSD_STATE:] * s0_ref[:, g * 512:(g + 1) * 512, :] + zz[:, :, :SSD_STATE])

    def headsum(a):
        return jnp.concatenate(
            [jnp.broadcast_to(jnp.sum(a[:, h * H:(h + 1) * H], axis=-1, keepdims=True), (bb, H))
             for h in range(ML_HEADS)], axis=-1)

    q = [useg(t, P_Q, ML_WIDTH) for t in toks]
    k = [useg(t, P_K, ML_WIDTH) * (H ** -0.5) for t in toks]
    v = [useg(t, P_V, ML_WIDTH) for t in toks]
    ic = [useg(t, S_I, ML_WIDTH) + ib_ref[...] for t in toks]
    bc = []
    for t in toks:
        fc = _log_sigmoid(useg(t, S_F, ML_WIDTH) + fb_ref[...])
        bc.append(fc + (bc[-1] if bc else 0.0))
    m0 = m0_ref[...]
    n0 = n0_ref[...]
    q_seq = to_seq(q).astype(bf16)
    qc = to_tok(jnp.concatenate(
        [jnp.einsum('btk,bvk->btv', q_seq[:, :, h * H:(h + 1) * H],
                    c0_ref[:, h * H:(h + 1) * H, :].astype(bf16), preferred_element_type=f32)
         for h in range(ML_HEADS)], axis=-1))
    mts = []
    for t in toks:
        dm = [bc[t] - bc[s] + ic[s] for s in range(t + 1)]
        g_t = bc[t] + m0
        mt = g_t
        for d in dm:
            mt = jnp.maximum(mt, d)
        mts.append(mt)
        inter_m = jnp.exp(g_t - mt)
        num = inter_m * qc[t]
        den = inter_m * headsum(q[t] * n0)
        for s in range(t + 1):
            w = jnp.exp(dm[s] - mt) * headsum(q[t] * k[s])
            num = num + w * v[s]
            den = den + w
        hout = num / jnp.maximum(jnp.abs(den), jnp.exp(-mt))
        y_ml = (hout * lax.rsqrt(headsum(hout * hout) * (1.0 / H) + EPS) * mnw_ref[...]
                * _sigmoid(useg(t, P_O, ML_WIDTH)))
        y_ref[t, :, SSD_WIDTH:SSD_WIDTH + ML_WIDTH] = y_ml
    m_new = mts[T - 1]
    m_ref[...] = m_new
    w_end = [jnp.exp(bc[T - 1] - bc[s] + ic[s] - m_new) for s in toks]
    dc = jnp.exp(bc[T - 1] + m0 - m_new)
    n_new = dc * n0
    for s in toks:
        n_new = n_new + w_end[s] * k[s]
    n_ref[...] = n_new
    lhs_m = update_lhs([v[s] * w_end[s] for s in toks], dc)
    k_seq = to_seq(k)
    for h in range(ML_HEADS):
        sl = slice(h * H, (h + 1) * H)
        zz = jnp.einsum('bkv,bkn->bvn', lhs_m[:, :, sl], ones_rhs(k_seq[:, :, sl]),
                        preferred_element_type=f32)
        c_ref[:, sl, :] = zz[:, :, H:] * c0_ref[:, sl, :] + zz[:, :, :H]

    xr = jnp.concatenate(conv(P_XR, LRU_WIDTH, lconv0_ref, lcw_ref, lcb_ref, lconv_ref), axis=0)
    rs, is_ = [], []
    for kk in range(LRU_BLOCKS):
        xk = xr[:, kk * LRU_BLOCK_DIM:(kk + 1) * LRU_BLOCK_DIM].astype(bf16)
        ri = jnp.dot(xk, wax_ref[kk], preferred_element_type=f32)
        rs.append(ri[:, 0:LRU_BLOCK_DIM])
        is_.append(ri[:, LRU_BLOCK_DIM:2 * LRU_BLOCK_DIM])
    r = _sigmoid(jnp.concatenate(rs, axis=1) + ba_ref[...])
    i_g = _sigmoid(jnp.concatenate(is_, axis=1) + bx_ref[...])
    log_a = -LRU_C * r * _softplus(-lam_ref[...])
    a_s = jnp.exp(log_a)
    u_s = jnp.sqrt(_neg_expm1(2.0 * log_a)) * i_g * xr
    cur = lh0_ref[...]
    for t in toks:
        rows = slice(t * bb, (t + 1) * bb)
        cur = a_s[rows] * cur + u_s[rows]
        y_ref[t, :, SSD_WIDTH + ML_WIDTH:MIX_WIDTH] = cur * _gelu_tanh(useg(t, P_GR, LRU_WIDTH))
    lh_ref[...] = cur


def _smix(l, u, states, prev, w, bb):
    t, b, _ = u.shape

    def sblk(a, ax):
        shape = tuple(None if d == 0 else bb if d == ax else n for d, n in enumerate(a.shape))
        return pl.BlockSpec(shape, lambda i: tuple(l if d == 0 else i if d == ax else 0
                                                   for d in range(a.ndim)))

    ublk = lambda n: pl.BlockSpec((t, bb, n), lambda i: (0, i, 0))
    outs = [jax.ShapeDtypeStruct((t, b, MIX_WIDTH), f32)]
    outs += [jax.ShapeDtypeStruct(a.shape, f32) for a, _ in states]
    weights = [w["ssd_conv_w"], w["ssd_conv_b"], w["s_dt_bias"], w["s_alog"], w["dskip"],
               w["ssd_norm_w"], w["s_i_bias"], w["s_f_bias"], w["ml_norm_w"],
               w["lru_conv_w"], w["lru_conv_b"], w["lru_wax"], w["lru_ba"], w["lru_bx"],
               w["lru_lambda"]]
    n_in = 1 + len(states) + len(weights)
    prev = list(prev)

    def body(*refs):
        _smix_kernel(*refs[:n_in], *refs[n_in + len(prev):])

    return pl.pallas_call(
        body,
        out_shape=outs,
        grid=(b // bb,),
        in_specs=[ublk(u.shape[-1])] + [sblk(a, ax) for a, ax in states]
        + [_layer(a, l) for a in weights]
        + [pl.BlockSpec(memory_space=pl.ANY)] * len(prev),
        out_specs=[ublk(MIX_WIDTH)] + [sblk(a, ax) for a, ax in states],
        input_output_aliases={n_in + k: 1 + k for k in range(len(prev))},
        compiler_params=pltpu.CompilerParams(
            dimension_semantics=("arbitrary",), vmem_limit_bytes=VMEM_LIMIT),
        name="sample_mix",
    )(u, *[a for a, _ in states], *weights, *prev)


def _prep_params(p):
    w_p, w_exp = _pack_w_in(p["w_in"])
    r = lambda a: a.reshape(N_LAYERS, 1, -1)
    rep = lambda a, n: r(jnp.repeat(a, n, axis=-1))
    small = jnp.concatenate([p["ssd_dt_bias"], p["ml_i_bias"], p["ml_f_bias"]], axis=-1)
    return dict(
        w_in_p=w_p, w_exp=w_exp,
        norm1_w=r(p["norm1_w"]), norm2_w=r(p["norm2_w"]),
        ssd_conv_w=p["ssd_conv_w"], ssd_conv_b=r(p["ssd_conv_b"]),
        p_small_bias=r(jnp.pad(small, ((0, 0), (0, V7X_LANES - small.shape[-1])))),
        p_alog=r(jnp.pad(p["ssd_a_log"], ((0, 0), (0, V7X_LANES - SSD_HEADS)))),
        s_dt_bias=rep(p["ssd_dt_bias"], SSD_HEAD_DIM), s_alog=rep(p["ssd_a_log"], SSD_HEAD_DIM),
        s_i_bias=rep(p["ml_i_bias"], ML_HEAD_DIM), s_f_bias=rep(p["ml_f_bias"], ML_HEAD_DIM),
        dskip=rep(p["ssd_d"], SSD_HEAD_DIM),
        ssd_norm_w=r(p["ssd_norm_w"]), ml_norm_w=r(p["ml_norm_w"]),
        lru_conv_w=p["lru_conv_w"], lru_conv_b=r(p["lru_conv_b"]),
        lru_wax=jnp.concatenate([p["lru_wa"], p["lru_wx"]], axis=-1).astype(bf16),
        lru_ba=r(p["lru_ba"]), lru_bx=r(p["lru_bx"]), lru_lambda=r(p["lru_lambda"]),
        w_out=_cast_pad(p["w_out"]), mlp_up=_cast_pad(p["mlp_up"]),
        mlp_down=_cast_pad(p["mlp_down"]),
    )


def kernel(x_prompt, x_sample, c_prompt, c_sample, state_ssm, state_ssd_conv, state_mlstm_c, state_mlstm_n, state_mlstm_m, state_lru_h, state_lru_conv, ada_w, ada_b, norm1_w, norm2_w, w_in, ssd_conv_w, ssd_conv_b, ssd_dt_bias, ssd_a_log, ssd_d, ssd_norm_w, ml_i_bias, ml_f_bias, ml_norm_w, lru_conv_w, lru_conv_b, lru_wa, lru_ba, lru_wx, lru_bx, lru_lambda, w_out, mlp_up, mlp_down, final_norm_w):
    p = dict(norm1_w=norm1_w, norm2_w=norm2_w, w_in=w_in, ssd_conv_w=ssd_conv_w,
             ssd_conv_b=ssd_conv_b, ssd_dt_bias=ssd_dt_bias, ssd_a_log=ssd_a_log, ssd_d=ssd_d,
             ssd_norm_w=ssd_norm_w, ml_i_bias=ml_i_bias, ml_f_bias=ml_f_bias,
             ml_norm_w=ml_norm_w, lru_conv_w=lru_conv_w, lru_conv_b=lru_conv_b, lru_wa=lru_wa,
             lru_ba=lru_ba, lru_wx=lru_wx, lru_bx=lru_bx, lru_lambda=lru_lambda, w_out=w_out,
             mlp_up=mlp_up, mlp_down=mlp_down)
    w = _prep_params(p)
    fnw = final_norm_w.reshape(1, D_MODEL)
    bp = x_prompt.shape[0]
    bs_ = x_sample.shape[0]

    mod = _ada(jnp.concatenate([c_sample, c_prompt], axis=0), ada_w, ada_b)
    mod = mod.reshape(N_LAYERS, bs_ + bp, 6, D_MODEL)

    xp = _interleave_rows(x_prompt)
    p_out = [jnp.zeros((N_LAYERS, bp) + s, f32) for s in (
        (SSD_HEADS // 2, 2 * SSD_HEAD_DIM, SSD_STATE),
        (CONV_W - 1, SSD_CONV_DIM), (ML_HEADS, ML_HEAD_DIM, ML_HEAD_DIM),
        (ML_HEADS, ML_HEAD_DIM), (ML_HEADS, V7X_LANES),
        (1, LRU_WIDTH), (CONV_W - 1, LRU_WIDTH))]
    big_states = ((N_LAYERS, bs_, SSD_HEADS * SSD_HEAD_DIM, SSD_STATE),
                  (N_LAYERS, bs_, ML_HEADS * ML_HEAD_DIM, ML_HEAD_DIM))
    for l in range(N_LAYERS):
        res = _player(l, xp, mod, bs_, w, fnw, l == N_LAYERS - 1, p_out,
                      big_states if l == 0 else ())
        xp, p_out = res[0], list(res[1:1 + N_STATE_OUTS])
        if l == 0:
            ssm_buf, c_buf = res[1 + N_STATE_OUTS:]
    xp = _deinterleave_rows(xp)
    ssm, sconv, mc, mn, mm, lh, lconv = p_out
    p_states = (ssm.reshape(N_LAYERS, bp, SSD_HEADS, SSD_HEAD_DIM, SSD_STATE), sconv, mc, mn,
                mm[..., 0], lh.reshape(N_LAYERS, bp, LRU_WIDTH), lconv)

    tok_major = lambda a: jnp.swapaxes(a, -3, -2)
    xs = tok_major(x_sample)
    mod_s = tok_major(mod[:, :bs_])
    st_in = (
        (state_ssm.reshape(N_LAYERS, bs_, SSD_HEADS * SSD_HEAD_DIM, SSD_STATE), 1),
        (tok_major(state_ssd_conv), 2),
        (state_mlstm_c.reshape(N_LAYERS, bs_, ML_HEADS * ML_HEAD_DIM, ML_HEAD_DIM), 1),
        (state_mlstm_n.reshape(N_LAYERS, bs_, ML_WIDTH), 1),
        (jnp.repeat(state_mlstm_m, ML_HEAD_DIM, axis=-1), 1),
        (state_lru_h, 1),
        (tok_major(state_lru_conv), 2),
    )
    st_out = [jnp.zeros(a.shape, f32) for a, _ in st_in]
    st_out[0], st_out[2] = ssm_buf, c_buf
    for l in range(N_LAYERS):
        u = _inproj(l, xs, mod_s, w["norm1_w"], w["w_in_p"], w["w_exp"], 32)
        ycat, *st_out = _smix(l, u, st_in, st_out, w, 8)
        xs = _outmlp(l, xs, ycat, mod_s, w["w_out"], w["norm2_w"], w["mlp_up"], w["mlp_down"],
                     fnw, 64, l == N_LAYERS - 1)
    xs = tok_major(xs)
    ssm, sconv, mc, mn, mm, lh, lconv = st_out
    s_states = (ssm.reshape(N_LAYERS, bs_, SSD_HEADS, SSD_HEAD_DIM, SSD_STATE), tok_major(sconv),
                mc.reshape(N_LAYERS, bs_, ML_HEADS, ML_HEAD_DIM, ML_HEAD_DIM),
                mn.reshape(N_LAYERS, bs_, ML_HEADS, ML_HEAD_DIM),
                mm.reshape(N_LAYERS, bs_, ML_HEADS, ML_HEAD_DIM)[..., 0],
                lh, tok_major(lconv))

    return (xp, xs) + p_states + s_states
```

```python
import functools
import math

import jax
import jax.numpy as jnp
from jax import lax
from jax.experimental import pallas as pl
from jax.experimental.pallas import tpu as pltpu

f32 = jnp.float32
bf16 = jnp.bfloat16

D_MODEL = 1024
N_LAYERS = 2
MIX_WIDTH = 2 * D_MODEL
SSD_WIDTH = 1024
SSD_HEAD_DIM = 64
SSD_HEADS = 16
SSD_GROUPS = 2
SSD_STATE = 128
SSD_CONV_DIM = SSD_WIDTH + 2 * SSD_GROUPS * SSD_STATE
ML_WIDTH = 512
ML_HEADS = 4
ML_HEAD_DIM = 128
LRU_WIDTH = 512
LRU_BLOCKS = 4
LRU_BLOCK_DIM = 128
LRU_C = 8.0
CONV_W = 4
CHUNK = 128
D_FF = 4 * D_MODEL
EPS = 1e-6
PAST_LEN = 16384
DEC_SEQ = 8

_O_Z, _O_XBC, _O_DT, _O_Q, _O_O_END, _O_I, _O_F, _O_XR, _O_END = (
    0, 1024, 2560, 2576, 4624, 4624, 4628, 4632, 5656)

P_Z = 0
P_XBC = 1024
P_Q = 2560
P_K = 3072
P_V = 3584
P_O = 4096
P_XR = 4608
P_GR = 5120
P_MAIN = 5632
P_SMALL = 5632
NP_COLS = 5760
S_DT = 5632
S_I = 6656
S_F = 7168
NS_COLS = 7680

V7X_LANES = 128
V7X_SUBLANES = 8
VMEM_LIMIT = 56 * 1024 * 1024
PLAYER_VMEM_LIMIT = 60 * 1024 * 1024
ROW_TILES = CHUNK // V7X_SUBLANES
LOG2E = 1.4426950408889634

_NT = (((1,), (1,)), ((), ()))
_TN = (((0,), (0,)), ((), ()))


def _softplus(x):
    return jnp.maximum(x, 0.0) + jnp.log1p(jnp.exp(-jnp.abs(x)))


def _log_sigmoid(x):
    return -_softplus(-x)


def _sigmoid(x):
    return 1.0 / (1.0 + jnp.exp(-x))


def _silu(x):
    return x * _sigmoid(x)


def _gelu_tanh(x):
    c = math.sqrt(2.0 / math.pi)
    return 0.5 * x * (1.0 + jnp.tanh(c * (x + 0.044715 * (x * x * x))))


def _neg_expm1(x):
    return -jnp.tanh(0.5 * x) * (jnp.exp(x) + 1.0)


def _rms(x, w):
    return x * lax.rsqrt(jnp.mean(x * x, axis=-1, keepdims=True) + EPS) * w


def _interleave_rows(x):
    b, t, d = x.shape
    return x.reshape(b, t // CHUNK, V7X_SUBLANES, ROW_TILES, d).swapaxes(2, 3).reshape(b, t, d)


def _deinterleave_rows(x):
    b, t, d = x.shape
    return x.reshape(b, t // CHUNK, ROW_TILES, V7X_SUBLANES, d).swapaxes(2, 3).reshape(b, t, d)


def _full(shape):
    n = len(shape)
    return pl.BlockSpec(shape, lambda *_: (0,) * n)


def _layer(a, l, resident=False):
    nd = a.ndim - 1
    kw = dict(pipeline_mode=pl.Buffered(1)) if resident else {}
    return pl.BlockSpec((None,) + a.shape[1:], lambda *_: (l,) + (0,) * nd, **kw)


def _ada_kernel(c_ref, w_ref, b_ref, o_ref):
    cs = _silu(c_ref[...]).astype(bf16)
    o_ref[0] = jnp.dot(cs, w_ref[0].astype(bf16), preferred_element_type=f32) + b_ref[0]


def _ada(c_all, ada_w, ada_b):
    n = c_all.shape[0]
    tn = 1024
    return pl.pallas_call(
        _ada_kernel,
        out_shape=jax.ShapeDtypeStruct((N_LAYERS, n, 6 * D_MODEL), f32),
        grid=(N_LAYERS, 6 * D_MODEL // tn),
        in_specs=[
            pl.BlockSpec((n, D_MODEL), lambda l, j: (0, 0)),
            pl.BlockSpec((1, D_MODEL, tn), lambda l, j: (l, 0, j)),
            pl.BlockSpec((1, 1, tn), lambda l, j: (l, 0, j)),
        ],
        out_specs=pl.BlockSpec((1, n, tn), lambda l, j: (l, 0, j)),
        compiler_params=pltpu.CompilerParams(
            dimension_semantics=("arbitrary", "arbitrary"), vmem_limit_bytes=VMEM_LIMIT),
        name="ada_mod",
    )(c_all, ada_w, ada_b.reshape(N_LAYERS, 1, 6 * D_MODEL))


PACK_ROWS = 128


def _pack_w_in_kernel(w_ref, wp_ref, we_ref):
    c = lambda a: a.astype(bf16)
    wp_ref[:, 0:P_Q] = c(w_ref[:, 0:_O_DT])
    wp_ref[:, P_Q:P_XR] = c(w_ref[:, _O_Q:_O_O_END])
    wp_ref[:, P_XR:P_MAIN] = c(w_ref[:, _O_XR:_O_END])
    dt = w_ref[:, _O_DT:_O_Q]
    gates = w_ref[:, _O_I:_O_XR]
    n_pad = V7X_LANES - SSD_HEADS - 2 * ML_HEADS
    wp_ref[:, P_SMALL:NP_COLS] = c(jnp.concatenate(
        [dt, gates, jnp.zeros((PACK_ROWS, n_pad), f32)], axis=1))
    rep = lambda a, j, n: jnp.broadcast_to(a[:, j:j + 1], (PACK_ROWS, n))
    we_ref[...] = c(jnp.concatenate(
        [rep(dt, h, SSD_HEAD_DIM) for h in range(SSD_HEADS)]
        + [rep(gates, h, ML_HEAD_DIM) for h in range(2 * ML_HEADS)], axis=1))


def _pack_w_in(w_in):
    n_l, d, n = w_in.shape
    blk = lambda m: pl.BlockSpec((None, PACK_ROWS, m), lambda l, i: (l, i, 0))
    return pl.pallas_call(
        _pack_w_in_kernel,
        out_shape=[jax.ShapeDtypeStruct((n_l, d, NP_COLS), bf16),
                   jax.ShapeDtypeStruct((n_l, d, NS_COLS - P_MAIN), bf16)],
        grid=(n_l, d // PACK_ROWS),
        in_specs=[blk(n)],
        out_specs=[blk(NP_COLS), blk(NS_COLS - P_MAIN)],
        compiler_params=pltpu.CompilerParams(
            dimension_semantics=("arbitrary", "arbitrary"), vmem_limit_bytes=VMEM_LIMIT),
        name="pack_w_in",
    )(w_in)


def _cast_pad_kernel(w_ref, o_ref):
    n = w_ref.shape[-1]
    o_ref[:, 0:n] = w_ref[...].astype(bf16)
    o_ref[:, n:] = jnp.zeros((w_ref.shape[0], o_ref.shape[-1] - n), bf16)


def _cast_pad(a):
    n_l, k, n = a.shape
    rows = 512
    return pl.pallas_call(
        _cast_pad_kernel,
        out_shape=jax.ShapeDtypeStruct((n_l, k, n + V7X_LANES), bf16),
        grid=(n_l, k // rows),
        in_specs=[pl.BlockSpec((None, rows, n), lambda l, i: (l, i, 0))],
        out_specs=pl.BlockSpec((None, rows, n + V7X_LANES), lambda l, i: (l, i, 0)),
        compiler_params=pltpu.CompilerParams(
            dimension_semantics=("arbitrary", "arbitrary"), vmem_limit_bytes=VMEM_LIMIT),
        name="cast_pad",
    )(a)


def _outmlp_kernel(x_ref, y_ref, mod_ref, wout_ref, n2w_ref, up_ref, down_ref, fnw_ref, o_ref,
                   *, final):
    x = x_ref[...]
    t, bb, d = x.shape
    m = t * bb
    ycat = y_ref[...].reshape(m, MIX_WIDTH).astype(bf16)
    mix = jnp.dot(ycat, wout_ref[:, 0:d], preferred_element_type=f32).reshape(t, bb, d)
    x1 = x + mod_ref[2] * mix
    hn = _rms(x1, n2w_ref[...]) * (1.0 + mod_ref[4]) + mod_ref[3]
    hn = hn.reshape(m, d).astype(bf16)
    ff = jnp.zeros((m, d), f32)
    fc = 1024
    for c in range(D_FF // fc):
        h = jnp.dot(hn, up_ref[:, c * fc:(c + 1) * fc], preferred_element_type=f32)
        h = jnp.square(jnp.maximum(h, 0.0)).astype(bf16)
        ff = ff + jnp.dot(h, down_ref[c * fc:(c + 1) * fc, 0:D_MODEL],
                          preferred_element_type=f32)
    x2 = x1 + mod_ref[5] * ff.reshape(t, bb, d)
    if final:
        x2 = _rms(x2, fnw_ref[...])
    o_ref[...] = x2


def _outmlp(l, x, ycat, mod, wout, n2w, up, down, fnw, bb, final):
    t, b, d = x.shape
    return pl.pallas_call(
        functools.partial(_outmlp_kernel, final=final),
        out_shape=jax.ShapeDtypeStruct((t, b, d), f32),
        grid=(b // bb,),
        in_specs=[
            pl.BlockSpec((t, bb, d), lambda i: (0, i, 0)),
            pl.BlockSpec((t, bb, MIX_WIDTH), lambda i: (0, i, 0)),
            pl.BlockSpec((None, 6, bb, d), lambda i: (l, 0, i, 0)),
            _layer(wout, l, resident=True),
            _layer(n2w, l),
            _layer(up, l, resident=True),
            _layer(down, l, resident=True),
            _full((1, d)),
        ],
        out_specs=pl.BlockSpec((t, bb, d), lambda i: (0, i, 0)),
        compiler_params=pltpu.CompilerParams(
            dimension_semantics=("arbitrary",), vmem_limit_bytes=VMEM_LIMIT),
        name="outproj_mlp",
    )(x, ycat, mod, wout, n2w, up, down, fnw)


PIECE = 512


def _stage_in(x, mod_ref, nw_ref, w_ref, u_ref):
    hn = _rms(x, nw_ref[...]) * (1.0 + mod_ref[0, 1:2, :]) + mod_ref[0, 0:1, :]
    hn = hn.astype(bf16)
    yield
    for c0 in range(0, NP_COLS, PIECE):
        c1 = min(c0 + PIECE, NP_COLS)
        u_ref[:, c0:c1] = jnp.dot(hn, w_ref[:, c0:c1], preferred_element_type=f32)
        yield


def _stage_out(x, y_ref, o_ref, rows, mod_ref, wout_ref, n2w_ref, up_ref, down_ref, fnw_ref,
               final):
    halves = [slice(h * PIECE, (h + 1) * PIECE) for h in range(D_MODEL // PIECE)]
    ycat = y_ref[...]
    mix = []
    for hs in halves:
        mix.append(jnp.dot(ycat, wout_ref[:, hs], preferred_element_type=f32))
        yield
    x1 = x + mod_ref[0, 2:3, :] * jnp.concatenate(mix, axis=1)
    hn = (_rms(x1, n2w_ref[...]) * (1.0 + mod_ref[0, 4:5, :]) + mod_ref[0, 3:4, :]).astype(bf16)
    yield
    ff = [jnp.zeros((x.shape[0], PIECE), f32) for _ in halves]
    fc = 2 * PIECE
    for c in range(D_FF // fc):
        h = []
        for k in range(fc // PIECE):
            cols = slice(c * fc + k * PIECE, c * fc + (k + 1) * PIECE)
            hk = jnp.dot(hn, up_ref[:, cols], preferred_element_type=f32)
            h.append(jnp.square(jnp.maximum(hk, 0.0)).astype(bf16))
            yield
        h = jnp.concatenate(h, axis=1)
        for i, hs in enumerate(halves):
            ff[i] = ff[i] + jnp.dot(h, down_ref[c * fc:(c + 1) * fc, hs],
                                    preferred_element_type=f32)
            yield
    x2 = x1 + mod_ref[0, 5:6, :] * jnp.concatenate(ff, axis=1)
    o_ref[0, rows, :] = _rms(x2, fnw_ref[...]) if final else x2
    yield


def _trace_interleaved(main, fillers, n_main, n_fill):
    fillers = list(fillers)
    due = 0.0
    nxt = 0
    main_live = True
    while main_live or fillers:
        if main_live:
            main_live = next(main, _DONE) is not _DONE
            due += n_fill / n_main
        else:
            due = float(len(fillers))
        while fillers and due >= 1.0:
            nxt %= len(fillers)
            if next(fillers[nxt], _DONE) is _DONE:
                fillers.pop(nxt)
            else:
                nxt += 1
                due -= 1.0


_DONE = object()


def _reset_state_if(first, state):
    for ref in state:
        ref[...] = jnp.where(first, 0.0, ref[...])


def _stage_mix(u_ref, y_ref, prm, state, ci, outs):
    (cw_ref, cb_ref, sbias_ref, alog_ref, dskip_ref, snw_ref, mnw_ref,
     lcw_ref, lcb_ref, wax_ref, ba_ref, bx_ref, lam_ref) = prm
    xbuf, lbuf, st, cst, nst, mst, hst = state
    L = CHUNK
    SUB = V7X_SUBLANES
    HIST = (CONV_W - 1) * SUB

    row = lax.broadcasted_iota(jnp.int32, (L, L), 0)
    col = lax.broadcasted_iota(jnp.int32, (L, L), 1)
    tok = lambda i: (i & (SUB - 1)) * ROW_TILES + (i >> 3)
    tri = tok(row) >= tok(col)
    lo = col < SSD_HEAD_DIM

    def bcl(a, j):
        return jnp.broadcast_to(a[:, j:j + 1], (L, L))

    def conv(u0, hist_ref, w_ref, b_ref, c0, c1):
        raw = u_ref[:, u0 + c0:u0 + c1]
        sub = lax.broadcasted_iota(jnp.int32, (SUB, c1 - c0), 0)
        prev = [pltpu.roll(jnp.where(sub == SUB - 1, hist_ref[k * SUB:(k + 1) * SUB, c0:c1],
                                     raw[L - HIST + k * SUB:L - HIST + (k + 1) * SUB, :]), 1, 0)
                for k in range(CONV_W - 1)]
        acc = b_ref[:, c0:c1] + raw * w_ref[CONV_W - 1:CONV_W, c0:c1]
        for d in range(1, CONV_W):
            shifted = jnp.concatenate(prev[CONV_W - 1 - d:] + [raw[0:L - d * SUB, :]], axis=0)
            acc = acc + shifted * w_ref[CONV_W - 1 - d:CONV_W - d, c0:c1]
        hist_ref[:, c0:c1] = raw[L - HIST:L, :]
        return acc

    cblk = 512
    xbc = []
    for c0 in range(0, SSD_CONV_DIM, cblk):
        xbc.append(_silu(conv(P_XBC, xbuf, cw_ref, cb_ref, c0, c0 + cblk)))
        yield
    xbc = jnp.concatenate(xbc, axis=1)
    xr = conv(P_XR, lbuf, lcw_ref, lcb_ref, 0, LRU_WIDTH)
    yield

    sm = u_ref[:,P_SMALL:P_SMALL + V7X_LANES] + sbias_ref[...]
    dt = _softplus(sm)
    a_row = -jnp.exp(alog_ref[...])
    gates = jnp.where(col < SSD_HEADS, dt * (a_row * LOG2E),
                      jnp.where((col >= 20) & (col < 24), _log_sigmoid(sm), 0.0))
    g_hi = gates.astype(bf16)
    g_r = gates - g_hi.astype(f32)
    g_mid = g_r.astype(bf16)
    g_lo = (g_r - g_mid.astype(f32)).astype(bf16)
    cum3 = jnp.dot(jnp.where(tri, 1.0, 0.0).astype(bf16),
                   jnp.concatenate([g_hi, g_mid, g_lo], axis=1), preferred_element_type=f32)
    cum = cum3[:, 0:L] + cum3[:, L:2 * L] + cum3[:, 2 * L:3 * L]
    cum_t = cum.T
    sm_t = sm.T
    yield

    ys = []
    for g in range(SSD_GROUPS):
        bm_t = xbc[:, SSD_WIDTH + g * SSD_STATE:SSD_WIDTH + (g + 1) * SSD_STATE].T.astype(bf16)
        cm = xbc[:, SSD_WIDTH + (SSD_GROUPS + g) * SSD_STATE:
                 SSD_WIDTH + (SSD_GROUPS + g + 1) * SSD_STATE].astype(bf16)
        cb = jnp.dot(cm, bm_t, preferred_element_type=f32)
        inter = jnp.dot(cm, st[g].astype(bf16), preferred_element_type=f32)
        yield
        xends, keeps = [], []
        for jj in range(4):
            jp = g * 4 + jj
            e0, e1 = 2 * jp, 2 * jp + 1
            c0 = bcl(cum, e0)
            c1 = bcl(cum, e1)
            cum_p = jnp.where(lo, c0, c1)
            dt_p = jnp.where(lo, bcl(dt, e0), bcl(dt, e1))
            xs_p = xbc[:, jp * L:(jp + 1) * L]
            xdt = xs_p * dt_p
            dec0 = jnp.exp2(jnp.where(tri, c0 - cum_t[e0:e0 + 1, :], -jnp.inf))
            dec1 = jnp.exp2(jnp.where(tri, c1 - cum_t[e1:e1 + 1, :], -jnp.inf))
            yield
            att =jnp.concatenate([(cb * dec0).astype(bf16), (cb * dec1).astype(bf16)], axis=1)
            x2 = jnp.concatenate([jnp.where(lo, xdt, 0.0).astype(bf16),
                                  jnp.where(lo, 0.0, xdt).astype(bf16)], axis=0)
            y_p = (jnp.dot(att, x2, preferred_element_type=f32)
                   + inter[:, jj * L:(jj + 1) * L] * jnp.exp2(cum_p)
                   + xs_p * dskip_ref[:, jp * L:(jp + 1) * L])
            ys.append(y_p)
            c_last = cum_p[L - 1:L, :]
            xends.append((xdt * jnp.exp2(c_last - cum_p)).astype(bf16))
            keeps.append(jnp.exp2(c_last))
            yield
        st[g] = (jnp.concatenate(keeps, axis=1) * st[g]
                 + jnp.dot(bm_t, jnp.concatenate(xends, axis=1), preferred_element_type=f32))
    y = jnp.concatenate(ys, axis=1)
    z = u_ref[:,P_Z:P_Z + SSD_WIDTH]
    y_ref[:,0:SSD_WIDTH] = _rms(y * _silu(z), snw_ref[...]).astype(bf16)
    yield

    scale = ML_HEAD_DIM ** -0.5
    for h in range(ML_HEADS):
        sl = slice(h * ML_HEAD_DIM, (h + 1) * ML_HEAD_DIM)
        q = u_ref[:,P_Q + h * ML_HEAD_DIM:P_Q + (h + 1) * ML_HEAD_DIM]
        k = u_ref[:,P_K + h * ML_HEAD_DIM:P_K + (h + 1) * ML_HEAD_DIM] * scale
        v = u_ref[:,P_V + h * ML_HEAD_DIM:P_V + (h + 1) * ML_HEAD_DIM]
        o = u_ref[:,P_O + h * ML_HEAD_DIM:P_O + (h + 1) * ML_HEAD_DIM]
        k_t = k.T.astype(bf16)
        jf = 20 + h
        ji = 16 + h
        b_col = cum[:, jf:jf + 1]
        i_col = sm[:, ji:ji + 1]
        dm = jnp.where(tri, bcl(cum, jf) - cum_t[jf:jf + 1, :] + sm_t[ji:ji + 1, :], -jnp.inf)
        m_prev = mst[h, :, 0:1]
        g_col = b_col + m_prev
        mt = jnp.maximum(g_col, jnp.max(dm, axis=-1, keepdims=True))
        yield
        w = jnp.exp(dm - mt) * jnp.dot(q.astype(bf16), k_t, preferred_element_type=f32)
        inter = jnp.exp(g_col - mt)
        yield
        ct_old = cst[h]
        n_old = nst[h]
        num = jnp.dot(jnp.concatenate([w.astype(bf16), (inter * q).astype(bf16)], axis=1),
                      jnp.concatenate([v.astype(bf16), ct_old.astype(bf16)], axis=0),
                      preferred_element_type=f32)
        den = (jnp.sum(w, axis=-1, keepdims=True)
               + inter * jnp.sum(q * n_old, axis=-1, keepdims=True))
        hout = num / jnp.maximum(jnp.abs(den), jnp.exp(-mt))
        yield
        m_new = mt[L - 1:L, :]
        b_last = b_col[L - 1:L, :]
        w_end = jnp.exp(b_last - b_col + i_col - m_new)
        dc = jnp.exp(b_last + m_prev - m_new)
        cst[h] = dc * ct_old + jnp.dot(k_t, (v * w_end).astype(bf16),
                                       preferred_element_type=f32)
        nst[h] = dc * n_old + jnp.sum(w_end * k, axis=0, keepdims=True)
        mst[h] = jnp.broadcast_to(m_new, (1, V7X_LANES))
        yh = _rms(hout, mnw_ref[:, sl]) * _sigmoid(o)
        y_ref[:,SSD_WIDTH + h * ML_HEAD_DIM:SSD_WIDTH + (h + 1) * ML_HEAD_DIM] = yh.astype(bf16)
        yield

    rs, is_ = [], []
    for kb_ in range(LRU_BLOCKS):
        xk = xr[:, kb_ * LRU_BLOCK_DIM:(kb_ + 1) * LRU_BLOCK_DIM].astype(bf16)
        ri = jnp.dot(xk, wax_ref[kb_], preferred_element_type=f32)
        rs.append(ri[:, 0:LRU_BLOCK_DIM])
        is_.append(ri[:, LRU_BLOCK_DIM:2 * LRU_BLOCK_DIM])
    r = _sigmoid(jnp.concatenate(rs, axis=1) + ba_ref[...])
    i_g = _sigmoid(jnp.concatenate(is_, axis=1) + bx_ref[...])
    yield
    log_a = -LRU_C * r * _softplus(-lam_ref[...])
    mult = jnp.sqrt(_neg_expm1(2.0 * log_a))
    rowl = lax.broadcasted_iota(jnp.int32, (L, LRU_WIDTH), 0)
    first_row = jnp.where(ci == 0, 0, -1)
    mult = jnp.where(rowl == first_row, 1.0, mult)
    a_s = jnp.exp(log_a)
    u_s = mult * i_g * xr
    yield
    tile = lambda a, v: a[v * SUB:(v + 1) * SUB, :]
    h_loc = [tile(u_s, 0)]
    a_cum = [tile(a_s, 0)]
    for v in range(1, ROW_TILES):
        h_loc.append(tile(a_s, v) * h_loc[-1] + tile(u_s, v))
        a_cum.append(tile(a_s, v) * a_cum[-1])
    a_e, h_e = a_cum[-1], h_loc[-1]
    subl = lax.broadcasted_iota(jnp.int32, (SUB, LRU_WIDTH), 0)
    k_ = 1
    while k_ < SUB:
        keep = subl >= k_
        a_sh = jnp.where(keep, pltpu.roll(a_e, k_, 0), 1.0)
        h_sh = jnp.where(keep, pltpu.roll(h_e, k_, 0), 0.0)
        h_e = a_e * h_sh + h_e
        a_e = a_e * a_sh
        k_ *= 2
    h0 = hst[...]
    run_end = h_e + a_e * h0
    run_in = jnp.where(subl == 0, h0, pltpu.roll(run_end, 1, 0))
    hr = jnp.concatenate([h_loc[v] + a_cum[v] * run_in for v in range(ROW_TILES)], axis=0)
    hst[...] = run_end[SUB - 1:SUB, :]
    gr = u_ref[:,P_GR:P_GR + LRU_WIDTH]
    y_ref[:,SSD_WIDTH + ML_WIDTH:MIX_WIDTH] = (hr * _gelu_tanh(gr)).astype(bf16)
    yield

    if outs is not None:
        ssm_ref, sconv_ref, mc_ref, mn_ref, mm_ref, lh_ref, lconv_ref = outs
        for g in range(SSD_GROUPS):
            for jj in range(4):
                ssm_ref[0, g * 4 + jj] = st[g, :, jj * L:(jj + 1) * L].T
        for k in range(CONV_W - 1):
            sconv_ref[0, k:k + 1, :] = xbuf[k * SUB + SUB - 1:(k + 1) * SUB, :]
            lconv_ref[0, k:k + 1, :] = lbuf[k * SUB + SUB - 1:(k + 1) * SUB, :]
        for h in range(ML_HEADS):
            mc_ref[0, h] = cst[h].T
            mn_ref[0, h:h + 1, :] = nst[h]
            mm_ref[0, h:h + 1, :] = mst[h]
        lh_ref[0] = hst[...]


N_MIX_PARAMS = 13
N_STATE_OUTS = 7
N_MIX_PHASES = 44
N_FILL_PIECES = (1 + -(-NP_COLS // PIECE)) + (D_MODEL // PIECE + 1
                                              + (D_FF // (2 * PIECE)) * (2 + D_MODEL // PIECE) + 1)


def _player_kernel(*refs, final, n_chunks, nc, n_clear):
    xa_ref, moda_ref, xc_ref, modc_ref, n1w_ref, win_ref = refs[:6]
    prm = refs[6:6 + N_MIX_PARAMS]
    wout_ref, n2w_ref, up_ref, down_ref, fnw_ref = refs[6 + N_MIX_PARAMS:11 + N_MIX_PARAMS]
    n_in = 11 + N_MIX_PARAMS + N_STATE_OUTS
    o_ref = refs[n_in]
    outs = refs[n_in + 1:n_in + 1 + N_STATE_OUTS]
    clear = refs[n_in + 1 + N_STATE_OUTS:n_in + 1 + N_STATE_OUTS + n_clear]
    u_bufs = refs[-11:-9]
    y_bufs = refs[-9:-7]
    state = refs[-7:]
    k = pl.program_id(0)
    n_steps = n_chunks // 2

    @pl.when(k == 0)
    def _():
        u_bufs[1][...] = jnp.zeros(u_bufs[1].shape, f32)
        y_bufs[1][...] = jnp.zeros(y_bufs[1].shape, bf16)
        for ref in state:
            ref[...] = jnp.zeros(ref.shape, f32)
        if n_clear:
            refs[-13][...] = jnp.zeros(refs[-13].shape, f32)

    def clear_copies():
        zeros, sems = refs[-13], refs[-12]
        copies = []
        for ci_, buf in enumerate(clear):
            n_b, rows = buf.shape[1], buf.shape[2]
            per_step = buf.shape[0] * n_b // n_steps
            for i in range(per_step):
                j = k * per_step + i
                copies.append(pltpu.make_async_copy(
                    zeros.at[0:rows], buf.at[j // n_b, j % n_b], sems.at[ci_, i]))
        return copies

    if n_clear:
        @pl.when(k < n_steps)
        def _():
            for cp in clear_copies():
                cp.start()

    for sub in range(2):
        rows = slice(sub * CHUNK, (sub + 1) * CHUNK)
        stage_in = _stage_in(xa_ref[0, rows, :], moda_ref, n1w_ref, win_ref, u_bufs[sub])
        jb = jnp.clip(2 * k + sub - 1, 0, n_chunks - 1)
        ci = jb % nc
        if sub == 1:
            _reset_state_if(ci == 0, state)
        stage_mix = _stage_mix(u_bufs[1 - sub], y_bufs[sub], prm, state, ci,
                               outs if sub == 0 else None)
        stage_out = _stage_out(xc_ref[0, rows, :], y_bufs[1 - sub], o_ref, rows, modc_ref,
                               wout_ref, n2w_ref, up_ref, down_ref, fnw_ref, final)
        _trace_interleaved(stage_mix, [stage_in, stage_out], N_MIX_PHASES, N_FILL_PIECES)

    if n_clear:
        @pl.when(k < n_steps)
        def _():
            for cp in clear_copies():
                cp.wait()


CLEAR_ROWS = 1024


def _player(l, x, mod, mod_row0, w, fnw, final, prev, clear_shapes=()):
    b, t, d = x.shape
    nc = t // CHUNK
    n_chunks = b * nc
    n_pairs = n_chunks // 2
    ppr = nc // 2
    pair_a = lambda k: jnp.minimum(k, n_pairs - 1)
    pair_c = lambda k: jnp.clip(k - 1, 0, n_pairs - 1)
    seq_b = lambda k: jnp.clip(2 * k - 1, 0, n_chunks - 1) // nc
    xspec = lambda pair: pl.BlockSpec((1, 2 * CHUNK, d), lambda k: (pair(k) // ppr, pair(k) % ppr, 0))
    mspec = lambda pair: pl.BlockSpec((None, 1, 6, d),
                                      lambda k: (l, mod_row0 + pair(k) // ppr, 0, 0))
    st_spec = lambda s: pl.BlockSpec((None, 1) + s[2:],
                                     lambda k: (l, seq_b(k)) + (0,) * (len(s) - 2))
    outs = [jax.ShapeDtypeStruct((b, t, d), f32)]
    outs += [jax.ShapeDtypeStruct(a.shape, f32) for a in prev]
    mix_params = [w["ssd_conv_w"], w["ssd_conv_b"], w["p_small_bias"], w["p_alog"], w["dskip"],
                  w["ssd_norm_w"], w["ml_norm_w"], w["lru_conv_w"], w["lru_conv_b"], w["lru_wax"],
                  w["lru_ba"], w["lru_bx"], w["lru_lambda"]]
    assert len(mix_params) == N_MIX_PARAMS and len(prev) == N_STATE_OUTS
    n_in = 11 + N_MIX_PARAMS
    clear_scratch = []
    if clear_shapes:
        for s in clear_shapes:
            assert s[3] == V7X_LANES and s[2] <= CLEAR_ROWS and (s[0] * s[1]) % n_pairs == 0
        outs += [jax.ShapeDtypeStruct(s, f32) for s in clear_shapes]
        per_step = max(s[0] * s[1] // n_pairs for s in clear_shapes)
        clear_scratch = [pltpu.VMEM((CLEAR_ROWS, V7X_LANES), f32),
                         pltpu.SemaphoreType.DMA((len(clear_shapes), per_step))]
    return pl.pallas_call(
        functools.partial(_player_kernel, final=final, n_chunks=n_chunks, nc=nc,
                          n_clear=len(clear_shapes)),
        out_shape=outs,
        grid=(n_pairs + 1,),
        in_specs=[xspec(pair_a), mspec(pair_a), xspec(pair_c), mspec(pair_c),
                  _layer(w["norm1_w"], l), _layer(w["w_in_p"], l, resident=True)]
        + [_layer(a, l) for a in mix_params]
        + [_layer(w["w_out"], l, resident=True), _layer(w["norm2_w"], l),
           _layer(w["mlp_up"], l, resident=True), _layer(w["mlp_down"], l, resident=True),
           _full((1, d))]
        + [pl.BlockSpec(memory_space=pl.ANY)] * len(prev),
        out_specs=[xspec(pair_c)] + [st_spec(a.shape) for a in prev]
        + [pl.BlockSpec(memory_space=pl.ANY)] * len(clear_shapes),
        input_output_aliases={n_in + i: 1 + i for i in range(len(prev))},
        scratch_shapes=clear_scratch + [
            pltpu.VMEM((CHUNK, NP_COLS), f32), pltpu.VMEM((CHUNK, NP_COLS), f32),
            pltpu.VMEM((CHUNK, MIX_WIDTH), bf16), pltpu.VMEM((CHUNK, MIX_WIDTH), bf16),
            pltpu.VMEM(((CONV_W - 1) * V7X_SUBLANES, SSD_CONV_DIM), f32),
            pltpu.VMEM(((CONV_W - 1) * V7X_SUBLANES, LRU_WIDTH), f32),
            pltpu.VMEM((SSD_GROUPS, SSD_STATE, 512), f32),
            pltpu.VMEM((ML_HEADS, ML_HEAD_DIM, ML_HEAD_DIM), f32),
            pltpu.VMEM((ML_HEADS, 1, ML_HEAD_DIM), f32),
            pltpu.VMEM((ML_HEADS, 1, V7X_LANES), f32),
            pltpu.VMEM((1, LRU_WIDTH), f32),
        ],
        compiler_params=pltpu.CompilerParams(
            dimension_semantics=("arbitrary",), vmem_limit_bytes=PLAYER_VMEM_LIMIT),
        name="prompt_layer",
    )(x, mod, x, mod, w["norm1_w"], w["w_in_p"], *mix_params,
      w["w_out"], w["norm2_w"], w["mlp_up"], w["mlp_down"], fnw, *prev)


def _smix_kernel(x_ref, mod_ref, nw_ref, wm_ref, we_ref,
                 s0_ref, sconv0_ref, c0_ref, n0_ref, m0_ref, lh0_ref, lconv0_ref,
                 cw_ref, cb_ref, dtb_ref, alog_ref, dskip_ref, snw_ref, ib_ref, fb_ref, mnw_ref,
                 lcw_ref, lcb_ref, wax_ref, ba_ref, bx_ref, lam_ref,
                 y_ref, s_ref, sconv_ref, c_ref, n_ref, m_ref, lh_ref, lconv_ref, u_ref):
    T = DEC_SEQ
    bb = u_ref.shape[1]
    H = ML_HEAD_DIM
    toks = range(T)

    hn = _rms(x_ref[...], nw_ref[...]) * (1.0 + mod_ref[1]) + mod_ref[0]
    hn = hn.reshape(T * bb, D_MODEL).astype(bf16)
    u_ref[:, :, 0:P_MAIN] = jnp.dot(hn, wm_ref[...], preferred_element_type=f32).reshape(
        T, bb, P_MAIN)
    u_ref[:, :, P_MAIN:NS_COLS] = jnp.dot(hn, we_ref[...], preferred_element_type=f32).reshape(
        T, bb, NS_COLS - P_MAIN)

    def useg(t, c0, n):
        return u_ref[t, :, c0:c0 + n]

    def to_seq(rows):
        return jnp.swapaxes(jnp.stack(rows, axis=0), 0, 1)

    def to_tok(a):
        a = jnp.swapaxes(a, 0, 1)
        return [a[t] for t in toks]

    def conv(c0, n, hist_ref, w_ref, b_ref, new_ref):
        xpad = [hist_ref[k] for k in range(CONV_W - 1)] + [useg(t, c0, n) for t in toks]
        for k in range(CONV_W - 1):
            new_ref[k] = xpad[T + k]
        out = []
        for t in toks:
            acc = b_ref[...] + xpad[t] * w_ref[0:1, :]
            for j in range(1, CONV_W):
                acc = acc + xpad[t + j] * w_ref[j:j + 1, :]
            out.append(acc)
        return out

    def split3(d):
        hi = d.astype(bf16).astype(f32)
        r1 = d - hi
        mid = r1.astype(bf16).astype(f32)
        return [hi, mid, (r1 - mid).astype(bf16).astype(f32)]

    def update_lhs(rows, decay):
        pad = [jnp.zeros_like(decay)] * (T - 3)
        return to_seq(rows + split3(decay) + pad).astype(bf16)

    def ones_rhs(kmat):
        zeros = jnp.zeros_like(kmat)
        return jnp.concatenate(
            [jnp.concatenate([kmat, zeros], axis=2),
             jnp.concatenate([zeros, jnp.ones_like(kmat)], axis=2)], axis=1).astype(bf16)

    xbc = [_silu(a) for a in conv(P_XBC, SSD_CONV_DIM, sconv0_ref, cw_ref, cb_ref, sconv_ref)]
    xs = [a[:, 0:SSD_WIDTH] for a in xbc]
    seg = lambda a, j: a[:, SSD_WIDTH + j * SSD_STATE:SSD_WIDTH + (j + 1) * SSD_STATE]
    bm = [[seg(a, g) for a in xbc] for g in range(SSD_GROUPS)]
    cm = [[seg(a, SSD_GROUPS + g) for a in xbc] for g in range(SSD_GROUPS)]
    a_row = -jnp.exp(alog_ref[...]) * LOG2E
    lane = lax.broadcasted_iota(jnp.int32, (bb, SSD_WIDTH), 1)
    cum, xdt = [], []
    for t in toks:
        dt = _softplus(useg(t, S_DT, SSD_WIDTH) + dtb_ref[...])
        cum.append(dt * a_row + (cum[-1] if cum else 0.0))
        xdt.append(xs[t] * dt)
    y = [xs[t] * dskip_ref[...] for t in toks]
    for t in toks:
        for s in range(t + 1):
            cb0 = jnp.sum(cm[0][t] * bm[0][s], axis=-1, keepdims=True)
            cb1 = jnp.sum(cm[1][t] * bm[1][s], axis=-1, keepdims=True)
            term = jnp.where(lane < 512, cb0, cb1) * xdt[s]
            y[t] = y[t] + (term if s == t else term * jnp.exp2(cum[t] - cum[s]))
    inter = []
    for g in range(SSD_GROUPS):
        inter.append(to_tok(jnp.einsum(
            'btn,bpn->btp', to_seq(cm[g]).astype(bf16),
            s0_ref[:, g * 512:(g + 1) * 512, :].astype(bf16), preferred_element_type=f32)))
    for t in toks:
        yt = y[t] + jnp.concatenate([inter[0][t], inter[1][t]], axis=-1) * jnp.exp2(cum[t])
        y_ref[t, :, 0:SSD_WIDTH] = _rms(yt * _silu(useg(t, P_Z, SSD_WIDTH)), snw_ref[...])
    xend = [xdt[s] if s == T - 1 else xdt[s] * jnp.exp2(cum[T - 1] - cum[s]) for s in toks]
    lhs = update_lhs(xend, jnp.exp2(cum[T - 1]))
    for g in range(SSD_GROUPS):
        zz = jnp.einsum('bkp,bkn->bpn', lhs[:, :, g * 512:(g + 1) * 512],
                        ones_rhs(to_seq(bm[g])), preferred_element_type=f32)
        s_ref[:, g * 512:(g + 1) * 512, :] = (
            zz[:, :, SSD_STATE:] * s0_ref[:, g * 512:(g + 1) * 512, :] + zz[:, :, :SSD_STATE])

    def headsum(a):
        return jnp.concatenate(
            [jnp.broadcast_to(jnp.sum(a[:, h * H:(h + 1) * H], axis=-1, keepdims=True), (bb, H))
             for h in range(ML_HEADS)], axis=-1)

    q = [useg(t, P_Q, ML_WIDTH) for t in toks]
    k = [useg(t, P_K, ML_WIDTH) * (H ** -0.5) for t in toks]
    v = [useg(t, P_V, ML_WIDTH) for t in toks]
    ic = [useg(t, S_I, ML_WIDTH) + ib_ref[...] for t in toks]
    bc = []
    for t in toks:
        fc = _log_sigmoid(useg(t, S_F, ML_WIDTH) + fb_ref[...])
        bc.append(fc + (bc[-1] if bc else 0.0))
    m0 = m0_ref[...]
    n0 = n0_ref[...]
    q_seq = to_seq(q).astype(bf16)
    qc = to_tok(jnp.concatenate(
        [jnp.einsum('btk,bvk->btv', q_seq[:, :, h * H:(h + 1) * H],
                    c0_ref[:, h * H:(h + 1) * H, :].astype(bf16), preferred_element_type=f32)
         for h in range(ML_HEADS)], axis=-1))
    mts = []
    for t in toks:
        dm = [bc[t] - bc[s] + ic[s] for s in range(t + 1)]
        g_t = bc[t] + m0
        mt = g_t
        for d in dm:
            mt = jnp.maximum(mt, d)
        mts.append(mt)
        inter_m = jnp.exp(g_t - mt)
        num = inter_m * qc[t]
        den = inter_m * headsum(q[t] * n0)
        for s in range(t + 1):
            w = jnp.exp(dm[s] - mt) * headsum(q[t] * k[s])
            num = num + w * v[s]
            den = den + w
        hout = num / jnp.maximum(jnp.abs(den), jnp.exp(-mt))
        y_ml = (hout * lax.rsqrt(headsum(hout * hout) * (1.0 / H) + EPS) * mnw_ref[...]
                * _sigmoid(useg(t, P_O, ML_WIDTH)))
        y_ref[t, :, SSD_WIDTH:SSD_WIDTH + ML_WIDTH] = y_ml
    m_new = mts[T - 1]
    m_ref[...] = m_new
    w_end = [jnp.exp(bc[T - 1] - bc[s] + ic[s] - m_new) for s in toks]
    dc = jnp.exp(bc[T - 1] + m0 - m_new)
    n_new = dc * n0
    for s in toks:
        n_new = n_new + w_end[s] * k[s]
    n_ref[...] = n_new
    lhs_m = update_lhs([v[s] * w_end[s] for s in toks], dc)
    k_seq = to_seq(k)
    for h in range(ML_HEADS):
        sl = slice(h * H, (h + 1) * H)
        zz = jnp.einsum('bkv,bkn->bvn', lhs_m[:, :, sl], ones_rhs(k_seq[:, :, sl]),
                        preferred_element_type=f32)
        c_ref[:, sl, :] = zz[:, :, H:] * c0_ref[:, sl, :] + zz[:, :, :H]

    xr = jnp.concatenate(conv(P_XR, LRU_WIDTH, lconv0_ref, lcw_ref, lcb_ref, lconv_ref), axis=0)
    rs, is_ = [], []
    for kk in range(LRU_BLOCKS):
        xk = xr[:, kk * LRU_BLOCK_DIM:(kk + 1) * LRU_BLOCK_DIM].astype(bf16)
        ri = jnp.dot(xk, wax_ref[kk], preferred_element_type=f32)
        rs.append(ri[:, 0:LRU_BLOCK_DIM])
        is_.append(ri[:, LRU_BLOCK_DIM:2 * LRU_BLOCK_DIM])
    r = _sigmoid(jnp.concatenate(rs, axis=1) + ba_ref[...])
    i_g = _sigmoid(jnp.concatenate(is_, axis=1) + bx_ref[...])
    log_a = -LRU_C * r * _softplus(-lam_ref[...])
    a_s = jnp.exp(log_a)
    u_s = jnp.sqrt(_neg_expm1(2.0 * log_a)) * i_g * xr
    cur = lh0_ref[...]
    for t in toks:
        rows = slice(t * bb, (t + 1) * bb)
        cur = a_s[rows] * cur + u_s[rows]
        y_ref[t, :, SSD_WIDTH + ML_WIDTH:MIX_WIDTH] = cur * _gelu_tanh(useg(t, P_GR, LRU_WIDTH))
    lh_ref[...] = cur


def _smix(l, x, mod, states, prev, w, bb):
    t, b, d = x.shape

    def sblk(a, ax):
        shape = tuple(None if d == 0 else bb if d == ax else n for d, n in enumerate(a.shape))
        return pl.BlockSpec(shape, lambda i: tuple(l if d == 0 else i if d == ax else 0
                                                   for d in range(a.ndim)))

    ublk = lambda n: pl.BlockSpec((t, bb, n), lambda i: (0, i, 0))
    outs = [jax.ShapeDtypeStruct((t, b, MIX_WIDTH), f32)]
    outs += [jax.ShapeDtypeStruct(a.shape, f32) for a, _ in states]
    weights = [w["ssd_conv_w"], w["ssd_conv_b"], w["s_dt_bias"], w["s_alog"], w["dskip"],
               w["ssd_norm_w"], w["s_i_bias"], w["s_f_bias"], w["ml_norm_w"],
               w["lru_conv_w"], w["lru_conv_b"], w["lru_wax"], w["lru_ba"], w["lru_bx"],
               w["lru_lambda"]]
    n_in = 5 + len(states) + len(weights)
    prev = list(prev)

    def body(*refs):
        _smix_kernel(*refs[:n_in], *refs[n_in + len(prev):])

    return pl.pallas_call(
        body,
        out_shape=outs,
        grid=(b // bb,),
        in_specs=[ublk(d), pl.BlockSpec((None, 6, bb, d), lambda i: (l, 0, i, 0)),
                  _layer(w["norm1_w"], l),
                  pl.BlockSpec((None, d, P_MAIN), lambda i: (l, 0, 0),
                               pipeline_mode=pl.Buffered(1)),
                  _layer(w["w_exp"], l, resident=True)]
        + [sblk(a, ax) for a, ax in states]
        + [_layer(a, l) for a in weights]
        + [pl.BlockSpec(memory_space=pl.ANY)] * len(prev),
        out_specs=[ublk(MIX_WIDTH)] + [sblk(a, ax) for a, ax in states],
        input_output_aliases={n_in + k: 1 + k for k in range(len(prev))},
        scratch_shapes=[pltpu.VMEM((t, bb, NS_COLS), f32)],
        compiler_params=pltpu.CompilerParams(
            dimension_semantics=("arbitrary",), vmem_limit_bytes=VMEM_LIMIT),
        name="sample_mix",
    )(x, mod, w["norm1_w"], w["w_in_p"], w["w_exp"], *[a for a, _ in states], *weights, *prev)


def _prep_params(p):
    w_p, w_exp = _pack_w_in(p["w_in"])
    r = lambda a: a.reshape(N_LAYERS, 1, -1)
    rep = lambda a, n: r(jnp.repeat(a, n, axis=-1))
    small = jnp.concatenate([p["ssd_dt_bias"], p["ml_i_bias"], p["ml_f_bias"]], axis=-1)
    return dict(
        w_in_p=w_p, w_exp=w_exp,
        norm1_w=r(p["norm1_w"]), norm2_w=r(p["norm2_w"]),
        ssd_conv_w=p["ssd_conv_w"], ssd_conv_b=r(p["ssd_conv_b"]),
        p_small_bias=r(jnp.pad(small, ((0, 0), (0, V7X_LANES - small.shape[-1])))),
        p_alog=r(jnp.pad(p["ssd_a_log"], ((0, 0), (0, V7X_LANES - SSD_HEADS)))),
        s_dt_bias=rep(p["ssd_dt_bias"], SSD_HEAD_DIM), s_alog=rep(p["ssd_a_log"], SSD_HEAD_DIM),
        s_i_bias=rep(p["ml_i_bias"], ML_HEAD_DIM), s_f_bias=rep(p["ml_f_bias"], ML_HEAD_DIM),
        dskip=rep(p["ssd_d"], SSD_HEAD_DIM),
        ssd_norm_w=r(p["ssd_norm_w"]), ml_norm_w=r(p["ml_norm_w"]),
        lru_conv_w=p["lru_conv_w"], lru_conv_b=r(p["lru_conv_b"]),
        lru_wax=jnp.concatenate([p["lru_wa"], p["lru_wx"]], axis=-1).astype(bf16),
        lru_ba=r(p["lru_ba"]), lru_bx=r(p["lru_bx"]), lru_lambda=r(p["lru_lambda"]),
        w_out=_cast_pad(p["w_out"]), mlp_up=_cast_pad(p["mlp_up"]),
        mlp_down=_cast_pad(p["mlp_down"]),
    )


def kernel(x_prompt, x_sample, c_prompt, c_sample, state_ssm, state_ssd_conv, state_mlstm_c, state_mlstm_n, state_mlstm_m, state_lru_h, state_lru_conv, ada_w, ada_b, norm1_w, norm2_w, w_in, ssd_conv_w, ssd_conv_b, ssd_dt_bias, ssd_a_log, ssd_d, ssd_norm_w, ml_i_bias, ml_f_bias, ml_norm_w, lru_conv_w, lru_conv_b, lru_wa, lru_ba, lru_wx, lru_bx, lru_lambda, w_out, mlp_up, mlp_down, final_norm_w):
    p = dict(norm1_w=norm1_w, norm2_w=norm2_w, w_in=w_in, ssd_conv_w=ssd_conv_w,
             ssd_conv_b=ssd_conv_b, ssd_dt_bias=ssd_dt_bias, ssd_a_log=ssd_a_log, ssd_d=ssd_d,
             ssd_norm_w=ssd_norm_w, ml_i_bias=ml_i_bias, ml_f_bias=ml_f_bias,
             ml_norm_w=ml_norm_w, lru_conv_w=lru_conv_w, lru_conv_b=lru_conv_b, lru_wa=lru_wa,
             lru_ba=lru_ba, lru_wx=lru_wx, lru_bx=lru_bx, lru_lambda=lru_lambda, w_out=w_out,
             mlp_up=mlp_up, mlp_down=mlp_down)
    w = _prep_params(p)
    fnw = final_norm_w.reshape(1, D_MODEL)
    bp = x_prompt.shape[0]
    bs_ = x_sample.shape[0]

    mod = _ada(jnp.concatenate([c_sample, c_prompt], axis=0), ada_w, ada_b)
    mod = mod.reshape(N_LAYERS, bs_ + bp, 6, D_MODEL)

    xp = _interleave_rows(x_prompt)
    p_out = [jnp.zeros((N_LAYERS, bp) + s, f32) for s in (
        (SSD_HEADS // 2, 2 * SSD_HEAD_DIM, SSD_STATE),
        (CONV_W - 1, SSD_CONV_DIM), (ML_HEADS, ML_HEAD_DIM, ML_HEAD_DIM),
        (ML_HEADS, ML_HEAD_DIM), (ML_HEADS, V7X_LANES),
        (1, LRU_WIDTH), (CONV_W - 1, LRU_WIDTH))]
    big_states = ((N_LAYERS, bs_, SSD_HEADS * SSD_HEAD_DIM, SSD_STATE),
                  (N_LAYERS, bs_, ML_HEADS * ML_HEAD_DIM, ML_HEAD_DIM))
    for l in range(N_LAYERS):
        res = _player(l, xp, mod, bs_, w, fnw, l == N_LAYERS - 1, p_out,
                      big_states if l == 0 else ())
        xp, p_out = res[0], list(res[1:1 + N_STATE_OUTS])
        if l == 0:
            ssm_buf, c_buf = res[1 + N_STATE_OUTS:]
    xp = _deinterleave_rows(xp)
    ssm, sconv, mc, mn, mm, lh, lconv = p_out
    p_states = (ssm.reshape(N_LAYERS, bp, SSD_HEADS, SSD_HEAD_DIM, SSD_STATE), sconv, mc, mn,
                mm[..., 0], lh.reshape(N_LAYERS, bp, LRU_WIDTH), lconv)

    tok_major = lambda a: jnp.swapaxes(a, -3, -2)
    xs = tok_major(x_sample)
    mod_s = tok_major(mod[:, :bs_])
    st_in = (
        (state_ssm.reshape(N_LAYERS, bs_, SSD_HEADS * SSD_HEAD_DIM, SSD_STATE), 1),
        (tok_major(state_ssd_conv), 2),
        (state_mlstm_c.reshape(N_LAYERS, bs_, ML_HEADS * ML_HEAD_DIM, ML_HEAD_DIM), 1),
        (state_mlstm_n.reshape(N_LAYERS, bs_, ML_WIDTH), 1),
        (jnp.repeat(state_mlstm_m, ML_HEAD_DIM, axis=-1), 1),
        (state_lru_h, 1),
        (tok_major(state_lru_conv), 2),
    )
    st_out = [jnp.zeros(a.shape, f32) for a, _ in st_in]
    st_out[0], st_out[2] = ssm_buf, c_buf
    for l in range(N_LAYERS):
        ycat, *st_out = _smix(l, xs, mod_s, st_in, st_out, w, 8)
        xs = _outmlp(l, xs, ycat, mod_s, w["w_out"], w["norm2_w"], w["mlp_up"], w["mlp_down"],
                     fnw, 64, l == N_LAYERS - 1)
    xs = tok_major(xs)
    ssm, sconv, mc, mn, mm, lh, lconv = st_out
    s_states = (ssm.reshape(N_LAYERS, bs_, SSD_HEADS, SSD_HEAD_DIM, SSD_STATE), tok_major(sconv),
                mc.reshape(N_LAYERS, bs_, ML_HEADS, ML_HEAD_DIM, ML_HEAD_DIM),
                mn.reshape(N_LAYERS, bs_, ML_HEADS, ML_HEAD_DIM),
                mm.reshape(N_LAYERS, bs_, ML_HEADS, ML_HEAD_DIM)[..., 0],
                lh, tok_major(lconv))

    return (xp, xs) + p_states + s_states
```

```python
import functools
import math

import jax
import jax.numpy as jnp
from jax import lax
from jax.experimental import pallas as pl
from jax.experimental.pallas import tpu as pltpu

f32 = jnp.float32
bf16 = jnp.bfloat16

D_MODEL = 1024
N_LAYERS = 2
MIX_WIDTH = 2 * D_MODEL
SSD_WIDTH = 1024
SSD_HEAD_DIM = 64
SSD_HEADS = 16
SSD_GROUPS = 2
SSD_STATE = 128
SSD_CONV_DIM = SSD_WIDTH + 2 * SSD_GROUPS * SSD_STATE
ML_WIDTH = 512
ML_HEADS = 4
ML_HEAD_DIM = 128
LRU_WIDTH = 512
LRU_BLOCKS = 4
LRU_BLOCK_DIM = 128
LRU_C = 8.0
CONV_W = 4
CHUNK = 128
D_FF = 4 * D_MODEL
EPS = 1e-6
PAST_LEN = 16384
DEC_SEQ = 8

_O_Z, _O_XBC, _O_DT, _O_Q, _O_O_END, _O_I, _O_F, _O_XR, _O_END = (
    0, 1024, 2560, 2576, 4624, 4624, 4628, 4632, 5656)

P_Z = 0
P_XBC = 1024
P_Q = 2560
P_K = 3072
P_V = 3584
P_O = 4096
P_XR = 4608
P_GR = 5120
P_MAIN = 5632
P_SMALL = 5632
NP_COLS = 5760
S_DT = 5632
S_I = 6656
S_F = 7168
NS_COLS = 7680

V7X_LANES = 128
V7X_SUBLANES = 8
VMEM_LIMIT = 56 * 1024 * 1024
PLAYER_VMEM_LIMIT = 60 * 1024 * 1024
ROW_TILES = CHUNK // V7X_SUBLANES
LOG2E = 1.4426950408889634

_NT = (((1,), (1,)), ((), ()))
_TN = (((0,), (0,)), ((), ()))


def _softplus(x):
    return jnp.maximum(x, 0.0) + jnp.log1p(jnp.exp(-jnp.abs(x)))


def _log_sigmoid(x):
    return -_softplus(-x)


def _sigmoid(x):
    return 1.0 / (1.0 + jnp.exp(-x))


def _silu(x):
    return x * _sigmoid(x)


def _gelu_tanh(x):
    c = math.sqrt(2.0 / math.pi)
    return 0.5 * x * (1.0 + jnp.tanh(c * (x + 0.044715 * (x * x * x))))


def _neg_expm1(x):
    return -jnp.tanh(0.5 * x) * (jnp.exp(x) + 1.0)


def _rms(x, w):
    return x * lax.rsqrt(jnp.mean(x * x, axis=-1, keepdims=True) + EPS) * w


def _interleave_rows(x):
    b, t, d = x.shape
    return x.reshape(b, t // CHUNK, V7X_SUBLANES, ROW_TILES, d).swapaxes(2, 3).reshape(b, t, d)


def _deinterleave_rows(x):
    b, t, d = x.shape
    return x.reshape(b, t // CHUNK, ROW_TILES, V7X_SUBLANES, d).swapaxes(2, 3).reshape(b, t, d)


def _full(shape):
    n = len(shape)
    return pl.BlockSpec(shape, lambda *_: (0,) * n)


def _layer(a, l, resident=False):
    nd = a.ndim - 1
    kw = dict(pipeline_mode=pl.Buffered(1)) if resident else {}
    return pl.BlockSpec((None,) + a.shape[1:], lambda *_: (l,) + (0,) * nd, **kw)


def _ada_kernel(c_ref, w_ref, b_ref, o_ref):
    kk = pl.program_id(1)

    @pl.when(kk == 0)
    def _():
        o_ref[0] = jnp.broadcast_to(b_ref[0], o_ref.shape[1:])

    cs = _silu(c_ref[...]).astype(bf16)
    o_ref[0] += jnp.dot(cs, w_ref[0].astype(bf16), preferred_element_type=f32)


def _ada(c_all, ada_w, ada_b):
    n = c_all.shape[0]
    tk = 256
    return pl.pallas_call(
        _ada_kernel,
        out_shape=jax.ShapeDtypeStruct((N_LAYERS, n, 6 * D_MODEL), f32),
        grid=(N_LAYERS, D_MODEL // tk),
        in_specs=[
            pl.BlockSpec((n, tk), lambda l, kk: (0, kk)),
            pl.BlockSpec((1, tk, 6 * D_MODEL), lambda l, kk: (l, kk, 0)),
            pl.BlockSpec((1, 1, 6 * D_MODEL), lambda l, kk: (l, 0, 0)),
        ],
        out_specs=pl.BlockSpec((1, n, 6 * D_MODEL), lambda l, kk: (l, 0, 0)),
        compiler_params=pltpu.CompilerParams(
            dimension_semantics=("arbitrary", "arbitrary"), vmem_limit_bytes=VMEM_LIMIT),
        name="ada_mod",
    )(c_all, ada_w, ada_b.reshape(N_LAYERS, 1, 6 * D_MODEL))


def _inproj_kernel(x_ref, mod_ref, nw_ref, wm_ref, we_ref, o_ref):
    x = x_ref[...]
    t, bb, d = x.shape
    hn = _rms(x, nw_ref[...]) * (1.0 + mod_ref[1]) + mod_ref[0]
    hn = hn.reshape(t * bb, d).astype(bf16)
    o_ref[:, :, 0:P_MAIN] = jnp.dot(hn, wm_ref[...], preferred_element_type=f32).reshape(
        t, bb, P_MAIN)
    o_ref[:, :, P_MAIN:NS_COLS] = jnp.dot(hn, we_ref[...], preferred_element_type=f32).reshape(
        t, bb, NS_COLS - P_MAIN)


def _inproj(l, x, mod, nw, w_packed, w_exp, bb):
    t, b, d = x.shape
    return pl.pallas_call(
        _inproj_kernel,
        out_shape=jax.ShapeDtypeStruct((t, b, NS_COLS), f32),
        grid=(b // bb,),
        in_specs=[
            pl.BlockSpec((t, bb, d), lambda i: (0, i, 0)),
            pl.BlockSpec((None, 6, bb, d), lambda i: (l, 0, i, 0)),
            _layer(nw, l),
            pl.BlockSpec((None, d, P_MAIN), lambda i: (l, 0, 0), pipeline_mode=pl.Buffered(1)),
            _layer(w_exp, l, resident=True),
        ],
        out_specs=pl.BlockSpec((t, bb, NS_COLS), lambda i: (0, i, 0)),
        compiler_params=pltpu.CompilerParams(
            dimension_semantics=("arbitrary",), vmem_limit_bytes=VMEM_LIMIT),
        name="inproj",
    )(x, mod, nw, w_packed, w_exp)


PACK_ROWS = 128


def _pack_w_in_kernel(w_ref, wp_ref, we_ref):
    c = lambda a: a.astype(bf16)
    wp_ref[:, 0:P_Q] = c(w_ref[:, 0:_O_DT])
    wp_ref[:, P_Q:P_XR] = c(w_ref[:, _O_Q:_O_O_END])
    wp_ref[:, P_XR:P_MAIN] = c(w_ref[:, _O_XR:_O_END])
    dt = w_ref[:, _O_DT:_O_Q]
    gates = w_ref[:, _O_I:_O_XR]
    n_pad = V7X_LANES - SSD_HEADS - 2 * ML_HEADS
    wp_ref[:, P_SMALL:NP_COLS] = c(jnp.concatenate(
        [dt, gates, jnp.zeros((PACK_ROWS, n_pad), f32)], axis=1))
    rep = lambda a, j, n: jnp.broadcast_to(a[:, j:j + 1], (PACK_ROWS, n))
    we_ref[...] = c(jnp.concatenate(
        [rep(dt, h, SSD_HEAD_DIM) for h in range(SSD_HEADS)]
        + [rep(gates, h, ML_HEAD_DIM) for h in range(2 * ML_HEADS)], axis=1))


def _pack_w_in(w_in):
    n_l, d, n = w_in.shape
    blk = lambda m: pl.BlockSpec((None, PACK_ROWS, m), lambda l, i: (l, i, 0))
    return pl.pallas_call(
        _pack_w_in_kernel,
        out_shape=[jax.ShapeDtypeStruct((n_l, d, NP_COLS), bf16),
                   jax.ShapeDtypeStruct((n_l, d, NS_COLS - P_MAIN), bf16)],
        grid=(n_l, d // PACK_ROWS),
        in_specs=[blk(n)],
        out_specs=[blk(NP_COLS), blk(NS_COLS - P_MAIN)],
        compiler_params=pltpu.CompilerParams(
            dimension_semantics=("arbitrary", "arbitrary"), vmem_limit_bytes=VMEM_LIMIT),
        name="pack_w_in",
    )(w_in)


def _cast_pad_kernel(w_ref, o_ref):
    n = w_ref.shape[-1]
    o_ref[:, 0:n] = w_ref[...].astype(bf16)
    o_ref[:, n:] = jnp.zeros((w_ref.shape[0], o_ref.shape[-1] - n), bf16)


def _cast_pad(a):
    n_l, k, n = a.shape
    rows = 512
    return pl.pallas_call(
        _cast_pad_kernel,
        out_shape=jax.ShapeDtypeStruct((n_l, k, n + V7X_LANES), bf16),
        grid=(n_l, k // rows),
        in_specs=[pl.BlockSpec((None, rows, n), lambda l, i: (l, i, 0))],
        out_specs=pl.BlockSpec((None, rows, n + V7X_LANES), lambda l, i: (l, i, 0)),
        compiler_params=pltpu.CompilerParams(
            dimension_semantics=("arbitrary", "arbitrary"), vmem_limit_bytes=VMEM_LIMIT),
        name="cast_pad",
    )(a)


def _outmlp_kernel(x_ref, y_ref, mod_ref, wout_ref, n2w_ref, up_ref, down_ref, fnw_ref, o_ref,
                   *, final):
    x = x_ref[...]
    t, bb, d = x.shape
    m = t * bb
    ycat = y_ref[...].reshape(m, MIX_WIDTH).astype(bf16)
    mix = jnp.dot(ycat, wout_ref[:, 0:d], preferred_element_type=f32).reshape(t, bb, d)
    x1 = x + mod_ref[2] * mix
    hn = _rms(x1, n2w_ref[...]) * (1.0 + mod_ref[4]) + mod_ref[3]
    hn = hn.reshape(m, d).astype(bf16)
    ff = jnp.zeros((m, d), f32)
    fc = 1024
    for c in range(D_FF // fc):
        h = jnp.dot(hn, up_ref[:, c * fc:(c + 1) * fc], preferred_element_type=f32)
        h = jnp.square(jnp.maximum(h, 0.0)).astype(bf16)
        ff = ff + jnp.dot(h, down_ref[c * fc:(c + 1) * fc, 0:D_MODEL],
                          preferred_element_type=f32)
    x2 = x1 + mod_ref[5] * ff.reshape(t, bb, d)
    if final:
        x2 = _rms(x2, fnw_ref[...])
    o_ref[...] = x2


def _outmlp(l, x, ycat, mod, wout, n2w, up, down, fnw, bb, final):
    t, b, d = x.shape
    return pl.pallas_call(
        functools.partial(_outmlp_kernel, final=final),
        out_shape=jax.ShapeDtypeStruct((t, b, d), f32),
        grid=(b // bb,),
        in_specs=[
            pl.BlockSpec((t, bb, d), lambda i: (0, i, 0)),
            pl.BlockSpec((t, bb, MIX_WIDTH), lambda i: (0, i, 0)),
            pl.BlockSpec((None, 6, bb, d), lambda i: (l, 0, i, 0)),
            _layer(wout, l, resident=True),
            _layer(n2w, l),
            _layer(up, l, resident=True),
            _layer(down, l, resident=True),
            _full((1, d)),
        ],
        out_specs=pl.BlockSpec((t, bb, d), lambda i: (0, i, 0)),
        compiler_params=pltpu.CompilerParams(
            dimension_semantics=("arbitrary",), vmem_limit_bytes=VMEM_LIMIT),
        name="outproj_mlp",
    )(x, ycat, mod, wout, n2w, up, down, fnw)


PIECE = 512


def _stage_in(x, mod_ref, nw_ref, w_ref, u_ref):
    hn = _rms(x, nw_ref[...]) * (1.0 + mod_ref[0, 1:2, :]) + mod_ref[0, 0:1, :]
    hn = hn.astype(bf16)
    yield
    for c0 in range(0, NP_COLS, PIECE):
        c1 = min(c0 + PIECE, NP_COLS)
        u_ref[:, c0:c1] = jnp.dot(hn, w_ref[:, c0:c1], preferred_element_type=f32)
        yield


def _stage_out(x, y_ref, o_ref, rows, mod_ref, wout_ref, n2w_ref, up_ref, down_ref, fnw_ref,
               final):
    halves = [slice(h * PIECE, (h + 1) * PIECE) for h in range(D_MODEL // PIECE)]
    ycat = y_ref[...]
    mix = []
    for hs in halves:
        mix.append(jnp.dot(ycat, wout_ref[:, hs], preferred_element_type=f32))
        yield
    x1 = x + mod_ref[0, 2:3, :] * jnp.concatenate(mix, axis=1)
    hn = (_rms(x1, n2w_ref[...]) * (1.0 + mod_ref[0, 4:5, :]) + mod_ref[0, 3:4, :]).astype(bf16)
    yield
    ff = [jnp.zeros((x.shape[0], PIECE), f32) for _ in halves]
    fc = 2 * PIECE
    for c in range(D_FF // fc):
        h = []
        for k in range(fc // PIECE):
            cols = slice(c * fc + k * PIECE, c * fc + (k + 1) * PIECE)
            hk = jnp.dot(hn, up_ref[:, cols], preferred_element_type=f32)
            h.append(jnp.square(jnp.maximum(hk, 0.0)).astype(bf16))
            yield
        h = jnp.concatenate(h, axis=1)
        for i, hs in enumerate(halves):
            ff[i] = ff[i] + jnp.dot(h, down_ref[c * fc:(c + 1) * fc, hs],
                                    preferred_element_type=f32)
            yield
    x2 = x1 + mod_ref[0, 5:6, :] * jnp.concatenate(ff, axis=1)
    o_ref[0, rows, :] = _rms(x2, fnw_ref[...]) if final else x2
    yield


def _trace_interleaved(main, fillers, n_main, n_fill):
    fillers = list(fillers)
    due = 0.0
    nxt = 0
    main_live = True
    while main_live or fillers:
        if main_live:
            main_live = next(main, _DONE) is not _DONE
            due += n_fill / n_main
        else:
            due = float(len(fillers))
        while fillers and due >= 1.0:
            nxt %= len(fillers)
            if next(fillers[nxt], _DONE) is _DONE:
                fillers.pop(nxt)
            else:
                nxt += 1
                due -= 1.0


_DONE = object()


def _reset_state_if(first, state):
    for ref in state:
        ref[...] = jnp.where(first, 0.0, ref[...])


def _stage_mix(u_ref, y_ref, prm, state, ci, outs):
    (cw_ref, cb_ref, sbias_ref, alog_ref, dskip_ref, snw_ref, mnw_ref,
     lcw_ref, lcb_ref, wax_ref, ba_ref, bx_ref, lam_ref) = prm
    xbuf, lbuf, st, cst, nst, mst, hst = state
    L = CHUNK
    SUB = V7X_SUBLANES
    HIST = (CONV_W - 1) * SUB

    row = lax.broadcasted_iota(jnp.int32, (L, L), 0)
    col = lax.broadcasted_iota(jnp.int32, (L, L), 1)
    tok = lambda i: (i & (SUB - 1)) * ROW_TILES + (i >> 3)
    tri = tok(row) >= tok(col)
    lo = col < SSD_HEAD_DIM

    def bcl(a, j):
        return jnp.broadcast_to(a[:, j:j + 1], (L, L))

    def conv(u0, hist_ref, w_ref, b_ref, c0, c1):
        raw = u_ref[:, u0 + c0:u0 + c1]
        sub = lax.broadcasted_iota(jnp.int32, (SUB, c1 - c0), 0)
        prev = [pltpu.roll(jnp.where(sub == SUB - 1, hist_ref[k * SUB:(k + 1) * SUB, c0:c1],
                                     raw[L - HIST + k * SUB:L - HIST + (k + 1) * SUB, :]), 1, 0)
                for k in range(CONV_W - 1)]
        acc = b_ref[:, c0:c1] + raw * w_ref[CONV_W - 1:CONV_W, c0:c1]
        for d in range(1, CONV_W):
            shifted = jnp.concatenate(prev[CONV_W - 1 - d:] + [raw[0:L - d * SUB, :]], axis=0)
            acc = acc + shifted * w_ref[CONV_W - 1 - d:CONV_W - d, c0:c1]
        hist_ref[:, c0:c1] = raw[L - HIST:L, :]
        return acc

    cblk = 512
    xbc = []
    for c0 in range(0, SSD_CONV_DIM, cblk):
        xbc.append(_silu(conv(P_XBC, xbuf, cw_ref, cb_ref, c0, c0 + cblk)))
        yield
    xbc = jnp.concatenate(xbc, axis=1)
    xr = conv(P_XR, lbuf, lcw_ref, lcb_ref, 0, LRU_WIDTH)
    yield

    sm = u_ref[:,P_SMALL:P_SMALL + V7X_LANES] + sbias_ref[...]
    dt = _softplus(sm)
    a_row = -jnp.exp(alog_ref[...])
    gates = jnp.where(col < SSD_HEADS, dt * (a_row * LOG2E),
                      jnp.where((col >= 20) & (col < 24), _log_sigmoid(sm), 0.0))
    g_hi = gates.astype(bf16)
    g_r = gates - g_hi.astype(f32)
    g_mid = g_r.astype(bf16)
    g_lo = (g_r - g_mid.astype(f32)).astype(bf16)
    cum3 = jnp.dot(jnp.where(tri, 1.0, 0.0).astype(bf16),
                   jnp.concatenate([g_hi, g_mid, g_lo], axis=1), preferred_element_type=f32)
    cum = cum3[:, 0:L] + cum3[:, L:2 * L] + cum3[:, 2 * L:3 * L]
    cum_t = cum.T
    sm_t = sm.T
    yield

    ys = []
    for g in range(SSD_GROUPS):
        bm_t = xbc[:, SSD_WIDTH + g * SSD_STATE:SSD_WIDTH + (g + 1) * SSD_STATE].T.astype(bf16)
        cm = xbc[:, SSD_WIDTH + (SSD_GROUPS + g) * SSD_STATE:
                 SSD_WIDTH + (SSD_GROUPS + g + 1) * SSD_STATE].astype(bf16)
        cb = jnp.dot(cm, bm_t, preferred_element_type=f32)
        inter = jnp.dot(cm, st[g].astype(bf16), preferred_element_type=f32)
        yield
        xends, keeps = [], []
        for jj in range(4):
            jp = g * 4 + jj
            e0, e1 = 2 * jp, 2 * jp + 1
            c0 = bcl(cum, e0)
            c1 = bcl(cum, e1)
            cum_p = jnp.where(lo, c0, c1)
            dt_p = jnp.where(lo, bcl(dt, e0), bcl(dt, e1))
            xs_p = xbc[:, jp * L:(jp + 1) * L]
            xdt = xs_p * dt_p
            dec0 = jnp.exp2(jnp.where(tri, c0 - cum_t[e0:e0 + 1, :], -jnp.inf))
            dec1 = jnp.exp2(jnp.where(tri, c1 - cum_t[e1:e1 + 1, :], -jnp.inf))
            yield
            att =jnp.concatenate([(cb * dec0).astype(bf16), (cb * dec1).astype(bf16)], axis=1)
            x2 = jnp.concatenate([jnp.where(lo, xdt, 0.0).astype(bf16),
                                  jnp.where(lo, 0.0, xdt).astype(bf16)], axis=0)
            y_p = (jnp.dot(att, x2, preferred_element_type=f32)
                   + inter[:, jj * L:(jj + 1) * L] * jnp.exp2(cum_p)
                   + xs_p * dskip_ref[:, jp * L:(jp + 1) * L])
            ys.append(y_p)
            c_last = cum_p[L - 1:L, :]
            xends.append((xdt * jnp.exp2(c_last - cum_p)).astype(bf16))
            keeps.append(jnp.exp2(c_last))
            yield
        st[g] = (jnp.concatenate(keeps, axis=1) * st[g]
                 + jnp.dot(bm_t, jnp.concatenate(xends, axis=1), preferred_element_type=f32))
    y = jnp.concatenate(ys, axis=1)
    z = u_ref[:,P_Z:P_Z + SSD_WIDTH]
    y_ref[:,0:SSD_WIDTH] = _rms(y * _silu(z), snw_ref[...]).astype(bf16)
    yield

    scale = ML_HEAD_DIM ** -0.5
    for h in range(ML_HEADS):
        sl = slice(h * ML_HEAD_DIM, (h + 1) * ML_HEAD_DIM)
        q = u_ref[:,P_Q + h * ML_HEAD_DIM:P_Q + (h + 1) * ML_HEAD_DIM]
        k = u_ref[:,P_K + h * ML_HEAD_DIM:P_K + (h + 1) * ML_HEAD_DIM] * scale
        v = u_ref[:,P_V + h * ML_HEAD_DIM:P_V + (h + 1) * ML_HEAD_DIM]
        o = u_ref[:,P_O + h * ML_HEAD_DIM:P_O + (h + 1) * ML_HEAD_DIM]
        k_t = k.T.astype(bf16)
        jf = 20 + h
        ji = 16 + h
        b_col = cum[:, jf:jf + 1]
        i_col = sm[:, ji:ji + 1]
        dm = jnp.where(tri, bcl(cum, jf) - cum_t[jf:jf + 1, :] + sm_t[ji:ji + 1, :], -jnp.inf)
        m_prev = mst[h, :, 0:1]
        g_col = b_col + m_prev
        mt = jnp.maximum(g_col, jnp.max(dm, axis=-1, keepdims=True))
        yield
        w = jnp.exp(dm - mt) * jnp.dot(q.astype(bf16), k_t, preferred_element_type=f32)
        inter = jnp.exp(g_col - mt)
        yield
        ct_old = cst[h]
        n_old = nst[h]
        num = jnp.dot(jnp.concatenate([w.astype(bf16), (inter * q).astype(bf16)], axis=1),
                      jnp.concatenate([v.astype(bf16), ct_old.astype(bf16)], axis=0),
                      preferred_element_type=f32)
        den = (jnp.sum(w, axis=-1, keepdims=True)
               + inter * jnp.sum(q * n_old, axis=-1, keepdims=True))
        hout = num / jnp.maximum(jnp.abs(den), jnp.exp(-mt))
        yield
        m_new = mt[L - 1:L, :]
        b_last = b_col[L - 1:L, :]
        w_end = jnp.exp(b_last - b_col + i_col - m_new)
        dc = jnp.exp(b_last + m_prev - m_new)
        cst[h] = dc * ct_old + jnp.dot(k_t, (v * w_end).astype(bf16),
                                       preferred_element_type=f32)
        nst[h] = dc * n_old + jnp.sum(w_end * k, axis=0, keepdims=True)
        mst[h] = jnp.broadcast_to(m_new, (1, V7X_LANES))
        yh = _rms(hout, mnw_ref[:, sl]) * _sigmoid(o)
        y_ref[:,SSD_WIDTH + h * ML_HEAD_DIM:SSD_WIDTH + (h + 1) * ML_HEAD_DIM] = yh.astype(bf16)
        yield

    rs, is_ = [], []
    for kb_ in range(LRU_BLOCKS):
        xk = xr[:, kb_ * LRU_BLOCK_DIM:(kb_ + 1) * LRU_BLOCK_DIM].astype(bf16)
        ri = jnp.dot(xk, wax_ref[kb_], preferred_element_type=f32)
        rs.append(ri[:, 0:LRU_BLOCK_DIM])
        is_.append(ri[:, LRU_BLOCK_DIM:2 * LRU_BLOCK_DIM])
    r = _sigmoid(jnp.concatenate(rs, axis=1) + ba_ref[...])
    i_g = _sigmoid(jnp.concatenate(is_, axis=1) + bx_ref[...])
    yield
    log_a = -LRU_C * r * _softplus(-lam_ref[...])
    mult = jnp.sqrt(_neg_expm1(2.0 * log_a))
    rowl = lax.broadcasted_iota(jnp.int32, (L, LRU_WIDTH), 0)
    first_row = jnp.where(ci == 0, 0, -1)
    mult = jnp.where(rowl == first_row, 1.0, mult)
    a_s = jnp.exp(log_a)
    u_s = mult * i_g * xr
    yield
    tile = lambda a, v: a[v * SUB:(v + 1) * SUB, :]
    h_loc = [tile(u_s, 0)]
    a_cum = [tile(a_s, 0)]
    for v in range(1, ROW_TILES):
        h_loc.append(tile(a_s, v) * h_loc[-1] + tile(u_s, v))
        a_cum.append(tile(a_s, v) * a_cum[-1])
    a_e, h_e = a_cum[-1], h_loc[-1]
    subl = lax.broadcasted_iota(jnp.int32, (SUB, LRU_WIDTH), 0)
    k_ = 1
    while k_ < SUB:
        keep = subl >= k_
        a_sh = jnp.where(keep, pltpu.roll(a_e, k_, 0), 1.0)
        h_sh = jnp.where(keep, pltpu.roll(h_e, k_, 0), 0.0)
        h_e = a_e * h_sh + h_e
        a_e = a_e * a_sh
        k_ *= 2
    h0 = hst[...]
    run_end = h_e + a_e * h0
    run_in = jnp.where(subl == 0, h0, pltpu.roll(run_end, 1, 0))
    hr = jnp.concatenate([h_loc[v] + a_cum[v] * run_in for v in range(ROW_TILES)], axis=0)
    hst[...] = run_end[SUB - 1:SUB, :]
    gr = u_ref[:,P_GR:P_GR + LRU_WIDTH]
    y_ref[:,SSD_WIDTH + ML_WIDTH:MIX_WIDTH] = (hr * _gelu_tanh(gr)).astype(bf16)
    yield

    if outs is not None:
        ssm_ref, sconv_ref, mc_ref, mn_ref, mm_ref, lh_ref, lconv_ref = outs
        for g in range(SSD_GROUPS):
            for jj in range(4):
                ssm_ref[0, g * 4 + jj] = st[g, :, jj * L:(jj + 1) * L].T
        for k in range(CONV_W - 1):
            sconv_ref[0, k:k + 1, :] = xbuf[k * SUB + SUB - 1:(k + 1) * SUB, :]
            lconv_ref[0, k:k + 1, :] = lbuf[k * SUB + SUB - 1:(k + 1) * SUB, :]
        for h in range(ML_HEADS):
            mc_ref[0, h] = cst[h].T
            mn_ref[0, h:h + 1, :] = nst[h]
            mm_ref[0, h:h + 1, :] = mst[h]
        lh_ref[0] = hst[...]


N_MIX_PARAMS = 13
N_STATE_OUTS = 7
N_MIX_PHASES = 44
N_FILL_PIECES = (1 + -(-NP_COLS // PIECE)) + (D_MODEL // PIECE + 1
                                              + (D_FF // (2 * PIECE)) * (2 + D_MODEL // PIECE) + 1)


def _player_kernel(*refs, final, n_chunks, nc, n_clear):
    xa_ref, moda_ref, xc_ref, modc_ref, n1w_ref, win_ref = refs[:6]
    prm = refs[6:6 + N_MIX_PARAMS]
    wout_ref, n2w_ref, up_ref, down_ref, fnw_ref = refs[6 + N_MIX_PARAMS:11 + N_MIX_PARAMS]
    n_in = 11 + N_MIX_PARAMS + N_STATE_OUTS
    o_ref = refs[n_in]
    outs = refs[n_in + 1:n_in + 1 + N_STATE_OUTS]
    clear = refs[n_in + 1 + N_STATE_OUTS:n_in + 1 + N_STATE_OUTS + n_clear]
    u_bufs = refs[-11:-9]
    y_bufs = refs[-9:-7]
    state = refs[-7:]
    k = pl.program_id(0)
    n_steps = n_chunks // 2

    @pl.when(k == 0)
    def _():
        u_bufs[1][...] = jnp.zeros(u_bufs[1].shape, f32)
        y_bufs[1][...] = jnp.zeros(y_bufs[1].shape, bf16)
        for ref in state:
            ref[...] = jnp.zeros(ref.shape, f32)
        if n_clear:
            refs[-13][...] = jnp.zeros(refs[-13].shape, f32)

    def clear_copies():
        zeros, sems = refs[-13], refs[-12]
        copies = []
        for ci_, buf in enumerate(clear):
            n_b, rows = buf.shape[1], buf.shape[2]
            per_step = buf.shape[0] * n_b // n_steps
            for i in range(per_step):
                j = k * per_step + i
                copies.append(pltpu.make_async_copy(
                    zeros.at[0:rows], buf.at[j // n_b, j % n_b], sems.at[ci_, i]))
        return copies

    if n_clear:
        @pl.when(k < n_steps)
        def _():
            for cp in clear_copies():
                cp.start()

    for sub in range(2):
        rows = slice(sub * CHUNK, (sub + 1) * CHUNK)
        stage_in = _stage_in(xa_ref[0, rows, :], moda_ref, n1w_ref, win_ref, u_bufs[sub])
        jb = jnp.clip(2 * k + sub - 1, 0, n_chunks - 1)
        ci = jb % nc
        if sub == 1:
            _reset_state_if(ci == 0, state)
        stage_mix = _stage_mix(u_bufs[1 - sub], y_bufs[sub], prm, state, ci,
                               outs if sub == 0 else None)
        stage_out = _stage_out(xc_ref[0, rows, :], y_bufs[1 - sub], o_ref, rows, modc_ref,
                               wout_ref, n2w_ref, up_ref, down_ref, fnw_ref, final)
        _trace_interleaved(stage_mix, [stage_in, stage_out], N_MIX_PHASES, N_FILL_PIECES)

    if n_clear:
        @pl.when(k < n_steps)
        def _():
            for cp in clear_copies():
                cp.wait()


CLEAR_ROWS = 1024


def _player(l, x, mod, mod_row0, w, fnw, final, prev, clear_shapes=()):
    b, t, d = x.shape
    nc = t // CHUNK
    n_chunks = b * nc
    n_pairs = n_chunks // 2
    ppr = nc // 2
    pair_a = lambda k: jnp.minimum(k, n_pairs - 1)
    pair_c = lambda k: jnp.clip(k - 1, 0, n_pairs - 1)
    seq_b = lambda k: jnp.clip(2 * k - 1, 0, n_chunks - 1) // nc
    xspec = lambda pair: pl.BlockSpec((1, 2 * CHUNK, d), lambda k: (pair(k) // ppr, pair(k) % ppr, 0))
    mspec = lambda pair: pl.BlockSpec((None, 1, 6, d),
                                      lambda k: (l, mod_row0 + pair(k) // ppr, 0, 0))
    st_spec = lambda s: pl.BlockSpec((None, 1) + s[2:],
                                     lambda k: (l, seq_b(k)) + (0,) * (len(s) - 2))
    outs = [jax.ShapeDtypeStruct((b, t, d), f32)]
    outs += [jax.ShapeDtypeStruct(a.shape, f32) for a in prev]
    mix_params = [w["ssd_conv_w"], w["ssd_conv_b"], w["p_small_bias"], w["p_alog"], w["dskip"],
                  w["ssd_norm_w"], w["ml_norm_w"], w["lru_conv_w"], w["lru_conv_b"], w["lru_wax"],
                  w["lru_ba"], w["lru_bx"], w["lru_lambda"]]
    assert len(mix_params) == N_MIX_PARAMS and len(prev) == N_STATE_OUTS
    n_in = 11 + N_MIX_PARAMS
    clear_scratch = []
    if clear_shapes:
        for s in clear_shapes:
            assert s[3] == V7X_LANES and s[2] <= CLEAR_ROWS and (s[0] * s[1]) % n_pairs == 0
        outs += [jax.ShapeDtypeStruct(s, f32) for s in clear_shapes]
        per_step = max(s[0] * s[1] // n_pairs for s in clear_shapes)
        clear_scratch = [pltpu.VMEM((CLEAR_ROWS, V7X_LANES), f32),
                         pltpu.SemaphoreType.DMA((len(clear_shapes), per_step))]
    return pl.pallas_call(
        functools.partial(_player_kernel, final=final, n_chunks=n_chunks, nc=nc,
                          n_clear=len(clear_shapes)),
        out_shape=outs,
        grid=(n_pairs + 1,),
        in_specs=[xspec(pair_a), mspec(pair_a), xspec(pair_c), mspec(pair_c),
                  _layer(w["norm1_w"], l), _layer(w["w_in_p"], l, resident=True)]
        + [_layer(a, l) for a in mix_params]
        + [_layer(w["w_out"], l, resident=True), _layer(w["norm2_w"], l),
           _layer(w["mlp_up"], l, resident=True), _layer(w["mlp_down"], l, resident=True),
           _full((1, d))]
        + [pl.BlockSpec(memory_space=pl.ANY)] * len(prev),
        out_specs=[xspec(pair_c)] + [st_spec(a.shape) for a in prev]
        + [pl.BlockSpec(memory_space=pl.ANY)] * len(clear_shapes),
        input_output_aliases={n_in + i: 1 + i for i in range(len(prev))},
        scratch_shapes=clear_scratch + [
            pltpu.VMEM((CHUNK, NP_COLS), f32), pltpu.VMEM((CHUNK, NP_COLS), f32),
            pltpu.VMEM((CHUNK, MIX_WIDTH), bf16), pltpu.VMEM((CHUNK, MIX_WIDTH), bf16),
            pltpu.VMEM(((CONV_W - 1) * V7X_SUBLANES, SSD_CONV_DIM), f32),
            pltpu.VMEM(((CONV_W - 1) * V7X_SUBLANES, LRU_WIDTH), f32),
            pltpu.VMEM((SSD_GROUPS, SSD_STATE, 512), f32),
            pltpu.VMEM((ML_HEADS, ML_HEAD_DIM, ML_HEAD_DIM), f32),
            pltpu.VMEM((ML_HEADS, 1, ML_HEAD_DIM), f32),
            pltpu.VMEM((ML_HEADS, 1, V7X_LANES), f32),
            pltpu.VMEM((1, LRU_WIDTH), f32),
        ],
        compiler_params=pltpu.CompilerParams(
            dimension_semantics=("arbitrary",), vmem_limit_bytes=PLAYER_VMEM_LIMIT),
        name="prompt_layer",
    )(x, mod, x, mod, w["norm1_w"], w["w_in_p"], *mix_params,
      w["w_out"], w["norm2_w"], w["mlp_up"], w["mlp_down"], fnw, *prev)


def _smix_kernel(u_ref, s0_ref, sconv0_ref, c0_ref, n0_ref, m0_ref, lh0_ref, lconv0_ref,
                 cw_ref, cb_ref, dtb_ref, alog_ref, dskip_ref, snw_ref, ib_ref, fb_ref, mnw_ref,
                 lcw_ref, lcb_ref, wax_ref, ba_ref, bx_ref, lam_ref,
                 y_ref, s_ref, sconv_ref, c_ref, n_ref, m_ref, lh_ref, lconv_ref):
    T = DEC_SEQ
    bb = u_ref.shape[1]
    H = ML_HEAD_DIM
    toks = range(T)

    def useg(t, c0, n):
        return u_ref[t, :, c0:c0 + n]

    def to_seq(rows):
        return jnp.swapaxes(jnp.stack(rows, axis=0), 0, 1)

    def to_tok(a):
        a = jnp.swapaxes(a, 0, 1)
        return [a[t] for t in toks]

    def conv(c0, n, hist_ref, w_ref, b_ref, new_ref):
        xpad = [hist_ref[k] for k in range(CONV_W - 1)] + [useg(t, c0, n) for t in toks]
        for k in range(CONV_W - 1):
            new_ref[k] = xpad[T + k]
        out = []
        for t in toks:
            acc = b_ref[...] + xpad[t] * w_ref[0:1, :]
            for j in range(1, CONV_W):
                acc = acc + xpad[t + j] * w_ref[j:j + 1, :]
            out.append(acc)
        return out

    def split3(d):
        hi = d.astype(bf16).astype(f32)
        r1 = d - hi
        mid = r1.astype(bf16).astype(f32)
        return [hi, mid, (r1 - mid).astype(bf16).astype(f32)]

    def update_lhs(rows, decay):
        pad = [jnp.zeros_like(decay)] * (T - 3)
        return to_seq(rows + split3(decay) + pad).astype(bf16)

    def ones_rhs(kmat):
        zeros = jnp.zeros_like(kmat)
        return jnp.concatenate(
            [jnp.concatenate([kmat, zeros], axis=2),
             jnp.concatenate([zeros, jnp.ones_like(kmat)], axis=2)], axis=1).astype(bf16)

    xbc = [_silu(a) for a in conv(P_XBC, SSD_CONV_DIM, sconv0_ref, cw_ref, cb_ref, sconv_ref)]
    xs = [a[:, 0:SSD_WIDTH] for a in xbc]
    seg = lambda a, j: a[:, SSD_WIDTH + j * SSD_STATE:SSD_WIDTH + (j + 1) * SSD_STATE]
    bm = [[seg(a, g) for a in xbc] for g in range(SSD_GROUPS)]
    cm = [[seg(a, SSD_GROUPS + g) for a in xbc] for g in range(SSD_GROUPS)]
    a_row = -jnp.exp(alog_ref[...]) * LOG2E
    lane = lax.broadcasted_iota(jnp.int32, (bb, SSD_WIDTH), 1)
    cum, xdt = [], []
    for t in toks:
        dt = _softplus(useg(t, S_DT, SSD_WIDTH) + dtb_ref[...])
        cum.append(dt * a_row + (cum[-1] if cum else 0.0))
        xdt.append(xs[t] * dt)
    y = [xs[t] * dskip_ref[...] for t in toks]
    for t in toks:
        for s in range(t + 1):
            cb0 = jnp.sum(cm[0][t] * bm[0][s], axis=-1, keepdims=True)
            cb1 = jnp.sum(cm[1][t] * bm[1][s], axis=-1, keepdims=True)
            term = jnp.where(lane < 512, cb0, cb1) * xdt[s]
            y[t] = y[t] + (term if s == t else term * jnp.exp2(cum[t] - cum[s]))
    inter = []
    for g in range(SSD_GROUPS):
        inter.append(to_tok(jnp.einsum(
            'btn,bpn->btp', to_seq(cm[g]).astype(bf16),
            s0_ref[:, g * 512:(g + 1) * 512, :].astype(bf16), preferred_element_type=f32)))
    for t in toks:
        yt = y[t] + jnp.concatenate([inter[0][t], inter[1][t]], axis=-1) * jnp.exp2(cum[t])
        y_ref[t, :, 0:SSD_WIDTH] = _rms(yt * _silu(useg(t, P_Z, SSD_WIDTH)), snw_ref[...])
    xend = [xdt[s] if s == T - 1 else xdt[s] * jnp.exp2(cum[T - 1] - cum[s]) for s in toks]
    lhs = update_lhs(xend, jnp.exp2(cum[T - 1]))
    for g in range(SSD_GROUPS):
        zz = jnp.einsum('bkp,bkn->bpn', lhs[:, :, g * 512:(g + 1) * 512],
                        ones_rhs(to_seq(bm[g])), preferred_element_type=f32)
        s_ref[:, g * 512:(g + 1) * 512, :] = (
            zz[:, :, SSD_STATE:] * s0_ref[:, g * 512:(g + 1) * 512, :] + zz[:, :, :SSD_STATE])

    def headsum(a):
        return jnp.concatenate(
            [jnp.broadcast_to(jnp.sum(a[:, h * H:(h + 1) * H], axis=-1, keepdims=True), (bb, H))
             for h in range(ML_HEADS)], axis=-1)

    q = [useg(t, P_Q, ML_WIDTH) for t in toks]
    k = [useg(t, P_K, ML_WIDTH) * (H ** -0.5) for t in toks]
    v = [useg(t, P_V, ML_WIDTH) for t in toks]
    ic = [useg(t, S_I, ML_WIDTH) + ib_ref[...] for t in toks]
    bc = []
    for t in toks:
        fc = _log_sigmoid(useg(t, S_F, ML_WIDTH) + fb_ref[...])
        bc.append(fc + (bc[-1] if bc else 0.0))
    m0 = m0_ref[...]
    n0 = n0_ref[...]
    q_seq = to_seq(q).astype(bf16)
    qc = to_tok(jnp.concatenate(
        [jnp.einsum('btk,bvk->btv', q_seq[:, :, h * H:(h + 1) * H],
                    c0_ref[:, h * H:(h + 1) * H, :].astype(bf16), preferred_element_type=f32)
         for h in range(ML_HEADS)], axis=-1))
    mts = []
    for t in toks:
        dm = [bc[t] - bc[s] + ic[s] for s in range(t + 1)]
        g_t = bc[t] + m0
        mt = g_t
        for d in dm:
            mt = jnp.maximum(mt, d)
        mts.append(mt)
        inter_m = jnp.exp(g_t - mt)
        num = inter_m * qc[t]
        den = inter_m * headsum(q[t] * n0)
        for s in range(t + 1):
            w = jnp.exp(dm[s] - mt) * headsum(q[t] * k[s])
            num = num + w * v[s]
            den = den + w
        hout = num / jnp.maximum(jnp.abs(den), jnp.exp(-mt))
        y_ml = (hout * lax.rsqrt(headsum(hout * hout) * (1.0 / H) + EPS) * mnw_ref[...]
                * _sigmoid(useg(t, P_O, ML_WIDTH)))
        y_ref[t, :, SSD_WIDTH:SSD_WIDTH + ML_WIDTH] = y_ml
    m_new = mts[T - 1]
    m_ref[...] = m_new
    w_end = [jnp.exp(bc[T - 1] - bc[s] + ic[s] - m_new) for s in toks]
    dc = jnp.exp(bc[T - 1] + m0 - m_new)
    n_new = dc * n0
    for s in toks:
        n_new = n_new + w_end[s] * k[s]
    n_ref[...] = n_new
    lhs_m = update_lhs([v[s] * w_end[s] for s in toks], dc)
    k_seq = to_seq(k)
    for h in range(ML_HEADS):
        sl = slice(h * H, (h + 1) * H)
        zz = jnp.einsum('bkv,bkn->bvn', lhs_m[:, :, sl], ones_rhs(k_seq[:, :, sl]),
                        preferred_element_type=f32)
        c_ref[:, sl, :] = zz[:, :, H:] * c0_ref[:, sl, :] + zz[:, :, :H]

    xr = jnp.concatenate(conv(P_XR, LRU_WIDTH, lconv0_ref, lcw_ref, lcb_ref, lconv_ref), axis=0)
    rs, is_ = [], []
    for kk in range(LRU_BLOCKS):
        xk = xr[:, kk * LRU_BLOCK_DIM:(kk + 1) * LRU_BLOCK_DIM].astype(bf16)
        ri = jnp.dot(xk, wax_ref[kk], preferred_element_type=f32)
        rs.append(ri[:, 0:LRU_BLOCK_DIM])
        is_.append(ri[:, LRU_BLOCK_DIM:2 * LRU_BLOCK_DIM])
    r = _sigmoid(jnp.concatenate(rs, axis=1) + ba_ref[...])
    i_g = _sigmoid(jnp.concatenate(is_, axis=1) + bx_ref[...])
    log_a = -LRU_C * r * _softplus(-lam_ref[...])
    a_s = jnp.exp(log_a)
    u_s = jnp.sqrt(_neg_expm1(2.0 * log_a)) * i_g * xr
    cur = lh0_ref[...]
    for t in toks:
        rows = slice(t * bb, (t + 1) * bb)
        cur = a_s[rows] * cur + u_s[rows]
        y_ref[t, :, SSD_WIDTH + ML_WIDTH:MIX_WIDTH] = cur * _gelu_tanh(useg(t, P_GR, LRU_WIDTH))
    lh_ref[...] = cur


def _smix(l, u, states, prev, w, bb):
    t, b, _ = u.shape

    def sblk(a, ax):
        shape = tuple(None if d == 0 else bb if d == ax else n for d, n in enumerate(a.shape))
        return pl.BlockSpec(shape, lambda i: tuple(l if d == 0 else i if d == ax else 0
                                                   for d in range(a.ndim)))

    ublk = lambda n: pl.BlockSpec((t, bb, n), lambda i: (0, i, 0))
    outs = [jax.ShapeDtypeStruct((t, b, MIX_WIDTH), f32)]
    outs += [jax.ShapeDtypeStruct(a.shape, f32) for a, _ in states]
    weights = [w["ssd_conv_w"], w["ssd_conv_b"], w["s_dt_bias"], w["s_alog"], w["dskip"],
               w["ssd_norm_w"], w["s_i_bias"], w["s_f_bias"], w["ml_norm_w"],
               w["lru_conv_w"], w["lru_conv_b"], w["lru_wax"], w["lru_ba"], w["lru_bx"],
               w["lru_lambda"]]
    n_in = 1 + len(states) + len(weights)
    prev = list(prev)

    def body(*refs):
        _smix_kernel(*refs[:n_in], *refs[n_in + len(prev):])

    return pl.pallas_call(
        body,
        out_shape=outs,
        grid=(b // bb,),
        in_specs=[ublk(u.shape[-1])] + [sblk(a, ax) for a, ax in states]
        + [_layer(a, l) for a in weights]
        + [pl.BlockSpec(memory_space=pl.ANY)] * len(prev),
        out_specs=[ublk(MIX_WIDTH)] + [sblk(a, ax) for a, ax in states],
        input_output_aliases={n_in + k: 1 + k for k in range(len(prev))},
        compiler_params=pltpu.CompilerParams(
            dimension_semantics=("arbitrary",), vmem_limit_bytes=VMEM_LIMIT),
        name="sample_mix",
    )(u, *[a for a, _ in states], *weights, *prev)


def _prep_params(p):
    w_p, w_exp = _pack_w_in(p["w_in"])
    r = lambda a: a.reshape(N_LAYERS, 1, -1)
    rep = lambda a, n: r(jnp.repeat(a, n, axis=-1))
    small = jnp.concatenate([p["ssd_dt_bias"], p["ml_i_bias"], p["ml_f_bias"]], axis=-1)
    return dict(
        w_in_p=w_p, w_exp=w_exp,
        norm1_w=r(p["norm1_w"]), norm2_w=r(p["norm2_w"]),
        ssd_conv_w=p["ssd_conv_w"], ssd_conv_b=r(p["ssd_conv_b"]),
        p_small_bias=r(jnp.pad(small, ((0, 0), (0, V7X_LANES - small.shape[-1])))),
        p_alog=r(jnp.pad(p["ssd_a_log"], ((0, 0), (0, V7X_LANES - SSD_HEADS)))),
        s_dt_bias=rep(p["ssd_dt_bias"], SSD_HEAD_DIM), s_alog=rep(p["ssd_a_log"], SSD_HEAD_DIM),
        s_i_bias=rep(p["ml_i_bias"], ML_HEAD_DIM), s_f_bias=rep(p["ml_f_bias"], ML_HEAD_DIM),
        dskip=rep(p["ssd_d"], SSD_HEAD_DIM),
        ssd_norm_w=r(p["ssd_norm_w"]), ml_norm_w=r(p["ml_norm_w"]),
        lru_conv_w=p["lru_conv_w"], lru_conv_b=r(p["lru_conv_b"]),
        lru_wax=jnp.concatenate([p["lru_wa"], p["lru_wx"]], axis=-1).astype(bf16),
        lru_ba=r(p["lru_ba"]), lru_bx=r(p["lru_bx"]), lru_lambda=r(p["lru_lambda"]),
        w_out=_cast_pad(p["w_out"]), mlp_up=_cast_pad(p["mlp_up"]),
        mlp_down=_cast_pad(p["mlp_down"]),
    )


def kernel(x_prompt, x_sample, c_prompt, c_sample, state_ssm, state_ssd_conv, state_mlstm_c, state_mlstm_n, state_mlstm_m, state_lru_h, state_lru_conv, ada_w, ada_b, norm1_w, norm2_w, w_in, ssd_conv_w, ssd_conv_b, ssd_dt_bias, ssd_a_log, ssd_d, ssd_norm_w, ml_i_bias, ml_f_bias, ml_norm_w, lru_conv_w, lru_conv_b, lru_wa, lru_ba, lru_wx, lru_bx, lru_lambda, w_out, mlp_up, mlp_down, final_norm_w):
    p = dict(norm1_w=norm1_w, norm2_w=norm2_w, w_in=w_in, ssd_conv_w=ssd_conv_w,
             ssd_conv_b=ssd_conv_b, ssd_dt_bias=ssd_dt_bias, ssd_a_log=ssd_a_log, ssd_d=ssd_d,
             ssd_norm_w=ssd_norm_w, ml_i_bias=ml_i_bias, ml_f_bias=ml_f_bias,
             ml_norm_w=ml_norm_w, lru_conv_w=lru_conv_w, lru_conv_b=lru_conv_b, lru_wa=lru_wa,
             lru_ba=lru_ba, lru_wx=lru_wx, lru_bx=lru_bx, lru_lambda=lru_lambda, w_out=w_out,
             mlp_up=mlp_up, mlp_down=mlp_down)
    w = _prep_params(p)
    fnw = final_norm_w.reshape(1, D_MODEL)
    bp = x_prompt.shape[0]
    bs_ = x_sample.shape[0]

    mod = _ada(jnp.concatenate([c_sample, c_prompt], axis=0), ada_w, ada_b)
    mod = mod.reshape(N_LAYERS, bs_ + bp, 6, D_MODEL)

    xp = _interleave_rows(x_prompt)
    p_out = [jnp.zeros((N_LAYERS, bp) + s, f32) for s in (
        (SSD_HEADS // 2, 2 * SSD_HEAD_DIM, SSD_STATE),
        (CONV_W - 1, SSD_CONV_DIM), (ML_HEADS, ML_HEAD_DIM, ML_HEAD_DIM),
        (ML_HEADS, ML_HEAD_DIM), (ML_HEADS, V7X_LANES),
        (1, LRU_WIDTH), (CONV_W - 1, LRU_WIDTH))]
    big_states = ((N_LAYERS, bs_, SSD_HEADS * SSD_HEAD_DIM, SSD_STATE),
                  (N_LAYERS, bs_, ML_HEADS * ML_HEAD_DIM, ML_HEAD_DIM))
    for l in range(N_LAYERS):
        res = _player(l, xp, mod, bs_, w, fnw, l == N_LAYERS - 1, p_out,
                      big_states if l == 0 else ())
        xp, p_out = res[0], list(res[1:1 + N_STATE_OUTS])
        if l == 0:
            ssm_buf, c_buf = res[1 + N_STATE_OUTS:]
    xp = _deinterleave_rows(xp)
    ssm, sconv, mc, mn, mm, lh, lconv = p_out
    p_states = (ssm.reshape(N_LAYERS, bp, SSD_HEADS, SSD_HEAD_DIM, SSD_STATE), sconv, mc, mn,
                mm[..., 0], lh.reshape(N_LAYERS, bp, LRU_WIDTH), lconv)

    tok_major = lambda a: jnp.swapaxes(a, -3, -2)
    xs = tok_major(x_sample)
    mod_s = tok_major(mod[:, :bs_])
    st_in = (
        (state_ssm.reshape(N_LAYERS, bs_, SSD_HEADS * SSD_HEAD_DIM, SSD_STATE), 1),
        (tok_major(state_ssd_conv), 2),
        (state_mlstm_c.reshape(N_LAYERS, bs_, ML_HEADS * ML_HEAD_DIM, ML_HEAD_DIM), 1),
        (state_mlstm_n.reshape(N_LAYERS, bs_, ML_WIDTH), 1),
        (jnp.repeat(state_mlstm_m, ML_HEAD_DIM, axis=-1), 1),
        (state_lru_h, 1),
        (tok_major(state_lru_conv), 2),
    )
    st_out = [jnp.zeros(a.shape, f32) for a, _ in st_in]
    st_out[0], st_out[2] = ssm_buf, c_buf
    for l in range(N_LAYERS):
        u = _inproj(l, xs, mod_s, w["norm1_w"], w["w_in_p"], w["w_exp"], 32)
        ycat, *st_out = _smix(l, u, st_in, st_out, w, 8)
        xs = _outmlp(l, xs, ycat, mod_s, w["w_out"], w["norm2_w"], w["mlp_up"], w["mlp_down"],
                     fnw, 64, l == N_LAYERS - 1)
    xs = tok_major(xs)
    ssm, sconv, mc, mn, mm, lh, lconv = st_out
    s_states = (ssm.reshape(N_LAYERS, bs_, SSD_HEADS, SSD_HEAD_DIM, SSD_STATE), tok_major(sconv),
                mc.reshape(N_LAYERS, bs_, ML_HEADS, ML_HEAD_DIM, ML_HEAD_DIM),
                mn.reshape(N_LAYERS, bs_, ML_HEADS, ML_HEAD_DIM),
                mm.reshape(N_LAYERS, bs_, ML_HEADS, ML_HEAD_DIM)[..., 0],
                lh, tok_major(lconv))

    return (xp, xs) + p_states + s_states
```

```python
import functools
import math

import jax
import jax.numpy as jnp
from jax import lax
from jax.experimental import pallas as pl
from jax.experimental.pallas import tpu as pltpu

f32 = jnp.float32
bf16 = jnp.bfloat16

D_MODEL = 1024
N_LAYERS = 2
MIX_WIDTH = 2 * D_MODEL
SSD_WIDTH = 1024
SSD_HEAD_DIM = 64
SSD_HEADS = 16
SSD_GROUPS = 2
SSD_STATE = 128
SSD_CONV_DIM = SSD_WIDTH + 2 * SSD_GROUPS * SSD_STATE
ML_WIDTH = 512
ML_HEADS = 4
ML_HEAD_DIM = 128
LRU_WIDTH = 512
LRU_BLOCKS = 4
LRU_BLOCK_DIM = 128
LRU_C = 8.0
CONV_W = 4
CHUNK = 128
D_FF = 4 * D_MODEL
EPS = 1e-6
DEC_SEQ = 8

_O_DT, _O_Q, _O_O_END, _O_I, _O_XR, _O_END = 2560, 2576, 4624, 4624, 4632, 5656

P_Z = 0
P_XBC = 1024
P_Q = 2560
P_K = 3072
P_V = 3584
P_O = 4096
P_XR = 4608
P_GR = 5120
P_MAIN = 5632
P_SMALL = 5632
NP_COLS = 5760
S_DT = 5632
S_I = 6656
S_F = 7168
NS_COLS = 7680

V7X_LANES = 128
V7X_SUBLANES = 8
VMEM_LIMIT = 56 * 1024 * 1024
PLAYER_VMEM_LIMIT = 60 * 1024 * 1024
ROW_TILES = CHUNK // V7X_SUBLANES
LOG2E = 1.4426950408889634

SMALL_I = SSD_HEADS
SMALL_F = SSD_HEADS + ML_HEADS
GROUP_LANES = SSD_WIDTH // SSD_GROUPS
SAMPLE_IN_ROWS = 32
SAMPLE_MIX_ROWS = 8
SAMPLE_OUT_ROWS = 64


def _softplus(x):
    return jnp.maximum(x, 0.0) + jnp.log1p(jnp.exp(-jnp.abs(x)))


def _log_sigmoid(x):
    return -_softplus(-x)


def _sigmoid(x):
    return 1.0 / (1.0 + jnp.exp(-x))


def _silu(x):
    return x * _sigmoid(x)


def _gelu_tanh(x):
    c = math.sqrt(2.0 / math.pi)
    return 0.5 * x * (1.0 + jnp.tanh(c * (x + 0.044715 * (x * x * x))))


def _neg_expm1(x):
    return -jnp.tanh(0.5 * x) * (jnp.exp(x) + 1.0)


def _rms(x, w):
    return x * lax.rsqrt(jnp.mean(x * x, axis=-1, keepdims=True) + EPS) * w


def _interleave_rows(x):
    b, t, d = x.shape
    return x.reshape(b, t // CHUNK, V7X_SUBLANES, ROW_TILES, d).swapaxes(2, 3).reshape(b, t, d)


def _deinterleave_rows(x):
    b, t, d = x.shape
    return x.reshape(b, t // CHUNK, ROW_TILES, V7X_SUBLANES, d).swapaxes(2, 3).reshape(b, t, d)


def _full(shape):
    n = len(shape)
    return pl.BlockSpec(shape, lambda *_: (0,) * n)


def _layer(a, l, resident=False):
    nd = a.ndim - 1
    kw = dict(pipeline_mode=pl.Buffered(1)) if resident else {}
    return pl.BlockSpec((None,) + a.shape[1:], lambda *_: (l,) + (0,) * nd, **kw)


def _ada_kernel(c_ref, w_ref, b_ref, o_ref):
    cs = _silu(c_ref[...]).astype(bf16)
    o_ref[0, 0] = jnp.dot(cs, w_ref[0].astype(bf16), preferred_element_type=f32) + b_ref[0]


def _ada(c_all, ada_w, ada_b):
    n = c_all.shape[0]
    tn = D_MODEL
    return pl.pallas_call(
        _ada_kernel,
        out_shape=jax.ShapeDtypeStruct((N_LAYERS, 6, n, D_MODEL), f32),
        grid=(N_LAYERS, 6),
        in_specs=[
            pl.BlockSpec((n, D_MODEL), lambda l, j: (0, 0)),
            pl.BlockSpec((1, D_MODEL, tn), lambda l, j: (l, 0, j)),
            pl.BlockSpec((1, 1, tn), lambda l, j: (l, 0, j)),
        ],
        out_specs=pl.BlockSpec((1, 1, n, tn), lambda l, j: (l, j, 0, 0)),
        compiler_params=pltpu.CompilerParams(
            dimension_semantics=("arbitrary", "arbitrary"), vmem_limit_bytes=VMEM_LIMIT),
        name="ada_mod",
    )(c_all, ada_w, ada_b.reshape(N_LAYERS, 1, 6 * D_MODEL))


def _inproj_kernel(x_ref, mod_ref, nw_ref, wm_ref, we_ref, o_ref):
    x = x_ref[...]
    t, bb, d = x.shape
    hn = _rms(x, nw_ref[...]) * (1.0 + mod_ref[1]) + mod_ref[0]
    hn = hn.reshape(t * bb, d).astype(bf16)
    o_ref[:, :, 0:P_MAIN] = jnp.dot(hn, wm_ref[...], preferred_element_type=f32).reshape(
        t, bb, P_MAIN)
    o_ref[:, :, P_MAIN:NS_COLS] = jnp.dot(hn, we_ref[...], preferred_element_type=f32).reshape(
        t, bb, NS_COLS - P_MAIN)


def _inproj(l, x, mod, nw, w_packed, w_exp, bb):
    t, b, d = x.shape
    return pl.pallas_call(
        _inproj_kernel,
        out_shape=jax.ShapeDtypeStruct((t, b, NS_COLS), f32),
        grid=(b // bb,),
        in_specs=[
            pl.BlockSpec((t, bb, d), lambda i: (0, i, 0)),
            pl.BlockSpec((None, 6, bb, d), lambda i: (l, 0, i, 0)),
            _layer(nw, l),
            pl.BlockSpec((None, d, P_MAIN), lambda i: (l, 0, 0), pipeline_mode=pl.Buffered(1)),
            _layer(w_exp, l, resident=True),
        ],
        out_specs=pl.BlockSpec((t, bb, NS_COLS), lambda i: (0, i, 0)),
        compiler_params=pltpu.CompilerParams(
            dimension_semantics=("arbitrary",), vmem_limit_bytes=VMEM_LIMIT),
        name="inproj",
    )(x, mod, nw, w_packed, w_exp)


PACK_ROWS = 128


def _pack_w_in_kernel(w_ref, wp_ref, we_ref):
    c = lambda a: a.astype(bf16)
    wp_ref[:, 0:P_Q] = c(w_ref[:, 0:_O_DT])
    wp_ref[:, P_Q:P_XR] = c(w_ref[:, _O_Q:_O_O_END])
    wp_ref[:, P_XR:P_MAIN] = c(w_ref[:, _O_XR:_O_END])
    dt = w_ref[:, _O_DT:_O_Q]
    gates = w_ref[:, _O_I:_O_XR]
    n_pad = V7X_LANES - SSD_HEADS - 2 * ML_HEADS
    wp_ref[:, P_SMALL:NP_COLS] = c(jnp.concatenate(
        [dt, gates, jnp.zeros((PACK_ROWS, n_pad), f32)], axis=1))
    rep = lambda a, j, n: jnp.broadcast_to(a[:, j:j + 1], (PACK_ROWS, n))
    we_ref[...] = c(jnp.concatenate(
        [rep(dt, h, SSD_HEAD_DIM) for h in range(SSD_HEADS)]
        + [rep(gates, h, ML_HEAD_DIM) for h in range(2 * ML_HEADS)], axis=1))


def _pack_w_in(w_in):
    n_l, d, n = w_in.shape
    blk = lambda m: pl.BlockSpec((None, PACK_ROWS, m), lambda l, i: (l, i, 0))
    return pl.pallas_call(
        _pack_w_in_kernel,
        out_shape=[jax.ShapeDtypeStruct((n_l, d, NP_COLS), bf16),
                   jax.ShapeDtypeStruct((n_l, d, NS_COLS - P_MAIN), bf16)],
        grid=(n_l, d // PACK_ROWS),
        in_specs=[blk(n)],
        out_specs=[blk(NP_COLS), blk(NS_COLS - P_MAIN)],
        compiler_params=pltpu.CompilerParams(
            dimension_semantics=("arbitrary", "arbitrary"), vmem_limit_bytes=VMEM_LIMIT),
        name="pack_w_in",
    )(w_in)


def _cast_pad_kernel(w_ref, o_ref):
    n = w_ref.shape[-1]
    o_ref[:, 0:n] = w_ref[...].astype(bf16)
    o_ref[:, n:] = jnp.zeros((w_ref.shape[0], o_ref.shape[-1] - n), bf16)


def _cast_pad(a):
    n_l, k, n = a.shape
    rows = 512
    return pl.pallas_call(
        _cast_pad_kernel,
        out_shape=jax.ShapeDtypeStruct((n_l, k, n + V7X_LANES), bf16),
        grid=(n_l, k // rows),
        in_specs=[pl.BlockSpec((None, rows, n), lambda l, i: (l, i, 0))],
        out_specs=pl.BlockSpec((None, rows, n + V7X_LANES), lambda l, i: (l, i, 0)),
        compiler_params=pltpu.CompilerParams(
            dimension_semantics=("arbitrary", "arbitrary"), vmem_limit_bytes=VMEM_LIMIT),
        name="cast_pad",
    )(a)


def _outmlp_kernel(x_ref, y_ref, mod_ref, wout_ref, n2w_ref, up_ref, down_ref, fnw_ref, o_ref,
                   *, final):
    x = x_ref[...]
    t, bb, d = x.shape
    m = t * bb
    ycat = y_ref[...].reshape(m, MIX_WIDTH).astype(bf16)
    mix = jnp.dot(ycat, wout_ref[:, 0:d], preferred_element_type=f32).reshape(t, bb, d)
    x1 = x + mod_ref[2] * mix
    hn = _rms(x1, n2w_ref[...]) * (1.0 + mod_ref[4]) + mod_ref[3]
    hn = hn.reshape(m, d).astype(bf16)
    ff = jnp.zeros((m, d), f32)
    fc = 1024
    for c in range(D_FF // fc):
        h = jnp.dot(hn, up_ref[:, c * fc:(c + 1) * fc], preferred_element_type=f32)
        h = jnp.square(jnp.maximum(h, 0.0)).astype(bf16)
        ff = ff + jnp.dot(h, down_ref[c * fc:(c + 1) * fc, 0:D_MODEL],
                          preferred_element_type=f32)
    x2 = x1 + mod_ref[5] * ff.reshape(t, bb, d)
    if final:
        x2 = _rms(x2, fnw_ref[...])
    o_ref[...] = x2


def _outmlp(l, x, ycat, mod, wout, n2w, up, down, fnw, bb, final):
    t, b, d = x.shape
    return pl.pallas_call(
        functools.partial(_outmlp_kernel, final=final),
        out_shape=jax.ShapeDtypeStruct((t, b, d), f32),
        grid=(b // bb,),
        in_specs=[
            pl.BlockSpec((t, bb, d), lambda i: (0, i, 0)),
            pl.BlockSpec((t, bb, MIX_WIDTH), lambda i: (0, i, 0)),
            pl.BlockSpec((None, 6, bb, d), lambda i: (l, 0, i, 0)),
            _layer(wout, l, resident=True),
            _layer(n2w, l),
            _layer(up, l, resident=True),
            _layer(down, l, resident=True),
            _full((1, d)),
        ],
        out_specs=pl.BlockSpec((t, bb, d), lambda i: (0, i, 0)),
        compiler_params=pltpu.CompilerParams(
            dimension_semantics=("arbitrary",), vmem_limit_bytes=VMEM_LIMIT),
        name="outproj_mlp",
    )(x, ycat, mod, wout, n2w, up, down, fnw)


PIECE = 512


def _stage_in(x, mod, nw_ref, w_ref, u_ref):
    hn = _rms(x, nw_ref[...]) * (1.0 + mod(1)) + mod(0)
    hn = hn.astype(bf16)
    yield
    for c0 in range(0, NP_COLS, PIECE):
        c1 = min(c0 + PIECE, NP_COLS)
        u_ref[:, c0:c1] = jnp.dot(hn, w_ref[:, c0:c1], preferred_element_type=f32)
        yield


def _stage_out(x, y_ref, o_ref, rows, mod, wout_ref, n2w_ref, up_ref, down_ref, fnw_ref,
               final):
    halves = [slice(h * PIECE, (h + 1) * PIECE) for h in range(D_MODEL // PIECE)]
    ycat = y_ref[...]
    mix = []
    for hs in halves:
        mix.append(jnp.dot(ycat, wout_ref[:, hs], preferred_element_type=f32))
        yield
    x1 = x + mod(2) * jnp.concatenate(mix, axis=1)
    hn = (_rms(x1, n2w_ref[...]) * (1.0 + mod(4)) + mod(3)).astype(bf16)
    yield
    ff = [jnp.zeros((x.shape[0], PIECE), f32) for _ in halves]
    fc = 2 * PIECE
    for c in range(D_FF // fc):
        h = []
        for k in range(fc // PIECE):
            cols = slice(c * fc + k * PIECE, c * fc + (k + 1) * PIECE)
            hk = jnp.dot(hn, up_ref[:, cols], preferred_element_type=f32)
            h.append(jnp.square(jnp.maximum(hk, 0.0)).astype(bf16))
            yield
        h = jnp.concatenate(h, axis=1)
        for i, hs in enumerate(halves):
            ff[i] = ff[i] + jnp.dot(h, down_ref[c * fc:(c + 1) * fc, hs],
                                    preferred_element_type=f32)
            yield
    x2 = x1 + mod(5) * jnp.concatenate(ff, axis=1)
    o_ref[0, rows, :] = _rms(x2, fnw_ref[...]) if final else x2
    yield


def _trace_interleaved(main, fillers, n_main, n_fill):
    fillers = list(fillers)
    due = 0.0
    nxt = 0
    main_live = True
    while main_live or fillers:
        if main_live:
            main_live = next(main, _DONE) is not _DONE
            due += n_fill / n_main
        else:
            due = float(len(fillers))
        while fillers and due >= 1.0:
            nxt %= len(fillers)
            if next(fillers[nxt], _DONE) is _DONE:
                fillers.pop(nxt)
            else:
                nxt += 1
                due -= 1.0


_DONE = object()


def _reset_state_if(first, state):
    for ref in state:
        ref[...] = jnp.where(first, 0.0, ref[...])


def _stage_mix(u_ref, y_ref, prm, state, ci, outs):
    (cw_ref, cb_ref, sbias_ref, alog_ref, dskip_ref, snw_ref, mnw_ref,
     lcw_ref, lcb_ref, wax_ref, ba_ref, bx_ref, lam_ref) = prm
    xbuf, lbuf, st, cst, nst, mst, hst = state
    L = CHUNK
    SUB = V7X_SUBLANES
    HIST = (CONV_W - 1) * SUB

    row = lax.broadcasted_iota(jnp.int32, (L, L), 0)
    col = lax.broadcasted_iota(jnp.int32, (L, L), 1)
    tok = lambda i: (i & (SUB - 1)) * ROW_TILES + (i >> 3)
    tri = tok(row) >= tok(col)
    lo = col < SSD_HEAD_DIM

    def bcl(a, j):
        return jnp.broadcast_to(a[:, j:j + 1], (L, L))

    def conv(u0, hist_ref, w_ref, b_ref, c0, c1):
        raw = u_ref[:, u0 + c0:u0 + c1]
        sub = lax.broadcasted_iota(jnp.int32, (SUB, c1 - c0), 0)
        prev = [pltpu.roll(jnp.where(sub == SUB - 1, hist_ref[k * SUB:(k + 1) * SUB, c0:c1],
                                     raw[L - HIST + k * SUB:L - HIST + (k + 1) * SUB, :]), 1, 0)
                for k in range(CONV_W - 1)]
        acc = b_ref[:, c0:c1] + raw * w_ref[CONV_W - 1:CONV_W, c0:c1]
        for d in range(1, CONV_W):
            shifted = jnp.concatenate(prev[CONV_W - 1 - d:] + [raw[0:L - d * SUB, :]], axis=0)
            acc = acc + shifted * w_ref[CONV_W - 1 - d:CONV_W - d, c0:c1]
        hist_ref[:, c0:c1] = raw[L - HIST:L, :]
        return acc

    cblk = 512
    xbc = []
    for c0 in range(0, SSD_CONV_DIM, cblk):
        xbc.append(_silu(conv(P_XBC, xbuf, cw_ref, cb_ref, c0, c0 + cblk)))
        yield
    xbc = jnp.concatenate(xbc, axis=1)
    xr = conv(P_XR, lbuf, lcw_ref, lcb_ref, 0, LRU_WIDTH)
    yield

    sm = u_ref[:,P_SMALL:P_SMALL + V7X_LANES] + sbias_ref[...]
    dt = _softplus(sm)
    a_row = -jnp.exp(alog_ref[...])
    gates = jnp.where(col < SSD_HEADS, dt * (a_row * LOG2E),
                      jnp.where((col >= SMALL_F) & (col < SMALL_F + ML_HEADS),
                                _log_sigmoid(sm), 0.0))
    g_hi = gates.astype(bf16)
    g_r = gates - g_hi.astype(f32)
    g_mid = g_r.astype(bf16)
    g_lo = (g_r - g_mid.astype(f32)).astype(bf16)
    cum3 = jnp.dot(jnp.where(tri, 1.0, 0.0).astype(bf16),
                   jnp.concatenate([g_hi, g_mid, g_lo], axis=1), preferred_element_type=f32)
    cum = cum3[:, 0:L] + cum3[:, L:2 * L] + cum3[:, 2 * L:3 * L]
    cum_t = cum.T
    sm_t = sm.T
    yield

    ys = []
    for g in range(SSD_GROUPS):
        bm_t = xbc[:, SSD_WIDTH + g * SSD_STATE:SSD_WIDTH + (g + 1) * SSD_STATE].T.astype(bf16)
        cm = xbc[:, SSD_WIDTH + (SSD_GROUPS + g) * SSD_STATE:
                 SSD_WIDTH + (SSD_GROUPS + g + 1) * SSD_STATE].astype(bf16)
        cb = jnp.dot(cm, bm_t, preferred_element_type=f32)
        inter = jnp.dot(cm, st[g].astype(bf16), preferred_element_type=f32)
        yield
        xends, keeps = [], []
        for jj in range(4):
            jp = g * 4 + jj
            e0, e1 = 2 * jp, 2 * jp + 1
            c0 = bcl(cum, e0)
            c1 = bcl(cum, e1)
            cum_p = jnp.where(lo, c0, c1)
            dt_p = jnp.where(lo, bcl(dt, e0), bcl(dt, e1))
            xs_p = xbc[:, jp * L:(jp + 1) * L]
            xdt = xs_p * dt_p
            dec0 = jnp.exp2(jnp.where(tri, c0 - cum_t[e0:e0 + 1, :], -jnp.inf))
            dec1 = jnp.exp2(jnp.where(tri, c1 - cum_t[e1:e1 + 1, :], -jnp.inf))
            yield
            att =jnp.concatenate([(cb * dec0).astype(bf16), (cb * dec1).astype(bf16)], axis=1)
            x2 = jnp.concatenate([jnp.where(lo, xdt, 0.0).astype(bf16),
                                  jnp.where(lo, 0.0, xdt).astype(bf16)], axis=0)
            y_p = (jnp.dot(att, x2, preferred_element_type=f32)
                   + inter[:, jj * L:(jj + 1) * L] * jnp.exp2(cum_p)
                   + xs_p * dskip_ref[:, jp * L:(jp + 1) * L])
            ys.append(y_p)
            c_last = cum_p[L - 1:L, :]
            xends.append((xdt * jnp.exp2(c_last - cum_p)).astype(bf16))
            keeps.append(jnp.exp2(c_last))
            yield
        st[g] = (jnp.concatenate(keeps, axis=1) * st[g]
                 + jnp.dot(bm_t, jnp.concatenate(xends, axis=1), preferred_element_type=f32))
    y = jnp.concatenate(ys, axis=1)
    z = u_ref[:,P_Z:P_Z + SSD_WIDTH]
    y_ref[:,0:SSD_WIDTH] = _rms(y * _silu(z), snw_ref[...]).astype(bf16)
    yield

    scale = ML_HEAD_DIM ** -0.5
    for h in range(ML_HEADS):
        sl = slice(h * ML_HEAD_DIM, (h + 1) * ML_HEAD_DIM)
        q = u_ref[:,P_Q + h * ML_HEAD_DIM:P_Q + (h + 1) * ML_HEAD_DIM]
        k = u_ref[:,P_K + h * ML_HEAD_DIM:P_K + (h + 1) * ML_HEAD_DIM] * scale
        v = u_ref[:,P_V + h * ML_HEAD_DIM:P_V + (h + 1) * ML_HEAD_DIM]
        o = u_ref[:,P_O + h * ML_HEAD_DIM:P_O + (h + 1) * ML_HEAD_DIM]
        k_t = k.T.astype(bf16)
        jf = SMALL_F + h
        ji = SMALL_I + h
        b_col = cum[:, jf:jf + 1]
        i_col = sm[:, ji:ji + 1]
        dm = jnp.where(tri, bcl(cum, jf) - cum_t[jf:jf + 1, :] + sm_t[ji:ji + 1, :], -jnp.inf)
        m_prev = mst[h, :, 0:1]
        g_col = b_col + m_prev
        mt = jnp.maximum(g_col, jnp.max(dm, axis=-1, keepdims=True))
        yield
        w = jnp.exp(dm - mt) * jnp.dot(q.astype(bf16), k_t, preferred_element_type=f32)
        inter = jnp.exp(g_col - mt)
        yield
        ct_old = cst[h]
        n_old = nst[h]
        num = jnp.dot(jnp.concatenate([w.astype(bf16), (inter * q).astype(bf16)], axis=1),
                      jnp.concatenate([v.astype(bf16), ct_old.astype(bf16)], axis=0),
                      preferred_element_type=f32)
        den = (jnp.sum(w, axis=-1, keepdims=True)
               + inter * jnp.sum(q * n_old, axis=-1, keepdims=True))
        hout = num / jnp.maximum(jnp.abs(den), jnp.exp(-mt))
        yield
        m_new = mt[L - 1:L, :]
        b_last = b_col[L - 1:L, :]
        w_end = jnp.exp(b_last - b_col + i_col - m_new)
        dc = jnp.exp(b_last + m_prev - m_new)
        cst[h] = dc * ct_old + jnp.dot(k_t, (v * w_end).astype(bf16),
                                       preferred_element_type=f32)
        nst[h] = dc * n_old + jnp.sum(w_end * k, axis=0, keepdims=True)
        mst[h] = jnp.broadcast_to(m_new, (1, V7X_LANES))
        yh = _rms(hout, mnw_ref[:, sl]) * _sigmoid(o)
        y_ref[:,SSD_WIDTH + h * ML_HEAD_DIM:SSD_WIDTH + (h + 1) * ML_HEAD_DIM] = yh.astype(bf16)
        yield

    rs, is_ = [], []
    for kb_ in range(LRU_BLOCKS):
        xk = xr[:, kb_ * LRU_BLOCK_DIM:(kb_ + 1) * LRU_BLOCK_DIM].astype(bf16)
        ri = jnp.dot(xk, wax_ref[kb_], preferred_element_type=f32)
        rs.append(ri[:, 0:LRU_BLOCK_DIM])
        is_.append(ri[:, LRU_BLOCK_DIM:2 * LRU_BLOCK_DIM])
    r = _sigmoid(jnp.concatenate(rs, axis=1) + ba_ref[...])
    i_g = _sigmoid(jnp.concatenate(is_, axis=1) + bx_ref[...])
    yield
    log_a = -LRU_C * r * _softplus(-lam_ref[...])
    mult = jnp.sqrt(_neg_expm1(2.0 * log_a))
    rowl = lax.broadcasted_iota(jnp.int32, (L, LRU_WIDTH), 0)
    first_row = jnp.where(ci == 0, 0, -1)
    mult = jnp.where(rowl == first_row, 1.0, mult)
    a_s = jnp.exp(log_a)
    u_s = mult * i_g * xr
    yield
    tile = lambda a, v: a[v * SUB:(v + 1) * SUB, :]
    h_loc = [tile(u_s, 0)]
    a_cum = [tile(a_s, 0)]
    for v in range(1, ROW_TILES):
        h_loc.append(tile(a_s, v) * h_loc[-1] + tile(u_s, v))
        a_cum.append(tile(a_s, v) * a_cum[-1])
    a_e, h_e = a_cum[-1], h_loc[-1]
    subl = lax.broadcasted_iota(jnp.int32, (SUB, LRU_WIDTH), 0)
    k_ = 1
    while k_ < SUB:
        keep = subl >= k_
        a_sh = jnp.where(keep, pltpu.roll(a_e, k_, 0), 1.0)
        h_sh = jnp.where(keep, pltpu.roll(h_e, k_, 0), 0.0)
        h_e = a_e * h_sh + h_e
        a_e = a_e * a_sh
        k_ *= 2
    h0 = hst[...]
    run_end = h_e + a_e * h0
    run_in = jnp.where(subl == 0, h0, pltpu.roll(run_end, 1, 0))
    hr = jnp.concatenate([h_loc[v] + a_cum[v] * run_in for v in range(ROW_TILES)], axis=0)
    hst[...] = run_end[SUB - 1:SUB, :]
    gr = u_ref[:,P_GR:P_GR + LRU_WIDTH]
    y_ref[:,SSD_WIDTH + ML_WIDTH:MIX_WIDTH] = (hr * _gelu_tanh(gr)).astype(bf16)
    yield

    if outs is not None:
        ssm_ref, sconv_ref, mc_ref, mn_ref, mm_ref, lh_ref, lconv_ref = outs
        for g in range(SSD_GROUPS):
            for jj in range(4):
                ssm_ref[0, g * 4 + jj] = st[g, :, jj * L:(jj + 1) * L].T
        for k in range(CONV_W - 1):
            sconv_ref[0, k:k + 1, :] = xbuf[k * SUB + SUB - 1:(k + 1) * SUB, :]
            lconv_ref[0, k:k + 1, :] = lbuf[k * SUB + SUB - 1:(k + 1) * SUB, :]
        for h in range(ML_HEADS):
            mc_ref[0, h] = cst[h].T
            mn_ref[0, h:h + 1, :] = nst[h]
            mm_ref[0, h:h + 1, :] = mst[h]
        lh_ref[0] = hst[...]


N_MIX_PARAMS = 13
N_STATE_OUTS = 7
N_MIX_PHASES = 44
N_FILL_PIECES = (1 + -(-NP_COLS // PIECE)) + (D_MODEL // PIECE + 1
                                              + (D_FF // (2 * PIECE)) * (2 + D_MODEL // PIECE) + 1)


def _player_kernel(*refs, final, n_chunks, nc, n_clear):
    xa_ref, mod_ref, xc_ref, n1w_ref, win_ref = refs[:5]
    prm = refs[5:5 + N_MIX_PARAMS]
    wout_ref, n2w_ref, up_ref, down_ref, fnw_ref = refs[5 + N_MIX_PARAMS:10 + N_MIX_PARAMS]
    n_in = 10 + N_MIX_PARAMS + N_STATE_OUTS
    o_ref = refs[n_in]
    outs = refs[n_in + 1:n_in + 1 + N_STATE_OUTS]
    clear = refs[n_in + 1 + N_STATE_OUTS:n_in + 1 + N_STATE_OUTS + n_clear]
    u_bufs = refs[-11:-9]
    y_bufs = refs[-9:-7]
    state = refs[-7:]
    k = pl.program_id(0)
    n_steps = n_chunks // 2

    @pl.when(k == 0)
    def _():
        u_bufs[1][...] = jnp.zeros(u_bufs[1].shape, f32)
        y_bufs[1][...] = jnp.zeros(y_bufs[1].shape, bf16)
        for ref in state:
            ref[...] = jnp.zeros(ref.shape, f32)
        if n_clear:
            refs[-13][...] = jnp.zeros(refs[-13].shape, f32)

    def clear_copies():
        zeros, sems = refs[-13], refs[-12]
        copies = []
        for ci_, buf in enumerate(clear):
            n_b, rows = buf.shape[1], buf.shape[2]
            per_step = buf.shape[0] * n_b // n_steps
            for i in range(per_step):
                j = k * per_step + i
                copies.append(pltpu.make_async_copy(
                    zeros.at[0:rows], buf.at[j // n_b, j % n_b], sems.at[ci_, i]))
        return copies

    if n_clear:
        @pl.when(k < n_steps)
        def _():
            for cp in clear_copies():
                cp.start()

    pair_seq = lambda pair: jnp.clip(pair, 0, n_steps - 1) // (nc // 2)
    mod_of = lambda seq: (lambda c: mod_ref[c, pl.ds(seq, 1), :])
    mod_a, mod_c = mod_of(pair_seq(k)), mod_of(pair_seq(k - 1))

    for sub in range(2):
        rows = slice(sub * CHUNK, (sub + 1) * CHUNK)
        stage_in = _stage_in(xa_ref[0, rows, :], mod_a, n1w_ref, win_ref, u_bufs[sub])
        jb = jnp.clip(2 * k + sub - 1, 0, n_chunks - 1)
        ci = jb % nc
        if sub == 1:
            _reset_state_if(ci == 0, state)
        stage_mix = _stage_mix(u_bufs[1 - sub], y_bufs[sub], prm, state, ci,
                               outs if sub == 0 else None)
        stage_out = _stage_out(xc_ref[0, rows, :], y_bufs[1 - sub], o_ref, rows, mod_c,
                               wout_ref, n2w_ref, up_ref, down_ref, fnw_ref, final)
        _trace_interleaved(stage_mix, [stage_in, stage_out], N_MIX_PHASES, N_FILL_PIECES)

    if n_clear:
        @pl.when(k < n_steps)
        def _():
            for cp in clear_copies():
                cp.wait()


CLEAR_ROWS = 1024


def _player(l, x, mod, mod_row0, w, fnw, final, prev, clear_shapes=()):
    b, t, d = x.shape
    nc = t // CHUNK
    n_chunks = b * nc
    n_pairs = n_chunks // 2
    ppr = nc // 2
    pair_a = lambda k: jnp.minimum(k, n_pairs - 1)
    pair_c = lambda k: jnp.clip(k - 1, 0, n_pairs - 1)
    seq_b = lambda k: jnp.clip(2 * k - 1, 0, n_chunks - 1) // nc
    xspec = lambda pair: pl.BlockSpec((1, 2 * CHUNK, d), lambda k: (pair(k) // ppr, pair(k) % ppr, 0))
    assert b <= V7X_SUBLANES and mod_row0 % V7X_SUBLANES == 0
    mspec = pl.BlockSpec((None, 6, V7X_SUBLANES, d), lambda k: (l, 0, mod_row0 // V7X_SUBLANES, 0))
    st_spec = lambda s: pl.BlockSpec((None, 1) + s[2:],
                                     lambda k: (l, seq_b(k)) + (0,) * (len(s) - 2))
    outs = [jax.ShapeDtypeStruct((b, t, d), f32)]
    outs += [jax.ShapeDtypeStruct(a.shape, f32) for a in prev]
    mix_params = [w["ssd_conv_w"], w["ssd_conv_b"], w["p_small_bias"], w["p_alog"], w["dskip"],
                  w["ssd_norm_w"], w["ml_norm_w"], w["lru_conv_w"], w["lru_conv_b"], w["lru_wax"],
                  w["lru_ba"], w["lru_bx"], w["lru_lambda"]]
    assert len(mix_params) == N_MIX_PARAMS and len(prev) == N_STATE_OUTS
    n_in = 10 + N_MIX_PARAMS
    clear_scratch = []
    if clear_shapes:
        for s in clear_shapes:
            assert s[3] == V7X_LANES and s[2] <= CLEAR_ROWS and (s[0] * s[1]) % n_pairs == 0
        outs += [jax.ShapeDtypeStruct(s, f32) for s in clear_shapes]
        per_step = max(s[0] * s[1] // n_pairs for s in clear_shapes)
        clear_scratch = [pltpu.VMEM((CLEAR_ROWS, V7X_LANES), f32),
                         pltpu.SemaphoreType.DMA((len(clear_shapes), per_step))]
    return pl.pallas_call(
        functools.partial(_player_kernel, final=final, n_chunks=n_chunks, nc=nc,
                          n_clear=len(clear_shapes)),
        out_shape=outs,
        grid=(n_pairs + 1,),
        in_specs=[xspec(pair_a), mspec, xspec(pair_c),
                  _layer(w["norm1_w"], l), _layer(w["w_in_p"], l, resident=True)]
        + [_layer(a, l) for a in mix_params]
        + [_layer(w["w_out"], l, resident=True), _layer(w["norm2_w"], l),
           _layer(w["mlp_up"], l, resident=True), _layer(w["mlp_down"], l, resident=True),
           _full((1, d))]
        + [pl.BlockSpec(memory_space=pl.ANY)] * len(prev),
        out_specs=[xspec(pair_c)] + [st_spec(a.shape) for a in prev]
        + [pl.BlockSpec(memory_space=pl.ANY)] * len(clear_shapes),
        input_output_aliases={n_in + i: 1 + i for i in range(len(prev))},
        scratch_shapes=clear_scratch + [
            pltpu.VMEM((CHUNK, NP_COLS), f32), pltpu.VMEM((CHUNK, NP_COLS), f32),
            pltpu.VMEM((CHUNK, MIX_WIDTH), bf16), pltpu.VMEM((CHUNK, MIX_WIDTH), bf16),
            pltpu.VMEM(((CONV_W - 1) * V7X_SUBLANES, SSD_CONV_DIM), f32),
            pltpu.VMEM(((CONV_W - 1) * V7X_SUBLANES, LRU_WIDTH), f32),
            pltpu.VMEM((SSD_GROUPS, SSD_STATE, GROUP_LANES), f32),
            pltpu.VMEM((ML_HEADS, ML_HEAD_DIM, ML_HEAD_DIM), f32),
            pltpu.VMEM((ML_HEADS, 1, ML_HEAD_DIM), f32),
            pltpu.VMEM((ML_HEADS, 1, V7X_LANES), f32),
            pltpu.VMEM((1, LRU_WIDTH), f32),
        ],
        compiler_params=pltpu.CompilerParams(
            dimension_semantics=("arbitrary",), vmem_limit_bytes=PLAYER_VMEM_LIMIT),
        name="prompt_layer",
    )(x, mod, x, w["norm1_w"], w["w_in_p"], *mix_params,
      w["w_out"], w["norm2_w"], w["mlp_up"], w["mlp_down"], fnw, *prev)


def _smix_kernel(u_ref, s0_ref, sconv0_ref, c0_ref, n0_ref, m0_ref, lh0_ref, lconv0_ref,
                 cw_ref, cb_ref, dtb_ref, alog_ref, dskip_ref, snw_ref, ib_ref, fb_ref, mnw_ref,
                 lcw_ref, lcb_ref, wax_ref, ba_ref, bx_ref, lam_ref,
                 y_ref, s_ref, sconv_ref, c_ref, n_ref, m_ref, lh_ref, lconv_ref):
    T = DEC_SEQ
    bb = u_ref.shape[1]
    H = ML_HEAD_DIM
    toks = range(T)

    def useg(t, c0, n):
        return u_ref[t, :, c0:c0 + n]

    def to_seq(rows):
        return jnp.swapaxes(jnp.stack(rows, axis=0), 0, 1)

    def to_tok(a):
        a = jnp.swapaxes(a, 0, 1)
        return [a[t] for t in toks]

    def conv(c0, n, hist_ref, w_ref, b_ref, new_ref):
        xpad = [hist_ref[k] for k in range(CONV_W - 1)] + [useg(t, c0, n) for t in toks]
        for k in range(CONV_W - 1):
            new_ref[k] = xpad[T + k]
        out = []
        for t in toks:
            acc = b_ref[...] + xpad[t] * w_ref[0:1, :]
            for j in range(1, CONV_W):
                acc = acc + xpad[t + j] * w_ref[j:j + 1, :]
            out.append(acc)
        return out

    def split3(d):
        hi = d.astype(bf16).astype(f32)
        r1 = d - hi
        mid = r1.astype(bf16).astype(f32)
        return [hi, mid, (r1 - mid).astype(bf16).astype(f32)]

    def update_lhs(rows, decay):
        pad = [jnp.zeros_like(decay)] * (T - 3)
        return to_seq(rows + split3(decay) + pad).astype(bf16)

    def ones_rhs(kmat):
        zeros = jnp.zeros_like(kmat)
        return jnp.concatenate(
            [jnp.concatenate([kmat, zeros], axis=2),
             jnp.concatenate([zeros, jnp.ones_like(kmat)], axis=2)], axis=1).astype(bf16)

    xbc = [_silu(a) for a in conv(P_XBC, SSD_CONV_DIM, sconv0_ref, cw_ref, cb_ref, sconv_ref)]
    xs = [a[:, 0:SSD_WIDTH] for a in xbc]
    seg = lambda a, j: a[:, SSD_WIDTH + j * SSD_STATE:SSD_WIDTH + (j + 1) * SSD_STATE]
    bm = [[seg(a, g) for a in xbc] for g in range(SSD_GROUPS)]
    cm = [[seg(a, SSD_GROUPS + g) for a in xbc] for g in range(SSD_GROUPS)]
    a_row = -jnp.exp(alog_ref[...]) * LOG2E
    lane = lax.broadcasted_iota(jnp.int32, (bb, SSD_WIDTH), 1)
    cum, xdt = [], []
    for t in toks:
        dt = _softplus(useg(t, S_DT, SSD_WIDTH) + dtb_ref[...])
        cum.append(dt * a_row + (cum[-1] if cum else 0.0))
        xdt.append(xs[t] * dt)
    y = [xs[t] * dskip_ref[...] for t in toks]
    for t in toks:
        for s in range(t + 1):
            cb0 = jnp.sum(cm[0][t] * bm[0][s], axis=-1, keepdims=True)
            cb1 = jnp.sum(cm[1][t] * bm[1][s], axis=-1, keepdims=True)
            term = jnp.where(lane < GROUP_LANES, cb0, cb1) * xdt[s]
            y[t] = y[t] + (term if s == t else term * jnp.exp2(cum[t] - cum[s]))
    inter = []
    for g in range(SSD_GROUPS):
        inter.append(to_tok(jnp.einsum(
            'btn,bpn->btp', to_seq(cm[g]).astype(bf16),
            s0_ref[:, g * GROUP_LANES:(g + 1) * GROUP_LANES, :].astype(bf16), preferred_element_type=f32)))
    for t in toks:
        yt = y[t] + jnp.concatenate([inter[0][t], inter[1][t]], axis=-1) * jnp.exp2(cum[t])
        y_ref[t, :, 0:SSD_WIDTH] = _rms(yt * _silu(useg(t, P_Z, SSD_WIDTH)), snw_ref[...])
    xend = [xdt[s] if s == T - 1 else xdt[s] * jnp.exp2(cum[T - 1] - cum[s]) for s in toks]
    lhs = update_lhs(xend, jnp.exp2(cum[T - 1]))
    for g in range(SSD_GROUPS):
        zz = jnp.einsum('bkp,bkn->bpn', lhs[:, :, g * GROUP_LANES:(g + 1) * GROUP_LANES],
                        ones_rhs(to_seq(bm[g])), preferred_element_type=f32)
        s_ref[:, g * GROUP_LANES:(g + 1) * GROUP_LANES, :] = (
            zz[:, :, SSD_STATE:] * s0_ref[:, g * GROUP_LANES:(g + 1) * GROUP_LANES, :] + zz[:, :, :SSD_STATE])

    def headsum(a):
        return jnp.concatenate(
            [jnp.broadcast_to(jnp.sum(a[:, h * H:(h + 1) * H], axis=-1, keepdims=True), (bb, H))
             for h in range(ML_HEADS)], axis=-1)

    q = [useg(t, P_Q, ML_WIDTH) for t in toks]
    k = [useg(t, P_K, ML_WIDTH) * (H ** -0.5) for t in toks]
    v = [useg(t, P_V, ML_WIDTH) for t in toks]
    ic = [useg(t, S_I, ML_WIDTH) + ib_ref[...] for t in toks]
    bc = []
    for t in toks:
        fc = _log_sigmoid(useg(t, S_F, ML_WIDTH) + fb_ref[...])
        bc.append(fc + (bc[-1] if bc else 0.0))
    m0 = m0_ref[...]
    n0 = n0_ref[...]
    q_seq = to_seq(q).astype(bf16)
    qc = to_tok(jnp.concatenate(
        [jnp.einsum('btk,bvk->btv', q_seq[:, :, h * H:(h + 1) * H],
                    c0_ref[:, h * H:(h + 1) * H, :].astype(bf16), preferred_element_type=f32)
         for h in range(ML_HEADS)], axis=-1))
    mts = []
    for t in toks:
        dm = [bc[t] - bc[s] + ic[s] for s in range(t + 1)]
        g_t = bc[t] + m0
        mt = g_t
        for d in dm:
            mt = jnp.maximum(mt, d)
        mts.append(mt)
        inter_m = jnp.exp(g_t - mt)
        num = inter_m * qc[t]
        den = inter_m * headsum(q[t] * n0)
        for s in range(t + 1):
            w = jnp.exp(dm[s] - mt) * headsum(q[t] * k[s])
            num = num + w * v[s]
            den = den + w
        hout = num / jnp.maximum(jnp.abs(den), jnp.exp(-mt))
        y_ml = (hout * lax.rsqrt(headsum(hout * hout) * (1.0 / H) + EPS) * mnw_ref[...]
                * _sigmoid(useg(t, P_O, ML_WIDTH)))
        y_ref[t, :, SSD_WIDTH:SSD_WIDTH + ML_WIDTH] = y_ml
    m_new = mts[T - 1]
    m_ref[...] = m_new
    w_end = [jnp.exp(bc[T - 1] - bc[s] + ic[s] - m_new) for s in toks]
    dc = jnp.exp(bc[T - 1] + m0 - m_new)
    n_new = dc * n0
    for s in toks:
        n_new = n_new + w_end[s] * k[s]
    n_ref[...] = n_new
    lhs_m = update_lhs([v[s] * w_end[s] for s in toks], dc)
    k_seq = to_seq(k)
    for h in range(ML_HEADS):
        sl = slice(h * H, (h + 1) * H)
        zz = jnp.einsum('bkv,bkn->bvn', lhs_m[:, :, sl], ones_rhs(k_seq[:, :, sl]),
                        preferred_element_type=f32)
        c_ref[:, sl, :] = zz[:, :, H:] * c0_ref[:, sl, :] + zz[:, :, :H]

    xr = jnp.concatenate(conv(P_XR, LRU_WIDTH, lconv0_ref, lcw_ref, lcb_ref, lconv_ref), axis=0)
    rs, is_ = [], []
    for kk in range(LRU_BLOCKS):
        xk = xr[:, kk * LRU_BLOCK_DIM:(kk + 1) * LRU_BLOCK_DIM].astype(bf16)
        ri = jnp.dot(xk, wax_ref[kk], preferred_element_type=f32)
        rs.append(ri[:, 0:LRU_BLOCK_DIM])
        is_.append(ri[:, LRU_BLOCK_DIM:2 * LRU_BLOCK_DIM])
    r = _sigmoid(jnp.concatenate(rs, axis=1) + ba_ref[...])
    i_g = _sigmoid(jnp.concatenate(is_, axis=1) + bx_ref[...])
    log_a = -LRU_C * r * _softplus(-lam_ref[...])
    a_s = jnp.exp(log_a)
    u_s = jnp.sqrt(_neg_expm1(2.0 * log_a)) * i_g * xr
    cur = lh0_ref[...]
    for t in toks:
        rows = slice(t * bb, (t + 1) * bb)
        cur = a_s[rows] * cur + u_s[rows]
        y_ref[t, :, SSD_WIDTH + ML_WIDTH:MIX_WIDTH] = cur * _gelu_tanh(useg(t, P_GR, LRU_WIDTH))
    lh_ref[...] = cur


def _smix(l, u, states, prev, w, bb):
    t, b, _ = u.shape

    def sblk(a, ax):
        shape = tuple(None if d == 0 else bb if d == ax else n for d, n in enumerate(a.shape))
        return pl.BlockSpec(shape, lambda i: tuple(l if d == 0 else i if d == ax else 0
                                                   for d in range(a.ndim)))

    ublk = lambda n: pl.BlockSpec((t, bb, n), lambda i: (0, i, 0))
    outs = [jax.ShapeDtypeStruct((t, b, MIX_WIDTH), f32)]
    outs += [jax.ShapeDtypeStruct(a.shape, f32) for a, _ in states]
    weights = [w["ssd_conv_w"], w["ssd_conv_b"], w["s_dt_bias"], w["s_alog"], w["dskip"],
               w["ssd_norm_w"], w["s_i_bias"], w["s_f_bias"], w["ml_norm_w"],
               w["lru_conv_w"], w["lru_conv_b"], w["lru_wax"], w["lru_ba"], w["lru_bx"],
               w["lru_lambda"]]
    n_in = 1 + len(states) + len(weights)
    prev = list(prev)

    def body(*refs):
        _smix_kernel(*refs[:n_in], *refs[n_in + len(prev):])

    return pl.pallas_call(
        body,
        out_shape=outs,
        grid=(b // bb,),
        in_specs=[ublk(u.shape[-1])] + [sblk(a, ax) for a, ax in states]
        + [_layer(a, l) for a in weights]
        + [pl.BlockSpec(memory_space=pl.ANY)] * len(prev),
        out_specs=[ublk(MIX_WIDTH)] + [sblk(a, ax) for a, ax in states],
        input_output_aliases={n_in + k: 1 + k for k in range(len(prev))},
        compiler_params=pltpu.CompilerParams(
            dimension_semantics=("arbitrary",), vmem_limit_bytes=VMEM_LIMIT),
        name="sample_mix",
    )(u, *[a for a, _ in states], *weights, *prev)


def _prep_params(p):
    w_p, w_exp = _pack_w_in(p["w_in"])
    r = lambda a: a.reshape(N_LAYERS, 1, -1)
    rep = lambda a, n: r(jnp.repeat(a, n, axis=-1))
    small = jnp.concatenate([p["ssd_dt_bias"], p["ml_i_bias"], p["ml_f_bias"]], axis=-1)
    return dict(
        w_in_p=w_p, w_exp=w_exp,
        norm1_w=r(p["norm1_w"]), norm2_w=r(p["norm2_w"]),
        ssd_conv_w=p["ssd_conv_w"], ssd_conv_b=r(p["ssd_conv_b"]),
        p_small_bias=r(jnp.pad(small, ((0, 0), (0, V7X_LANES - small.shape[-1])))),
        p_alog=r(jnp.pad(p["ssd_a_log"], ((0, 0), (0, V7X_LANES - SSD_HEADS)))),
        s_dt_bias=rep(p["ssd_dt_bias"], SSD_HEAD_DIM), s_alog=rep(p["ssd_a_log"], SSD_HEAD_DIM),
        s_i_bias=rep(p["ml_i_bias"], ML_HEAD_DIM), s_f_bias=rep(p["ml_f_bias"], ML_HEAD_DIM),
        dskip=rep(p["ssd_d"], SSD_HEAD_DIM),
        ssd_norm_w=r(p["ssd_norm_w"]), ml_norm_w=r(p["ml_norm_w"]),
        lru_conv_w=p["lru_conv_w"], lru_conv_b=r(p["lru_conv_b"]),
        lru_wax=jnp.concatenate([p["lru_wa"], p["lru_wx"]], axis=-1).astype(bf16),
        lru_ba=r(p["lru_ba"]), lru_bx=r(p["lru_bx"]), lru_lambda=r(p["lru_lambda"]),
        w_out=_cast_pad(p["w_out"]), mlp_up=_cast_pad(p["mlp_up"]),
        mlp_down=_cast_pad(p["mlp_down"]),
    )


def kernel(x_prompt, x_sample, c_prompt, c_sample, state_ssm, state_ssd_conv, state_mlstm_c, state_mlstm_n, state_mlstm_m, state_lru_h, state_lru_conv, ada_w, ada_b, norm1_w, norm2_w, w_in, ssd_conv_w, ssd_conv_b, ssd_dt_bias, ssd_a_log, ssd_d, ssd_norm_w, ml_i_bias, ml_f_bias, ml_norm_w, lru_conv_w, lru_conv_b, lru_wa, lru_ba, lru_wx, lru_bx, lru_lambda, w_out, mlp_up, mlp_down, final_norm_w):
    p = dict(norm1_w=norm1_w, norm2_w=norm2_w, w_in=w_in, ssd_conv_w=ssd_conv_w,
             ssd_conv_b=ssd_conv_b, ssd_dt_bias=ssd_dt_bias, ssd_a_log=ssd_a_log, ssd_d=ssd_d,
             ssd_norm_w=ssd_norm_w, ml_i_bias=ml_i_bias, ml_f_bias=ml_f_bias,
             ml_norm_w=ml_norm_w, lru_conv_w=lru_conv_w, lru_conv_b=lru_conv_b, lru_wa=lru_wa,
             lru_ba=lru_ba, lru_wx=lru_wx, lru_bx=lru_bx, lru_lambda=lru_lambda, w_out=w_out,
             mlp_up=mlp_up, mlp_down=mlp_down)
    w = _prep_params(p)
    fnw = final_norm_w.reshape(1, D_MODEL)
    bp = x_prompt.shape[0]
    bs_ = x_sample.shape[0]

    mod = _ada(jnp.concatenate([c_sample, c_prompt], axis=0), ada_w, ada_b)

    xp = _interleave_rows(x_prompt)
    p_out = [jnp.zeros((N_LAYERS, bp) + s, f32) for s in (
        (SSD_HEADS // 2, 2 * SSD_HEAD_DIM, SSD_STATE),
        (CONV_W - 1, SSD_CONV_DIM), (ML_HEADS, ML_HEAD_DIM, ML_HEAD_DIM),
        (ML_HEADS, ML_HEAD_DIM), (ML_HEADS, V7X_LANES),
        (1, LRU_WIDTH), (CONV_W - 1, LRU_WIDTH))]
    big_states = ((N_LAYERS, bs_, SSD_HEADS * SSD_HEAD_DIM, SSD_STATE),
                  (N_LAYERS, bs_, ML_HEADS * ML_HEAD_DIM, ML_HEAD_DIM))
    for l in range(N_LAYERS):
        res = _player(l, xp, mod, bs_, w, fnw, l == N_LAYERS - 1, p_out,
                      big_states if l == 0 else ())
        xp, p_out = res[0], list(res[1:1 + N_STATE_OUTS])
        if l == 0:
            ssm_buf, c_buf = res[1 + N_STATE_OUTS:]
    xp = _deinterleave_rows(xp)
    ssm, sconv, mc, mn, mm, lh, lconv = p_out
    p_states = (ssm.reshape(N_LAYERS, bp, SSD_HEADS, SSD_HEAD_DIM, SSD_STATE), sconv, mc, mn,
                mm[..., 0], lh.reshape(N_LAYERS, bp, LRU_WIDTH), lconv)

    tok_major = lambda a: jnp.swapaxes(a, -3, -2)
    xs = tok_major(x_sample)
    mod_s = mod
    st_in = (
        (state_ssm.reshape(N_LAYERS, bs_, SSD_HEADS * SSD_HEAD_DIM, SSD_STATE), 1),
        (tok_major(state_ssd_conv), 2),
        (state_mlstm_c.reshape(N_LAYERS, bs_, ML_HEADS * ML_HEAD_DIM, ML_HEAD_DIM), 1),
        (state_mlstm_n.reshape(N_LAYERS, bs_, ML_WIDTH), 1),
        (jnp.repeat(state_mlstm_m, ML_HEAD_DIM, axis=-1), 1),
        (state_lru_h, 1),
        (tok_major(state_lru_conv), 2),
    )
    st_out = [jnp.zeros(a.shape, f32) for a, _ in st_in]
    st_out[0], st_out[2] = ssm_buf, c_buf
    for l in range(N_LAYERS):
        u = _inproj(l, xs, mod_s, w["norm1_w"], w["w_in_p"], w["w_exp"], SAMPLE_IN_ROWS)
        ycat, *st_out = _smix(l, u, st_in, st_out, w, SAMPLE_MIX_ROWS)
        xs = _outmlp(l, xs, ycat, mod_s, w["w_out"], w["norm2_w"], w["mlp_up"], w["mlp_down"],
                     fnw, SAMPLE_OUT_ROWS, l == N_LAYERS - 1)
    xs = tok_major(xs)
    ssm, sconv, mc, mn, mm, lh, lconv = st_out
    s_states = (ssm.reshape(N_LAYERS, bs_, SSD_HEADS, SSD_HEAD_DIM, SSD_STATE), tok_major(sconv),
                mc.reshape(N_LAYERS, bs_, ML_HEADS, ML_HEAD_DIM, ML_HEAD_DIM),
                mn.reshape(N_LAYERS, bs_, ML_HEADS, ML_HEAD_DIM),
                mm.reshape(N_LAYERS, bs_, ML_HEADS, ML_HEAD_DIM)[..., 0],
                lh, tok_major(lconv))

    return (xp, xs) + p_states + s_states
```

```python
import functools
import math

import jax
import jax.numpy as jnp
from jax import lax
from jax.experimental import pallas as pl
from jax.experimental.pallas import tpu as pltpu

f32 = jnp.float32
bf16 = jnp.bfloat16

D_MODEL = 1024
N_LAYERS = 2
MIX_WIDTH = 2 * D_MODEL
SSD_WIDTH = 1024
SSD_HEAD_DIM = 64
SSD_HEADS = 16
SSD_GROUPS = 2
SSD_STATE = 128
SSD_CONV_DIM = SSD_WIDTH + 2 * SSD_GROUPS * SSD_STATE
ML_WIDTH = 512
ML_HEADS = 4
ML_HEAD_DIM = 128
LRU_WIDTH = 512
LRU_BLOCKS = 4
LRU_BLOCK_DIM = 128
LRU_C = 8.0
CONV_W = 4
CHUNK = 128
D_FF = 4 * D_MODEL
EPS = 1e-6
DEC_SEQ = 8

_O_DT, _O_Q, _O_O_END, _O_I, _O_XR, _O_END = 2560, 2576, 4624, 4624, 4632, 5656

P_Z = 0
P_XBC = 1024
P_Q = 2560
P_K = 3072
P_V = 3584
P_O = 4096
P_XR = 4608
P_GR = 5120
P_MAIN = 5632
P_SMALL = 5632
NP_COLS = 5760
S_DT = 5632
S_I = 6656
S_F = 7168
NS_COLS = 7680

V7X_LANES = 128
V7X_SUBLANES = 8
VMEM_LIMIT = 56 * 1024 * 1024
PLAYER_VMEM_LIMIT = 60 * 1024 * 1024
ROW_TILES = CHUNK // V7X_SUBLANES
LOG2E = 1.4426950408889634

SMALL_I = SSD_HEADS
SMALL_F = SSD_HEADS + ML_HEADS
GROUP_LANES = SSD_WIDTH // SSD_GROUPS
SAMPLE_IN_ROWS = 16
SAMPLE_MIX_ROWS = 8
SAMPLE_OUT_ROWS = 32


def _softplus(x):
    return jnp.maximum(x, 0.0) + jnp.log1p(jnp.exp(-jnp.abs(x)))


def _log_sigmoid(x):
    return -_softplus(-x)


def _sigmoid(x):
    return 1.0 / (1.0 + jnp.exp2(x * -LOG2E))


def _silu(x):
    return x * _sigmoid(x)


def _gelu_tanh(x):
    c = 2.0 * math.sqrt(2.0 / math.pi)
    return x * _sigmoid(x * (c + (c * 0.044715) * (x * x)))


def _lru_decay(r, lam_ref):
    log_a = r * (-LRU_C * _softplus(-lam_ref[...]))
    a = jnp.exp(log_a)
    return a, jnp.sqrt(-jnp.tanh(log_a) * (a * a + 1.0))


def _rms(x, w):
    return x * lax.rsqrt(jnp.mean(x * x, axis=-1, keepdims=True) + EPS) * w


def _interleave_rows(x):
    b, t, d = x.shape
    return x.reshape(b, t // CHUNK, V7X_SUBLANES, ROW_TILES, d).swapaxes(2, 3).reshape(b, t, d)


def _deinterleave_rows(x):
    b, t, d = x.shape
    return x.reshape(b, t // CHUNK, ROW_TILES, V7X_SUBLANES, d).swapaxes(2, 3).reshape(b, t, d)


def _full(shape):
    n = len(shape)
    return pl.BlockSpec(shape, lambda *_: (0,) * n)


def _layer(a, l, resident=False):
    nd = a.ndim - 1
    kw = dict(pipeline_mode=pl.Buffered(1)) if resident else {}
    return pl.BlockSpec((None,) + a.shape[1:], lambda *_: (l,) + (0,) * nd, **kw)


def _ada_kernel(c_ref, w_ref, b_ref, o_ref):
    cs = _silu(c_ref[...]).astype(bf16)
    o_ref[0, 0] = jnp.dot(cs, w_ref[0].astype(bf16), preferred_element_type=f32) + b_ref[0]


def _ada(c_all, ada_w, ada_b):
    n = c_all.shape[0]
    tn = D_MODEL
    return pl.pallas_call(
        _ada_kernel,
        out_shape=jax.ShapeDtypeStruct((N_LAYERS, 6, n, D_MODEL), f32),
        grid=(N_LAYERS, 6),
        in_specs=[
            pl.BlockSpec((n, D_MODEL), lambda l, j: (0, 0)),
            pl.BlockSpec((1, D_MODEL, tn), lambda l, j: (l, 0, j)),
            pl.BlockSpec((1, 1, tn), lambda l, j: (l, 0, j)),
        ],
        out_specs=pl.BlockSpec((1, 1, n, tn), lambda l, j: (l, j, 0, 0)),
        compiler_params=pltpu.CompilerParams(
            dimension_semantics=("arbitrary", "arbitrary"), vmem_limit_bytes=VMEM_LIMIT),
        name="ada_mod",
    )(c_all, ada_w, ada_b.reshape(N_LAYERS, 1, 6 * D_MODEL))


def _inproj_kernel(x_ref, mod_ref, nw_ref, wm_ref, we_ref, o_ref):
    x = x_ref[...]
    t, bb, d = x.shape
    hn = _rms(x, nw_ref[...]) * (1.0 + mod_ref[1]) + mod_ref[0]
    hn = hn.reshape(t * bb, d).astype(bf16)
    o_ref[:, :, 0:P_MAIN] = jnp.dot(hn, wm_ref[...], preferred_element_type=f32).reshape(
        t, bb, P_MAIN)
    o_ref[:, :, P_MAIN:NS_COLS] = jnp.dot(hn, we_ref[...], preferred_element_type=f32).reshape(
        t, bb, NS_COLS - P_MAIN)


def _inproj(l, x, mod, nw, w_packed, w_exp, bb):
    t, b, d = x.shape
    return pl.pallas_call(
        _inproj_kernel,
        out_shape=jax.ShapeDtypeStruct((t, b, NS_COLS), f32),
        grid=(b // bb,),
        in_specs=[
            pl.BlockSpec((t, bb, d), lambda i: (0, i, 0)),
            pl.BlockSpec((None, 6, bb, d), lambda i: (l, 0, i, 0)),
            _layer(nw, l),
            pl.BlockSpec((None, d, P_MAIN), lambda i: (l, 0, 0), pipeline_mode=pl.Buffered(1)),
            _layer(w_exp, l, resident=True),
        ],
        out_specs=pl.BlockSpec((t, bb, NS_COLS), lambda i: (0, i, 0)),
        compiler_params=pltpu.CompilerParams(
            dimension_semantics=("arbitrary",), vmem_limit_bytes=VMEM_LIMIT),
        name="inproj",
    )(x, mod, nw, w_packed, w_exp)


PACK_ROWS = 128


def _pack_w_in_kernel(w_ref, wp_ref, we_ref):
    c = lambda a: a.astype(bf16)
    wp_ref[:, 0:P_Q] = c(w_ref[:, 0:_O_DT])
    wp_ref[:, P_Q:P_XR] = c(w_ref[:, _O_Q:_O_O_END])
    wp_ref[:, P_XR:P_MAIN] = c(w_ref[:, _O_XR:_O_END])
    dt = w_ref[:, _O_DT:_O_Q]
    gates = w_ref[:, _O_I:_O_XR]
    n_pad = V7X_LANES - SSD_HEADS - 2 * ML_HEADS
    wp_ref[:, P_SMALL:NP_COLS] = c(jnp.concatenate(
        [dt, gates, jnp.zeros((PACK_ROWS, n_pad), f32)], axis=1))
    rep = lambda a, j, n: jnp.broadcast_to(a[:, j:j + 1], (PACK_ROWS, n))
    we_ref[...] = c(jnp.concatenate(
        [rep(dt, h, SSD_HEAD_DIM) for h in range(SSD_HEADS)]
        + [rep(gates, h, ML_HEAD_DIM) for h in range(2 * ML_HEADS)], axis=1))


def _pack_w_in(w_in):
    n_l, d, n = w_in.shape
    blk = lambda m: pl.BlockSpec((None, PACK_ROWS, m), lambda l, i: (l, i, 0))
    return pl.pallas_call(
        _pack_w_in_kernel,
        out_shape=[jax.ShapeDtypeStruct((n_l, d, NP_COLS), bf16),
                   jax.ShapeDtypeStruct((n_l, d, NS_COLS - P_MAIN), bf16)],
        grid=(n_l, d // PACK_ROWS),
        in_specs=[blk(n)],
        out_specs=[blk(NP_COLS), blk(NS_COLS - P_MAIN)],
        compiler_params=pltpu.CompilerParams(
            dimension_semantics=("arbitrary", "arbitrary"), vmem_limit_bytes=VMEM_LIMIT),
        name="pack_w_in",
    )(w_in)


def _cast_pad_kernel(w_ref, o_ref):
    n = w_ref.shape[-1]
    o_ref[:, 0:n] = w_ref[...].astype(bf16)
    o_ref[:, n:] = jnp.zeros((w_ref.shape[0], o_ref.shape[-1] - n), bf16)


def _cast_pad(a):
    n_l, k, n = a.shape
    rows = 512
    return pl.pallas_call(
        _cast_pad_kernel,
        out_shape=jax.ShapeDtypeStruct((n_l, k, n + V7X_LANES), bf16),
        grid=(n_l, k // rows),
        in_specs=[pl.BlockSpec((None, rows, n), lambda l, i: (l, i, 0))],
        out_specs=pl.BlockSpec((None, rows, n + V7X_LANES), lambda l, i: (l, i, 0)),
        compiler_params=pltpu.CompilerParams(
            dimension_semantics=("arbitrary", "arbitrary"), vmem_limit_bytes=VMEM_LIMIT),
        name="cast_pad",
    )(a)


def _outmlp_kernel(x_ref, y_ref, mod_ref, wout_ref, n2w_ref, up_ref, down_ref, fnw_ref, o_ref,
                   *, final):
    x = x_ref[...]
    t, bb, d = x.shape
    m = t * bb
    ycat = y_ref[...].reshape(m, MIX_WIDTH).astype(bf16)
    mix = jnp.dot(ycat, wout_ref[:, 0:d], preferred_element_type=f32).reshape(t, bb, d)
    x1 = x + mod_ref[2] * mix
    hn = _rms(x1, n2w_ref[...]) * (1.0 + mod_ref[4]) + mod_ref[3]
    hn = hn.reshape(m, d).astype(bf16)
    ff = jnp.zeros((m, d), f32)
    fc = 1024
    for c in range(D_FF // fc):
        h = jnp.dot(hn, up_ref[:, c * fc:(c + 1) * fc], preferred_element_type=f32)
        h = jnp.square(jnp.maximum(h, 0.0)).astype(bf16)
        ff = ff + jnp.dot(h, down_ref[c * fc:(c + 1) * fc, 0:D_MODEL],
                          preferred_element_type=f32)
    x2 = x1 + mod_ref[5] * ff.reshape(t, bb, d)
    if final:
        x2 = _rms(x2, fnw_ref[...])
    o_ref[...] = x2


def _outmlp(l, x, ycat, mod, wout, n2w, up, down, fnw, bb, final):
    t, b, d = x.shape
    return pl.pallas_call(
        functools.partial(_outmlp_kernel, final=final),
        out_shape=jax.ShapeDtypeStruct((t, b, d), f32),
        grid=(b // bb,),
        in_specs=[
            pl.BlockSpec((t, bb, d), lambda i: (0, i, 0)),
            pl.BlockSpec((t, bb, MIX_WIDTH), lambda i: (0, i, 0)),
            pl.BlockSpec((None, 6, bb, d), lambda i: (l, 0, i, 0)),
            _layer(wout, l, resident=True),
            _layer(n2w, l),
            _layer(up, l, resident=True),
            _layer(down, l, resident=True),
            _full((1, d)),
        ],
        out_specs=pl.BlockSpec((t, bb, d), lambda i: (0, i, 0)),
        compiler_params=pltpu.CompilerParams(
            dimension_semantics=("arbitrary",), vmem_limit_bytes=VMEM_LIMIT),
        name="outproj_mlp",
    )(x, ycat, mod, wout, n2w, up, down, fnw)


PIECE = 512


def _stage_in(x, mod, nw_ref, w_ref, u_ref):
    hn = _rms(x, nw_ref[...] * (1.0 + mod(1))) + mod(0)
    hn = hn.astype(bf16)
    yield
    for c0 in range(0, NP_COLS, PIECE):
        c1 = min(c0 + PIECE, NP_COLS)
        u_ref[:, c0:c1] = jnp.dot(hn, w_ref[:, c0:c1], preferred_element_type=f32)
        yield


def _stage_out(x, y_ref, o_ref, rows, mod, wout_ref, n2w_ref, up_ref, down_ref, fnw_ref,
               final):
    halves = [slice(h * PIECE, (h + 1) * PIECE) for h in range(D_MODEL // PIECE)]
    ycat = y_ref[...]
    mix = []
    for hs in halves:
        mix.append(jnp.dot(ycat, wout_ref[:, hs], preferred_element_type=f32))
        yield
    x1 = x + mod(2) * jnp.concatenate(mix, axis=1)
    hn = (_rms(x1, n2w_ref[...] * (1.0 + mod(4))) + mod(3)).astype(bf16)
    yield
    ff = [jnp.zeros((x.shape[0], PIECE), f32) for _ in halves]
    fc = 2 * PIECE
    for c in range(D_FF // fc):
        h = []
        for k in range(fc // PIECE):
            cols = slice(c * fc + k * PIECE, c * fc + (k + 1) * PIECE)
            hk = jnp.dot(hn, up_ref[:, cols], preferred_element_type=f32)
            h.append(jnp.square(jnp.maximum(hk, 0.0)).astype(bf16))
            yield
        h = jnp.concatenate(h, axis=1)
        for i, hs in enumerate(halves):
            ff[i] = ff[i] + jnp.dot(h, down_ref[c * fc:(c + 1) * fc, hs],
                                    preferred_element_type=f32)
            yield
    x2 = x1 + mod(5) * jnp.concatenate(ff, axis=1)
    o_ref[0, rows, :] = _rms(x2, fnw_ref[...]) if final else x2
    yield


def _trace_interleaved(main, fillers, n_main, n_fill):
    fillers = list(fillers)
    due = 0.0
    nxt = 0
    main_live = True
    while main_live or fillers:
        if main_live:
            main_live = next(main, _DONE) is not _DONE
            due += n_fill / n_main
        else:
            due = float(len(fillers))
        while fillers and due >= 1.0:
            nxt %= len(fillers)
            if next(fillers[nxt], _DONE) is _DONE:
                fillers.pop(nxt)
            else:
                nxt += 1
                due -= 1.0


_DONE = object()


def _reset_state_if(first, state):
    for ref in state:
        ref[...] = jnp.where(first, 0.0, ref[...])


def _stage_mix(u_ref, y_ref, prm, state, ci, outs):
    (cw_ref, cb_ref, sbias_ref, alog_ref, dskip_ref, snw_ref, mnw_ref,
     lcw_ref, lcb_ref, wax_ref, ba_ref, bx_ref, lam_ref) = prm
    xbuf, lbuf, st, cst, nst, mst, hst = state
    L = CHUNK
    SUB = V7X_SUBLANES
    HIST = (CONV_W - 1) * SUB

    row = lax.broadcasted_iota(jnp.int32, (L, L), 0)
    col = lax.broadcasted_iota(jnp.int32, (L, L), 1)
    tok = lambda i: (i & (SUB - 1)) * ROW_TILES + (i >> 3)
    tri = tok(row) >= tok(col)
    lo = col < SSD_HEAD_DIM

    def bcl(a, j):
        return jnp.broadcast_to(a[:, j:j + 1], (L, L))

    def conv(u0, hist_ref, w_ref, b_ref, c0, c1):
        raw = u_ref[:, u0 + c0:u0 + c1]
        sub = lax.broadcasted_iota(jnp.int32, (SUB, c1 - c0), 0)
        prev = [pltpu.roll(jnp.where(sub == SUB - 1, hist_ref[k * SUB:(k + 1) * SUB, c0:c1],
                                     raw[L - HIST + k * SUB:L - HIST + (k + 1) * SUB, :]), 1, 0)
                for k in range(CONV_W - 1)]
        acc = b_ref[:, c0:c1] + raw * w_ref[CONV_W - 1:CONV_W, c0:c1]
        for d in range(1, CONV_W):
            shifted = jnp.concatenate(prev[CONV_W - 1 - d:] + [raw[0:L - d * SUB, :]], axis=0)
            acc = acc + shifted * w_ref[CONV_W - 1 - d:CONV_W - d, c0:c1]
        hist_ref[:, c0:c1] = raw[L - HIST:L, :]
        return acc

    cblk = 512
    xbc = []
    for c0 in range(0, SSD_CONV_DIM, cblk):
        xbc.append(_silu(conv(P_XBC, xbuf, cw_ref, cb_ref, c0, c0 + cblk)))
        yield
    xbc = jnp.concatenate(xbc, axis=1)
    xr = conv(P_XR, lbuf, lcw_ref, lcb_ref, 0, LRU_WIDTH)
    yield

    sm = u_ref[:,P_SMALL:P_SMALL + V7X_LANES] + sbias_ref[...]
    dt = _softplus(sm)
    a_row = -jnp.exp(alog_ref[...])
    gates = jnp.where(col < SSD_HEADS, dt * (a_row * LOG2E),
                      jnp.where((col >= SMALL_F) & (col < SMALL_F + ML_HEADS),
                                _log_sigmoid(sm), 0.0))
    g_hi = gates.astype(bf16)
    g_r = gates - g_hi.astype(f32)
    g_mid = g_r.astype(bf16)
    g_lo = (g_r - g_mid.astype(f32)).astype(bf16)
    cum3 = jnp.dot(jnp.where(tri, 1.0, 0.0).astype(bf16),
                   jnp.concatenate([g_hi, g_mid, g_lo], axis=1), preferred_element_type=f32)
    cum = cum3[:, 0:L] + cum3[:, L:2 * L] + cum3[:, 2 * L:3 * L]
    cum_t = cum.T
    sm_t = sm.T
    yield

    ys = []
    for g in range(SSD_GROUPS):
        bm_t = xbc[:, SSD_WIDTH + g * SSD_STATE:SSD_WIDTH + (g + 1) * SSD_STATE].T.astype(bf16)
        cm = xbc[:, SSD_WIDTH + (SSD_GROUPS + g) * SSD_STATE:
                 SSD_WIDTH + (SSD_GROUPS + g + 1) * SSD_STATE].astype(bf16)
        cb = jnp.dot(cm, bm_t, preferred_element_type=f32)
        inter = jnp.dot(cm, st[g].astype(bf16), preferred_element_type=f32)
        yield
        xends, keeps = [], []
        for jj in range(4):
            jp = g * 4 + jj
            e0, e1 = 2 * jp, 2 * jp + 1
            c0 = bcl(cum, e0)
            c1 = bcl(cum, e1)
            cum_p = jnp.where(lo, c0, c1)
            dt_p = jnp.where(lo, bcl(dt, e0), bcl(dt, e1))
            xs_p = xbc[:, jp * L:(jp + 1) * L]
            xdt = xs_p * dt_p
            dec0 = jnp.exp2(jnp.where(tri, c0 - cum_t[e0:e0 + 1, :], -jnp.inf))
            dec1 = jnp.exp2(jnp.where(tri, c1 - cum_t[e1:e1 + 1, :], -jnp.inf))
            yield
            att =jnp.concatenate([(cb * dec0).astype(bf16), (cb * dec1).astype(bf16)], axis=1)
            x2 = jnp.concatenate([jnp.where(lo, xdt, 0.0).astype(bf16),
                                  jnp.where(lo, 0.0, xdt).astype(bf16)], axis=0)
            y_p = (jnp.dot(att, x2, preferred_element_type=f32)
                   + inter[:, jj * L:(jj + 1) * L] * jnp.exp2(cum_p)
                   + xs_p * dskip_ref[:, jp * L:(jp + 1) * L])
            ys.append(y_p)
            c_last = cum_p[L - 1:L, :]
            xends.append((xdt * jnp.exp2(c_last - cum_p)).astype(bf16))
            keeps.append(jnp.exp2(c_last))
            yield
        st[g] = (jnp.concatenate(keeps, axis=1) * st[g]
                 + jnp.dot(bm_t, jnp.concatenate(xends, axis=1), preferred_element_type=f32))
    y = jnp.concatenate(ys, axis=1)
    z = u_ref[:,P_Z:P_Z + SSD_WIDTH]
    y_ref[:,0:SSD_WIDTH] = _rms(y * _silu(z), snw_ref[...]).astype(bf16)
    yield

    scale = ML_HEAD_DIM ** -0.5
    for h in range(ML_HEADS):
        sl = slice(h * ML_HEAD_DIM, (h + 1) * ML_HEAD_DIM)
        q = u_ref[:,P_Q + h * ML_HEAD_DIM:P_Q + (h + 1) * ML_HEAD_DIM]
        k = u_ref[:,P_K + h * ML_HEAD_DIM:P_K + (h + 1) * ML_HEAD_DIM] * scale
        v = u_ref[:,P_V + h * ML_HEAD_DIM:P_V + (h + 1) * ML_HEAD_DIM]
        o = u_ref[:,P_O + h * ML_HEAD_DIM:P_O + (h + 1) * ML_HEAD_DIM]
        k_t = k.T.astype(bf16)
        jf = SMALL_F + h
        ji = SMALL_I + h
        b_col = cum[:, jf:jf + 1]
        i_col = sm[:, ji:ji + 1]
        dm = jnp.where(tri, bcl(cum, jf) - cum_t[jf:jf + 1, :] + sm_t[ji:ji + 1, :], -jnp.inf)
        m_prev = mst[h, :, 0:1]
        g_col = b_col + m_prev
        mt = jnp.maximum(g_col, jnp.max(dm, axis=-1, keepdims=True))
        yield
        w = jnp.exp(dm - mt) * jnp.dot(q.astype(bf16), k_t, preferred_element_type=f32)
        inter = jnp.exp(g_col - mt)
        yield
        ct_old = cst[h]
        n_old = nst[h]
        num = jnp.dot(jnp.concatenate([w.astype(bf16), (inter * q).astype(bf16)], axis=1),
                      jnp.concatenate([v.astype(bf16), ct_old.astype(bf16)], axis=0),
                      preferred_element_type=f32)
        den = (jnp.sum(w, axis=-1, keepdims=True)
               + inter * jnp.sum(q * n_old, axis=-1, keepdims=True))
        hout = num / jnp.maximum(jnp.abs(den), jnp.exp(-mt))
        yield
        m_new = mt[L - 1:L, :]
        b_last = b_col[L - 1:L, :]
        w_end = jnp.exp(b_last - b_col + i_col - m_new)
        dc = jnp.exp(b_last + m_prev - m_new)
        cst[h] = dc * ct_old + jnp.dot(k_t, (v * w_end).astype(bf16),
                                       preferred_element_type=f32)
        nst[h] = dc * n_old + jnp.sum(w_end * k, axis=0, keepdims=True)
        mst[h] = jnp.broadcast_to(m_new, (1, V7X_LANES))
        yh = _rms(hout, mnw_ref[:, sl]) * _sigmoid(o)
        y_ref[:,SSD_WIDTH + h * ML_HEAD_DIM:SSD_WIDTH + (h + 1) * ML_HEAD_DIM] = yh.astype(bf16)
        yield

    rs, is_ = [], []
    for kb_ in range(LRU_BLOCKS):
        xk = xr[:, kb_ * LRU_BLOCK_DIM:(kb_ + 1) * LRU_BLOCK_DIM].astype(bf16)
        ri = jnp.dot(xk, wax_ref[kb_], preferred_element_type=f32)
        rs.append(ri[:, 0:LRU_BLOCK_DIM])
        is_.append(ri[:, LRU_BLOCK_DIM:2 * LRU_BLOCK_DIM])
    r = _sigmoid(jnp.concatenate(rs, axis=1) + ba_ref[...])
    i_g = _sigmoid(jnp.concatenate(is_, axis=1) + bx_ref[...])
    yield
    a_s, mult = _lru_decay(r, lam_ref)
    rowl = lax.broadcasted_iota(jnp.int32, (L, LRU_WIDTH), 0)
    first_row = jnp.where(ci == 0, 0, -1)
    mult = jnp.where(rowl == first_row, 1.0, mult)
    u_s = mult * i_g * xr
    yield
    tile = lambda a, v: a[v * SUB:(v + 1) * SUB, :]
    h_loc = [tile(u_s, 0)]
    a_cum = [tile(a_s, 0)]
    for v in range(1, ROW_TILES):
        h_loc.append(tile(a_s, v) * h_loc[-1] + tile(u_s, v))
        a_cum.append(tile(a_s, v) * a_cum[-1])
    a_e, h_e = a_cum[-1], h_loc[-1]
    subl = lax.broadcasted_iota(jnp.int32, (SUB, LRU_WIDTH), 0)
    k_ = 1
    while k_ < SUB:
        keep = subl >= k_
        a_sh = jnp.where(keep, pltpu.roll(a_e, k_, 0), 1.0)
        h_sh = jnp.where(keep, pltpu.roll(h_e, k_, 0), 0.0)
        h_e = a_e * h_sh + h_e
        a_e = a_e * a_sh
        k_ *= 2
    h0 = hst[...]
    run_end = h_e + a_e * h0
    run_in = jnp.where(subl == 0, h0, pltpu.roll(run_end, 1, 0))
    hr = jnp.concatenate([h_loc[v] + a_cum[v] * run_in for v in range(ROW_TILES)], axis=0)
    hst[...] = run_end[SUB - 1:SUB, :]
    gr = u_ref[:,P_GR:P_GR + LRU_WIDTH]
    y_ref[:,SSD_WIDTH + ML_WIDTH:MIX_WIDTH] = (hr * _gelu_tanh(gr)).astype(bf16)
    yield

    if outs is not None:
        ssm_ref, sconv_ref, mc_ref, mn_ref, mm_ref, lh_ref, lconv_ref = outs
        for g in range(SSD_GROUPS):
            for jj in range(4):
                ssm_ref[0, g * 4 + jj] = st[g, :, jj * L:(jj + 1) * L].T
        for k in range(CONV_W - 1):
            sconv_ref[0, k:k + 1, :] = xbuf[k * SUB + SUB - 1:(k + 1) * SUB, :]
            lconv_ref[0, k:k + 1, :] = lbuf[k * SUB + SUB - 1:(k + 1) * SUB, :]
        for h in range(ML_HEADS):
            mc_ref[0, h] = cst[h].T
            mn_ref[0, h:h + 1, :] = nst[h]
            mm_ref[0, h:h + 1, :] = mst[h]
        lh_ref[0] = hst[...]


N_MIX_PARAMS = 13
N_STATE_OUTS = 7
N_MIX_PHASES = 44
N_FILL_PIECES = (1 + -(-NP_COLS // PIECE)) + (D_MODEL // PIECE + 1
                                              + (D_FF // (2 * PIECE)) * (2 + D_MODEL // PIECE) + 1)


def _player_kernel(*refs, final, n_chunks, nc, n_clear):
    xa_ref, mod_ref, xc_ref, n1w_ref, win_ref = refs[:5]
    prm = refs[5:5 + N_MIX_PARAMS]
    wout_ref, n2w_ref, up_ref, down_ref, fnw_ref = refs[5 + N_MIX_PARAMS:10 + N_MIX_PARAMS]
    n_in = 10 + N_MIX_PARAMS + N_STATE_OUTS
    o_ref = refs[n_in]
    outs = refs[n_in + 1:n_in + 1 + N_STATE_OUTS]
    clear = refs[n_in + 1 + N_STATE_OUTS:n_in + 1 + N_STATE_OUTS + n_clear]
    u_bufs = refs[-11:-9]
    y_bufs = refs[-9:-7]
    state = refs[-7:]
    k = pl.program_id(0)
    n_steps = n_chunks // 2

    @pl.when(k == 0)
    def _():
        u_bufs[1][...] = jnp.zeros(u_bufs[1].shape, f32)
        y_bufs[1][...] = jnp.zeros(y_bufs[1].shape, bf16)
        for ref in state:
            ref[...] = jnp.zeros(ref.shape, f32)
        if n_clear:
            refs[-13][...] = jnp.zeros(refs[-13].shape, f32)

    def clear_copies():
        zeros, sems = refs[-13], refs[-12]
        copies = []
        for ci_, buf in enumerate(clear):
            n_b, rows = buf.shape[1], buf.shape[2]
            per_step = buf.shape[0] * n_b // n_steps
            for i in range(per_step):
                j = k * per_step + i
                copies.append(pltpu.make_async_copy(
                    zeros.at[0:rows], buf.at[j // n_b, j % n_b], sems.at[ci_, i]))
        return copies

    if n_clear:
        @pl.when(k < n_steps)
        def _():
            for cp in clear_copies():
                cp.start()

    pair_seq = lambda pair: jnp.clip(pair, 0, n_steps - 1) // (nc // 2)
    mod_of = lambda seq: (lambda c: mod_ref[c, pl.ds(seq, 1), :])
    mod_a, mod_c = mod_of(pair_seq(k)), mod_of(pair_seq(k - 1))

    for sub in range(2):
        rows = slice(sub * CHUNK, (sub + 1) * CHUNK)
        stage_in = _stage_in(xa_ref[0, rows, :], mod_a, n1w_ref, win_ref, u_bufs[sub])
        jb = jnp.clip(2 * k + sub - 1, 0, n_chunks - 1)
        ci = jb % nc
        if sub == 1:
            _reset_state_if(ci == 0, state)
        stage_mix = _stage_mix(u_bufs[1 - sub], y_bufs[sub], prm, state, ci,
                               outs if sub == 0 else None)
        stage_out = _stage_out(xc_ref[0, rows, :], y_bufs[1 - sub], o_ref, rows, mod_c,
                               wout_ref, n2w_ref, up_ref, down_ref, fnw_ref, final)
        _trace_interleaved(stage_mix, [stage_in, stage_out], N_MIX_PHASES, N_FILL_PIECES)

    if n_clear:
        @pl.when(k < n_steps)
        def _():
            for cp in clear_copies():
                cp.wait()


CLEAR_ROWS = 1024


def _player(l, x, mod, mod_row0, w, fnw, final, prev, clear_shapes=()):
    b, t, d = x.shape
    nc = t // CHUNK
    n_chunks = b * nc
    n_pairs = n_chunks // 2
    ppr = nc // 2
    pair_a = lambda k: jnp.minimum(k, n_pairs - 1)
    pair_c = lambda k: jnp.clip(k - 1, 0, n_pairs - 1)
    seq_b = lambda k: jnp.clip(2 * k - 1, 0, n_chunks - 1) // nc
    xspec = lambda pair: pl.BlockSpec((1, 2 * CHUNK, d), lambda k: (pair(k) // ppr, pair(k) % ppr, 0))
    assert b <= V7X_SUBLANES and mod_row0 % V7X_SUBLANES == 0
    mspec = pl.BlockSpec((None, 6, V7X_SUBLANES, d), lambda k: (l, 0, mod_row0 // V7X_SUBLANES, 0))
    st_spec = lambda s: pl.BlockSpec((None, 1) + s[2:],
                                     lambda k: (l, seq_b(k)) + (0,) * (len(s) - 2))
    outs = [jax.ShapeDtypeStruct((b, t, d), f32)]
    outs += [jax.ShapeDtypeStruct(a.shape, f32) for a in prev]
    mix_params = [w["ssd_conv_w"], w["ssd_conv_b"], w["p_small_bias"], w["p_alog"], w["dskip"],
                  w["ssd_norm_w"], w["ml_norm_w"], w["lru_conv_w"], w["lru_conv_b"], w["lru_wax"],
                  w["lru_ba"], w["lru_bx"], w["lru_lambda"]]
    assert len(mix_params) == N_MIX_PARAMS and len(prev) == N_STATE_OUTS
    n_in = 10 + N_MIX_PARAMS
    clear_scratch = []
    if clear_shapes:
        for s in clear_shapes:
            assert s[3] == V7X_LANES and s[2] <= CLEAR_ROWS and (s[0] * s[1]) % n_pairs == 0
        outs += [jax.ShapeDtypeStruct(s, f32) for s in clear_shapes]
        per_step = max(s[0] * s[1] // n_pairs for s in clear_shapes)
        clear_scratch = [pltpu.VMEM((CLEAR_ROWS, V7X_LANES), f32),
                         pltpu.SemaphoreType.DMA((len(clear_shapes), per_step))]
    return pl.pallas_call(
        functools.partial(_player_kernel, final=final, n_chunks=n_chunks, nc=nc,
                          n_clear=len(clear_shapes)),
        out_shape=outs,
        grid=(n_pairs + 1,),
        in_specs=[xspec(pair_a), mspec, xspec(pair_c),
                  _layer(w["norm1_w"], l), _layer(w["w_in_p"], l, resident=True)]
        + [_layer(a, l) for a in mix_params]
        + [_layer(w["w_out"], l, resident=True), _layer(w["norm2_w"], l),
           _layer(w["mlp_up"], l, resident=True), _layer(w["mlp_down"], l, resident=True),
           _full((1, d))]
        + [pl.BlockSpec(memory_space=pl.ANY)] * len(prev),
        out_specs=[xspec(pair_c)] + [st_spec(a.shape) for a in prev]
        + [pl.BlockSpec(memory_space=pl.ANY)] * len(clear_shapes),
        input_output_aliases={n_in + i: 1 + i for i in range(len(prev))},
        scratch_shapes=clear_scratch + [
            pltpu.VMEM((CHUNK, NP_COLS), f32), pltpu.VMEM((CHUNK, NP_COLS), f32),
            pltpu.VMEM((CHUNK, MIX_WIDTH), bf16), pltpu.VMEM((CHUNK, MIX_WIDTH), bf16),
            pltpu.VMEM(((CONV_W - 1) * V7X_SUBLANES, SSD_CONV_DIM), f32),
            pltpu.VMEM(((CONV_W - 1) * V7X_SUBLANES, LRU_WIDTH), f32),
            pltpu.VMEM((SSD_GROUPS, SSD_STATE, GROUP_LANES), f32),
            pltpu.VMEM((ML_HEADS, ML_HEAD_DIM, ML_HEAD_DIM), f32),
            pltpu.VMEM((ML_HEADS, 1, ML_HEAD_DIM), f32),
            pltpu.VMEM((ML_HEADS, 1, V7X_LANES), f32),
            pltpu.VMEM((1, LRU_WIDTH), f32),
        ],
        compiler_params=pltpu.CompilerParams(
            dimension_semantics=("arbitrary",), vmem_limit_bytes=PLAYER_VMEM_LIMIT),
        name="prompt_layer",
    )(x, mod, x, w["norm1_w"], w["w_in_p"], *mix_params,
      w["w_out"], w["norm2_w"], w["mlp_up"], w["mlp_down"], fnw, *prev)


def _smix_kernel(u_ref, s0_ref, sconv0_ref, c0_ref, n0_ref, m0_ref, lh0_ref, lconv0_ref,
                 cw_ref, cb_ref, dtb_ref, alog_ref, dskip_ref, snw_ref, ib_ref, fb_ref, mnw_ref,
                 lcw_ref, lcb_ref, wax_ref, ba_ref, bx_ref, lam_ref,
                 y_ref, s_ref, sconv_ref, c_ref, n_ref, m_ref, lh_ref, lconv_ref):
    T = DEC_SEQ
    bb = u_ref.shape[1]
    H = ML_HEAD_DIM
    toks = range(T)

    def useg(t, c0, n):
        return u_ref[t, :, c0:c0 + n]

    def to_seq(rows):
        return jnp.swapaxes(jnp.stack(rows, axis=0), 0, 1)

    def to_tok(a):
        a = jnp.swapaxes(a, 0, 1)
        return [a[t] for t in toks]

    def conv(c0, n, hist_ref, w_ref, b_ref, new_ref):
        xpad = [hist_ref[k] for k in range(CONV_W - 1)] + [useg(t, c0, n) for t in toks]
        for k in range(CONV_W - 1):
            new_ref[k] = xpad[T + k]
        out = []
        for t in toks:
            acc = b_ref[...] + xpad[t] * w_ref[0:1, :]
            for j in range(1, CONV_W):
                acc = acc + xpad[t + j] * w_ref[j:j + 1, :]
            out.append(acc)
        return out

    def split3(d):
        hi = d.astype(bf16).astype(f32)
        r1 = d - hi
        mid = r1.astype(bf16).astype(f32)
        return [hi, mid, (r1 - mid).astype(bf16).astype(f32)]

    def update_lhs(rows, decay):
        pad = [jnp.zeros_like(decay)] * (T - 3)
        return to_seq(rows + split3(decay) + pad).astype(bf16)

    def ones_rhs(kmat):
        zeros = jnp.zeros_like(kmat)
        return jnp.concatenate(
            [jnp.concatenate([kmat, zeros], axis=2),
             jnp.concatenate([zeros, jnp.ones_like(kmat)], axis=2)], axis=1).astype(bf16)

    xbc = [_silu(a) for a in conv(P_XBC, SSD_CONV_DIM, sconv0_ref, cw_ref, cb_ref, sconv_ref)]
    xs = [a[:, 0:SSD_WIDTH] for a in xbc]
    seg = lambda a, j: a[:, SSD_WIDTH + j * SSD_STATE:SSD_WIDTH + (j + 1) * SSD_STATE]
    bm = [[seg(a, g) for a in xbc] for g in range(SSD_GROUPS)]
    cm = [[seg(a, SSD_GROUPS + g) for a in xbc] for g in range(SSD_GROUPS)]
    a_row = -jnp.exp(alog_ref[...]) * LOG2E
    lane = lax.broadcasted_iota(jnp.int32, (bb, SSD_WIDTH), 1)
    cum, xdt = [], []
    for t in toks:
        dt = _softplus(useg(t, S_DT, SSD_WIDTH) + dtb_ref[...])
        cum.append(dt * a_row + (cum[-1] if cum else 0.0))
        xdt.append(xs[t] * dt)
    y = [xs[t] * dskip_ref[...] for t in toks]
    for t in toks:
        for s in range(t + 1):
            cb0 = jnp.sum(cm[0][t] * bm[0][s], axis=-1, keepdims=True)
            cb1 = jnp.sum(cm[1][t] * bm[1][s], axis=-1, keepdims=True)
            term = jnp.where(lane < GROUP_LANES, cb0, cb1) * xdt[s]
            y[t] = y[t] + (term if s == t else term * jnp.exp2(cum[t] - cum[s]))
    inter = []
    for g in range(SSD_GROUPS):
        inter.append(to_tok(jnp.einsum(
            'btn,bpn->btp', to_seq(cm[g]).astype(bf16),
            s0_ref[:, g * GROUP_LANES:(g + 1) * GROUP_LANES, :].astype(bf16), preferred_element_type=f32)))
    for t in toks:
        yt = y[t] + jnp.concatenate([inter[0][t], inter[1][t]], axis=-1) * jnp.exp2(cum[t])
        y_ref[t, :, 0:SSD_WIDTH] = _rms(yt * _silu(useg(t, P_Z, SSD_WIDTH)), snw_ref[...])
    xend = [xdt[s] if s == T - 1 else xdt[s] * jnp.exp2(cum[T - 1] - cum[s]) for s in toks]
    lhs = update_lhs(xend, jnp.exp2(cum[T - 1]))
    for g in range(SSD_GROUPS):
        zz = jnp.einsum('bkp,bkn->bpn', lhs[:, :, g * GROUP_LANES:(g + 1) * GROUP_LANES],
                        ones_rhs(to_seq(bm[g])), preferred_element_type=f32)
        s_ref[:, g * GROUP_LANES:(g + 1) * GROUP_LANES, :] = (
            zz[:, :, SSD_STATE:] * s0_ref[:, g * GROUP_LANES:(g + 1) * GROUP_LANES, :] + zz[:, :, :SSD_STATE])

    def headsum(a):
        return jnp.concatenate(
            [jnp.broadcast_to(jnp.sum(a[:, h * H:(h + 1) * H], axis=-1, keepdims=True), (bb, H))
             for h in range(ML_HEADS)], axis=-1)

    q = [useg(t, P_Q, ML_WIDTH) for t in toks]
    k = [useg(t, P_K, ML_WIDTH) * (H ** -0.5) for t in toks]
    v = [useg(t, P_V, ML_WIDTH) for t in toks]
    ic = [useg(t, S_I, ML_WIDTH) + ib_ref[...] for t in toks]
    bc = []
    for t in toks:
        fc = _log_sigmoid(useg(t, S_F, ML_WIDTH) + fb_ref[...])
        bc.append(fc + (bc[-1] if bc else 0.0))
    m0 = m0_ref[...]
    n0 = n0_ref[...]
    q_seq = to_seq(q).astype(bf16)
    qc = to_tok(jnp.concatenate(
        [jnp.einsum('btk,bvk->btv', q_seq[:, :, h * H:(h + 1) * H],
                    c0_ref[:, h * H:(h + 1) * H, :].astype(bf16), preferred_element_type=f32)
         for h in range(ML_HEADS)], axis=-1))
    mts = []
    for t in toks:
        dm = [bc[t] - bc[s] + ic[s] for s in range(t + 1)]
        g_t = bc[t] + m0
        mt = g_t
        for d in dm:
            mt = jnp.maximum(mt, d)
        mts.append(mt)
        inter_m = jnp.exp(g_t - mt)
        num = inter_m * qc[t]
        den = inter_m * headsum(q[t] * n0)
        for s in range(t + 1):
            w = jnp.exp(dm[s] - mt) * headsum(q[t] * k[s])
            num = num + w * v[s]
            den = den + w
        hout = num / jnp.maximum(jnp.abs(den), jnp.exp(-mt))
        y_ml = (hout * lax.rsqrt(headsum(hout * hout) * (1.0 / H) + EPS) * mnw_ref[...]
                * _sigmoid(useg(t, P_O, ML_WIDTH)))
        y_ref[t, :, SSD_WIDTH:SSD_WIDTH + ML_WIDTH] = y_ml
    m_new = mts[T - 1]
    m_ref[...] = m_new
    w_end = [jnp.exp(bc[T - 1] - bc[s] + ic[s] - m_new) for s in toks]
    dc = jnp.exp(bc[T - 1] + m0 - m_new)
    n_new = dc * n0
    for s in toks:
        n_new = n_new + w_end[s] * k[s]
    n_ref[...] = n_new
    lhs_m = update_lhs([v[s] * w_end[s] for s in toks], dc)
    k_seq = to_seq(k)
    for h in range(ML_HEADS):
        sl = slice(h * H, (h + 1) * H)
        zz = jnp.einsum('bkv,bkn->bvn', lhs_m[:, :, sl], ones_rhs(k_seq[:, :, sl]),
                        preferred_element_type=f32)
        c_ref[:, sl, :] = zz[:, :, H:] * c0_ref[:, sl, :] + zz[:, :, :H]

    xr = jnp.concatenate(conv(P_XR, LRU_WIDTH, lconv0_ref, lcw_ref, lcb_ref, lconv_ref), axis=0)
    rs, is_ = [], []
    for kk in range(LRU_BLOCKS):
        xk = xr[:, kk * LRU_BLOCK_DIM:(kk + 1) * LRU_BLOCK_DIM].astype(bf16)
        ri = jnp.dot(xk, wax_ref[kk], preferred_element_type=f32)
        rs.append(ri[:, 0:LRU_BLOCK_DIM])
        is_.append(ri[:, LRU_BLOCK_DIM:2 * LRU_BLOCK_DIM])
    r = _sigmoid(jnp.concatenate(rs, axis=1) + ba_ref[...])
    i_g = _sigmoid(jnp.concatenate(is_, axis=1) + bx_ref[...])
    a_s, mult = _lru_decay(r, lam_ref)
    u_s = mult * i_g * xr
    cur = lh0_ref[...]
    for t in toks:
        rows = slice(t * bb, (t + 1) * bb)
        cur = a_s[rows] * cur + u_s[rows]
        y_ref[t, :, SSD_WIDTH + ML_WIDTH:MIX_WIDTH] = cur * _gelu_tanh(useg(t, P_GR, LRU_WIDTH))
    lh_ref[...] = cur


def _smix(l, u, states, prev, w, bb):
    t, b, _ = u.shape

    def sblk(a, ax):
        shape = tuple(None if d == 0 else bb if d == ax else n for d, n in enumerate(a.shape))
        return pl.BlockSpec(shape, lambda i: tuple(l if d == 0 else i if d == ax else 0
                                                   for d in range(a.ndim)))

    ublk = lambda n: pl.BlockSpec((t, bb, n), lambda i: (0, i, 0))
    outs = [jax.ShapeDtypeStruct((t, b, MIX_WIDTH), f32)]
    outs += [jax.ShapeDtypeStruct(a.shape, f32) for a, _ in states]
    weights = [w["ssd_conv_w"], w["ssd_conv_b"], w["s_dt_bias"], w["s_alog"], w["dskip"],
               w["ssd_norm_w"], w["s_i_bias"], w["s_f_bias"], w["ml_norm_w"],
               w["lru_conv_w"], w["lru_conv_b"], w["lru_wax"], w["lru_ba"], w["lru_bx"],
               w["lru_lambda"]]
    n_in = 1 + len(states) + len(weights)
    prev = list(prev)

    def body(*refs):
        _smix_kernel(*refs[:n_in], *refs[n_in + len(prev):])

    return pl.pallas_call(
        body,
        out_shape=outs,
        grid=(b // bb,),
        in_specs=[ublk(u.shape[-1])] + [sblk(a, ax) for a, ax in states]
        + [_layer(a, l) for a in weights]
        + [pl.BlockSpec(memory_space=pl.ANY)] * len(prev),
        out_specs=[ublk(MIX_WIDTH)] + [sblk(a, ax) for a, ax in states],
        input_output_aliases={n_in + k: 1 + k for k in range(len(prev))},
        compiler_params=pltpu.CompilerParams(
            dimension_semantics=("arbitrary",), vmem_limit_bytes=VMEM_LIMIT),
        name="sample_mix",
    )(u, *[a for a, _ in states], *weights, *prev)


def _prep_params(p):
    w_p, w_exp = _pack_w_in(p["w_in"])
    r = lambda a: a.reshape(N_LAYERS, 1, -1)
    rep = lambda a, n: r(jnp.repeat(a, n, axis=-1))
    small = jnp.concatenate([p["ssd_dt_bias"], p["ml_i_bias"], p["ml_f_bias"]], axis=-1)
    return dict(
        w_in_p=w_p, w_exp=w_exp,
        norm1_w=r(p["norm1_w"]), norm2_w=r(p["norm2_w"]),
        ssd_conv_w=p["ssd_conv_w"], ssd_conv_b=r(p["ssd_conv_b"]),
        p_small_bias=r(jnp.pad(small, ((0, 0), (0, V7X_LANES - small.shape[-1])))),
        p_alog=r(jnp.pad(p["ssd_a_log"], ((0, 0), (0, V7X_LANES - SSD_HEADS)))),
        s_dt_bias=rep(p["ssd_dt_bias"], SSD_HEAD_DIM), s_alog=rep(p["ssd_a_log"], SSD_HEAD_DIM),
        s_i_bias=rep(p["ml_i_bias"], ML_HEAD_DIM), s_f_bias=rep(p["ml_f_bias"], ML_HEAD_DIM),
        dskip=rep(p["ssd_d"], SSD_HEAD_DIM),
        ssd_norm_w=r(p["ssd_norm_w"]), ml_norm_w=r(p["ml_norm_w"]),
        lru_conv_w=p["lru_conv_w"], lru_conv_b=r(p["lru_conv_b"]),
        lru_wax=jnp.concatenate([p["lru_wa"], p["lru_wx"]], axis=-1).astype(bf16),
        lru_ba=r(p["lru_ba"]), lru_bx=r(p["lru_bx"]), lru_lambda=r(p["lru_lambda"]),
        w_out=_cast_pad(p["w_out"]), mlp_up=_cast_pad(p["mlp_up"]),
        mlp_down=_cast_pad(p["mlp_down"]),
    )


def kernel(x_prompt, x_sample, c_prompt, c_sample, state_ssm, state_ssd_conv, state_mlstm_c, state_mlstm_n, state_mlstm_m, state_lru_h, state_lru_conv, ada_w, ada_b, norm1_w, norm2_w, w_in, ssd_conv_w, ssd_conv_b, ssd_dt_bias, ssd_a_log, ssd_d, ssd_norm_w, ml_i_bias, ml_f_bias, ml_norm_w, lru_conv_w, lru_conv_b, lru_wa, lru_ba, lru_wx, lru_bx, lru_lambda, w_out, mlp_up, mlp_down, final_norm_w):
    p = dict(norm1_w=norm1_w, norm2_w=norm2_w, w_in=w_in, ssd_conv_w=ssd_conv_w,
             ssd_conv_b=ssd_conv_b, ssd_dt_bias=ssd_dt_bias, ssd_a_log=ssd_a_log, ssd_d=ssd_d,
             ssd_norm_w=ssd_norm_w, ml_i_bias=ml_i_bias, ml_f_bias=ml_f_bias,
             ml_norm_w=ml_norm_w, lru_conv_w=lru_conv_w, lru_conv_b=lru_conv_b, lru_wa=lru_wa,
             lru_ba=lru_ba, lru_wx=lru_wx, lru_bx=lru_bx, lru_lambda=lru_lambda, w_out=w_out,
             mlp_up=mlp_up, mlp_down=mlp_down)
    w = _prep_params(p)
    fnw = final_norm_w.reshape(1, D_MODEL)
    bp = x_prompt.shape[0]
    bs_ = x_sample.shape[0]

    mod = _ada(jnp.concatenate([c_sample, c_prompt], axis=0), ada_w, ada_b)

    xp = _interleave_rows(x_prompt)
    p_out = [jnp.zeros((N_LAYERS, bp) + s, f32) for s in (
        (SSD_HEADS // 2, 2 * SSD_HEAD_DIM, SSD_STATE),
        (CONV_W - 1, SSD_CONV_DIM), (ML_HEADS, ML_HEAD_DIM, ML_HEAD_DIM),
        (ML_HEADS, ML_HEAD_DIM), (ML_HEADS, V7X_LANES),
        (1, LRU_WIDTH), (CONV_W - 1, LRU_WIDTH))]
    big_states = ((N_LAYERS, bs_, SSD_HEADS * SSD_HEAD_DIM, SSD_STATE),
                  (N_LAYERS, bs_, ML_HEADS * ML_HEAD_DIM, ML_HEAD_DIM))
    for l in range(N_LAYERS):
        res = _player(l, xp, mod, bs_, w, fnw, l == N_LAYERS - 1, p_out,
                      big_states if l == 0 else ())
        xp, p_out = res[0], list(res[1:1 + N_STATE_OUTS])
        if l == 0:
            ssm_buf, c_buf = res[1 + N_STATE_OUTS:]
    xp = _deinterleave_rows(xp)
    ssm, sconv, mc, mn, mm, lh, lconv = p_out
    p_states = (ssm.reshape(N_LAYERS, bp, SSD_HEADS, SSD_HEAD_DIM, SSD_STATE), sconv, mc, mn,
                mm[..., 0], lh.reshape(N_LAYERS, bp, LRU_WIDTH), lconv)

    tok_major = lambda a: jnp.swapaxes(a, -3, -2)
    xs = tok_major(x_sample)
    mod_s = mod
    st_in = (
        (state_ssm.reshape(N_LAYERS, bs_, SSD_HEADS * SSD_HEAD_DIM, SSD_STATE), 1),
        (tok_major(state_ssd_conv), 2),
        (state_mlstm_c.reshape(N_LAYERS, bs_, ML_HEADS * ML_HEAD_DIM, ML_HEAD_DIM), 1),
        (state_mlstm_n.reshape(N_LAYERS, bs_, ML_WIDTH), 1),
        (jnp.repeat(state_mlstm_m, ML_HEAD_DIM, axis=-1), 1),
        (state_lru_h, 1),
        (tok_major(state_lru_conv), 2),
    )
    st_out = [jnp.zeros(a.shape, f32) for a, _ in st_in]
    st_out[0], st_out[2] = ssm_buf, c_buf
    for l in range(N_LAYERS):
        u = _inproj(l, xs, mod_s, w["norm1_w"], w["w_in_p"], w["w_exp"], SAMPLE_IN_ROWS)
        ycat, *st_out = _smix(l, u, st_in, st_out, w, SAMPLE_MIX_ROWS)
        xs = _outmlp(l, xs, ycat, mod_s, w["w_out"], w["norm2_w"], w["mlp_up"], w["mlp_down"],
                     fnw, SAMPLE_OUT_ROWS, l == N_LAYERS - 1)
    xs = tok_major(xs)
    ssm, sconv, mc, mn, mm, lh, lconv = st_out
    s_states = (ssm.reshape(N_LAYERS, bs_, SSD_HEADS, SSD_HEAD_DIM, SSD_STATE), tok_major(sconv),
                mc.reshape(N_LAYERS, bs_, ML_HEADS, ML_HEAD_DIM, ML_HEAD_DIM),
                mn.reshape(N_LAYERS, bs_, ML_HEADS, ML_HEAD_DIM),
                mm.reshape(N_LAYERS, bs_, ML_HEADS, ML_HEAD_DIM)[..., 0],
                lh, tok_major(lconv))

    return (xp, xs) + p_states + s_states
```

```python
import functools
import math

import jax
import jax.numpy as jnp
from jax import lax
from jax.experimental import pallas as pl
from jax.experimental.pallas import tpu as pltpu

f32 = jnp.float32
bf16 = jnp.bfloat16

D_MODEL = 1024
N_LAYERS = 2
MIX_WIDTH = 2 * D_MODEL
SSD_WIDTH = 1024
SSD_HEAD_DIM = 64
SSD_HEADS = 16
SSD_GROUPS = 2
SSD_STATE = 128
SSD_CONV_DIM = SSD_WIDTH + 2 * SSD_GROUPS * SSD_STATE
ML_WIDTH = 512
ML_HEADS = 4
ML_HEAD_DIM = 128
LRU_WIDTH = 512
LRU_BLOCKS = 4
LRU_BLOCK_DIM = 128
LRU_C = 8.0
CONV_W = 4
CHUNK = 128
D_FF = 4 * D_MODEL
EPS = 1e-6
DEC_SEQ = 8

_O_DT, _O_Q, _O_O_END, _O_I, _O_XR, _O_END = 2560, 2576, 4624, 4624, 4632, 5656

P_Z = 0
P_XBC = 1024
P_Q = 2560
P_K = 3072
P_V = 3584
P_O = 4096
P_XR = 4608
P_GR = 5120
P_MAIN = 5632
P_SMALL = 5632
NP_COLS = 5760
S_DT = 5632
S_I = 6656
S_F = 7168
NS_COLS = 7680

V7X_LANES = 128
V7X_SUBLANES = 8
VMEM_LIMIT = 56 * 1024 * 1024
PLAYER_VMEM_LIMIT = 60 * 1024 * 1024
ROW_TILES = CHUNK // V7X_SUBLANES
LOG2E = 1.4426950408889634

SMALL_I = SSD_HEADS
SMALL_F = SSD_HEADS + ML_HEADS
GROUP_LANES = SSD_WIDTH // SSD_GROUPS
SAMPLE_IN_ROWS = 32
SAMPLE_MIX_ROWS = 8
SAMPLE_OUT_ROWS = 64


def _softplus(x):
    return jnp.maximum(x, 0.0) + jnp.log1p(jnp.exp(-jnp.abs(x)))


def _log_sigmoid(x):
    return -_softplus(-x)


def _sigmoid(x):
    return 1.0 / (1.0 + jnp.exp2(x * -LOG2E))


def _silu(x):
    return x * _sigmoid(x)


def _gelu_tanh(x):
    c = 2.0 * math.sqrt(2.0 / math.pi)
    return x * _sigmoid(x * (c + (c * 0.044715) * (x * x)))


def _lru_decay(r, lam_ref):
    log_a = r * (-LRU_C * _softplus(-lam_ref[...]))
    a = jnp.exp(log_a)
    return a, jnp.sqrt(-jnp.tanh(log_a) * (a * a + 1.0))


def _rms(x, w):
    return x * lax.rsqrt(jnp.mean(x * x, axis=-1, keepdims=True) + EPS) * w


def _interleave_rows(x):
    b, t, d = x.shape
    return x.reshape(b, t // CHUNK, V7X_SUBLANES, ROW_TILES, d).swapaxes(2, 3).reshape(b, t, d)


def _deinterleave_rows(x):
    b, t, d = x.shape
    return x.reshape(b, t // CHUNK, ROW_TILES, V7X_SUBLANES, d).swapaxes(2, 3).reshape(b, t, d)


def _full(shape):
    n = len(shape)
    return pl.BlockSpec(shape, lambda *_: (0,) * n)


def _layer(a, l, resident=False):
    nd = a.ndim - 1
    kw = dict(pipeline_mode=pl.Buffered(1)) if resident else {}
    return pl.BlockSpec((None,) + a.shape[1:], lambda *_: (l,) + (0,) * nd, **kw)


def _ada_kernel(c_ref, w_ref, b_ref, o_ref):
    cs = _silu(c_ref[...]).astype(bf16)
    o_ref[0, 0] = jnp.dot(cs, w_ref[0].astype(bf16), preferred_element_type=f32) + b_ref[0]


def _ada(c_all, ada_w, ada_b):
    n = c_all.shape[0]
    tn = D_MODEL
    return pl.pallas_call(
        _ada_kernel,
        out_shape=jax.ShapeDtypeStruct((N_LAYERS, 6, n, D_MODEL), f32),
        grid=(N_LAYERS, 6),
        in_specs=[
            pl.BlockSpec((n, D_MODEL), lambda l, j: (0, 0)),
            pl.BlockSpec((1, D_MODEL, tn), lambda l, j: (l, 0, j)),
            pl.BlockSpec((1, 1, tn), lambda l, j: (l, 0, j)),
        ],
        out_specs=pl.BlockSpec((1, 1, n, tn), lambda l, j: (l, j, 0, 0)),
        compiler_params=pltpu.CompilerParams(
            dimension_semantics=("arbitrary", "arbitrary"), vmem_limit_bytes=VMEM_LIMIT),
        name="ada_mod",
    )(c_all, ada_w, ada_b.reshape(N_LAYERS, 1, 6 * D_MODEL))


def _inproj_kernel(x_ref, mod_ref, nw_ref, wm_ref, we_ref, o_ref):
    x = x_ref[...]
    t, bb, d = x.shape
    hn = _rms(x, nw_ref[...]) * (1.0 + mod_ref[1]) + mod_ref[0]
    hn = hn.reshape(t * bb, d).astype(bf16)
    o_ref[:, :, 0:P_MAIN] = jnp.dot(hn, wm_ref[...], preferred_element_type=f32).reshape(
        t, bb, P_MAIN)
    o_ref[:, :, P_MAIN:NS_COLS] = jnp.dot(hn, we_ref[...], preferred_element_type=f32).reshape(
        t, bb, NS_COLS - P_MAIN)


def _inproj(l, x, mod, nw, w_packed, w_exp, bb):
    t, b, d = x.shape
    return pl.pallas_call(
        _inproj_kernel,
        out_shape=jax.ShapeDtypeStruct((t, b, NS_COLS), f32),
        grid=(b // bb,),
        in_specs=[
            pl.BlockSpec((t, bb, d), lambda i: (0, i, 0)),
            pl.BlockSpec((None, 6, bb, d), lambda i: (l, 0, i, 0)),
            _layer(nw, l),
            pl.BlockSpec((None, d, P_MAIN), lambda i: (l, 0, 0), pipeline_mode=pl.Buffered(1)),
            _layer(w_exp, l, resident=True),
        ],
        out_specs=pl.BlockSpec((t, bb, NS_COLS), lambda i: (0, i, 0)),
        compiler_params=pltpu.CompilerParams(
            dimension_semantics=("arbitrary",), vmem_limit_bytes=VMEM_LIMIT),
        name="inproj",
    )(x, mod, nw, w_packed, w_exp)


PACK_ROWS = 128


def _pack_w_in_kernel(w_ref, wp_ref, we_ref):
    c = lambda a: a.astype(bf16)
    wp_ref[:, 0:P_Q] = c(w_ref[:, 0:_O_DT])
    wp_ref[:, P_Q:P_XR] = c(w_ref[:, _O_Q:_O_O_END])
    wp_ref[:, P_XR:P_MAIN] = c(w_ref[:, _O_XR:_O_END])
    dt = w_ref[:, _O_DT:_O_Q]
    gates = w_ref[:, _O_I:_O_XR]
    n_pad = V7X_LANES - SSD_HEADS - 2 * ML_HEADS
    wp_ref[:, P_SMALL:NP_COLS] = c(jnp.concatenate(
        [dt, gates, jnp.zeros((PACK_ROWS, n_pad), f32)], axis=1))
    rep = lambda a, j, n: jnp.broadcast_to(a[:, j:j + 1], (PACK_ROWS, n))
    we_ref[...] = c(jnp.concatenate(
        [rep(dt, h, SSD_HEAD_DIM) for h in range(SSD_HEADS)]
        + [rep(gates, h, ML_HEAD_DIM) for h in range(2 * ML_HEADS)], axis=1))


def _pack_w_in(w_in):
    n_l, d, n = w_in.shape
    blk = lambda m: pl.BlockSpec((None, PACK_ROWS, m), lambda l, i: (l, i, 0))
    return pl.pallas_call(
        _pack_w_in_kernel,
        out_shape=[jax.ShapeDtypeStruct((n_l, d, NP_COLS), bf16),
                   jax.ShapeDtypeStruct((n_l, d, NS_COLS - P_MAIN), bf16)],
        grid=(n_l, d // PACK_ROWS),
        in_specs=[blk(n)],
        out_specs=[blk(NP_COLS), blk(NS_COLS - P_MAIN)],
        compiler_params=pltpu.CompilerParams(
            dimension_semantics=("arbitrary", "arbitrary"), vmem_limit_bytes=VMEM_LIMIT),
        name="pack_w_in",
    )(w_in)


def _cast_pad_kernel(w_ref, o_ref):
    n = w_ref.shape[-1]
    o_ref[:, 0:n] = w_ref[...].astype(bf16)
    o_ref[:, n:] = jnp.zeros((w_ref.shape[0], o_ref.shape[-1] - n), bf16)


def _cast_pad(a):
    n_l, k, n = a.shape
    rows = 512
    return pl.pallas_call(
        _cast_pad_kernel,
        out_shape=jax.ShapeDtypeStruct((n_l, k, n + V7X_LANES), bf16),
        grid=(n_l, k // rows),
        in_specs=[pl.BlockSpec((None, rows, n), lambda l, i: (l, i, 0))],
        out_specs=pl.BlockSpec((None, rows, n + V7X_LANES), lambda l, i: (l, i, 0)),
        compiler_params=pltpu.CompilerParams(
            dimension_semantics=("arbitrary", "arbitrary"), vmem_limit_bytes=VMEM_LIMIT),
        name="cast_pad",
    )(a)


def _outmlp_kernel(x_ref, y_ref, mod_ref, wout_ref, n2w_ref, up_ref, down_ref, fnw_ref, o_ref,
                   *, final):
    x = x_ref[...]
    t, bb, d = x.shape
    m = t * bb
    ycat = y_ref[...].reshape(m, MIX_WIDTH).astype(bf16)
    mix = jnp.dot(ycat, wout_ref[:, 0:d], preferred_element_type=f32).reshape(t, bb, d)
    x1 = x + mod_ref[2] * mix
    hn = _rms(x1, n2w_ref[...]) * (1.0 + mod_ref[4]) + mod_ref[3]
    hn = hn.reshape(m, d).astype(bf16)
    ff = jnp.zeros((m, d), f32)
    fc = 1024
    for c in range(D_FF // fc):
        h = jnp.dot(hn, up_ref[:, c * fc:(c + 1) * fc], preferred_element_type=f32)
        h = jnp.square(jnp.maximum(h, 0.0)).astype(bf16)
        ff = ff + jnp.dot(h, down_ref[c * fc:(c + 1) * fc, 0:D_MODEL],
                          preferred_element_type=f32)
    x2 = x1 + mod_ref[5] * ff.reshape(t, bb, d)
    if final:
        x2 = _rms(x2, fnw_ref[...])
    o_ref[...] = x2


def _outmlp(l, x, ycat, mod, wout, n2w, up, down, fnw, bb, final):
    t, b, d = x.shape
    return pl.pallas_call(
        functools.partial(_outmlp_kernel, final=final),
        out_shape=jax.ShapeDtypeStruct((t, b, d), f32),
        grid=(b // bb,),
        in_specs=[
            pl.BlockSpec((t, bb, d), lambda i: (0, i, 0)),
            pl.BlockSpec((t, bb, MIX_WIDTH), lambda i: (0, i, 0)),
            pl.BlockSpec((None, 6, bb, d), lambda i: (l, 0, i, 0)),
            _layer(wout, l, resident=True),
            _layer(n2w, l),
            _layer(up, l, resident=True),
            _layer(down, l, resident=True),
            _full((1, d)),
        ],
        out_specs=pl.BlockSpec((t, bb, d), lambda i: (0, i, 0)),
        compiler_params=pltpu.CompilerParams(
            dimension_semantics=("arbitrary",), vmem_limit_bytes=VMEM_LIMIT),
        name="outproj_mlp",
    )(x, ycat, mod, wout, n2w, up, down, fnw)


PIECE = 512


def _stage_in(x, mod, nw_ref, w_ref, u_ref):
    hn = _rms(x, nw_ref[...] * (1.0 + mod(1))) + mod(0)
    hn = hn.astype(bf16)
    yield
    for c0 in range(0, NP_COLS, PIECE):
        c1 = min(c0 + PIECE, NP_COLS)
        u_ref[:, c0:c1] = jnp.dot(hn, w_ref[:, c0:c1], preferred_element_type=f32)
        yield


def _stage_out(x, y_ref, o_ref, rows, mod, wout_ref, n2w_ref, up_ref, down_ref, fnw_ref,
               final):
    halves = [slice(h * PIECE, (h + 1) * PIECE) for h in range(D_MODEL // PIECE)]
    ycat = y_ref[...]
    mix = []
    for hs in halves:
        mix.append(jnp.dot(ycat, wout_ref[:, hs], preferred_element_type=f32))
        yield
    x1 = x + mod(2) * jnp.concatenate(mix, axis=1)
    hn = (_rms(x1, n2w_ref[...] * (1.0 + mod(4))) + mod(3)).astype(bf16)
    yield
    ff = [jnp.zeros((x.shape[0], PIECE), f32) for _ in halves]
    fc = 2 * PIECE
    for c in range(D_FF // fc):
        h = []
        for k in range(fc // PIECE):
            cols = slice(c * fc + k * PIECE, c * fc + (k + 1) * PIECE)
            hk = jnp.dot(hn, up_ref[:, cols], preferred_element_type=f32)
            h.append(jnp.square(jnp.maximum(hk, 0.0)).astype(bf16))
            yield
        h = jnp.concatenate(h, axis=1)
        for i, hs in enumerate(halves):
            ff[i] = ff[i] + jnp.dot(h, down_ref[c * fc:(c + 1) * fc, hs],
                                    preferred_element_type=f32)
            yield
    x2 = x1 + mod(5) * jnp.concatenate(ff, axis=1)
    o_ref[0, rows, :] = _rms(x2, fnw_ref[...]) if final else x2
    yield


def _trace_interleaved(main, fillers, n_main, n_fill):
    fillers = list(fillers)
    due = 0.0
    nxt = 0
    main_live = True
    while main_live or fillers:
        if main_live:
            main_live = next(main, _DONE) is not _DONE
            due += n_fill / n_main
        else:
            due = float(len(fillers))
        while fillers and due >= 1.0:
            nxt %= len(fillers)
            if next(fillers[nxt], _DONE) is _DONE:
                fillers.pop(nxt)
            else:
                nxt += 1
                due -= 1.0


_DONE = object()


def _reset_state_if(first, state):
    for ref in state:
        ref[...] = jnp.where(first, 0.0, ref[...])


def _stage_mix(u_ref, y_ref, prm, state, ci, outs):
    (cw_ref, cb_ref, sbias_ref, alog_ref, dskip_ref, snw_ref, mnw_ref,
     lcw_ref, lcb_ref, wax_ref, ba_ref, bx_ref, lam_ref) = prm
    xbuf, lbuf, st, cst, nst, mst, hst = state
    L = CHUNK
    SUB = V7X_SUBLANES
    HIST = (CONV_W - 1) * SUB

    row = lax.broadcasted_iota(jnp.int32, (L, L), 0)
    col = lax.broadcasted_iota(jnp.int32, (L, L), 1)
    tok = lambda i: (i & (SUB - 1)) * ROW_TILES + (i >> 3)
    tri = tok(row) >= tok(col)
    lo = col < SSD_HEAD_DIM

    def bcl(a, j):
        return jnp.broadcast_to(a[:, j:j + 1], (L, L))

    def conv(u0, hist_ref, w_ref, b_ref, c0, c1):
        raw = u_ref[:, u0 + c0:u0 + c1]
        sub = lax.broadcasted_iota(jnp.int32, (SUB, c1 - c0), 0)
        prev = [pltpu.roll(jnp.where(sub == SUB - 1, hist_ref[k * SUB:(k + 1) * SUB, c0:c1],
                                     raw[L - HIST + k * SUB:L - HIST + (k + 1) * SUB, :]), 1, 0)
                for k in range(CONV_W - 1)]
        acc = b_ref[:, c0:c1] + raw * w_ref[CONV_W - 1:CONV_W, c0:c1]
        for d in range(1, CONV_W):
            shifted = jnp.concatenate(prev[CONV_W - 1 - d:] + [raw[0:L - d * SUB, :]], axis=0)
            acc = acc + shifted * w_ref[CONV_W - 1 - d:CONV_W - d, c0:c1]
        hist_ref[:, c0:c1] = raw[L - HIST:L, :]
        return acc

    cblk = 512
    xbc = []
    for c0 in range(0, SSD_CONV_DIM, cblk):
        xbc.append(_silu(conv(P_XBC, xbuf, cw_ref, cb_ref, c0, c0 + cblk)))
        yield
    xbc = jnp.concatenate(xbc, axis=1)
    xr = conv(P_XR, lbuf, lcw_ref, lcb_ref, 0, LRU_WIDTH)
    yield

    sm = u_ref[:,P_SMALL:P_SMALL + V7X_LANES] + sbias_ref[...]
    dt = _softplus(sm)
    a_row = -jnp.exp(alog_ref[...])
    gates = jnp.where(col < SSD_HEADS, dt * (a_row * LOG2E),
                      jnp.where((col >= SMALL_F) & (col < SMALL_F + ML_HEADS),
                                _log_sigmoid(sm), 0.0))
    g_hi = gates.astype(bf16)
    g_r = gates - g_hi.astype(f32)
    g_mid = g_r.astype(bf16)
    g_lo = (g_r - g_mid.astype(f32)).astype(bf16)
    cum3 = jnp.dot(jnp.where(tri, 1.0, 0.0).astype(bf16),
                   jnp.concatenate([g_hi, g_mid, g_lo], axis=1), preferred_element_type=f32)
    cum = cum3[:, 0:L] + cum3[:, L:2 * L] + cum3[:, 2 * L:3 * L]
    cum_t = cum.T
    sm_t = sm.T
    yield

    ys = []
    for g in range(SSD_GROUPS):
        bm_t = xbc[:, SSD_WIDTH + g * SSD_STATE:SSD_WIDTH + (g + 1) * SSD_STATE].T.astype(bf16)
        cm = xbc[:, SSD_WIDTH + (SSD_GROUPS + g) * SSD_STATE:
                 SSD_WIDTH + (SSD_GROUPS + g + 1) * SSD_STATE].astype(bf16)
        cb = jnp.dot(cm, bm_t, preferred_element_type=f32)
        inter = jnp.dot(cm, st[g].astype(bf16), preferred_element_type=f32)
        yield
        xends, keeps = [], []
        for jj in range(4):
            jp = g * 4 + jj
            e0, e1 = 2 * jp, 2 * jp + 1
            c0 = bcl(cum, e0)
            c1 = bcl(cum, e1)
            cum_p = jnp.where(lo, c0, c1)
            dt_p = jnp.where(lo, bcl(dt, e0), bcl(dt, e1))
            xs_p = xbc[:, jp * L:(jp + 1) * L]
            xdt = xs_p * dt_p
            dec0 = jnp.exp2(jnp.where(tri, c0 - cum_t[e0:e0 + 1, :], -jnp.inf))
            dec1 = jnp.exp2(jnp.where(tri, c1 - cum_t[e1:e1 + 1, :], -jnp.inf))
            yield
            att =jnp.concatenate([(cb * dec0).astype(bf16), (cb * dec1).astype(bf16)], axis=1)
            x2 = jnp.concatenate([jnp.where(lo, xdt, 0.0).astype(bf16),
                                  jnp.where(lo, 0.0, xdt).astype(bf16)], axis=0)
            y_p = (jnp.dot(att, x2, preferred_element_type=f32)
                   + inter[:, jj * L:(jj + 1) * L] * jnp.exp2(cum_p)
                   + xs_p * dskip_ref[:, jp * L:(jp + 1) * L])
            ys.append(y_p)
            c_last = cum_p[L - 1:L, :]
            xends.append((xdt * jnp.exp2(c_last - cum_p)).astype(bf16))
            keeps.append(jnp.exp2(c_last))
            yield
        st[g] = (jnp.concatenate(keeps, axis=1) * st[g]
                 + jnp.dot(bm_t, jnp.concatenate(xends, axis=1), preferred_element_type=f32))
    y = jnp.concatenate(ys, axis=1)
    z = u_ref[:,P_Z:P_Z + SSD_WIDTH]
    y_ref[:,0:SSD_WIDTH] = _rms(y * _silu(z), snw_ref[...]).astype(bf16)
    yield

    scale = ML_HEAD_DIM ** -0.5
    for h in range(ML_HEADS):
        sl = slice(h * ML_HEAD_DIM, (h + 1) * ML_HEAD_DIM)
        q = u_ref[:,P_Q + h * ML_HEAD_DIM:P_Q + (h + 1) * ML_HEAD_DIM]
        k = u_ref[:,P_K + h * ML_HEAD_DIM:P_K + (h + 1) * ML_HEAD_DIM] * scale
        v = u_ref[:,P_V + h * ML_HEAD_DIM:P_V + (h + 1) * ML_HEAD_DIM]
        o = u_ref[:,P_O + h * ML_HEAD_DIM:P_O + (h + 1) * ML_HEAD_DIM]
        k_t = k.T.astype(bf16)
        jf = SMALL_F + h
        ji = SMALL_I + h
        b_col = cum[:, jf:jf + 1]
        i_col = sm[:, ji:ji + 1]
        dm = jnp.where(tri, bcl(cum, jf) - cum_t[jf:jf + 1, :] + sm_t[ji:ji + 1, :], -jnp.inf)
        m_prev = mst[h, :, 0:1]
        g_col = b_col + m_prev
        mt = jnp.maximum(g_col, jnp.max(dm, axis=-1, keepdims=True))
        yield
        w = jnp.exp(dm - mt) * jnp.dot(q.astype(bf16), k_t, preferred_element_type=f32)
        inter = jnp.exp(g_col - mt)
        yield
        ct_old = cst[h]
        n_old = nst[h]
        num = jnp.dot(jnp.concatenate([w.astype(bf16), (inter * q).astype(bf16)], axis=1),
                      jnp.concatenate([v.astype(bf16), ct_old.astype(bf16)], axis=0),
                      preferred_element_type=f32)
        den = (jnp.sum(w, axis=-1, keepdims=True)
               + inter * jnp.sum(q * n_old, axis=-1, keepdims=True))
        hout = num / jnp.maximum(jnp.abs(den), jnp.exp(-mt))
        yield
        m_new = mt[L - 1:L, :]
        b_last = b_col[L - 1:L, :]
        w_end = jnp.exp(b_last - b_col + i_col - m_new)
        dc = jnp.exp(b_last + m_prev - m_new)
        cst[h] = dc * ct_old + jnp.dot(k_t, (v * w_end).astype(bf16),
                                       preferred_element_type=f32)
        nst[h] = dc * n_old + jnp.sum(w_end * k, axis=0, keepdims=True)
        mst[h] = jnp.broadcast_to(m_new, (1, V7X_LANES))
        yh = _rms(hout, mnw_ref[:, sl]) * _sigmoid(o)
        y_ref[:,SSD_WIDTH + h * ML_HEAD_DIM:SSD_WIDTH + (h + 1) * ML_HEAD_DIM] = yh.astype(bf16)
        yield

    rs, is_ = [], []
    for kb_ in range(LRU_BLOCKS):
        xk = xr[:, kb_ * LRU_BLOCK_DIM:(kb_ + 1) * LRU_BLOCK_DIM].astype(bf16)
        ri = jnp.dot(xk, wax_ref[kb_], preferred_element_type=f32)
        rs.append(ri[:, 0:LRU_BLOCK_DIM])
        is_.append(ri[:, LRU_BLOCK_DIM:2 * LRU_BLOCK_DIM])
    r = _sigmoid(jnp.concatenate(rs, axis=1) + ba_ref[...])
    i_g = _sigmoid(jnp.concatenate(is_, axis=1) + bx_ref[...])
    yield
    a_s, mult = _lru_decay(r, lam_ref)
    rowl = lax.broadcasted_iota(jnp.int32, (L, LRU_WIDTH), 0)
    first_row = jnp.where(ci == 0, 0, -1)
    mult = jnp.where(rowl == first_row, 1.0, mult)
    u_s = mult * i_g * xr
    yield
    tile = lambda a, v: a[v * SUB:(v + 1) * SUB, :]
    h_loc = [tile(u_s, 0)]
    a_cum = [tile(a_s, 0)]
    for v in range(1, ROW_TILES):
        h_loc.append(tile(a_s, v) * h_loc[-1] + tile(u_s, v))
        a_cum.append(tile(a_s, v) * a_cum[-1])
    a_e, h_e = a_cum[-1], h_loc[-1]
    subl = lax.broadcasted_iota(jnp.int32, (SUB, LRU_WIDTH), 0)
    k_ = 1
    while k_ < SUB:
        keep = subl >= k_
        a_sh = jnp.where(keep, pltpu.roll(a_e, k_, 0), 1.0)
        h_sh = jnp.where(keep, pltpu.roll(h_e, k_, 0), 0.0)
        h_e = a_e * h_sh + h_e
        a_e = a_e * a_sh
        k_ *= 2
    h0 = hst[...]
    run_end = h_e + a_e * h0
    run_in = jnp.where(subl == 0, h0, pltpu.roll(run_end, 1, 0))
    hr = jnp.concatenate([h_loc[v] + a_cum[v] * run_in for v in range(ROW_TILES)], axis=0)
    hst[...] = run_end[SUB - 1:SUB, :]
    gr = u_ref[:,P_GR:P_GR + LRU_WIDTH]
    y_ref[:,SSD_WIDTH + ML_WIDTH:MIX_WIDTH] = (hr * _gelu_tanh(gr)).astype(bf16)
    yield

    if outs is not None:
        ssm_ref, sconv_ref, mc_ref, mn_ref, mm_ref, lh_ref, lconv_ref = outs
        for g in range(SSD_GROUPS):
            for jj in range(4):
                ssm_ref[0, g * 4 + jj] = st[g, :, jj * L:(jj + 1) * L].T
        for k in range(CONV_W - 1):
            sconv_ref[0, k:k + 1, :] = xbuf[k * SUB + SUB - 1:(k + 1) * SUB, :]
            lconv_ref[0, k:k + 1, :] = lbuf[k * SUB + SUB - 1:(k + 1) * SUB, :]
        for h in range(ML_HEADS):
            mc_ref[0, h] = cst[h].T
            mn_ref[0, h:h + 1, :] = nst[h]
            mm_ref[0, h:h + 1, :] = mst[h]
        lh_ref[0] = hst[...]


N_MIX_PARAMS = 13
N_STATE_OUTS = 7
N_MIX_PHASES = 44
N_FILL_PIECES = (1 + -(-NP_COLS // PIECE)) + (D_MODEL // PIECE + 1
                                              + (D_FF // (2 * PIECE)) * (2 + D_MODEL // PIECE) + 1)


def _player_kernel(*refs, final, n_chunks, nc, n_clear):
    xa_ref, mod_ref, xc_ref, n1w_ref, win_ref = refs[:5]
    prm = refs[5:5 + N_MIX_PARAMS]
    wout_ref, n2w_ref, up_ref, down_ref, fnw_ref = refs[5 + N_MIX_PARAMS:10 + N_MIX_PARAMS]
    n_in = 10 + N_MIX_PARAMS + N_STATE_OUTS
    o_ref = refs[n_in]
    outs = refs[n_in + 1:n_in + 1 + N_STATE_OUTS]
    clear = refs[n_in + 1 + N_STATE_OUTS:n_in + 1 + N_STATE_OUTS + n_clear]
    u_bufs = refs[-11:-9]
    y_bufs = refs[-9:-7]
    state = refs[-7:]
    k = pl.program_id(0)
    n_steps = n_chunks // 2

    @pl.when(k == 0)
    def _():
        u_bufs[1][...] = jnp.zeros(u_bufs[1].shape, f32)
        y_bufs[1][...] = jnp.zeros(y_bufs[1].shape, bf16)
        for ref in state:
            ref[...] = jnp.zeros(ref.shape, f32)
        if n_clear:
            refs[-13][...] = jnp.zeros(refs[-13].shape, f32)

    def clear_copies():
        zeros, sems = refs[-13], refs[-12]
        copies = []
        for ci_, buf in enumerate(clear):
            n_b, rows = buf.shape[1], buf.shape[2]
            per_step = buf.shape[0] * n_b // n_steps
            for i in range(per_step):
                j = k * per_step + i
                copies.append(pltpu.make_async_copy(
                    zeros.at[0:rows], buf.at[j // n_b, j % n_b], sems.at[ci_, i]))
        return copies

    if n_clear:
        @pl.when(k < n_steps)
        def _():
            for cp in clear_copies():
                cp.start(priority=1)

    pair_seq = lambda pair: jnp.clip(pair, 0, n_steps - 1) // (nc // 2)
    mod_of = lambda seq: (lambda c: mod_ref[c, pl.ds(seq, 1), :])
    mod_a, mod_c = mod_of(pair_seq(k)), mod_of(pair_seq(k - 1))

    for sub in range(2):
        rows = slice(sub * CHUNK, (sub + 1) * CHUNK)
        stage_in = _stage_in(xa_ref[0, rows, :], mod_a, n1w_ref, win_ref, u_bufs[sub])
        jb = jnp.clip(2 * k + sub - 1, 0, n_chunks - 1)
        ci = jb % nc
        if sub == 1:
            _reset_state_if(ci == 0, state)
        stage_mix = _stage_mix(u_bufs[1 - sub], y_bufs[sub], prm, state, ci,
                               outs if sub == 0 else None)
        stage_out = _stage_out(xc_ref[0, rows, :], y_bufs[1 - sub], o_ref, rows, mod_c,
                               wout_ref, n2w_ref, up_ref, down_ref, fnw_ref, final)
        _trace_interleaved(stage_mix, [stage_in, stage_out], N_MIX_PHASES, N_FILL_PIECES)

    if n_clear:
        @pl.when(k < n_steps)
        def _():
            for cp in clear_copies():
                cp.wait()


CLEAR_ROWS = 1024


def _player(l, x, mod, mod_row0, w, fnw, final, prev, clear_shapes=()):
    b, t, d = x.shape
    nc = t // CHUNK
    n_chunks = b * nc
    n_pairs = n_chunks // 2
    ppr = nc // 2
    pair_a = lambda k: jnp.minimum(k, n_pairs - 1)
    pair_c = lambda k: jnp.clip(k - 1, 0, n_pairs - 1)
    seq_b = lambda k: jnp.clip(2 * k - 1, 0, n_chunks - 1) // nc
    xspec = lambda pair: pl.BlockSpec((1, 2 * CHUNK, d), lambda k: (pair(k) // ppr, pair(k) % ppr, 0))
    assert b <= V7X_SUBLANES and mod_row0 % V7X_SUBLANES == 0
    mspec = pl.BlockSpec((None, 6, V7X_SUBLANES, d), lambda k: (l, 0, mod_row0 // V7X_SUBLANES, 0))
    st_spec = lambda s: pl.BlockSpec((None, 1) + s[2:],
                                     lambda k: (l, seq_b(k)) + (0,) * (len(s) - 2))
    outs = [jax.ShapeDtypeStruct((b, t, d), f32)]
    outs += [jax.ShapeDtypeStruct(a.shape, f32) for a in prev]
    mix_params = [w["ssd_conv_w"], w["ssd_conv_b"], w["p_small_bias"], w["p_alog"], w["dskip"],
                  w["ssd_norm_w"], w["ml_norm_w"], w["lru_conv_w"], w["lru_conv_b"], w["lru_wax"],
                  w["lru_ba"], w["lru_bx"], w["lru_lambda"]]
    assert len(mix_params) == N_MIX_PARAMS and len(prev) == N_STATE_OUTS
    n_in = 10 + N_MIX_PARAMS
    clear_scratch = []
    if clear_shapes:
        for s in clear_shapes:
            assert s[3] == V7X_LANES and s[2] <= CLEAR_ROWS and (s[0] * s[1]) % n_pairs == 0
        outs += [jax.ShapeDtypeStruct(s, f32) for s in clear_shapes]
        per_step = max(s[0] * s[1] // n_pairs for s in clear_shapes)
        clear_scratch = [pltpu.VMEM((CLEAR_ROWS, V7X_LANES), f32),
                         pltpu.SemaphoreType.DMA((len(clear_shapes), per_step))]
    return pl.pallas_call(
        functools.partial(_player_kernel, final=final, n_chunks=n_chunks, nc=nc,
                          n_clear=len(clear_shapes)),
        out_shape=outs,
        grid=(n_pairs + 1,),
        in_specs=[xspec(pair_a), mspec, xspec(pair_c),
                  _layer(w["norm1_w"], l), _layer(w["w_in_p"], l, resident=True)]
        + [_layer(a, l) for a in mix_params]
        + [_layer(w["w_out"], l, resident=True), _layer(w["norm2_w"], l),
           _layer(w["mlp_up"], l, resident=True), _layer(w["mlp_down"], l, resident=True),
           _full((1, d))]
        + [pl.BlockSpec(memory_space=pl.ANY)] * len(prev),
        out_specs=[xspec(pair_c)] + [st_spec(a.shape) for a in prev]
        + [pl.BlockSpec(memory_space=pl.ANY)] * len(clear_shapes),
        input_output_aliases={n_in + i: 1 + i for i in range(len(prev))},
        scratch_shapes=clear_scratch + [
            pltpu.VMEM((CHUNK, NP_COLS), f32), pltpu.VMEM((CHUNK, NP_COLS), f32),
            pltpu.VMEM((CHUNK, MIX_WIDTH), bf16), pltpu.VMEM((CHUNK, MIX_WIDTH), bf16),
            pltpu.VMEM(((CONV_W - 1) * V7X_SUBLANES, SSD_CONV_DIM), f32),
            pltpu.VMEM(((CONV_W - 1) * V7X_SUBLANES, LRU_WIDTH), f32),
            pltpu.VMEM((SSD_GROUPS, SSD_STATE, GROUP_LANES), f32),
            pltpu.VMEM((ML_HEADS, ML_HEAD_DIM, ML_HEAD_DIM), f32),
            pltpu.VMEM((ML_HEADS, 1, ML_HEAD_DIM), f32),
            pltpu.VMEM((ML_HEADS, 1, V7X_LANES), f32),
            pltpu.VMEM((1, LRU_WIDTH), f32),
        ],
        compiler_params=pltpu.CompilerParams(
            dimension_semantics=("arbitrary",), vmem_limit_bytes=PLAYER_VMEM_LIMIT),
        name="prompt_layer",
    )(x, mod, x, w["norm1_w"], w["w_in_p"], *mix_params,
      w["w_out"], w["norm2_w"], w["mlp_up"], w["mlp_down"], fnw, *prev)


def _smix_kernel(u_ref, s0_ref, sconv0_ref, c0_ref, n0_ref, m0_ref, lh0_ref, lconv0_ref,
                 cw_ref, cb_ref, dtb_ref, alog_ref, dskip_ref, snw_ref, ib_ref, fb_ref, mnw_ref,
                 lcw_ref, lcb_ref, wax_ref, ba_ref, bx_ref, lam_ref,
                 y_ref, s_ref, sconv_ref, c_ref, n_ref, m_ref, lh_ref, lconv_ref):
    T = DEC_SEQ
    bb = u_ref.shape[1]
    H = ML_HEAD_DIM
    toks = range(T)

    def useg(t, c0, n):
        return u_ref[t, :, c0:c0 + n]

    def to_seq(rows):
        return jnp.swapaxes(jnp.stack(rows, axis=0), 0, 1)

    def to_tok(a):
        a = jnp.swapaxes(a, 0, 1)
        return [a[t] for t in toks]

    def conv(c0, n, hist_ref, w_ref, b_ref, new_ref):
        xpad = [hist_ref[k] for k in range(CONV_W - 1)] + [useg(t, c0, n) for t in toks]
        for k in range(CONV_W - 1):
            new_ref[k] = xpad[T + k]
        out = []
        for t in toks:
            acc = b_ref[...] + xpad[t] * w_ref[0:1, :]
            for j in range(1, CONV_W):
                acc = acc + xpad[t + j] * w_ref[j:j + 1, :]
            out.append(acc)
        return out

    def split3(d):
        hi = d.astype(bf16).astype(f32)
        r1 = d - hi
        mid = r1.astype(bf16).astype(f32)
        return [hi, mid, (r1 - mid).astype(bf16).astype(f32)]

    def update_lhs(rows, decay):
        pad = [jnp.zeros_like(decay)] * (T - 3)
        return to_seq(rows + split3(decay) + pad).astype(bf16)

    def ones_rhs(kmat):
        zeros = jnp.zeros_like(kmat)
        return jnp.concatenate(
            [jnp.concatenate([kmat, zeros], axis=2),
             jnp.concatenate([zeros, jnp.ones_like(kmat)], axis=2)], axis=1).astype(bf16)

    xbc = [_silu(a) for a in conv(P_XBC, SSD_CONV_DIM, sconv0_ref, cw_ref, cb_ref, sconv_ref)]
    xs = [a[:, 0:SSD_WIDTH] for a in xbc]
    seg = lambda a, j: a[:, SSD_WIDTH + j * SSD_STATE:SSD_WIDTH + (j + 1) * SSD_STATE]
    bm = [[seg(a, g) for a in xbc] for g in range(SSD_GROUPS)]
    cm = [[seg(a, SSD_GROUPS + g) for a in xbc] for g in range(SSD_GROUPS)]
    a_row = -jnp.exp(alog_ref[...]) * LOG2E
    lane = lax.broadcasted_iota(jnp.int32, (bb, SSD_WIDTH), 1)
    cum, xdt = [], []
    for t in toks:
        dt = _softplus(useg(t, S_DT, SSD_WIDTH) + dtb_ref[...])
        cum.append(dt * a_row + (cum[-1] if cum else 0.0))
        xdt.append(xs[t] * dt)
    y = [xs[t] * dskip_ref[...] for t in toks]
    for t in toks:
        for s in range(t + 1):
            cb0 = jnp.sum(cm[0][t] * bm[0][s], axis=-1, keepdims=True)
            cb1 = jnp.sum(cm[1][t] * bm[1][s], axis=-1, keepdims=True)
            term = jnp.where(lane < GROUP_LANES, cb0, cb1) * xdt[s]
            y[t] = y[t] + (term if s == t else term * jnp.exp2(cum[t] - cum[s]))
    inter = []
    for g in range(SSD_GROUPS):
        inter.append(to_tok(jnp.einsum(
            'btn,bpn->btp', to_seq(cm[g]).astype(bf16),
            s0_ref[:, g * GROUP_LANES:(g + 1) * GROUP_LANES, :].astype(bf16), preferred_element_type=f32)))
    for t in toks:
        yt = y[t] + jnp.concatenate([inter[0][t], inter[1][t]], axis=-1) * jnp.exp2(cum[t])
        y_ref[t, :, 0:SSD_WIDTH] = _rms(yt * _silu(useg(t, P_Z, SSD_WIDTH)), snw_ref[...])
    xend = [xdt[s] if s == T - 1 else xdt[s] * jnp.exp2(cum[T - 1] - cum[s]) for s in toks]
    lhs = update_lhs(xend, jnp.exp2(cum[T - 1]))
    for g in range(SSD_GROUPS):
        zz = jnp.einsum('bkp,bkn->bpn', lhs[:, :, g * GROUP_LANES:(g + 1) * GROUP_LANES],
                        ones_rhs(to_seq(bm[g])), preferred_element_type=f32)
        s_ref[:, g * GROUP_LANES:(g + 1) * GROUP_LANES, :] = (
            zz[:, :, SSD_STATE:] * s0_ref[:, g * GROUP_LANES:(g + 1) * GROUP_LANES, :] + zz[:, :, :SSD_STATE])

    def headsum(a):
        return jnp.concatenate(
            [jnp.broadcast_to(jnp.sum(a[:, h * H:(h + 1) * H], axis=-1, keepdims=True), (bb, H))
             for h in range(ML_HEADS)], axis=-1)

    q = [useg(t, P_Q, ML_WIDTH) for t in toks]
    k = [useg(t, P_K, ML_WIDTH) * (H ** -0.5) for t in toks]
    v = [useg(t, P_V, ML_WIDTH) for t in toks]
    ic = [useg(t, S_I, ML_WIDTH) + ib_ref[...] for t in toks]
    bc = []
    for t in toks:
        fc = _log_sigmoid(useg(t, S_F, ML_WIDTH) + fb_ref[...])
        bc.append(fc + (bc[-1] if bc else 0.0))
    m0 = m0_ref[...]
    n0 = n0_ref[...]
    q_seq = to_seq(q).astype(bf16)
    qc = to_tok(jnp.concatenate(
        [jnp.einsum('btk,bvk->btv', q_seq[:, :, h * H:(h + 1) * H],
                    c0_ref[:, h * H:(h + 1) * H, :].astype(bf16), preferred_element_type=f32)
         for h in range(ML_HEADS)], axis=-1))
    mts = []
    for t in toks:
        dm = [bc[t] - bc[s] + ic[s] for s in range(t + 1)]
        g_t = bc[t] + m0
        mt = g_t
        for d in dm:
            mt = jnp.maximum(mt, d)
        mts.append(mt)
        inter_m = jnp.exp(g_t - mt)
        num = inter_m * qc[t]
        den = inter_m * headsum(q[t] * n0)
        for s in range(t + 1):
            w = jnp.exp(dm[s] - mt) * headsum(q[t] * k[s])
            num = num + w * v[s]
            den = den + w
        hout = num / jnp.maximum(jnp.abs(den), jnp.exp(-mt))
        y_ml = (hout * lax.rsqrt(headsum(hout * hout) * (1.0 / H) + EPS) * mnw_ref[...]
                * _sigmoid(useg(t, P_O, ML_WIDTH)))
        y_ref[t, :, SSD_WIDTH:SSD_WIDTH + ML_WIDTH] = y_ml
    m_new = mts[T - 1]
    m_ref[...] = m_new
    w_end = [jnp.exp(bc[T - 1] - bc[s] + ic[s] - m_new) for s in toks]
    dc = jnp.exp(bc[T - 1] + m0 - m_new)
    n_new = dc * n0
    for s in toks:
        n_new = n_new + w_end[s] * k[s]
    n_ref[...] = n_new
    lhs_m = update_lhs([v[s] * w_end[s] for s in toks], dc)
    k_seq = to_seq(k)
    for h in range(ML_HEADS):
        sl = slice(h * H, (h + 1) * H)
        zz = jnp.einsum('bkv,bkn->bvn', lhs_m[:, :, sl], ones_rhs(k_seq[:, :, sl]),
                        preferred_element_type=f32)
        c_ref[:, sl, :] = zz[:, :, H:] * c0_ref[:, sl, :] + zz[:, :, :H]

    xr = jnp.concatenate(conv(P_XR, LRU_WIDTH, lconv0_ref, lcw_ref, lcb_ref, lconv_ref), axis=0)
    rs, is_ = [], []
    for kk in range(LRU_BLOCKS):
        xk = xr[:, kk * LRU_BLOCK_DIM:(kk + 1) * LRU_BLOCK_DIM].astype(bf16)
        ri = jnp.dot(xk, wax_ref[kk], preferred_element_type=f32)
        rs.append(ri[:, 0:LRU_BLOCK_DIM])
        is_.append(ri[:, LRU_BLOCK_DIM:2 * LRU_BLOCK_DIM])
    r = _sigmoid(jnp.concatenate(rs, axis=1) + ba_ref[...])
    i_g = _sigmoid(jnp.concatenate(is_, axis=1) + bx_ref[...])
    a_s, mult = _lru_decay(r, lam_ref)
    u_s = mult * i_g * xr
    cur = lh0_ref[...]
    for t in toks:
        rows = slice(t * bb, (t + 1) * bb)
        cur = a_s[rows] * cur + u_s[rows]
        y_ref[t, :, SSD_WIDTH + ML_WIDTH:MIX_WIDTH] = cur * _gelu_tanh(useg(t, P_GR, LRU_WIDTH))
    lh_ref[...] = cur


def _smix(l, u, states, prev, w, bb):
    t, b, _ = u.shape

    def sblk(a, ax):
        shape = tuple(None if d == 0 else bb if d == ax else n for d, n in enumerate(a.shape))
        return pl.BlockSpec(shape, lambda i: tuple(l if d == 0 else i if d == ax else 0
                                                   for d in range(a.ndim)))

    ublk = lambda n: pl.BlockSpec((t, bb, n), lambda i: (0, i, 0))
    outs = [jax.ShapeDtypeStruct((t, b, MIX_WIDTH), f32)]
    outs += [jax.ShapeDtypeStruct(a.shape, f32) for a, _ in states]
    weights = [w["ssd_conv_w"], w["ssd_conv_b"], w["s_dt_bias"], w["s_alog"], w["dskip"],
               w["ssd_norm_w"], w["s_i_bias"], w["s_f_bias"], w["ml_norm_w"],
               w["lru_conv_w"], w["lru_conv_b"], w["lru_wax"], w["lru_ba"], w["lru_bx"],
               w["lru_lambda"]]
    n_in = 1 + len(states) + len(weights)
    prev = list(prev)

    def body(*refs):
        _smix_kernel(*refs[:n_in], *refs[n_in + len(prev):])

    return pl.pallas_call(
        body,
        out_shape=outs,
        grid=(b // bb,),
        in_specs=[ublk(u.shape[-1])] + [sblk(a, ax) for a, ax in states]
        + [_layer(a, l) for a in weights]
        + [pl.BlockSpec(memory_space=pl.ANY)] * len(prev),
        out_specs=[ublk(MIX_WIDTH)] + [sblk(a, ax) for a, ax in states],
        input_output_aliases={n_in + k: 1 + k for k in range(len(prev))},
        compiler_params=pltpu.CompilerParams(
            dimension_semantics=("arbitrary",), vmem_limit_bytes=VMEM_LIMIT),
        name="sample_mix",
    )(u, *[a for a, _ in states], *weights, *prev)


def _prep_params(p):
    w_p, w_exp = _pack_w_in(p["w_in"])
    r = lambda a: a.reshape(N_LAYERS, 1, -1)
    rep = lambda a, n: r(jnp.repeat(a, n, axis=-1))
    small = jnp.concatenate([p["ssd_dt_bias"], p["ml_i_bias"], p["ml_f_bias"]], axis=-1)
    return dict(
        w_in_p=w_p, w_exp=w_exp,
        norm1_w=r(p["norm1_w"]), norm2_w=r(p["norm2_w"]),
        ssd_conv_w=p["ssd_conv_w"], ssd_conv_b=r(p["ssd_conv_b"]),
        p_small_bias=r(jnp.pad(small, ((0, 0), (0, V7X_LANES - small.shape[-1])))),
        p_alog=r(jnp.pad(p["ssd_a_log"], ((0, 0), (0, V7X_LANES - SSD_HEADS)))),
        s_dt_bias=rep(p["ssd_dt_bias"], SSD_HEAD_DIM), s_alog=rep(p["ssd_a_log"], SSD_HEAD_DIM),
        s_i_bias=rep(p["ml_i_bias"], ML_HEAD_DIM), s_f_bias=rep(p["ml_f_bias"], ML_HEAD_DIM),
        dskip=rep(p["ssd_d"], SSD_HEAD_DIM),
        ssd_norm_w=r(p["ssd_norm_w"]), ml_norm_w=r(p["ml_norm_w"]),
        lru_conv_w=p["lru_conv_w"], lru_conv_b=r(p["lru_conv_b"]),
        lru_wax=jnp.concatenate([p["lru_wa"], p["lru_wx"]], axis=-1).astype(bf16),
        lru_ba=r(p["lru_ba"]), lru_bx=r(p["lru_bx"]), lru_lambda=r(p["lru_lambda"]),
        w_out=_cast_pad(p["w_out"]), mlp_up=_cast_pad(p["mlp_up"]),
        mlp_down=_cast_pad(p["mlp_down"]),
    )


def kernel(x_prompt, x_sample, c_prompt, c_sample, state_ssm, state_ssd_conv, state_mlstm_c, state_mlstm_n, state_mlstm_m, state_lru_h, state_lru_conv, ada_w, ada_b, norm1_w, norm2_w, w_in, ssd_conv_w, ssd_conv_b, ssd_dt_bias, ssd_a_log, ssd_d, ssd_norm_w, ml_i_bias, ml_f_bias, ml_norm_w, lru_conv_w, lru_conv_b, lru_wa, lru_ba, lru_wx, lru_bx, lru_lambda, w_out, mlp_up, mlp_down, final_norm_w):
    p = dict(norm1_w=norm1_w, norm2_w=norm2_w, w_in=w_in, ssd_conv_w=ssd_conv_w,
             ssd_conv_b=ssd_conv_b, ssd_dt_bias=ssd_dt_bias, ssd_a_log=ssd_a_log, ssd_d=ssd_d,
             ssd_norm_w=ssd_norm_w, ml_i_bias=ml_i_bias, ml_f_bias=ml_f_bias,
             ml_norm_w=ml_norm_w, lru_conv_w=lru_conv_w, lru_conv_b=lru_conv_b, lru_wa=lru_wa,
             lru_ba=lru_ba, lru_wx=lru_wx, lru_bx=lru_bx, lru_lambda=lru_lambda, w_out=w_out,
             mlp_up=mlp_up, mlp_down=mlp_down)
    w = _prep_params(p)
    fnw = final_norm_w.reshape(1, D_MODEL)
    bp = x_prompt.shape[0]
    bs_ = x_sample.shape[0]

    mod = _ada(jnp.concatenate([c_sample, c_prompt], axis=0), ada_w, ada_b)

    xp = _interleave_rows(x_prompt)
    p_out = [jnp.zeros((N_LAYERS, bp) + s, f32) for s in (
        (SSD_HEADS // 2, 2 * SSD_HEAD_DIM, SSD_STATE),
        (CONV_W - 1, SSD_CONV_DIM), (ML_HEADS, ML_HEAD_DIM, ML_HEAD_DIM),
        (ML_HEADS, ML_HEAD_DIM), (ML_HEADS, V7X_LANES),
        (1, LRU_WIDTH), (CONV_W - 1, LRU_WIDTH))]
    big_states = ((N_LAYERS, bs_, SSD_HEADS * SSD_HEAD_DIM, SSD_STATE),
                  (N_LAYERS, bs_, ML_HEADS * ML_HEAD_DIM, ML_HEAD_DIM))
    for l in range(N_LAYERS):
        res = _player(l, xp, mod, bs_, w, fnw, l == N_LAYERS - 1, p_out,
                      big_states if l == 0 else ())
        xp, p_out = res[0], list(res[1:1 + N_STATE_OUTS])
        if l == 0:
            ssm_buf, c_buf = res[1 + N_STATE_OUTS:]
    xp = _deinterleave_rows(xp)
    ssm, sconv, mc, mn, mm, lh, lconv = p_out
    p_states = (ssm.reshape(N_LAYERS, bp, SSD_HEADS, SSD_HEAD_DIM, SSD_STATE), sconv, mc, mn,
                mm[..., 0], lh.reshape(N_LAYERS, bp, LRU_WIDTH), lconv)

    tok_major = lambda a: jnp.swapaxes(a, -3, -2)
    xs = tok_major(x_sample)
    mod_s = mod
    st_in = (
        (state_ssm.reshape(N_LAYERS, bs_, SSD_HEADS * SSD_HEAD_DIM, SSD_STATE), 1),
        (tok_major(state_ssd_conv), 2),
        (state_mlstm_c.reshape(N_LAYERS, bs_, ML_HEADS * ML_HEAD_DIM, ML_HEAD_DIM), 1),
        (state_mlstm_n.reshape(N_LAYERS, bs_, ML_WIDTH), 1),
        (jnp.repeat(state_mlstm_m, ML_HEAD_DIM, axis=-1), 1),
        (state_lru_h, 1),
        (tok_major(state_lru_conv), 2),
    )
    st_out = [jnp.zeros(a.shape, f32) for a, _ in st_in]
    st_out[0], st_out[2] = ssm_buf, c_buf
    for l in range(N_LAYERS):
        u = _inproj(l, xs, mod_s, w["norm1_w"], w["w_in_p"], w["w_exp"], SAMPLE_IN_ROWS)
        ycat, *st_out = _smix(l, u, st_in, st_out, w, SAMPLE_MIX_ROWS)
        xs = _outmlp(l, xs, ycat, mod_s, w["w_out"], w["norm2_w"], w["mlp_up"], w["mlp_down"],
                     fnw, SAMPLE_OUT_ROWS, l == N_LAYERS - 1)
    xs = tok_major(xs)
    ssm, sconv, mc, mn, mm, lh, lconv = st_out
    s_states = (ssm.reshape(N_LAYERS, bs_, SSD_HEADS, SSD_HEAD_DIM, SSD_STATE), tok_major(sconv),
                mc.reshape(N_LAYERS, bs_, ML_HEADS, ML_HEAD_DIM, ML_HEAD_DIM),
                mn.reshape(N_LAYERS, bs_, ML_HEADS, ML_HEAD_DIM),
                mm.reshape(N_LAYERS, bs_, ML_HEADS, ML_HEAD_DIM)[..., 0],
                lh, tok_major(lconv))

    return (xp, xs) + p_states + s_states
```
